```python
import math
import jax, jax.numpy as jnp
from jax import lax
import numpy as np

D_MODEL = 1024
BATCH = 16
SEQ = 256
DEPTH = 2
DEC_BATCH = 2
DEC_SEQ = 1024
PAST_LEN = 512

GRID_W = 64
D_MIX = D_MODEL
N_MIXERS = 4
D_GROUP = D_MIX // N_MIXERS
MLSTM_HEADS = 4
MLSTM_DH = D_GROUP // MLSTM_HEADS
MLSTM_CHUNK = 64
GMLP_GROUPS = 4
GMLP_CHUNK = 128
GMLP_DG = D_GROUP // GMLP_GROUPS
POOL_WINDOWS = (2, 4, 8, 16)
POOL_DG = D_GROUP // len(POOL_WINDOWS)
CONV_W = 3
D_FF = 2816
FFN_CONV_W = 3
N_MOD = 6
N_GATE_COLS = 4 * MLSTM_HEADS
D_IN = 4 * D_GROUP + N_GATE_COLS + 2 * D_GROUP + D_GROUP + 3 * D_GROUP
EPS = 1e-6
POS_BASE = 10000.0

kernel_name = "hybrid_mlstm_gmlp_pool_conv_dit_step"


def rms_norm(x, g):
    xf = x.astype(jnp.float32)
    y = xf * lax.rsqrt(jnp.mean(xf * xf, axis=-1, keepdims=True) + EPS)
    return (y * g.astype(jnp.float32)).astype(x.dtype)


def dwconv_centred(x, w):
    k = w.shape[0]
    p = k // 2
    t = x.shape[1]
    xp = jnp.pad(x, ((0, 0), (p, k - 1 - p), (0, 0)))
    return sum(xp[:, j:j + t] * w[j] for j in range(k))


def grid_sincos(t_len, d):
    rows = t_len // GRID_W
    r = jnp.repeat(jnp.arange(rows, dtype=jnp.float32), GRID_W)
    col = jnp.tile(jnp.arange(GRID_W, dtype=jnp.float32), rows)
    quarter = d // 4
    omega = 1.0 / (POS_BASE ** (jnp.arange(quarter, dtype=jnp.float32) / quarter))

    def emb(p):
        a = p[:, None] * omega[None, :]
        return jnp.concatenate([jnp.sin(a), jnp.cos(a)], axis=-1)

    return jnp.concatenate([emb(r), emb(col)], axis=-1)


def mlstm_chunkwise(q, k, v, ig, lf, C0, n0, m0):
    B, T, H, D = q.shape
    L = MLSTM_CHUNK
    nc = T // L

    def to_chunks(a):
        a = a.reshape((B, nc, L) + a.shape[2:])
        a = jnp.moveaxis(a, 3, 2)
        return jnp.moveaxis(a, 1, 0)

    xs = (to_chunks(q), to_chunks(k), to_chunks(v), to_chunks(ig), to_chunks(lf))
    lower = jnp.tril(jnp.ones((L, L), dtype=bool))

    def step(carry, inp):
        C, n, m = carry
        qb, kb, vb, ib, fb = inp
        b = jnp.cumsum(fb, axis=-1)
        dmat = b[..., :, None] - b[..., None, :] + ib[..., None, :]
        dmat = jnp.where(lower, dmat, -jnp.inf)
        inter = b + m[..., None]
        m_row = jnp.maximum(jnp.max(dmat, axis=-1), inter)
        w_intra = jnp.exp(dmat - m_row[..., None])
        w_inter = jnp.exp(inter - m_row)
        s = jnp.einsum('bhtd,bhsd->bhts', qb, kb) * w_intra
        num = (w_inter[..., None] * jnp.einsum('bhtd,bhde->bhte', qb, C)
               + jnp.einsum('bhts,bhse->bhte', s, vb))
        den = w_inter * jnp.einsum('bhtd,bhd->bht', qb, n) + jnp.sum(s, axis=-1)
        h = num / jnp.maximum(jnp.abs(den), jnp.exp(-m_row))[..., None]
        b_last = b[..., -1]
        wk = b_last[..., None] - b + ib
        m_new = jnp.maximum(b_last + m, jnp.max(wk, axis=-1))
        a_prev = jnp.exp(b_last + m - m_new)
        a_k = jnp.exp(wk - m_new[..., None])
        C_new = a_prev[..., None, None] * C + jnp.einsum('bhs,bhsd,bhse->bhde', a_k, kb, vb)
        n_new = a_prev[..., None] * n + jnp.einsum('bhs,bhsd->bhd', a_k, kb)
        return (C_new, n_new, m_new), h

    (C, n, m), hs = lax.scan(step, (C0, n0, m0), xs)
    h = jnp.moveaxis(hs, 0, 1)
    h = jnp.moveaxis(h, 2, 3).reshape(B, T, H, D)
    return h, C, n, m


def bi_mlstm(q, k, v, gates, st):
    C0, n0, m0 = st
    h_f, Cf, nf, mf = mlstm_chunkwise(q, k, v, gates[:, :, 0], jax.nn.log_sigmoid(gates[:, :, 1]),
                                      C0[:, 0], n0[:, 0], m0[:, 0])
    fl = lambda a: jnp.flip(a, axis=1)
    h_b, Cb, nb, mb = mlstm_chunkwise(fl(q), fl(k), fl(v), fl(gates[:, :, 2]),
                                      fl(jax.nn.log_sigmoid(gates[:, :, 3])),
                                      C0[:, 1], n0[:, 1], m0[:, 1])
    h = h_f + fl(h_b)
    return h, (jnp.stack([Cf, Cb], axis=1), jnp.stack([nf, nb], axis=1), jnp.stack([mf, mb], axis=1))


def chunk_spatial_gate(u, v, ws, bs):
    B, T, _ = u.shape
    nchunk = T // GMLP_CHUNK
    vg = v.reshape(B, nchunk, GMLP_CHUNK, GMLP_GROUPS, GMLP_DG)
    mixed = jnp.einsum('gts,bnsgc->bntgc', ws, vg) + bs.T[None, None, :, :, None]
    return u * mixed.reshape(B, T, D_GROUP).astype(u.dtype)


def multiscale_pool(z, w_pool, scale):
    B, T, _ = z.shape
    zf = z.astype(jnp.float32).reshape(B, T, len(POOL_WINDOWS), POOL_DG)
    P = jnp.concatenate([jnp.zeros((B, 1, len(POOL_WINDOWS), POOL_DG), jnp.float32),
                         jnp.cumsum(zf, axis=1)], axis=1)
    t = jnp.arange(T)
    outs = []
    for g, w in enumerate(POOL_WINDOWS):
        lo = jnp.clip(t - w // 2, 0, T)
        hi = jnp.clip(t + w // 2, 0, T)
        Pg = P[:, :, g]
        cnt = (hi - lo).astype(jnp.float32)[None, :, None]
        outs.append((Pg[:, hi] - Pg[:, lo]) / cnt - zf[:, :, g])
    pooled = jnp.stack(outs, axis=2)
    y = jnp.einsum('btgc,gcd->btgd', pooled, w_pool.astype(jnp.float32)).reshape(B, T, D_GROUP)
    return (y * scale.astype(jnp.float32)).astype(z.dtype)


def mixer_sublayer(h, st, w_in, gate_b, gmlp_ws, gmlp_b, pool_w, pool_scale, conv_w, w_out):
    B, T, _ = h.shape
    z = h @ w_in
    sizes = (D_GROUP,) * 4 + (N_GATE_COLS,) + (D_GROUP,) * 2 + (D_GROUP,) + (D_GROUP,) * 3
    idx = [int(i) for i in np.cumsum(sizes)[:-1]]
    q, k, v, o, g, gu, gv, pz, cb, cc, cx = jnp.split(z, idx, axis=-1)
    heads = lambda a: a.reshape(B, T, MLSTM_HEADS, MLSTM_DH).astype(jnp.float32)
    gates = g.reshape(B, T, 4, MLSTM_HEADS).astype(jnp.float32) + gate_b.astype(jnp.float32)
    h_a, new_st = bi_mlstm(heads(q), heads(k) * (MLSTM_DH ** -0.5), heads(v), gates, st)
    y_a = (jax.nn.sigmoid(o.astype(jnp.float32)) * h_a.reshape(B, T, D_GROUP)).astype(h.dtype)
    y_b = chunk_spatial_gate(gu, gv, gmlp_ws, gmlp_b)
    y_c = multiscale_pool(pz, pool_w, pool_scale)
    y_d = cb * dwconv_centred(cc * cx, conv_w)
    y = jnp.concatenate([y_a, y_b, y_c, y_d], axis=-1) @ w_out
    return y, new_st


def trunk_layer(x, cond, st, norm_g, ada_w, ada_b, w_in, gate_b, gmlp_ws, gmlp_b, pool_w,
                pool_scale, conv_w, w_out, ffn_up, ffn_conv, ffn_down):
    mod = (jax.nn.silu(cond) @ ada_w + ada_b).astype(x.dtype).reshape(cond.shape[0], N_MOD, 1, D_MODEL)
    sh1, sc1, gt1, sh2, sc2, gt2 = [mod[:, i] for i in range(N_MOD)]
    h = rms_norm(x, norm_g[0]) * (1 + sc1) + sh1
    y, st = mixer_sublayer(h, st, w_in, gate_b, gmlp_ws, gmlp_b, pool_w, pool_scale, conv_w, w_out)
    x = x + gt1 * rms_norm(y, norm_g[1])
    h = rms_norm(x, norm_g[2]) * (1 + sc2) + sh2
    a = dwconv_centred(h @ ffn_up, ffn_conv)
    val, gate = jnp.split(a, 2, axis=-1)
    y = (jax.nn.silu(gate) * val) @ ffn_down
    x = x + gt2 * rms_norm(y, norm_g[3])
    return x, st


def setup_inputs(seed: int = 0) -> dict:
    key = jax.random.key(seed)
    ks = jax.random.split(key, 24)
    nrm = lambda k, s: jax.random.normal(k, s, jnp.float32)
    H = MLSTM_HEADS
    fg_base = jnp.linspace(3.0, 6.0, H, dtype=jnp.float32)
    zero_h = jnp.zeros((H,), jnp.float32)
    gate_base = jnp.stack([zero_h, fg_base, zero_h, fg_base], axis=0)
    return {
        "x_prompt": nrm(ks[0], (BATCH, SEQ, D_MODEL)),
        "x_sample": nrm(ks[1], (DEC_BATCH, DEC_SEQ, D_MODEL)),
        "state_C": 0.1 * nrm(ks[2], (DEC_BATCH, DEPTH, 2, H, MLSTM_DH, MLSTM_DH)),
        "state_n": 0.1 * nrm(ks[3], (DEC_BATCH, DEPTH, 2, H, MLSTM_DH)),
        "state_m": 0.5 * nrm(ks[4], (DEC_BATCH, DEPTH, 2, H)),
        "c": nrm(ks[5], (DEC_BATCH, D_MODEL)),
        "c_ctx": nrm(ks[6], (D_MODEL,)),
        "norm_g": 1.0 + 0.1 * nrm(ks[7], (DEPTH, 4, D_MODEL)),
        "ada_w": 0.5 * D_MODEL ** -0.5 * nrm(ks[8], (DEPTH, D_MODEL, N_MOD * D_MODEL)),
        "ada_b": 0.02 * nrm(ks[9], (DEPTH, N_MOD * D_MODEL)),
        "w_in": D_MODEL ** -0.5 * nrm(ks[10], (DEPTH, D_MODEL, D_IN)),
        "gate_b": gate_base[None] + 0.1 * nrm(ks[11], (DEPTH, 4, H)),
        "gmlp_ws": GMLP_CHUNK ** -0.5 * nrm(ks[12], (DEPTH, GMLP_GROUPS, GMLP_CHUNK, GMLP_CHUNK)),
        "gmlp_b": 1.0 + 0.1 * nrm(ks[13], (DEPTH, GMLP_GROUPS, GMLP_CHUNK)),
        "pool_w": POOL_DG ** -0.5 * nrm(ks[14], (DEPTH, len(POOL_WINDOWS), POOL_DG, POOL_DG)),
        "pool_scale": 1.0 + 0.1 * nrm(ks[15], (DEPTH, D_GROUP)),
        "conv_w": CONV_W ** -0.5 * nrm(ks[16], (DEPTH, CONV_W, D_GROUP)),
        "w_out": D_MIX ** -0.5 * nrm(ks[17], (DEPTH, D_MIX, D_MODEL)),
        "ffn_up": D_MODEL ** -0.5 * nrm(ks[18], (DEPTH, D_MODEL, 2 * D_FF)),
        "ffn_conv": FFN_CONV_W ** -0.5 * nrm(ks[19], (DEPTH, FFN_CONV_W, 2 * D_FF)),
        "ffn_down": D_FF ** -0.5 * nrm(ks[20], (DEPTH, D_FF, D_MODEL)),
    }


def reference(x_prompt, x_sample, state_C, state_n, state_m, c, c_ctx, norm_g, ada_w, ada_b, w_in,
              gate_b, gmlp_ws, gmlp_b, pool_w, pool_scale, conv_w, w_out, ffn_up, ffn_conv, ffn_down):
    f32 = jnp.float32
    H, Dh = MLSTM_HEADS, MLSTM_DH

    def lw(l):
        return (norm_g[l], ada_w[l], ada_b[l], w_in[l], gate_b[l], gmlp_ws[l], gmlp_b[l], pool_w[l],
                pool_scale[l], conv_w[l], w_out[l], ffn_up[l], ffn_conv[l], ffn_down[l])

    Bp = x_prompt.shape[0]
    zero_state = (jnp.zeros((Bp, 2, H, Dh, Dh), f32), jnp.zeros((Bp, 2, H, Dh), f32),
                  jnp.zeros((Bp, 2, H), f32))
    xp = x_prompt
    Cs, ns, ms = [], [], []
    for l in range(DEPTH):
        xp, (C_l, n_l, m_l) = trunk_layer(xp, c_ctx[None, :], zero_state, *lw(l))
        Cs.append(C_l)
        ns.append(n_l)
        ms.append(m_l)

    T = x_sample.shape[1]
    xs = x_sample + grid_sincos(T, D_MODEL).astype(x_sample.dtype)[None]
    for l in range(DEPTH):
        st = (state_C[:, l].astype(f32), state_n[:, l].astype(f32), state_m[:, l].astype(f32))
        xs, _ = trunk_layer(xs, c, st, *lw(l))

    return (xp, xs, jnp.stack(Cs, axis=1), jnp.stack(ns, axis=1), jnp.stack(ms, axis=1))
```

```python
import functools

import numpy as np
import jax
import jax.numpy as jnp
from jax import lax
from jax.experimental import pallas as pl
from jax.experimental.pallas import tpu as pltpu

F32 = jnp.float32
BF16 = jnp.bfloat16

D_MODEL = 1024
DEPTH = 2
HEADS = 4
DH = 64
D_GROUP = 256
D_FF = 2816
N_MOD = 6
EPS = 1e-6
GRID_W = 64
POS_BASE = 10000.0

CHUNK = 256
GMLP_CHUNK = 128
HALO = 8
LANES = 128

C_Q, C_K, C_V, C_O = 0, 256, 512, 768
C_GU, C_GV, C_PZ, C_CB, C_CC, C_CX = 1024, 1280, 1536, 1792, 2048, 2304
C_GATE = 2560
Z_COLS = C_GATE + LANES
GATE_STRIDE = 8

FFN_ROWS = 1024
FFN_SPLIT = 2
FFN_SUB = 256
VMEM_LIMIT = 60 * 1024 * 1024


def _rms(x, g):
    ms = jnp.mean(x * x, axis=-1, keepdims=True)
    return x * lax.rsqrt(ms + EPS) * g


def _sigmoid(x):
    return 1.0 / (1.0 + jnp.exp(-x))


def _log_sigmoid(x):
    return jnp.minimum(x, 0.0) - jnp.log1p(jnp.exp(-jnp.abs(x)))


def _scan_lanes(x, op, reverse, fill):
    n = x.shape[1]
    lane = lax.broadcasted_iota(jnp.int32, x.shape, 1)
    k = 1
    while k < n:
        if reverse:
            sh = jnp.where(lane < n - k, pltpu.roll(x, n - k, axis=1), fill)
        else:
            sh = jnp.where(lane >= k, pltpu.roll(x, k, axis=1), fill)
        x = op(x, sh)
        k *= 2
    return x


def _shift_down(x, k, row):
    del row
    return pltpu.roll(x, k, axis=0)


def _shift_up(x, k, row):
    del row
    return pltpu.roll(x, x.shape[0] - k, axis=0)


def _mod_kernel(c_ref, w_ref, b_ref, o_ref):
    c = c_ref[...]
    s = (c * _sigmoid(c)).astype(BF16)
    o_ref[0] = jnp.dot(s, w_ref[0].astype(BF16), preferred_element_type=F32) + b_ref[0]


def _modulation(conds, ada_w, ada_b):
    n_out = N_MOD * D_MODEL
    bn = 1536
    return pl.pallas_call(
        _mod_kernel,
        out_shape=jax.ShapeDtypeStruct((DEPTH, 8, n_out), F32),
        grid=(DEPTH, n_out // bn),
        in_specs=[
            pl.BlockSpec((8, D_MODEL), lambda l, j: (0, 0)),
            pl.BlockSpec((1, D_MODEL, bn), lambda l, j: (l, 0, j)),
            pl.BlockSpec((1, 1, bn), lambda l, j: (l, 0, j)),
        ],
        out_specs=pl.BlockSpec((1, 8, bn), lambda l, j: (l, 0, j)),
        compiler_params=pltpu.CompilerParams(
            dimension_semantics=("arbitrary", "arbitrary"), vmem_limit_bytes=VMEM_LIMIT),
        name="adaln_mod",
    )(conds, ada_w, ada_b.reshape(DEPTH, 1, n_out))


def _mlstm_pass(z_ref, gb_ref, ha_ref, seq_len, reverse, c_aug, m_prev, need_final, accumulate):
    nc = seq_len // CHUNK
    order = range(nc - 1, -1, -1) if reverse else range(nc)
    row = lax.broadcasted_iota(jnp.int32, (CHUNK, CHUNK), 0)
    col = lax.broadcasted_iota(jnp.int32, (CHUNK, CHUNK), 1)
    tri = (col >= row) if reverse else (col <= row)
    lane = lax.broadcasted_iota(jnp.int32, (CHUNK, LANES), 1)
    ones_col = jnp.where(lane == DH, 1.0, 0.0)
    ig0 = 2 * GATE_STRIDE if reverse else 0
    fg0 = ig0 + GATE_STRIDE
    last = 0 if reverse else CHUNK - 1
    for step, c in enumerate(order):
        r0 = HALO + c * CHUNK
        gates = z_ref[r0:r0 + CHUNK, C_GATE:C_GATE + LANES] + gb_ref[...]
        gates_t = gates.T
        ig_rows = gates_t[ig0:ig0 + GATE_STRIDE]
        b_rows = _scan_lanes(_log_sigmoid(gates_t[fg0:fg0 + GATE_STRIDE]), jnp.add, reverse, 0.0)
        g_rows = ig_rows - b_rows
        cm_rows = _scan_lanes(g_rows, jnp.maximum, reverse, -jnp.inf)
        stack = jnp.concatenate(
            [b_rows, cm_rows, jnp.zeros((LANES - 2 * GATE_STRIDE, CHUNK), F32)], axis=0)
        cols = stack.T
        update = need_final or step < nc - 1
        for h in range(HEADS):
            q = z_ref[r0:r0 + CHUNK, C_Q + DH * h:C_Q + DH * (h + 1)].astype(BF16)
            k = (z_ref[r0:r0 + CHUNK, C_K + DH * h:C_K + DH * (h + 1)] * (DH ** -0.5)).astype(BF16)
            vp = z_ref[r0:r0 + CHUNK, C_V + LANES * (h // 2):C_V + LANES * (h // 2 + 1)]
            if h % 2:
                vp = pltpu.roll(vp, DH, axis=1)
            v_aug = jnp.where(lane < DH, vp, ones_col)
            b_col = cols[:, h:h + 1]
            m_col = jnp.maximum(cols[:, GATE_STRIDE + h:GATE_STRIDE + h + 1], m_prev[h])
            s = lax.dot_general(q, k, (((1,), (1,)), ((), ())), preferred_element_type=F32)
            w = jnp.exp(jnp.where(tri, g_rows[h:h + 1] - m_col, -jnp.inf))
            tot = jnp.dot((s * w).astype(BF16), v_aug.astype(BF16), preferred_element_type=F32)
            if c_aug[h] is not None:
                qc = jnp.dot(q, c_aug[h].astype(BF16), preferred_element_type=F32)
                tot = tot + jnp.exp(m_prev[h] - m_col) * qc
            num = tot[:, :DH]
            den = tot[:, DH:DH + 1]
            hh = num / jnp.maximum(jnp.abs(den), jnp.exp(-(b_col + m_col)))
            rows = slice(c * CHUNK, (c + 1) * CHUNK)
            lanes = slice(DH * h, DH * (h + 1))
            if accumulate:
                ha_ref[rows, lanes] = ha_ref[rows, lanes] + hh
            else:
                ha_ref[rows, lanes] = hh
            if update:
                m_last = m_col[last:last + 1]
                b_last = b_col[last:last + 1]
                a_col = jnp.exp(gates[:, ig0 + h:ig0 + h + 1] - b_col - m_last)
                c_add = lax.dot_general(k, (a_col * v_aug).astype(BF16), (((0,), (0,)), ((), ())),
                                        preferred_element_type=F32)
                if c_aug[h] is not None:
                    c_aug[h] = jnp.exp(m_prev[h] - m_last) * c_aug[h] + c_add
                else:
                    c_aug[h] = c_add
                m_prev[h] = b_last + m_last


def _mixer_kernel(*refs, seq_len, add_pos, has_state, emit_state):
    it = iter(refs)
    x_ref = next(it)
    if add_pos:
        er_ref = next(it)
        ec_ref = next(it)
    mod_ref = next(it)
    ng_ref = next(it)
    win_ref = next(it)
    gb_ref = next(it)
    ws_ref = next(it)
    gbias_ref = next(it)
    pbd_ref = next(it)
    psc_ref = next(it)
    cw_ref = next(it)
    wout_ref = next(it)
    if has_state:
        c0_ref = next(it)
        m0_ref = next(it)
    x1_ref = next(it)
    h2_ref = next(it)
    if emit_state:
        cout_ref = next(it)
        mout_ref = next(it)
    z_ref = next(it)
    ha_ref = next(it)
    ycat_ref = next(it)

    nblk = seq_len // CHUNK
    mod = mod_ref[0]
    sh1, sc1, gt1, sh2, sc2 = (mod[i:i + 1] for i in range(5))
    ng = ng_ref[...]

    zeros_halo = jnp.zeros((HALO, Z_COLS - C_PZ), F32)
    z_ref[0:HALO, C_PZ:Z_COLS] = zeros_halo
    z_ref[HALO + seq_len:2 * HALO + seq_len, C_PZ:Z_COLS] = zeros_halo

    for r in range(nblk):
        rows = slice(r * CHUNK, (r + 1) * CHUNK)
        xb = x_ref[0, rows, :]
        if add_pos:
            pieces = []
            for g in range(CHUNK // GRID_W):
                gi = r * (CHUNK // GRID_W) + g
                er = jnp.broadcast_to(er_ref[gi:gi + 1, :], (GRID_W, D_MODEL // 2))
                pieces.append(jnp.concatenate([er, ec_ref[...]], axis=1))
            xb = xb + jnp.concatenate(pieces, axis=0)
            x1_ref[0, rows, :] = xb
        hb = (_rms(xb, ng[0:1]) * (1.0 + sc1) + sh1).astype(BF16)
        z_ref[HALO + r * CHUNK:HALO + (r + 1) * CHUNK, :] = jnp.dot(
            hb, win_ref[...], preferred_element_type=F32)

    for d, reverse in enumerate((False, True)):
        if has_state:
            c_aug = [c0_ref[0, d * HEADS + h] for h in range(HEADS)]
            m_prev = [m0_ref[0, d * HEADS + h:d * HEADS + h + 1, 0:1] for h in range(HEADS)]
        else:
            c_aug = [None] * HEADS
            m_prev = [jnp.zeros((1, 1), F32)] * HEADS
        _mlstm_pass(z_ref, gb_ref, ha_ref, seq_len, reverse, c_aug, m_prev,
                    need_final=emit_state, accumulate=(d == 1))
        if emit_state:
            for h in range(HEADS):
                cout_ref[0, d * HEADS + h] = c_aug[h]
                mout_ref[0, d * HEADS + h:d * HEADS + h + 1, :] = jnp.broadcast_to(
                    m_prev[h], (1, LANES))

    row_h = lax.broadcasted_iota(jnp.int32, (CHUNK + 2 * HALO, D_GROUP), 0)
    lane_g = lax.broadcasted_iota(jnp.int32, (CHUNK, D_GROUP), 1) // (D_GROUP // 4)
    lg = lax.broadcasted_iota(jnp.int32, (GMLP_CHUNK, D_GROUP), 1) // (D_GROUP // 4)
    row_c = lax.broadcasted_iota(jnp.int32, (CHUNK, D_GROUP), 0)
    half = jnp.where(lane_g == 0, 1, jnp.where(lane_g == 1, 2, jnp.where(lane_g == 2, 4, 8)))
    cw = cw_ref[...]
    for r in range(nblk):
        rows = slice(r * CHUNK, (r + 1) * CHUNK)
        zrows = slice(HALO + r * CHUNK, HALO + (r + 1) * CHUNK)
        hrows = slice(r * CHUNK, (r + 1) * CHUNK + 2 * HALO)
        inner = slice(HALO, HALO + CHUNK)

        ycat_ref[rows, 0:D_GROUP] = (
            _sigmoid(z_ref[zrows, C_O:C_O + D_GROUP]) * ha_ref[rows, :]).astype(BF16)

        for s in range(CHUNK // GMLP_CHUNK):
            zs = slice(HALO + r * CHUNK + s * GMLP_CHUNK, HALO + r * CHUNK + (s + 1) * GMLP_CHUNK)
            vch = z_ref[zs, C_GV:C_GV + D_GROUP].astype(BF16)
            mixed = gbias_ref[...]
            for g in range(4):
                mg = jnp.dot(ws_ref[g], vch, preferred_element_type=F32)
                mixed = mixed + jnp.where(lg == g, mg, 0.0)
            yb = z_ref[zs, C_GU:C_GU + D_GROUP] * mixed
            ys = slice(r * CHUNK + s * GMLP_CHUNK, r * CHUNK + (s + 1) * GMLP_CHUNK)
            ycat_ref[ys, D_GROUP:2 * D_GROUP] = yb.astype(BF16)

        pz = z_ref[hrows, C_PZ:C_PZ + D_GROUP]
        a1 = pz
        a2 = a1 + _shift_down(a1, 1, row_h)
        a4 = a2 + _shift_down(a2, 2, row_h)
        a8 = a4 + _shift_down(a4, 4, row_h)
        b1 = pz
        b2 = b1 + _shift_up(b1, 1, row_h)
        b4 = b2 + _shift_up(b2, 2, row_h)
        b8 = b4 + _shift_up(b4, 4, row_h)
        win = [(_shift_down(a, 1, row_h) + b)[inner] for a, b in ((a1, b1), (a2, b2), (a4, b4), (a8, b8))]
        wsum = jnp.where(lane_g == 0, win[0], jnp.where(lane_g == 1, win[1],
                         jnp.where(lane_g == 2, win[2], win[3])))
        t_abs = row_c + r * CHUNK
        cnt = (jnp.minimum(t_abs + half, seq_len) - jnp.maximum(t_abs - half, 0)).astype(F32)
        pooled = wsum / cnt - pz[inner]
        yc = jnp.dot(pooled.astype(BF16), pbd_ref[...], preferred_element_type=F32) * psc_ref[...]
        ycat_ref[rows, 2 * D_GROUP:3 * D_GROUP] = yc.astype(BF16)

        u = z_ref[hrows, C_CC:C_CC + D_GROUP] * z_ref[hrows, C_CX:C_CX + D_GROUP]
        conv = (cw[0:1] * _shift_down(u, 1, row_h)[inner] + cw[1:2] * u[inner]
                + cw[2:3] * _shift_up(u, 1, row_h)[inner])
        yd = z_ref[zrows, C_CB:C_CB + D_GROUP] * conv
        ycat_ref[rows, 3 * D_GROUP:4 * D_GROUP] = yd.astype(BF16)

    for r in range(nblk):
        rows = slice(r * CHUNK, (r + 1) * CHUNK)
        y = jnp.dot(ycat_ref[rows, :], wout_ref[...], preferred_element_type=F32)
        xb = x1_ref[0, rows, :] if add_pos else x_ref[0, rows, :]
        x1 = xb + gt1 * _rms(y, ng[1:2])
        x1_ref[0, rows, :] = x1
        h2_ref[0, rows, :] = (_rms(x1, ng[2:3]) * (1.0 + sc2) + sh2).astype(BF16)


def _const_spec(shape):
    nd = len(shape)
    return pl.BlockSpec(shape, lambda b: (0,) * nd, pipeline_mode=pl.Buffered(1))


def _mixer_call(x, mods, cond_of, lw, pos=None, state=None, emit_state=False):
    bsz, seq_len, _ = x.shape
    add_pos = pos is not None
    has_state = state is not None
    args = [x]
    in_specs = [pl.BlockSpec((1, seq_len, D_MODEL), lambda b: (b, 0, 0))]
    if add_pos:
        args += list(pos)
        in_specs += [_const_spec(pos[0].shape), _const_spec(pos[1].shape)]
    args.append(mods)
    in_specs.append(pl.BlockSpec((1, 8, D_MODEL), lambda b: (cond_of(b), 0, 0)))
    for name in ("norm_g", "w_in", "gate_b", "gmlp_ws", "gmlp_bias", "pool_bd", "pool_scale",
                 "conv_w", "w_out"):
        args.append(lw[name])
        in_specs.append(_const_spec(lw[name].shape))
    if has_state:
        args += list(state)
        in_specs += [pl.BlockSpec((1, 2 * HEADS, DH, LANES), lambda b: (b, 0, 0, 0)),
                     pl.BlockSpec((1, 2 * HEADS, LANES), lambda b: (b, 0, 0))]
    out_shape = [jax.ShapeDtypeStruct((bsz, seq_len, D_MODEL), F32),
                 jax.ShapeDtypeStruct((bsz, seq_len, D_MODEL), BF16)]
    out_specs = [pl.BlockSpec((1, seq_len, D_MODEL), lambda b: (b, 0, 0)),
                 pl.BlockSpec((1, seq_len, D_MODEL), lambda b: (b, 0, 0))]
    if emit_state:
        out_shape += [jax.ShapeDtypeStruct((bsz, 2 * HEADS, DH, LANES), F32),
                      jax.ShapeDtypeStruct((bsz, 2 * HEADS, LANES), F32)]
        out_specs += [pl.BlockSpec((1, 2 * HEADS, DH, LANES), lambda b: (b, 0, 0, 0)),
                      pl.BlockSpec((1, 2 * HEADS, LANES), lambda b: (b, 0, 0))]
    kern = functools.partial(_mixer_kernel, seq_len=seq_len, add_pos=add_pos,
                             has_state=has_state, emit_state=emit_state)
    return pl.pallas_call(
        kern,
        out_shape=out_shape,
        grid=(bsz,),
        in_specs=in_specs,
        out_specs=out_specs,
        scratch_shapes=[pltpu.VMEM((seq_len + 2 * HALO, Z_COLS), F32),
                        pltpu.VMEM((seq_len, D_GROUP), F32),
                        pltpu.VMEM((seq_len, D_MODEL), BF16)],
        compiler_params=pltpu.CompilerParams(
            dimension_semantics=("arbitrary",), vmem_limit_bytes=VMEM_LIMIT),
        name="mixer_t%d" % seq_len,
    )(*args)


def _ffn_kernel(x1_ref, h2_ref, mod_ref, ng_ref, up_ref, cw_ref, down_ref, o_ref, acc_ref, act_ref,
                *, seq_len):
    j = pl.program_id(1)
    rows = x1_ref.shape[0]
    nsub = up_ref.shape[1] // FFN_SUB

    @pl.when(j == 0)
    def _():
        acc_ref[...] = jnp.zeros_like(acc_ref)

    h2 = h2_ref[...]
    t_in_seq = lax.broadcasted_iota(jnp.int32, (rows, FFN_SUB), 0) & (seq_len - 1)
    first = t_in_seq == 0
    last = t_in_seq == seq_len - 1
    for s in range(nsub):
        cols = slice(s * FFN_SUB, (s + 1) * FFN_SUB)
        a = jnp.dot(h2, up_ref[:, cols], preferred_element_type=F32)
        cw = cw_ref[:, cols]
        a_prev = jnp.where(first, 0.0, pltpu.roll(a, 1, axis=0))
        a_next = jnp.where(last, 0.0, pltpu.roll(a, rows - 1, axis=0))
        ac = cw[0:1] * a_prev + cw[1:2] * a + cw[2:3] * a_next
        val = ac[:, :FFN_SUB // 2]
        gate = ac[:, FFN_SUB // 2:]
        act = gate * _sigmoid(gate) * val
        act_ref[:, s * (FFN_SUB // 2):(s + 1) * (FFN_SUB // 2)] = act.astype(BF16)
    acc_ref[...] += jnp.dot(act_ref[...], down_ref[...], preferred_element_type=F32)

    @pl.when(j == pl.num_programs(1) - 1)
    def _():
        gt2 = mod_ref[0][5:6]
        o_ref[...] = x1_ref[...] + gt2 * _rms(acc_ref[...], ng_ref[3:4])


def _ffn_call(x1, h2, mods, cond_of, lw, seq_len):
    n = x1.shape[0]
    up_cols = 2 * D_FF // FFN_SPLIT
    dn_rows = D_FF // FFN_SPLIT
    return pl.pallas_call(
        functools.partial(_ffn_kernel, seq_len=seq_len),
        out_shape=jax.ShapeDtypeStruct((n, D_MODEL), F32),
        grid=(n // FFN_ROWS, FFN_SPLIT),
        in_specs=[
            pl.BlockSpec((FFN_ROWS, D_MODEL), lambda i, j: (i, 0)),
            pl.BlockSpec((FFN_ROWS, D_MODEL), lambda i, j: (i, 0)),
            pl.BlockSpec((1, 8, D_MODEL), lambda i, j: (cond_of(i), 0, 0)),
            pl.BlockSpec((8, D_MODEL), lambda i, j: (0, 0)),
            pl.BlockSpec((D_MODEL, up_cols), lambda i, j: (0, j)),
            pl.BlockSpec((8, up_cols), lambda i, j: (0, j)),
            pl.BlockSpec((dn_rows, D_MODEL), lambda i, j: (j, 0)),
        ],
        out_specs=pl.BlockSpec((FFN_ROWS, D_MODEL), lambda i, j: (i, 0)),
        scratch_shapes=[pltpu.VMEM((FFN_ROWS, D_MODEL), F32),
                        pltpu.VMEM((FFN_ROWS, dn_rows), BF16)],
        compiler_params=pltpu.CompilerParams(
            dimension_semantics=("arbitrary", "arbitrary"), vmem_limit_bytes=VMEM_LIMIT),
        name="ffn_t%d" % seq_len,
    )(x1, h2, mods, lw["norm_g"], lw["ffn_up"], lw["ffn_conv"], lw["ffn_down"])


def _pos_tables(t_len):
    quarter = D_MODEL // 4
    omega = (1.0 / (np.float32(POS_BASE) ** (np.arange(quarter, dtype=np.float32) / np.float32(quarter))))
    omega = omega.astype(np.float32)

    def emb(p):
        a = p.astype(np.float32)[:, None] * omega[None, :]
        return np.concatenate([np.sin(a), np.cos(a)], axis=-1).astype(np.float32)

    return jnp.asarray(emb(np.arange(t_len // GRID_W))), jnp.asarray(emb(np.arange(GRID_W)))


def _pad_rows(a, rows=8):
    return jnp.pad(a, ((0, rows - a.shape[0]), (0, 0)))


def _layer_weights(l, norm_g, w_in, gate_b, gmlp_ws, gmlp_b, pool_w, pool_scale, conv_w, w_out,
                   ffn_up, ffn_conv, ffn_down):
    w = w_in[l]
    gcols = w[:, 4 * D_GROUP:4 * D_GROUP + 4 * HEADS].reshape(D_MODEL, 4, HEADS)
    gcols = jnp.pad(gcols, ((0, 0), (0, 0), (0, GATE_STRIDE - HEADS))).reshape(D_MODEL, 4 * GATE_STRIDE)
    gcols = jnp.pad(gcols, ((0, 0), (0, LANES - 4 * GATE_STRIDE)))
    w_perm = jnp.concatenate([w[:, :4 * D_GROUP], w[:, 4 * D_GROUP + 4 * HEADS:], gcols], axis=1)
    gb = jnp.pad(gate_b[l], ((0, 0), (0, GATE_STRIDE - HEADS))).reshape(1, 4 * GATE_STRIDE)
    gb = jnp.pad(gb, ((0, 0), (0, LANES - 4 * GATE_STRIDE)))
    pool_bd = jnp.zeros((D_GROUP, D_GROUP), F32)
    dg = D_GROUP // 4
    for g in range(4):
        pool_bd = pool_bd.at[g * dg:(g + 1) * dg, g * dg:(g + 1) * dg].set(pool_w[l, g])
    nblk = D_FF // (FFN_SUB // 2)

    def interleave(a):
        lead = a.shape[:-1]
        val = a[..., :D_FF].reshape(lead + (nblk, 1, FFN_SUB // 2))
        gate = a[..., D_FF:].reshape(lead + (nblk, 1, FFN_SUB // 2))
        return jnp.concatenate([val, gate], axis=-2).reshape(lead + (2 * D_FF,))

    return {
        "norm_g": _pad_rows(norm_g[l]),
        "w_in": w_perm.astype(BF16),
        "gate_b": gb,
        "gmlp_ws": gmlp_ws[l].astype(BF16),
        "gmlp_bias": jnp.repeat(gmlp_b[l].T, D_GROUP // 4, axis=1),
        "pool_bd": pool_bd.astype(BF16),
        "pool_scale": pool_scale[l].reshape(1, D_GROUP),
        "conv_w": _pad_rows(conv_w[l]),
        "w_out": w_out[l].astype(BF16),
        "ffn_up": interleave(ffn_up[l]).astype(BF16),
        "ffn_conv": _pad_rows(interleave(ffn_conv[l])),
        "ffn_down": ffn_down[l].astype(BF16),
    }


def kernel(x_prompt, x_sample, state_C, state_n, state_m, c, c_ctx, norm_g, ada_w, ada_b, w_in,
           gate_b, gmlp_ws, gmlp_b, pool_w, pool_scale, conv_w, w_out, ffn_up, ffn_conv, ffn_down):
    bp, tp, _ = x_prompt.shape
    bs, ts, _ = x_sample.shape
    assert tp == CHUNK and ts % CHUNK == 0 and FFN_ROWS % tp == 0 and ts == FFN_ROWS

    conds = _pad_rows(jnp.concatenate([c_ctx[None, :], c], axis=0))
    mod_all = _modulation(conds, ada_w, ada_b)
    pos = _pos_tables(ts)

    xp, xs = x_prompt, x_sample
    cs, ns, ms = [], [], []
    for l in range(DEPTH):
        lw = _layer_weights(l, norm_g, w_in, gate_b, gmlp_ws, gmlp_b, pool_w, pool_scale, conv_w,
                            w_out, ffn_up, ffn_conv, ffn_down)
        mods = mod_all[l, :1 + bs].reshape(1 + bs, N_MOD, D_MODEL)
        mods = jnp.pad(mods, ((0, 0), (0, 8 - N_MOD), (0, 0)))

        x1, h2, c_aug, m_fin = _mixer_call(xp, mods, lambda b: 0, lw, emit_state=True)
        xp = _ffn_call(x1.reshape(bp * tp, D_MODEL), h2.reshape(bp * tp, D_MODEL), mods,
                       lambda i: 0, lw, tp).reshape(bp, tp, D_MODEL)
        cs.append(c_aug[..., :DH].reshape(bp, 2, HEADS, DH, DH))
        ns.append(c_aug[..., DH].reshape(bp, 2, HEADS, DH))
        ms.append(m_fin[..., 0].reshape(bp, 2, HEADS))

        c0 = jnp.concatenate(
            [state_C[:, l], state_n[:, l][..., None],
             jnp.zeros((bs, 2, HEADS, DH, LANES - DH - 1), F32)], axis=-1).reshape(bs, 2 * HEADS, DH, LANES)
        m0 = jnp.broadcast_to(state_m[:, l].reshape(bs, 2 * HEADS, 1), (bs, 2 * HEADS, LANES))
        x1, h2 = _mixer_call(xs, mods, lambda b: 1 + b, lw, pos=pos if l == 0 else None,
                             state=(c0.astype(F32), m0.astype(F32)))
        xs = _ffn_call(x1.reshape(bs * ts, D_MODEL), h2.reshape(bs * ts, D_MODEL), mods,
                       lambda i: 1 + i, lw, ts).reshape(bs, ts, D_MODEL)

    return (xp, xs, jnp.stack(cs, axis=1), jnp.stack(ns, axis=1), jnp.stack(ms, axis=1))
```

```python
import functools

import numpy as np
import jax
import jax.numpy as jnp
from jax import lax
from jax.experimental import pallas as pl
from jax.experimental.pallas import tpu as pltpu

F32 = jnp.float32
BF16 = jnp.bfloat16

D_MODEL = 1024
DEPTH = 2
HEADS = 4
PAIRS = HEADS // 2
DH = 64
D_GROUP = 256
D_FF = 2816
N_MOD = 6
EPS = 1e-6
GRID_W = 64
POS_BASE = 10000.0
D_IN = 2576
N_GATE = 4 * HEADS

CHUNK = 256
GMLP_CHUNK = 128
HALO = 8
LANES = 128

C_Q, C_K, C_V, C_O = 0, 256, 512, 768
C_GU, C_GV, C_PZ, C_CB, C_CC, C_CX = 1024, 1280, 1536, 1792, 2048, 2304
Z_COLS = 2560
C_GATE = Z_COLS
W_COLS = Z_COLS + LANES
L_BF, L_MF, L_BB, L_MB = 0, 8, 16, 24

FFN_ROWS = 1024
FFN_SPLIT = 2
FFN_SUB = 256
VMEM_LIMIT = 60 * 1024 * 1024


def _rms(x, g):
    ms = jnp.mean(x * x, axis=-1, keepdims=True)
    return x * lax.rsqrt(ms + EPS) * g


def _sigmoid(x):
    return 1.0 / (1.0 + jnp.exp(-x))


def _log_sigmoid(x):
    return jnp.minimum(x, 0.0) - jnp.log1p(jnp.exp(-jnp.abs(x)))


def _scan_lanes(x, op, reverse, fill):
    n = x.shape[1]
    lane = lax.broadcasted_iota(jnp.int32, x.shape, 1)
    k = 1
    while k < n:
        if reverse:
            sh = jnp.where(lane < n - k, pltpu.roll(x, n - k, axis=1), fill)
        else:
            sh = jnp.where(lane >= k, pltpu.roll(x, k, axis=1), fill)
        x = op(x, sh)
        k *= 2
    return x


def _shift_down(x, k):
    return pltpu.roll(x, k, axis=0)


def _shift_up(x, k):
    return pltpu.roll(x, x.shape[0] - k, axis=0)


def _prep_win_kernel(w_ref, sel_ref, o_ref):
    o_ref[0, :, 0:4 * D_GROUP] = w_ref[0, :, 0:4 * D_GROUP].astype(BF16)
    o_ref[0, :, 4 * D_GROUP:Z_COLS] = w_ref[0, :, 4 * D_GROUP + N_GATE:D_IN].astype(BF16)
    g = w_ref[0, :, 4 * D_GROUP:4 * D_GROUP + LANES].astype(BF16)
    o_ref[0, :, C_GATE:W_COLS] = jnp.dot(g, sel_ref[...], preferred_element_type=F32).astype(BF16)


def _prep_up_kernel(w_ref, o_ref):
    half = FFN_SUB // 2
    for m in range(D_FF // half):
        o_ref[0, :, FFN_SUB * m:FFN_SUB * m + half] = w_ref[0, :, half * m:half * (m + 1)].astype(BF16)
        o_ref[0, :, FFN_SUB * m + half:FFN_SUB * (m + 1)] = (
            w_ref[0, :, D_FF + half * m:D_FF + half * (m + 1)].astype(BF16))


def _cast_kernel(w_ref, o_ref):
    o_ref[...] = w_ref[...].astype(BF16)


def _prep_weights(w_in, w_out, ffn_up, ffn_down):
    sel = np.zeros((LANES, LANES), np.float32)
    for j in range(N_GATE):
        sel[j, 8 * (j // HEADS) + j % HEADS] = 1.0
    params = pltpu.CompilerParams(dimension_semantics=("arbitrary", "arbitrary"),
                                  vmem_limit_bytes=VMEM_LIMIT)
    rb = 256
    win_p = pl.pallas_call(
        _prep_win_kernel, out_shape=jax.ShapeDtypeStruct((DEPTH, D_MODEL, W_COLS), BF16),
        grid=(DEPTH, D_MODEL // rb),
        in_specs=[pl.BlockSpec((1, rb, D_IN), lambda l, i: (l, i, 0)),
                  pl.BlockSpec((LANES, LANES), lambda l, i: (0, 0))],
        out_specs=pl.BlockSpec((1, rb, W_COLS), lambda l, i: (l, i, 0)),
        compiler_params=params, name="prep_w_in")(w_in, jnp.asarray(sel, BF16))
    rb = 128
    up_p = pl.pallas_call(
        _prep_up_kernel, out_shape=jax.ShapeDtypeStruct((DEPTH, D_MODEL, 2 * D_FF), BF16),
        grid=(DEPTH, D_MODEL // rb),
        in_specs=[pl.BlockSpec((1, rb, 2 * D_FF), lambda l, i: (l, i, 0))],
        out_specs=pl.BlockSpec((1, rb, 2 * D_FF), lambda l, i: (l, i, 0)),
        compiler_params=params, name="prep_ffn_up")(ffn_up)

    def cast(w, rows, name):
        _, r, c = w.shape
        return pl.pallas_call(
            _cast_kernel, out_shape=jax.ShapeDtypeStruct(w.shape, BF16),
            grid=(DEPTH, r // rows),
            in_specs=[pl.BlockSpec((1, rows, c), lambda l, i: (l, i, 0))],
            out_specs=pl.BlockSpec((1, rows, c), lambda l, i: (l, i, 0)),
            compiler_params=params, name=name)(w)

    return win_p, cast(w_out, 512, "prep_w_out"), up_p, cast(ffn_down, 704, "prep_ffn_down")


def _mod_kernel(c_ref, w_ref, b_ref, o_ref):
    c = c_ref[...]
    s = (c * _sigmoid(c)).astype(BF16)
    o_ref[0] = jnp.dot(s, w_ref[0].astype(BF16), preferred_element_type=F32) + b_ref[0]


def _modulation(conds, ada_w, ada_b):
    n_out = N_MOD * D_MODEL
    bn = 1536
    return pl.pallas_call(
        _mod_kernel,
        out_shape=jax.ShapeDtypeStruct((DEPTH, 8, n_out), F32),
        grid=(DEPTH, n_out // bn),
        in_specs=[
            pl.BlockSpec((8, D_MODEL), lambda l, j: (0, 0)),
            pl.BlockSpec((1, D_MODEL, bn), lambda l, j: (l, 0, j)),
            pl.BlockSpec((1, 1, bn), lambda l, j: (l, 0, j)),
        ],
        out_specs=pl.BlockSpec((1, 8, bn), lambda l, j: (l, 0, j)),
        compiler_params=pltpu.CompilerParams(
            dimension_semantics=("arbitrary", "arbitrary"), vmem_limit_bytes=VMEM_LIMIT),
        name="adaln_mod",
    )(conds, ada_w, ada_b.reshape(DEPTH, 1, n_out))


def _mod_rows(mod_ref, cond):
    row = mod_ref[0, pl.ds(cond, 1), :]
    return [row[:, i * D_MODEL:(i + 1) * D_MODEL] for i in range(N_MOD)]


def _gate_stats(gates):
    gt = gates.T
    b_f = _scan_lanes(_log_sigmoid(gt[8:16]), jnp.add, False, 0.0)
    g_f = gt[0:8] - b_f
    cm_f = _scan_lanes(g_f, jnp.maximum, False, -jnp.inf)
    b_b = _scan_lanes(_log_sigmoid(gt[24:32]), jnp.add, True, 0.0)
    g_b = gt[16:24] - b_b
    cm_b = _scan_lanes(g_b, jnp.maximum, True, -jnp.inf)
    stack = jnp.concatenate([b_f, cm_f, b_b, cm_b, jnp.zeros((LANES - 32, CHUNK), F32)], axis=0)
    return jnp.concatenate([g_f, g_b], axis=0), stack.T


def _pair_cols(mat, lane0, lane_lt_dh):
    return jnp.where(lane_lt_dh, mat[:, lane0:lane0 + 1], mat[:, lane0 + 1:lane0 + 2])


def _mlstm(z_ref, gcol_ref, grow_ref, cols_ref, bd_ref, ycat_ref, nc, state0, m0, need_final):
    lane_row = lax.broadcasted_iota(jnp.int32, (1, LANES), 1)
    lane = lax.broadcasted_iota(jnp.int32, (CHUNK, LANES), 1)
    lt_dh = lane < DH
    row_ll = lax.broadcasted_iota(jnp.int32, (CHUNK, CHUNK), 0)
    col_ll = lax.broadcasted_iota(jnp.int32, (CHUNK, CHUNK), 1)
    tri = (col_ll <= row_ll, col_ll >= row_ll)
    bd_row = lax.broadcasted_iota(jnp.int32, (LANES, 2 * LANES), 0)
    bd_col = lax.broadcasted_iota(jnp.int32, (LANES, 2 * LANES), 1)
    bd_mask = (bd_row < DH) == ((bd_col & (LANES - 1)) < DH)
    dirs = ((0, L_BF, L_MF, CHUNK - 1), (1, L_BB, L_MB, 0))

    m_start = [[None] * nc, [None] * nc]
    m_fin = [None, None]
    for d, _, _, last in dirs:
        m = m0
        order = range(nc) if d == 0 else range(nc - 1, -1, -1)
        for step, c in enumerate(order):
            m_start[d][c] = m
            if need_final or step < nc - 1:
                cl = cols_ref[c, last:last + 1, :]
                m = pltpu.roll(cl, 8, axis=1) + jnp.maximum(m, cl)
        m_fin[d] = m

    have_state = [[False] * nc, [False] * nc]
    fin = [None, None]
    for d, l_b, l_m, last in dirs:
        bd = list(state0[d])
        order = range(nc) if d == 0 else range(nc - 1, -1, -1)
        for step, c in enumerate(order):
            if bd[0] is not None:
                have_state[d][c] = True
                for p in range(PAIRS):
                    bd_ref[d, c, p] = bd[p].astype(BF16)
            if not (need_final or step < nc - 1):
                continue
            rows = slice(HALO + c * CHUNK, HALO + (c + 1) * CHUNK)
            cols = cols_ref[c]
            m_last = jnp.maximum(cols[last:last + 1, :], m_start[d][c])
            a_col = jnp.exp(gcol_ref[c] - cols - pltpu.roll(m_last, LANES - 8, axis=1))
            a_prev = jnp.exp(m_start[d][c] - m_last)
            for p in range(PAIRS):
                a_pair = _pair_cols(a_col, l_b + 2 * p, lt_dh)
                v_pair = z_ref[rows, C_V + LANES * p:C_V + LANES * (p + 1)]
                rhs = jnp.concatenate([v_pair * a_pair, a_pair], axis=1).astype(BF16)
                k_pair = (z_ref[rows, C_K + LANES * p:C_K + LANES * (p + 1)] * (DH ** -0.5)).astype(BF16)
                upd = lax.dot_general(k_pair, rhs, (((0,), (0,)), ((), ())),
                                      preferred_element_type=F32)
                upd = jnp.where(bd_mask, upd, 0.0)
                if bd[p] is None:
                    bd[p] = upd
                else:
                    scale = jnp.where(bd_row < DH, a_prev[:, l_m + 2 * p:l_m + 2 * p + 1],
                                      a_prev[:, l_m + 2 * p + 1:l_m + 2 * p + 2])
                    bd[p] = scale * bd[p] + upd
        fin[d] = bd

    for c in range(nc):
        rows = slice(HALO + c * CHUNK, HALO + (c + 1) * CHUNK)
        cols = cols_ref[c]
        grow = grow_ref[c]
        m_prev = jnp.where(lane_row < L_BB, m_start[0][c], m_start[1][c])
        m_all = jnp.maximum(cols, m_prev)
        w_inter = jnp.exp(m_prev - m_all)
        e_neg = jnp.exp(-(pltpu.roll(cols, 8, axis=1) + m_all))
        for p in range(PAIRS):
            q_pair = z_ref[rows, C_Q + LANES * p:C_Q + LANES * (p + 1)].astype(BF16)
            k_f32 = z_ref[rows, C_K + LANES * p:C_K + LANES * (p + 1)] * (DH ** -0.5)
            v_pair = z_ref[rows, C_V + LANES * p:C_V + LANES * (p + 1)]
            p_cat = ([], [])
            va = []
            for j in range(2):
                h = 2 * p + j
                own = lt_dh if j == 0 else jnp.logical_not(lt_dh)
                k_own = jnp.where(own, k_f32, 0.0).astype(BF16)
                s = lax.dot_general(q_pair, k_own, (((1,), (1,)), ((), ())),
                                    preferred_element_type=F32)
                va.append(jnp.concatenate([jnp.where(own, v_pair, 0.0), jnp.where(own, 1.0, 0.0)],
                                          axis=1).astype(BF16))
                for d, _, l_m, _ in dirs:
                    w = jnp.exp(jnp.where(tri[d], grow[8 * d + h:8 * d + h + 1] - m_all[:, l_m + h:l_m + h + 1],
                                          -jnp.inf))
                    p_cat[d].append((s * w).astype(BF16))
            va = jnp.concatenate(va, axis=0)
            h_sum = None
            for d, _, l_m, _ in dirs:
                tot = jnp.dot(jnp.concatenate(p_cat[d], axis=1), va, preferred_element_type=F32)
                if have_state[d][c]:
                    e_pair = _pair_cols(w_inter, l_m + 2 * p, lt_dh)
                    qc = jnp.dot(q_pair, bd_ref[d, c, p], preferred_element_type=F32)
                    tot = tot + jnp.concatenate([e_pair, e_pair], axis=1) * qc
                floor = _pair_cols(e_neg, l_m + 2 * p, lt_dh)
                h_dir = tot[:, :LANES] / jnp.maximum(jnp.abs(tot[:, LANES:]), floor)
                h_sum = h_dir if h_sum is None else h_sum + h_dir
            o_pair = z_ref[rows, C_O + LANES * p:C_O + LANES * (p + 1)]
            ycat_ref[c * CHUNK:(c + 1) * CHUNK, LANES * p:LANES * (p + 1)] = (
                _sigmoid(o_pair) * h_sum).astype(BF16)

    m_row = jnp.where(lane_row < L_BB, m_fin[0], m_fin[1])
    return fin, m_row


def _mixer_kernel(*refs, seq_len, add_pos, has_state, emit_state, cond_base):
    it = iter(refs)
    x_ref = next(it)
    if add_pos:
        er_ref = next(it)
        ec_ref = next(it)
    mod_ref = next(it)
    ng_ref = next(it)
    win_ref = next(it)
    gb_ref = next(it)
    ws_ref = next(it)
    gbias_ref = next(it)
    pbd_ref = next(it)
    psc_ref = next(it)
    cw_ref = next(it)
    wout_ref = next(it)
    if has_state:
        bd0_ref = next(it)
        m0_ref = next(it)
    x1_ref = next(it)
    h2_ref = next(it)
    if emit_state:
        cout_ref = next(it)
        nout_ref = next(it)
        mout_ref = next(it)
    z_ref = next(it)
    gcol_ref = next(it)
    grow_ref = next(it)
    cols_ref = next(it)
    bd_ref = next(it)
    ycat_ref = next(it)

    nblk = seq_len // CHUNK
    cond = cond_base if cond_base == 0 else cond_base + pl.program_id(0)
    sh1, sc1, gt1, sh2, sc2, _ = _mod_rows(mod_ref, cond)
    ng = ng_ref[0]

    zeros_halo = jnp.zeros((HALO, Z_COLS - C_PZ), F32)
    z_ref[0:HALO, C_PZ:Z_COLS] = zeros_halo
    z_ref[HALO + seq_len:2 * HALO + seq_len, C_PZ:Z_COLS] = zeros_halo

    for r in range(nblk):
        rows = slice(r * CHUNK, (r + 1) * CHUNK)
        xb = x_ref[0, rows, :]
        if add_pos:
            pieces = []
            for g in range(CHUNK // GRID_W):
                gi = r * (CHUNK // GRID_W) + g
                er = jnp.broadcast_to(er_ref[gi:gi + 1, :], (GRID_W, D_MODEL // 2))
                pieces.append(jnp.concatenate([er, ec_ref[...]], axis=1))
            xb = xb + jnp.concatenate(pieces, axis=0)
            x1_ref[0, rows, :] = xb
        hb = (_rms(xb, ng[0:1]) * (1.0 + sc1) + sh1).astype(BF16)
        gates = jnp.dot(hb, win_ref[0, :, C_GATE:W_COLS], preferred_element_type=F32) + gb_ref[0]
        gcol_ref[r] = gates
        z_ref[HALO + r * CHUNK:HALO + (r + 1) * CHUNK, :] = jnp.dot(
            hb, win_ref[0, :, 0:Z_COLS], preferred_element_type=F32)
        grow, cols = _gate_stats(gates)
        grow_ref[r] = grow
        cols_ref[r] = cols

    if has_state:
        state0 = [[bd0_ref[0, 0, d, p] for p in range(PAIRS)] for d in range(2)]
        m0 = m0_ref[0, 0]
    else:
        state0 = [[None] * PAIRS, [None] * PAIRS]
        m0 = jnp.zeros((1, LANES), F32)
    fin, m_row = _mlstm(z_ref, gcol_ref, grow_ref, cols_ref, bd_ref, ycat_ref, nblk, state0, m0,
                        need_final=emit_state)
    if emit_state:
        for d in range(2):
            for p in range(PAIRS):
                for j in range(2):
                    blk = fin[d][p][DH * j:DH * (j + 1), :]
                    cout_ref[0, d * HEADS + 2 * p + j] = blk[:, DH * j:DH * (j + 1)]
                    nout_ref[0, d * HEADS + 2 * p + j] = blk[:, LANES + DH * j:LANES + DH * (j + 1)]
        mout_ref[0] = m_row

    lane_g = lax.broadcasted_iota(jnp.int32, (CHUNK, D_GROUP), 1) // (D_GROUP // 4)
    lg = lax.broadcasted_iota(jnp.int32, (GMLP_CHUNK, D_GROUP), 1) // (D_GROUP // 4)
    row_c = lax.broadcasted_iota(jnp.int32, (CHUNK, D_GROUP), 0)
    half = jnp.where(lane_g == 0, 1, jnp.where(lane_g == 1, 2, jnp.where(lane_g == 2, 4, 8)))
    cw = cw_ref[0]
    for r in range(nblk):
        rows = slice(r * CHUNK, (r + 1) * CHUNK)
        zrows = slice(HALO + r * CHUNK, HALO + (r + 1) * CHUNK)
        hrows = slice(r * CHUNK, (r + 1) * CHUNK + 2 * HALO)
        inner = slice(HALO, HALO + CHUNK)

        for s in range(CHUNK // GMLP_CHUNK):
            zs = slice(HALO + r * CHUNK + s * GMLP_CHUNK, HALO + r * CHUNK + (s + 1) * GMLP_CHUNK)
            vch = z_ref[zs, C_GV:C_GV + D_GROUP].astype(BF16)
            mixed = gbias_ref[0]
            for g in range(4):
                mg = jnp.dot(ws_ref[0, g].astype(BF16), vch, preferred_element_type=F32)
                mixed = mixed + jnp.where(lg == g, mg, 0.0)
            yb = z_ref[zs, C_GU:C_GU + D_GROUP] * mixed
            ys = slice(r * CHUNK + s * GMLP_CHUNK, r * CHUNK + (s + 1) * GMLP_CHUNK)
            ycat_ref[ys, D_GROUP:2 * D_GROUP] = yb.astype(BF16)

        pz = z_ref[hrows, C_PZ:C_PZ + D_GROUP]
        a1 = pz
        a2 = a1 + _shift_down(a1, 1)
        a4 = a2 + _shift_down(a2, 2)
        a8 = a4 + _shift_down(a4, 4)
        b1 = pz
        b2 = b1 + _shift_up(b1, 1)
        b4 = b2 + _shift_up(b2, 2)
        b8 = b4 + _shift_up(b4, 4)
        win = [(_shift_down(a, 1) + b)[inner] for a, b in ((a1, b1), (a2, b2), (a4, b4), (a8, b8))]
        wsum = jnp.where(lane_g == 0, win[0], jnp.where(lane_g == 1, win[1],
                         jnp.where(lane_g == 2, win[2], win[3])))
        t_abs = row_c + r * CHUNK
        cnt = (jnp.minimum(t_abs + half, seq_len) - jnp.maximum(t_abs - half, 0)).astype(F32)
        pooled = wsum / cnt - pz[inner]
        yc = jnp.dot(pooled.astype(BF16), pbd_ref[0], preferred_element_type=F32) * psc_ref[0]
        ycat_ref[rows, 2 * D_GROUP:3 * D_GROUP] = yc.astype(BF16)

        u = z_ref[hrows, C_CC:C_CC + D_GROUP] * z_ref[hrows, C_CX:C_CX + D_GROUP]
        conv = (cw[0:1] * _shift_down(u, 1)[inner] + cw[1:2] * u[inner]
                + cw[2:3] * _shift_up(u, 1)[inner])
        yd = z_ref[zrows, C_CB:C_CB + D_GROUP] * conv
        ycat_ref[rows, 3 * D_GROUP:4 * D_GROUP] = yd.astype(BF16)

    for r in range(nblk):
        rows = slice(r * CHUNK, (r + 1) * CHUNK)
        y = jnp.dot(ycat_ref[rows, :], wout_ref[0], preferred_element_type=F32)
        xb = x1_ref[0, rows, :] if add_pos else x_ref[0, rows, :]
        x1 = xb + gt1 * _rms(y, ng[1:2])
        x1_ref[0, rows, :] = x1
        h2_ref[0, rows, :] = (_rms(x1, ng[2:3]) * (1.0 + sc2) + sh2).astype(BF16)


def _layer_spec(shape, layer):
    nd = len(shape)
    return pl.BlockSpec((1,) + tuple(shape[1:]), lambda b: (layer,) + (0,) * (nd - 1),
                        pipeline_mode=pl.Buffered(1))


def _mixer_call(x, layer, mod_all, wts, cond_base, pos=None, state=None, emit_state=False):
    bsz, seq_len, _ = x.shape
    nblk = seq_len // CHUNK
    add_pos = pos is not None
    has_state = state is not None
    args = [x]
    in_specs = [pl.BlockSpec((1, seq_len, D_MODEL), lambda b: (b, 0, 0))]
    if add_pos:
        args += list(pos)
        in_specs += [pl.BlockSpec(p.shape, lambda b: (0, 0), pipeline_mode=pl.Buffered(1)) for p in pos]
    args.append(mod_all)
    in_specs.append(_layer_spec(mod_all.shape, layer))
    for name in ("norm_g", "w_in", "gate_b", "gmlp_ws", "gmlp_bias", "pool_bd", "pool_scale",
                 "conv_w", "w_out"):
        args.append(wts[name])
        in_specs.append(_layer_spec(wts[name].shape, layer))
    if has_state:
        args += list(state)
        in_specs += [pl.BlockSpec((1, 1, 2, PAIRS, LANES, 2 * LANES), lambda b: (b, layer, 0, 0, 0, 0)),
                     pl.BlockSpec((1, 1, 1, LANES), lambda b: (b, layer, 0, 0))]
    out_shape = [jax.ShapeDtypeStruct((bsz, seq_len, D_MODEL), F32),
                 jax.ShapeDtypeStruct((bsz, seq_len, D_MODEL), BF16)]
    out_specs = [pl.BlockSpec((1, seq_len, D_MODEL), lambda b: (b, 0, 0)),
                 pl.BlockSpec((1, seq_len, D_MODEL), lambda b: (b, 0, 0))]
    if emit_state:
        out_shape += [jax.ShapeDtypeStruct((bsz, 2 * HEADS, DH, DH), F32),
                      jax.ShapeDtypeStruct((bsz, 2 * HEADS, DH, DH), F32),
                      jax.ShapeDtypeStruct((bsz, 1, LANES), F32)]
        out_specs += [pl.BlockSpec((1, 2 * HEADS, DH, DH), lambda b: (b, 0, 0, 0)),
                      pl.BlockSpec((1, 2 * HEADS, DH, DH), lambda b: (b, 0, 0, 0)),
                      pl.BlockSpec((1, 1, LANES), lambda b: (b, 0, 0))]
    kern = functools.partial(_mixer_kernel, seq_len=seq_len, add_pos=add_pos,
                             has_state=has_state, emit_state=emit_state, cond_base=cond_base)
    return pl.pallas_call(
        kern,
        out_shape=out_shape,
        grid=(bsz,),
        in_specs=in_specs,
        out_specs=out_specs,
        scratch_shapes=[pltpu.VMEM((seq_len + 2 * HALO, Z_COLS), F32),
                        pltpu.VMEM((nblk, CHUNK, LANES), F32),
                        pltpu.VMEM((nblk, 16, CHUNK), F32),
                        pltpu.VMEM((nblk, CHUNK, LANES), F32),
                        pltpu.VMEM((2, nblk, PAIRS, LANES, 2 * LANES), BF16),
                        pltpu.VMEM((seq_len, D_MODEL), BF16)],
        compiler_params=pltpu.CompilerParams(
            dimension_semantics=("arbitrary",), vmem_limit_bytes=VMEM_LIMIT),
        name="mixer_t%d" % seq_len,
    )(*args)


def _ffn_kernel(x1_ref, h2_ref, mod_ref, ng_ref, up_ref, cwv_ref, cwg_ref, down_ref, o_ref,
                acc_ref, act_ref, *, seq_len, cond_base):
    j = pl.program_id(1)
    rows = x1_ref.shape[0]
    half = FFN_SUB // 2
    nsub = up_ref.shape[2] // FFN_SUB

    @pl.when(j == 0)
    def _():
        acc_ref[...] = jnp.zeros_like(acc_ref)

    h2 = h2_ref[...]
    t_in_seq = lax.broadcasted_iota(jnp.int32, (rows, FFN_SUB), 0) & (seq_len - 1)
    first = t_in_seq == 0
    last = t_in_seq == seq_len - 1
    for s in range(nsub):
        a = jnp.dot(h2, up_ref[0, :, s * FFN_SUB:(s + 1) * FFN_SUB], preferred_element_type=F32)
        cw = jnp.concatenate([cwv_ref[0, :, s * half:(s + 1) * half],
                              cwg_ref[0, :, s * half:(s + 1) * half]], axis=1)
        a_prev = jnp.where(first, 0.0, pltpu.roll(a, 1, axis=0))
        a_next = jnp.where(last, 0.0, pltpu.roll(a, rows - 1, axis=0))
        ac = cw[0:1] * a_prev + cw[1:2] * a + cw[2:3] * a_next
        val = ac[:, :half]
        gate = ac[:, half:]
        act = gate * _sigmoid(gate) * val
        act_ref[:, s * half:(s + 1) * half] = act.astype(BF16)
    acc_ref[...] += jnp.dot(act_ref[...], down_ref[0], preferred_element_type=F32)

    @pl.when(j == pl.num_programs(1) - 1)
    def _():
        cond = cond_base if cond_base == 0 else cond_base + pl.program_id(0)
        gt2 = _mod_rows(mod_ref, cond)[5]
        o_ref[...] = x1_ref[...] + gt2 * _rms(acc_ref[...], ng_ref[0][3:4])


def _ffn_call(x1, h2, layer, mod_all, wts, seq_len, cond_base):
    n = x1.shape[0]
    up_cols = 2 * D_FF // FFN_SPLIT
    dn_rows = D_FF // FFN_SPLIT
    return pl.pallas_call(
        functools.partial(_ffn_kernel, seq_len=seq_len, cond_base=cond_base),
        out_shape=jax.ShapeDtypeStruct((n, D_MODEL), F32),
        grid=(n // FFN_ROWS, FFN_SPLIT),
        in_specs=[
            pl.BlockSpec((FFN_ROWS, D_MODEL), lambda i, j: (i, 0)),
            pl.BlockSpec((FFN_ROWS, D_MODEL), lambda i, j: (i, 0)),
            pl.BlockSpec((1, 8, N_MOD * D_MODEL), lambda i, j: (layer, 0, 0)),
            pl.BlockSpec((1, 4, D_MODEL), lambda i, j: (layer, 0, 0)),
            pl.BlockSpec((1, D_MODEL, up_cols), lambda i, j: (layer, 0, j)),
            pl.BlockSpec((1, 3, dn_rows), lambda i, j: (layer, 0, j)),
            pl.BlockSpec((1, 3, dn_rows), lambda i, j: (layer, 0, FFN_SPLIT + j)),
            pl.BlockSpec((1, dn_rows, D_MODEL), lambda i, j: (layer, j, 0)),
        ],
        out_specs=pl.BlockSpec((FFN_ROWS, D_MODEL), lambda i, j: (i, 0)),
        scratch_shapes=[pltpu.VMEM((FFN_ROWS, D_MODEL), F32),
                        pltpu.VMEM((FFN_ROWS, dn_rows), BF16)],
        compiler_params=pltpu.CompilerParams(
            dimension_semantics=("arbitrary", "arbitrary"), vmem_limit_bytes=VMEM_LIMIT),
        name="ffn_t%d" % seq_len,
    )(x1, h2, mod_all, wts["norm_g"], wts["ffn_up"], wts["ffn_conv"], wts["ffn_conv"], wts["ffn_down"])


def _pos_tables(t_len):
    quarter = D_MODEL // 4
    omega = (1.0 / (np.float32(POS_BASE) ** (np.arange(quarter, dtype=np.float32) / np.float32(quarter))))
    omega = omega.astype(np.float32)

    def emb(p):
        a = p.astype(np.float32)[:, None] * omega[None, :]
        return np.concatenate([np.sin(a), np.cos(a)], axis=-1).astype(np.float32)

    return jnp.asarray(emb(np.arange(t_len // GRID_W))), jnp.asarray(emb(np.arange(GRID_W)))


def _pair_states(state_c, state_n, state_m):
    bs = state_c.shape[0]
    eye = jnp.eye(2, dtype=F32)
    c = state_c.astype(F32).reshape(bs, DEPTH, 2, PAIRS, 2, DH, DH)
    n = state_n.astype(F32).reshape(bs, DEPTH, 2, PAIRS, 2, DH)
    c_bd = jnp.einsum("...jde,jk->...jdke", c, eye).reshape(bs, DEPTH, 2, PAIRS, LANES, LANES)
    n_bd = jnp.einsum("...jd,jk,e->...jdke", n, eye, jnp.ones((DH,), F32)).reshape(
        bs, DEPTH, 2, PAIRS, LANES, LANES)
    m = jnp.pad(state_m.astype(F32), ((0, 0), (0, 0), (0, 0), (0, 16 - HEADS)))
    m = jnp.pad(m.reshape(bs, DEPTH, 1, 32), ((0, 0), (0, 0), (0, 0), (L_MF, LANES - 32 - L_MF)))
    return jnp.concatenate([c_bd, n_bd], axis=-1), m


def kernel(x_prompt, x_sample, state_C, state_n, state_m, c, c_ctx, norm_g, ada_w, ada_b, w_in,
           gate_b, gmlp_ws, gmlp_b, pool_w, pool_scale, conv_w, w_out, ffn_up, ffn_conv, ffn_down):
    bp, tp, _ = x_prompt.shape
    bs, ts, _ = x_sample.shape
    assert tp == CHUNK and ts % CHUNK == 0 and FFN_ROWS % tp == 0 and ts == FFN_ROWS

    conds = jnp.concatenate([c_ctx[None, :], c, jnp.zeros((7 - bs, D_MODEL), F32)], axis=0)
    mod_all = _modulation(conds, ada_w, ada_b)
    win_p, wout_p, up_p, down_p = _prep_weights(w_in, w_out, ffn_up, ffn_down)

    gb = jnp.pad(gate_b, ((0, 0), (0, 0), (0, 8 - HEADS))).reshape(DEPTH, 1, 32)
    dg = D_GROUP // 4
    wts = {
        "norm_g": norm_g,
        "w_in": win_p,
        "gate_b": jnp.pad(gb, ((0, 0), (0, 0), (0, LANES - 32))),
        "gmlp_ws": gmlp_ws,
        "gmlp_bias": jnp.repeat(jnp.swapaxes(gmlp_b, 1, 2), dg, axis=2),
        "pool_bd": jnp.einsum("lgcd,gh->lgchd", pool_w, jnp.eye(4, dtype=F32)).reshape(
            DEPTH, D_GROUP, D_GROUP).astype(BF16),
        "pool_scale": pool_scale.reshape(DEPTH, 1, D_GROUP),
        "conv_w": conv_w,
        "w_out": wout_p,
        "ffn_up": up_p,
        "ffn_conv": ffn_conv,
        "ffn_down": down_p,
    }
    pos = _pos_tables(ts)
    state = _pair_states(state_C, state_n, state_m)

    xp, xs = x_prompt, x_sample
    cs, ns, ms = [], [], []
    for l in range(DEPTH):
        x1, h2, c_fin, n_fin, m_fin = _mixer_call(xp, l, mod_all, wts, 0, emit_state=True)
        xp = _ffn_call(x1.reshape(bp * tp, D_MODEL), h2.reshape(bp * tp, D_MODEL), l, mod_all, wts,
                       tp, 0).reshape(bp, tp, D_MODEL)
        cs.append(c_fin.reshape(bp, 2, HEADS, DH, DH))
        ns.append(n_fin[..., 0].reshape(bp, 2, HEADS, DH))
        ms.append(jnp.stack([m_fin[:, 0, L_MF:L_MF + HEADS], m_fin[:, 0, L_MB:L_MB + HEADS]], axis=1))

        x1, h2 = _mixer_call(xs, l, mod_all, wts, 1, pos=pos if l == 0 else None, state=state)
        xs = _ffn_call(x1.reshape(bs * ts, D_MODEL), h2.reshape(bs * ts, D_MODEL), l, mod_all, wts,
                       ts, 1).reshape(bs, ts, D_MODEL)

    return (xp, xs, jnp.stack(cs, axis=1), jnp.stack(ns, axis=1), jnp.stack(ms, axis=1))
```

```python
import functools

import numpy as np
import jax
import jax.numpy as jnp
from jax import lax
from jax.experimental import pallas as pl
from jax.experimental.pallas import tpu as pltpu

F32 = jnp.float32
BF16 = jnp.bfloat16

D_MODEL = 1024
DEPTH = 2
HEADS = 4
PAIRS = HEADS // 2
DH = 64
D_GROUP = 256
D_FF = 2816
N_MOD = 6
EPS = 1e-6
GRID_W = 64
POS_BASE = 10000.0
D_IN = 2576
N_GATE = 4 * HEADS

CHUNK = 256
GMLP_CHUNK = 128
HALO = 8
LANES = 128

C_Q, C_K, C_V, C_O = 0, 256, 512, 768
C_GU, C_GV, C_PZ, C_CB, C_CC, C_CX = 1024, 1280, 1536, 1792, 2048, 2304
Z_COLS = 2560
C_GATE = Z_COLS
W_COLS = Z_COLS + LANES
L_BF, L_MF, L_BB, L_MB = 0, 8, 16, 24

FFN_ROWS = 1024
FFN_SUB = 256
VMEM_LIMIT = 60 * 1024 * 1024


def _rms(x, g):
    ms = jnp.mean(x * x, axis=-1, keepdims=True)
    return x * lax.rsqrt(ms + EPS) * g


def _sigmoid(x):
    return 1.0 / (1.0 + jnp.exp(-x))


def _log_sigmoid(x):
    return jnp.minimum(x, 0.0) - jnp.log1p(jnp.exp(-jnp.abs(x)))


def _scan_lanes(x, op, reverse, fill):
    n = x.shape[1]
    lane = lax.broadcasted_iota(jnp.int32, x.shape, 1)
    k = 1
    while k < n:
        if reverse:
            sh = jnp.where(lane < n - k, pltpu.roll(x, n - k, axis=1), fill)
        else:
            sh = jnp.where(lane >= k, pltpu.roll(x, k, axis=1), fill)
        x = op(x, sh)
        k *= 2
    return x


def _shift_down(x, k):
    return pltpu.roll(x, k, axis=0)


def _shift_up(x, k):
    return pltpu.roll(x, x.shape[0] - k, axis=0)


def _prep_win_kernel(wt_ref, sel_ref, o_ref):
    for n in range(Z_COLS // LANES):
        r0 = LANES * n if n < 4 * D_GROUP // LANES else LANES * n + N_GATE
        o_ref[0, :, LANES * n:LANES * (n + 1)] = wt_ref[0, r0:r0 + LANES, :].T.astype(BF16)
    g = wt_ref[0, 4 * D_GROUP:4 * D_GROUP + LANES, :].T.astype(BF16)
    o_ref[0, :, C_GATE:W_COLS] = jnp.dot(g, sel_ref[...], preferred_element_type=F32).astype(BF16)


def _prep_up_kernel(w_ref, o_ref):
    half = FFN_SUB // 2
    for m in range(D_FF // half):
        o_ref[0, :, FFN_SUB * m:FFN_SUB * m + half] = w_ref[0, :, half * m:half * (m + 1)].astype(BF16)
        o_ref[0, :, FFN_SUB * m + half:FFN_SUB * (m + 1)] = (
            w_ref[0, :, D_FF + half * m:D_FF + half * (m + 1)].astype(BF16))


def _cast_kernel(w_ref, o_ref):
    o_ref[...] = w_ref[...].astype(BF16)


def _prep_weights(w_in, w_out, ffn_up, ffn_down):
    sel = np.zeros((LANES, LANES), np.float32)
    for j in range(N_GATE):
        sel[j, 8 * (j // HEADS) + j % HEADS] = 1.0
    params = pltpu.CompilerParams(dimension_semantics=("arbitrary", "arbitrary"),
                                  vmem_limit_bytes=VMEM_LIMIT)
    win_p = pl.pallas_call(
        _prep_win_kernel, out_shape=jax.ShapeDtypeStruct((DEPTH, D_MODEL, W_COLS), BF16),
        grid=(DEPTH, 1),
        in_specs=[pl.BlockSpec((1, D_IN, D_MODEL), lambda l, i: (l, 0, 0)),
                  pl.BlockSpec((LANES, LANES), lambda l, i: (0, 0))],
        out_specs=pl.BlockSpec((1, D_MODEL, W_COLS), lambda l, i: (l, 0, 0)),
        compiler_params=params, name="prep_w_in")(jnp.swapaxes(w_in, 1, 2), jnp.asarray(sel, BF16))
    rb = 128
    up_p = pl.pallas_call(
        _prep_up_kernel, out_shape=jax.ShapeDtypeStruct((DEPTH, D_MODEL, 2 * D_FF), BF16),
        grid=(DEPTH, D_MODEL // rb),
        in_specs=[pl.BlockSpec((1, rb, 2 * D_FF), lambda l, i: (l, i, 0))],
        out_specs=pl.BlockSpec((1, rb, 2 * D_FF), lambda l, i: (l, i, 0)),
        compiler_params=params, name="prep_ffn_up")(ffn_up)

    def cast(w, rows, name):
        _, r, c = w.shape
        return pl.pallas_call(
            _cast_kernel, out_shape=jax.ShapeDtypeStruct(w.shape, BF16),
            grid=(DEPTH, r // rows),
            in_specs=[pl.BlockSpec((1, rows, c), lambda l, i: (l, i, 0))],
            out_specs=pl.BlockSpec((1, rows, c), lambda l, i: (l, i, 0)),
            compiler_params=params, name=name)(w)

    return win_p, cast(w_out, 512, "prep_w_out"), up_p, cast(ffn_down, 704, "prep_ffn_down")


def _mod_kernel(c_ref, w_ref, b_ref, o_ref):
    c = c_ref[...]
    s = (c * _sigmoid(c)).astype(BF16)
    o_ref[0] = jnp.dot(s, w_ref[0].astype(BF16), preferred_element_type=F32) + b_ref[0]


def _modulation(conds, ada_w, ada_b):
    n_out = N_MOD * D_MODEL
    bn = 1536
    return pl.pallas_call(
        _mod_kernel,
        out_shape=jax.ShapeDtypeStruct((DEPTH, 8, n_out), F32),
        grid=(DEPTH, n_out // bn),
        in_specs=[
            pl.BlockSpec((8, D_MODEL), lambda l, j: (0, 0)),
            pl.BlockSpec((1, D_MODEL, bn), lambda l, j: (l, 0, j)),
            pl.BlockSpec((1, 1, bn), lambda l, j: (l, 0, j)),
        ],
        out_specs=pl.BlockSpec((1, 8, bn), lambda l, j: (l, 0, j)),
        compiler_params=pltpu.CompilerParams(
            dimension_semantics=("arbitrary", "arbitrary"), vmem_limit_bytes=VMEM_LIMIT),
        name="adaln_mod",
    )(conds, ada_w, ada_b.reshape(DEPTH, 1, n_out))


def _mod_rows(mod_ref, cond):
    row = mod_ref[0, pl.ds(cond, 1), :]
    return [row[:, i * D_MODEL:(i + 1) * D_MODEL] for i in range(N_MOD)]


def _gate_stats(gates):
    gt = gates.T
    b_f = _scan_lanes(_log_sigmoid(gt[8:16]), jnp.add, False, 0.0)
    g_f = gt[0:8] - b_f
    cm_f = _scan_lanes(g_f, jnp.maximum, False, -jnp.inf)
    b_b = _scan_lanes(_log_sigmoid(gt[24:32]), jnp.add, True, 0.0)
    g_b = gt[16:24] - b_b
    cm_b = _scan_lanes(g_b, jnp.maximum, True, -jnp.inf)
    stack = jnp.concatenate([b_f, cm_f, b_b, cm_b, jnp.zeros((LANES - 32, CHUNK), F32)], axis=0)
    return jnp.concatenate([g_f, g_b], axis=0), stack.T


def _pair_cols(mat, lane0, lane_lt_dh):
    return jnp.where(lane_lt_dh, mat[:, lane0:lane0 + 1], mat[:, lane0 + 1:lane0 + 2])


def _mlstm(z_ref, gcol_ref, grow_ref, cols_ref, bd_ref, ycat_ref, nc, state0, m0, need_final):
    lane_row = lax.broadcasted_iota(jnp.int32, (1, LANES), 1)
    lane = lax.broadcasted_iota(jnp.int32, (CHUNK, LANES), 1)
    lt_dh = lane < DH
    row_ll = lax.broadcasted_iota(jnp.int32, (CHUNK, CHUNK), 0)
    col_ll = lax.broadcasted_iota(jnp.int32, (CHUNK, CHUNK), 1)
    tri = (col_ll <= row_ll, col_ll >= row_ll)
    bd_row = lax.broadcasted_iota(jnp.int32, (LANES, 2 * LANES), 0)
    bd_col = lax.broadcasted_iota(jnp.int32, (LANES, 2 * LANES), 1)
    bd_mask = (bd_row < DH) == ((bd_col & (LANES - 1)) < DH)
    dirs = ((0, L_BF, L_MF, CHUNK - 1), (1, L_BB, L_MB, 0))

    m_start = [[None] * nc, [None] * nc]
    m_fin = [None, None]
    for d, _, _, last in dirs:
        m = m0
        order = range(nc) if d == 0 else range(nc - 1, -1, -1)
        for step, c in enumerate(order):
            m_start[d][c] = m
            if need_final or step < nc - 1:
                cl = cols_ref[c, last:last + 1, :]
                m = pltpu.roll(cl, 8, axis=1) + jnp.maximum(m, cl)
        m_fin[d] = m

    have_state = [[False] * nc, [False] * nc]
    fin = [None, None]
    for d, l_b, l_m, last in dirs:
        bd = list(state0[d])
        order = range(nc) if d == 0 else range(nc - 1, -1, -1)
        for step, c in enumerate(order):
            if bd[0] is not None:
                have_state[d][c] = True
                for p in range(PAIRS):
                    bd_ref[d, c, p] = bd[p].astype(BF16)
            if not (need_final or step < nc - 1):
                continue
            rows = slice(HALO + c * CHUNK, HALO + (c + 1) * CHUNK)
            cols = cols_ref[c]
            m_last = jnp.maximum(cols[last:last + 1, :], m_start[d][c])
            a_col = jnp.exp(gcol_ref[c] - cols - pltpu.roll(m_last, LANES - 8, axis=1))
            a_prev = jnp.exp(m_start[d][c] - m_last)
            for p in range(PAIRS):
                a_pair = _pair_cols(a_col, l_b + 2 * p, lt_dh)
                v_pair = z_ref[rows, C_V + LANES * p:C_V + LANES * (p + 1)]
                rhs = jnp.concatenate([v_pair * a_pair, a_pair], axis=1).astype(BF16)
                k_pair = (z_ref[rows, C_K + LANES * p:C_K + LANES * (p + 1)] * (DH ** -0.5)).astype(BF16)
                upd = lax.dot_general(k_pair, rhs, (((0,), (0,)), ((), ())),
                                      preferred_element_type=F32)
                upd = jnp.where(bd_mask, upd, 0.0)
                if bd[p] is None:
                    bd[p] = upd
                else:
                    scale = jnp.where(bd_row < DH, a_prev[:, l_m + 2 * p:l_m + 2 * p + 1],
                                      a_prev[:, l_m + 2 * p + 1:l_m + 2 * p + 2])
                    bd[p] = scale * bd[p] + upd
        fin[d] = bd

    for c in range(nc):
        rows = slice(HALO + c * CHUNK, HALO + (c + 1) * CHUNK)
        cols = cols_ref[c]
        grow = grow_ref[c]
        m_prev = jnp.where(lane_row < L_BB, m_start[0][c], m_start[1][c])
        m_all = jnp.maximum(cols, m_prev)
        w_inter = jnp.exp(m_prev - m_all)
        e_neg = jnp.exp(-(pltpu.roll(cols, 8, axis=1) + m_all))
        for p in range(PAIRS):
            q_pair = z_ref[rows, C_Q + LANES * p:C_Q + LANES * (p + 1)].astype(BF16)
            k_f32 = z_ref[rows, C_K + LANES * p:C_K + LANES * (p + 1)] * (DH ** -0.5)
            v_pair = z_ref[rows, C_V + LANES * p:C_V + LANES * (p + 1)]
            p_cat = ([], [])
            va = []
            for j in range(2):
                h = 2 * p + j
                own = lt_dh if j == 0 else jnp.logical_not(lt_dh)
                k_own = jnp.where(own, k_f32, 0.0).astype(BF16)
                s = lax.dot_general(q_pair, k_own, (((1,), (1,)), ((), ())),
                                    preferred_element_type=F32)
                va.append(jnp.concatenate([jnp.where(own, v_pair, 0.0), jnp.where(own, 1.0, 0.0)],
                                          axis=1).astype(BF16))
                for d, _, l_m, _ in dirs:
                    w = jnp.exp(jnp.where(tri[d], grow[8 * d + h:8 * d + h + 1] - m_all[:, l_m + h:l_m + h + 1],
                                          -jnp.inf))
                    p_cat[d].append((s * w).astype(BF16))
            va = jnp.concatenate(va, axis=0)
            h_sum = None
            for d, _, l_m, _ in dirs:
                tot = jnp.dot(jnp.concatenate(p_cat[d], axis=1), va, preferred_element_type=F32)
                if have_state[d][c]:
                    e_pair = _pair_cols(w_inter, l_m + 2 * p, lt_dh)
                    qc = jnp.dot(q_pair, bd_ref[d, c, p], preferred_element_type=F32)
                    tot = tot + jnp.concatenate([e_pair, e_pair], axis=1) * qc
                floor = _pair_cols(e_neg, l_m + 2 * p, lt_dh)
                h_dir = tot[:, :LANES] / jnp.maximum(jnp.abs(tot[:, LANES:]), floor)
                h_sum = h_dir if h_sum is None else h_sum + h_dir
            o_pair = z_ref[rows, C_O + LANES * p:C_O + LANES * (p + 1)]
            ycat_ref[c * CHUNK:(c + 1) * CHUNK, LANES * p:LANES * (p + 1)] = (
                _sigmoid(o_pair) * h_sum).astype(BF16)

    m_row = jnp.where(lane_row < L_BB, m_fin[0], m_fin[1])
    return fin, m_row


def _mixer_kernel(*refs, seq_len, add_pos, has_state, emit_state, cond_base, n_aliased):
    it = iter(refs)
    x_ref = next(it)
    if add_pos:
        er_ref = next(it)
        ec_ref = next(it)
    mod_ref = next(it)
    ng_ref = next(it)
    win_ref = next(it)
    gb_ref = next(it)
    ws_ref = next(it)
    gbias_ref = next(it)
    pbd_ref = next(it)
    psc_ref = next(it)
    cw_ref = next(it)
    wout_ref = next(it)
    if has_state:
        bd0_ref = next(it)
        m0_ref = next(it)
    for _ in range(n_aliased):
        next(it)
    x1_ref = next(it)
    h2_ref = next(it)
    if emit_state:
        cout_ref = next(it)
        nout_ref = next(it)
        mout_ref = next(it)
    z_ref = next(it)
    gcol_ref = next(it)
    grow_ref = next(it)
    cols_ref = next(it)
    bd_ref = next(it)
    ycat_ref = next(it)

    nblk = seq_len // CHUNK
    cond = cond_base if cond_base == 0 else cond_base + pl.program_id(0)
    sh1, sc1, gt1, sh2, sc2, _ = _mod_rows(mod_ref, cond)
    ng = ng_ref[0]

    zeros_halo = jnp.zeros((HALO, Z_COLS - C_PZ), F32)
    z_ref[0:HALO, C_PZ:Z_COLS] = zeros_halo
    z_ref[HALO + seq_len:2 * HALO + seq_len, C_PZ:Z_COLS] = zeros_halo

    for r in range(nblk):
        rows = slice(r * CHUNK, (r + 1) * CHUNK)
        xb = x_ref[0, rows, :]
        if add_pos:
            pieces = []
            for g in range(CHUNK // GRID_W):
                gi = r * (CHUNK // GRID_W) + g
                er = jnp.broadcast_to(er_ref[gi:gi + 1, :], (GRID_W, D_MODEL // 2))
                pieces.append(jnp.concatenate([er, ec_ref[...]], axis=1))
            xb = xb + jnp.concatenate(pieces, axis=0)
            x1_ref[0, rows, :] = xb
        hb = (_rms(xb, ng[0:1]) * (1.0 + sc1) + sh1).astype(BF16)
        gates = jnp.dot(hb, win_ref[0, :, C_GATE:W_COLS], preferred_element_type=F32) + gb_ref[0]
        gcol_ref[r] = gates
        z_ref[HALO + r * CHUNK:HALO + (r + 1) * CHUNK, :] = jnp.dot(
            hb, win_ref[0, :, 0:Z_COLS], preferred_element_type=F32)
        grow, cols = _gate_stats(gates)
        grow_ref[r] = grow
        cols_ref[r] = cols

    if has_state:
        state0 = [[bd0_ref[0, 0, d, p] for p in range(PAIRS)] for d in range(2)]
        m0 = m0_ref[0, 0]
    else:
        state0 = [[None] * PAIRS, [None] * PAIRS]
        m0 = jnp.zeros((1, LANES), F32)
    fin, m_row = _mlstm(z_ref, gcol_ref, grow_ref, cols_ref, bd_ref, ycat_ref, nblk, state0, m0,
                        need_final=emit_state)
    if emit_state:
        for d in range(2):
            for p in range(PAIRS):
                for j in range(2):
                    cout_ref[0, 0, d, 2 * p + j] = fin[d][p][DH * j:DH * (j + 1), DH * j:DH * (j + 1)]
                n_t = fin[d][p][:, LANES:].T
                nout_ref[0, 0, d:d + 1, LANES * p:LANES * (p + 1)] = n_t[0:1] + n_t[DH:DH + 1]
        mout_ref[0] = m_row

    lane_g = lax.broadcasted_iota(jnp.int32, (CHUNK, D_GROUP), 1) // (D_GROUP // 4)
    lg = lax.broadcasted_iota(jnp.int32, (GMLP_CHUNK, D_GROUP), 1) // (D_GROUP // 4)
    row_c = lax.broadcasted_iota(jnp.int32, (CHUNK, D_GROUP), 0)
    half = jnp.where(lane_g == 0, 1, jnp.where(lane_g == 1, 2, jnp.where(lane_g == 2, 4, 8)))
    cw = cw_ref[0]
    for r in range(nblk):
        rows = slice(r * CHUNK, (r + 1) * CHUNK)
        zrows = slice(HALO + r * CHUNK, HALO + (r + 1) * CHUNK)
        hrows = slice(r * CHUNK, (r + 1) * CHUNK + 2 * HALO)
        inner = slice(HALO, HALO + CHUNK)

        for s in range(CHUNK // GMLP_CHUNK):
            zs = slice(HALO + r * CHUNK + s * GMLP_CHUNK, HALO + r * CHUNK + (s + 1) * GMLP_CHUNK)
            vch = z_ref[zs, C_GV:C_GV + D_GROUP].astype(BF16)
            mixed = gbias_ref[0]
            for g in range(4):
                mg = jnp.dot(ws_ref[0, g].astype(BF16), vch, preferred_element_type=F32)
                mixed = mixed + jnp.where(lg == g, mg, 0.0)
            yb = z_ref[zs, C_GU:C_GU + D_GROUP] * mixed
            ys = slice(r * CHUNK + s * GMLP_CHUNK, r * CHUNK + (s + 1) * GMLP_CHUNK)
            ycat_ref[ys, D_GROUP:2 * D_GROUP] = yb.astype(BF16)

        pz = z_ref[hrows, C_PZ:C_PZ + D_GROUP]
        a1 = pz
        a2 = a1 + _shift_down(a1, 1)
        a4 = a2 + _shift_down(a2, 2)
        a8 = a4 + _shift_down(a4, 4)
        b1 = pz
        b2 = b1 + _shift_up(b1, 1)
        b4 = b2 + _shift_up(b2, 2)
        b8 = b4 + _shift_up(b4, 4)
        win = [(_shift_down(a, 1) + b)[inner] for a, b in ((a1, b1), (a2, b2), (a4, b4), (a8, b8))]
        wsum = jnp.where(lane_g == 0, win[0], jnp.where(lane_g == 1, win[1],
                         jnp.where(lane_g == 2, win[2], win[3])))
        t_abs = row_c + r * CHUNK
        cnt = (jnp.minimum(t_abs + half, seq_len) - jnp.maximum(t_abs - half, 0)).astype(F32)
        pooled = wsum / cnt - pz[inner]
        yc = jnp.dot(pooled.astype(BF16), pbd_ref[0], preferred_element_type=F32) * psc_ref[0]
        ycat_ref[rows, 2 * D_GROUP:3 * D_GROUP] = yc.astype(BF16)

        u = z_ref[hrows, C_CC:C_CC + D_GROUP] * z_ref[hrows, C_CX:C_CX + D_GROUP]
        conv = (cw[0:1] * _shift_down(u, 1)[inner] + cw[1:2] * u[inner]
                + cw[2:3] * _shift_up(u, 1)[inner])
        yd = z_ref[zrows, C_CB:C_CB + D_GROUP] * conv
        ycat_ref[rows, 3 * D_GROUP:4 * D_GROUP] = yd.astype(BF16)

    for r in range(nblk):
        rows = slice(r * CHUNK, (r + 1) * CHUNK)
        y = jnp.dot(ycat_ref[rows, :], wout_ref[0], preferred_element_type=F32)
        xb = x1_ref[0, rows, :] if add_pos else x_ref[0, rows, :]
        x1 = xb + gt1 * _rms(y, ng[1:2])
        x1_ref[0, rows, :] = x1
        h2_ref[0, rows, :] = (_rms(x1, ng[2:3]) * (1.0 + sc2) + sh2).astype(BF16)


def _layer_spec(shape, layer):
    nd = len(shape)
    return pl.BlockSpec((1,) + tuple(shape[1:]), lambda b: (layer,) + (0,) * (nd - 1),
                        pipeline_mode=pl.Buffered(1))


def _mixer_call(x, layer, mod_all, wts, cond_base, pos=None, state=None, emit_state=False,
                prev_states=None):
    bsz, seq_len, _ = x.shape
    nblk = seq_len // CHUNK
    add_pos = pos is not None
    has_state = state is not None
    aliases = {}
    args = [x]
    in_specs = [pl.BlockSpec((1, seq_len, D_MODEL), lambda b: (b, 0, 0))]
    if add_pos:
        args += list(pos)
        in_specs += [pl.BlockSpec(p.shape, lambda b: (0, 0), pipeline_mode=pl.Buffered(1)) for p in pos]
    args.append(mod_all)
    in_specs.append(_layer_spec(mod_all.shape, layer))
    for name in ("norm_g", "w_in", "gate_b", "gmlp_ws", "gmlp_bias", "pool_bd", "pool_scale",
                 "conv_w", "w_out"):
        args.append(wts[name])
        in_specs.append(_layer_spec(wts[name].shape, layer))
    if has_state:
        args += list(state)
        in_specs += [pl.BlockSpec((1, 1, 2, PAIRS, LANES, 2 * LANES), lambda b: (b, layer, 0, 0, 0, 0)),
                     pl.BlockSpec((1, 1, 1, LANES), lambda b: (b, layer, 0, 0))]
    out_shape = [jax.ShapeDtypeStruct((bsz, seq_len, D_MODEL), F32),
                 jax.ShapeDtypeStruct((bsz, seq_len, D_MODEL), BF16)]
    out_specs = [pl.BlockSpec((1, seq_len, D_MODEL), lambda b: (b, 0, 0)),
                 pl.BlockSpec((1, seq_len, D_MODEL), lambda b: (b, 0, 0))]
    if emit_state:
        if prev_states is not None:
            aliases = {len(args): 2, len(args) + 1: 3}
            args += list(prev_states)
            in_specs += [pl.BlockSpec(memory_space=pl.ANY)] * 2
        out_shape += [jax.ShapeDtypeStruct((bsz, DEPTH, 2, HEADS, DH, DH), F32),
                      jax.ShapeDtypeStruct((bsz, DEPTH, 2, HEADS * DH), F32),
                      jax.ShapeDtypeStruct((bsz, 1, LANES), F32)]
        out_specs += [pl.BlockSpec((1, 1, 2, HEADS, DH, DH), lambda b: (b, layer, 0, 0, 0, 0)),
                      pl.BlockSpec((1, 1, 2, HEADS * DH), lambda b: (b, layer, 0, 0)),
                      pl.BlockSpec((1, 1, LANES), lambda b: (b, 0, 0))]
    kern = functools.partial(_mixer_kernel, seq_len=seq_len, add_pos=add_pos,
                             has_state=has_state, emit_state=emit_state, cond_base=cond_base,
                             n_aliased=len(aliases))
    return pl.pallas_call(
        kern,
        out_shape=out_shape,
        grid=(bsz,),
        in_specs=in_specs,
        out_specs=out_specs,
        input_output_aliases=aliases,
        scratch_shapes=[pltpu.VMEM((seq_len + 2 * HALO, Z_COLS), F32),
                        pltpu.VMEM((nblk, CHUNK, LANES), F32),
                        pltpu.VMEM((nblk, 16, CHUNK), F32),
                        pltpu.VMEM((nblk, CHUNK, LANES), F32),
                        pltpu.VMEM((2, nblk, PAIRS, LANES, 2 * LANES), BF16),
                        pltpu.VMEM((seq_len, D_MODEL), BF16)],
        compiler_params=pltpu.CompilerParams(
            dimension_semantics=("arbitrary",), vmem_limit_bytes=VMEM_LIMIT),
        name="mixer_t%d" % seq_len,
    )(*args)


def _ffn_kernel(x1_ref, h2_ref, mod_ref, ng_ref, up_ref, cw_ref, down_ref, o_ref, act_ref,
                *, seq_len, cond_base):
    rows = x1_ref.shape[0]
    half = FFN_SUB // 2
    t_in_seq = lax.broadcasted_iota(jnp.int32, (rows, FFN_SUB), 0) & (seq_len - 1)
    first = t_in_seq == 0
    last = t_in_seq == seq_len - 1
    h2 = h2_ref[...]
    for s in range(D_FF // half):
        a = jnp.dot(h2, up_ref[0, :, s * FFN_SUB:(s + 1) * FFN_SUB],
                    preferred_element_type=F32)
        cw = jnp.concatenate([cw_ref[0, :, s * half:(s + 1) * half],
                              cw_ref[0, :, D_FF + s * half:D_FF + (s + 1) * half]], axis=1)
        a_prev = jnp.where(first, 0.0, pltpu.roll(a, 1, axis=0))
        a_next = jnp.where(last, 0.0, pltpu.roll(a, rows - 1, axis=0))
        ac = cw[0:1] * a_prev + cw[1:2] * a + cw[2:3] * a_next
        hg = 0.5 * ac[:, half:]
        act = (hg * (1.0 + jnp.tanh(hg))) * ac[:, :half]
        act_ref[:, s * half:(s + 1) * half] = act.astype(BF16)
    y = jnp.dot(act_ref[...], down_ref[0], preferred_element_type=F32)
    cond = cond_base if cond_base == 0 else cond_base + pl.program_id(0)
    gt2 = _mod_rows(mod_ref, cond)[5]
    o_ref[...] = x1_ref[...] + gt2 * _rms(y, ng_ref[0][3:4])


def _ffn_call(x1, h2, layer, mod_all, wts, seq_len, cond_base):
    n = x1.shape[0]

    def resident(shape):
        nd = len(shape)
        return pl.BlockSpec((1,) + tuple(shape[1:]), lambda i: (layer,) + (0,) * (nd - 1),
                            pipeline_mode=pl.Buffered(1))

    return pl.pallas_call(
        functools.partial(_ffn_kernel, seq_len=seq_len, cond_base=cond_base),
        out_shape=jax.ShapeDtypeStruct((n, D_MODEL), F32),
        grid=(n // FFN_ROWS,),
        in_specs=[
            pl.BlockSpec((FFN_ROWS, D_MODEL), lambda i: (i, 0)),
            pl.BlockSpec((FFN_ROWS, D_MODEL), lambda i: (i, 0)),
            resident(mod_all.shape),
            resident(wts["norm_g"].shape),
            resident(wts["ffn_up"].shape),
            resident(wts["ffn_conv"].shape),
            resident(wts["ffn_down"].shape),
        ],
        out_specs=pl.BlockSpec((FFN_ROWS, D_MODEL), lambda i: (i, 0)),
        scratch_shapes=[pltpu.VMEM((FFN_ROWS, D_FF), BF16)],
        compiler_params=pltpu.CompilerParams(
            dimension_semantics=("arbitrary",), vmem_limit_bytes=VMEM_LIMIT),
        name="ffn_t%d" % seq_len,
    )(x1, h2, mod_all, wts["norm_g"], wts["ffn_up"], wts["ffn_conv"], wts["ffn_down"])


def _pos_tables(t_len):
    quarter = D_MODEL // 4
    omega = (1.0 / (np.float32(POS_BASE) ** (np.arange(quarter, dtype=np.float32) / np.float32(quarter))))
    omega = omega.astype(np.float32)

    def emb(p):
        a = p.astype(np.float32)[:, None] * omega[None, :]
        return np.concatenate([np.sin(a), np.cos(a)], axis=-1).astype(np.float32)

    return jnp.asarray(emb(np.arange(t_len // GRID_W))), jnp.asarray(emb(np.arange(GRID_W)))


def _pair_states(state_c, state_n, state_m):
    bs = state_c.shape[0]
    eye = jnp.eye(2, dtype=F32)
    c = state_c.astype(F32).reshape(bs, DEPTH, 2, PAIRS, 2, DH, DH)
    n = state_n.astype(F32).reshape(bs, DEPTH, 2, PAIRS, 2, DH)
    c_bd = jnp.einsum("...jde,jk->...jdke", c, eye).reshape(bs, DEPTH, 2, PAIRS, LANES, LANES)
    n_bd = jnp.einsum("...jd,jk,e->...jdke", n, eye, jnp.ones((DH,), F32)).reshape(
        bs, DEPTH, 2, PAIRS, LANES, LANES)
    m = jnp.pad(state_m.astype(F32), ((0, 0), (0, 0), (0, 0), (0, 16 - HEADS)))
    m = jnp.pad(m.reshape(bs, DEPTH, 1, 32), ((0, 0), (0, 0), (0, 0), (L_MF, LANES - 32 - L_MF)))
    return jnp.concatenate([c_bd, n_bd], axis=-1), m


def kernel(x_prompt, x_sample, state_C, state_n, state_m, c, c_ctx, norm_g, ada_w, ada_b, w_in,
           gate_b, gmlp_ws, gmlp_b, pool_w, pool_scale, conv_w, w_out, ffn_up, ffn_conv, ffn_down):
    bp, tp, _ = x_prompt.shape
    bs, ts, _ = x_sample.shape
    assert tp == CHUNK and ts % CHUNK == 0 and FFN_ROWS % tp == 0 and ts == FFN_ROWS

    conds = jnp.concatenate([c_ctx[None, :], c, jnp.zeros((7 - bs, D_MODEL), F32)], axis=0)
    mod_all = _modulation(conds, ada_w, ada_b)
    win_p, wout_p, up_p, down_p = _prep_weights(w_in, w_out, ffn_up, ffn_down)

    gb = jnp.pad(gate_b, ((0, 0), (0, 0), (0, 8 - HEADS))).reshape(DEPTH, 1, 32)
    dg = D_GROUP // 4
    wts = {
        "norm_g": norm_g,
        "w_in": win_p,
        "gate_b": jnp.pad(gb, ((0, 0), (0, 0), (0, LANES - 32))),
        "gmlp_ws": gmlp_ws,
        "gmlp_bias": jnp.repeat(jnp.swapaxes(gmlp_b, 1, 2), dg, axis=2),
        "pool_bd": jnp.einsum("lgcd,gh->lgchd", pool_w, jnp.eye(4, dtype=F32)).reshape(
            DEPTH, D_GROUP, D_GROUP).astype(BF16),
        "pool_scale": pool_scale.reshape(DEPTH, 1, D_GROUP),
        "conv_w": conv_w,
        "w_out": wout_p,
        "ffn_up": up_p,
        "ffn_conv": ffn_conv,
        "ffn_down": down_p,
    }
    pos = _pos_tables(ts)
    state = _pair_states(state_C, state_n, state_m)

    xp, xs = x_prompt, x_sample
    prev, ms = None, []
    for l in range(DEPTH):
        x1, h2, c_all, n_all, m_fin = _mixer_call(xp, l, mod_all, wts, 0, emit_state=True,
                                                  prev_states=prev)
        prev = (c_all, n_all)
        xp = _ffn_call(x1.reshape(bp * tp, D_MODEL), h2.reshape(bp * tp, D_MODEL), l, mod_all, wts,
                       tp, 0).reshape(bp, tp, D_MODEL)
        ms.append(jnp.stack([m_fin[:, 0, L_MF:L_MF + HEADS], m_fin[:, 0, L_MB:L_MB + HEADS]], axis=1))

        x1, h2 = _mixer_call(xs, l, mod_all, wts, 1, pos=pos if l == 0 else None, state=state)
        xs = _ffn_call(x1.reshape(bs * ts, D_MODEL), h2.reshape(bs * ts, D_MODEL), l, mod_all, wts,
                       ts, 1).reshape(bs, ts, D_MODEL)

    return (xp, xs, c_all, n_all.reshape(bp, DEPTH, 2, HEADS, DH), jnp.stack(ms, axis=1))
```

```python
import functools

import numpy as np
import jax
import jax.numpy as jnp
from jax import lax
from jax.experimental import pallas as pl
from jax.experimental.pallas import tpu as pltpu

F32 = jnp.float32
BF16 = jnp.bfloat16

D_MODEL = 1024
DEPTH = 2
HEADS = 4
PAIRS = HEADS // 2
DH = 64
D_GROUP = 256
D_FF = 2816
N_MOD = 6
EPS = 1e-6
GRID_W = 64
POS_BASE = 10000.0
D_IN = 2576
N_GATE = 4 * HEADS

CHUNK = 256
GMLP_CHUNK = 128
HALO = 8
LANES = 128

C_Q, C_K, C_V, C_O = 0, 256, 512, 768
C_GU, C_GV, C_PZ, C_CB, C_CC, C_CX = 1024, 1280, 1536, 1792, 2048, 2304
Z_COLS = 2560
C_GATE = Z_COLS
W_COLS = Z_COLS + LANES
Z_GROUP = 512
L_BF, L_MF, L_BB, L_MB = 0, 8, 16, 24

MIXER_ROWS = 1024
FFN_ROWS = 1024
FFN_SUB = 256
VMEM_LIMIT = 60 * 1024 * 1024


def _rms(x, g):
    ms = jnp.mean(x * x, axis=-1, keepdims=True)
    return x * lax.rsqrt(ms + EPS) * g


def _sigmoid(x):
    return 1.0 / (1.0 + jnp.exp(-x))


def _log_sigmoid(x):
    return jnp.minimum(x, 0.0) - jnp.log1p(jnp.exp(-jnp.abs(x)))


def _scan_lanes(x, op, reverse, fill):
    n = x.shape[1]
    lane = lax.broadcasted_iota(jnp.int32, x.shape, 1)
    k = 1
    while k < n:
        if reverse:
            sh = jnp.where(lane < n - k, pltpu.roll(x, n - k, axis=1), fill)
        else:
            sh = jnp.where(lane >= k, pltpu.roll(x, k, axis=1), fill)
        x = op(x, sh)
        k *= 2
    return x


def _shift_down(x, k):
    return pltpu.roll(x, k, axis=0)


def _shift_up(x, k):
    return pltpu.roll(x, x.shape[0] - k, axis=0)


def _prep_win_kernel(wt_ref, sel_ref, o_ref):
    for n in range(Z_COLS // LANES):
        r0 = LANES * n if n < 4 * D_GROUP // LANES else LANES * n + N_GATE
        o_ref[0, :, LANES * n:LANES * (n + 1)] = wt_ref[0, r0:r0 + LANES, :].T.astype(BF16)
    g = wt_ref[0, 4 * D_GROUP:4 * D_GROUP + LANES, :].T.astype(BF16)
    o_ref[0, :, C_GATE:W_COLS] = jnp.dot(g, sel_ref[...], preferred_element_type=F32).astype(BF16)


def _prep_up_kernel(w_ref, o_ref):
    half = FFN_SUB // 2
    for m in range(D_FF // half):
        o_ref[0, :, FFN_SUB * m:FFN_SUB * m + half] = w_ref[0, :, half * m:half * (m + 1)].astype(BF16)
        o_ref[0, :, FFN_SUB * m + half:FFN_SUB * (m + 1)] = (
            w_ref[0, :, D_FF + half * m:D_FF + half * (m + 1)].astype(BF16))


def _cast_kernel(w_ref, o_ref):
    o_ref[...] = w_ref[...].astype(BF16)


def _prep_weights(w_in, w_out, ffn_up, ffn_down):
    sel = np.zeros((LANES, LANES), np.float32)
    for j in range(N_GATE):
        sel[j, 8 * (j // HEADS) + j % HEADS] = 1.0
    params = pltpu.CompilerParams(dimension_semantics=("arbitrary", "arbitrary"),
                                  vmem_limit_bytes=VMEM_LIMIT)
    win_p = pl.pallas_call(
        _prep_win_kernel, out_shape=jax.ShapeDtypeStruct((DEPTH, D_MODEL, W_COLS), BF16),
        grid=(DEPTH, 1),
        in_specs=[pl.BlockSpec((1, D_IN, D_MODEL), lambda l, i: (l, 0, 0)),
                  pl.BlockSpec((LANES, LANES), lambda l, i: (0, 0))],
        out_specs=pl.BlockSpec((1, D_MODEL, W_COLS), lambda l, i: (l, 0, 0)),
        compiler_params=params, name="prep_w_in")(jnp.swapaxes(w_in, 1, 2), jnp.asarray(sel, BF16))
    rb = 128
    up_p = pl.pallas_call(
        _prep_up_kernel, out_shape=jax.ShapeDtypeStruct((DEPTH, D_MODEL, 2 * D_FF), BF16),
        grid=(DEPTH, D_MODEL // rb),
        in_specs=[pl.BlockSpec((1, rb, 2 * D_FF), lambda l, i: (l, i, 0))],
        out_specs=pl.BlockSpec((1, rb, 2 * D_FF), lambda l, i: (l, i, 0)),
        compiler_params=params, name="prep_ffn_up")(ffn_up)

    def cast(w, rows, name):
        _, r, c = w.shape
        return pl.pallas_call(
            _cast_kernel, out_shape=jax.ShapeDtypeStruct(w.shape, BF16),
            grid=(DEPTH, r // rows),
            in_specs=[pl.BlockSpec((1, rows, c), lambda l, i: (l, i, 0))],
            out_specs=pl.BlockSpec((1, rows, c), lambda l, i: (l, i, 0)),
            compiler_params=params, name=name)(w)

    return win_p, cast(w_out, 512, "prep_w_out"), up_p, cast(ffn_down, 704, "prep_ffn_down")


def _mod_kernel(c_ref, w_ref, b_ref, o_ref):
    c = c_ref[...]
    s = (c * _sigmoid(c)).astype(BF16)
    o_ref[0] = jnp.dot(s, w_ref[0].astype(BF16), preferred_element_type=F32) + b_ref[0]


def _modulation(conds, ada_w, ada_b):
    n_out = N_MOD * D_MODEL
    bn = 1536
    return pl.pallas_call(
        _mod_kernel,
        out_shape=jax.ShapeDtypeStruct((DEPTH, 8, n_out), F32),
        grid=(DEPTH, n_out // bn),
        in_specs=[
            pl.BlockSpec((8, D_MODEL), lambda l, j: (0, 0)),
            pl.BlockSpec((1, D_MODEL, bn), lambda l, j: (l, 0, j)),
            pl.BlockSpec((1, 1, bn), lambda l, j: (l, 0, j)),
        ],
        out_specs=pl.BlockSpec((1, 8, bn), lambda l, j: (l, 0, j)),
        compiler_params=pltpu.CompilerParams(
            dimension_semantics=("arbitrary", "arbitrary"), vmem_limit_bytes=VMEM_LIMIT),
        name="adaln_mod",
    )(conds, ada_w, ada_b.reshape(DEPTH, 1, n_out))


def _mod_rows(mod_ref, cond):
    row = mod_ref[0, pl.ds(cond, 1), :]
    return [row[:, i * D_MODEL:(i + 1) * D_MODEL] for i in range(N_MOD)]


def _gate_stats(gates):
    gt = gates.T
    b_f = _scan_lanes(_log_sigmoid(gt[8:16]), jnp.add, False, 0.0)
    g_f = gt[0:8] - b_f
    cm_f = _scan_lanes(g_f, jnp.maximum, False, -jnp.inf)
    b_b = _scan_lanes(_log_sigmoid(gt[24:32]), jnp.add, True, 0.0)
    g_b = gt[16:24] - b_b
    cm_b = _scan_lanes(g_b, jnp.maximum, True, -jnp.inf)
    stack = jnp.concatenate([b_f, cm_f, b_b, cm_b, jnp.zeros((LANES - 32, CHUNK), F32)], axis=0)
    return jnp.concatenate([g_f, g_b], axis=0), stack.T


def _pair_cols(mat, lane0, lane_lt_dh):
    return jnp.where(lane_lt_dh, mat[:, lane0:lane0 + 1], mat[:, lane0 + 1:lane0 + 2])


def _mlstm(z_ref, gcol_ref, grow_ref, cols_ref, bd_ref, ycat_ref, nc, state0, m0, need_final):
    lane_row = lax.broadcasted_iota(jnp.int32, (1, LANES), 1)
    lane = lax.broadcasted_iota(jnp.int32, (CHUNK, LANES), 1)
    lt_dh = lane < DH
    row_ll = lax.broadcasted_iota(jnp.int32, (CHUNK, CHUNK), 0)
    col_ll = lax.broadcasted_iota(jnp.int32, (CHUNK, CHUNK), 1)
    tri = (col_ll <= row_ll, col_ll >= row_ll)
    bd_row = lax.broadcasted_iota(jnp.int32, (LANES, 2 * LANES), 0)
    bd_col = lax.broadcasted_iota(jnp.int32, (LANES, 2 * LANES), 1)
    bd_mask = (bd_row < DH) == ((bd_col & (LANES - 1)) < DH)
    dirs = ((0, L_BF, L_MF, CHUNK - 1), (1, L_BB, L_MB, 0))

    m_start = [[None] * nc, [None] * nc]
    m_fin = [None, None]
    for d, _, _, last in dirs:
        m = m0
        order = range(nc) if d == 0 else range(nc - 1, -1, -1)
        for step, c in enumerate(order):
            m_start[d][c] = m
            if need_final or step < nc - 1:
                cl = cols_ref[c, last:last + 1, :]
                m = pltpu.roll(cl, 8, axis=1) + jnp.maximum(m, cl)
        m_fin[d] = m

    have_state = [[False] * nc, [False] * nc]
    fin = [None, None]
    for d, l_b, l_m, last in dirs:
        bd = list(state0[d])
        order = range(nc) if d == 0 else range(nc - 1, -1, -1)
        for step, c in enumerate(order):
            if bd[0] is not None:
                have_state[d][c] = True
                for p in range(PAIRS):
                    bd_ref[d, c, p] = bd[p].astype(BF16)
            if not (need_final or step < nc - 1):
                continue
            rows = slice(HALO + c * CHUNK, HALO + (c + 1) * CHUNK)
            cols = cols_ref[c]
            m_last = jnp.maximum(cols[last:last + 1, :], m_start[d][c])
            a_col = jnp.exp(gcol_ref[c] - cols - pltpu.roll(m_last, LANES - 8, axis=1))
            a_prev = jnp.exp(m_start[d][c] - m_last)
            for p in range(PAIRS):
                a_pair = _pair_cols(a_col, l_b + 2 * p, lt_dh)
                v_pair = z_ref[rows, C_V + LANES * p:C_V + LANES * (p + 1)]
                rhs = jnp.concatenate([v_pair * a_pair, a_pair], axis=1).astype(BF16)
                k_pair = (z_ref[rows, C_K + LANES * p:C_K + LANES * (p + 1)] * (DH ** -0.5)).astype(BF16)
                upd = lax.dot_general(k_pair, rhs, (((0,), (0,)), ((), ())),
                                      preferred_element_type=F32)
                upd = jnp.where(bd_mask, upd, 0.0)
                if bd[p] is None:
                    bd[p] = upd
                else:
                    scale = jnp.where(bd_row < DH, a_prev[:, l_m + 2 * p:l_m + 2 * p + 1],
                                      a_prev[:, l_m + 2 * p + 1:l_m + 2 * p + 2])
                    bd[p] = scale * bd[p] + upd
        fin[d] = bd

    for c in range(nc):
        rows = slice(HALO + c * CHUNK, HALO + (c + 1) * CHUNK)
        cols = cols_ref[c]
        grow = grow_ref[c]
        m_prev = jnp.where(lane_row < L_BB, m_start[0][c], m_start[1][c])
        m_all = jnp.maximum(cols, m_prev)
        w_inter = jnp.exp(m_prev - m_all)
        e_neg = jnp.exp(-(pltpu.roll(cols, 8, axis=1) + m_all))
        for p in range(PAIRS):
            q_pair = z_ref[rows, C_Q + LANES * p:C_Q + LANES * (p + 1)].astype(BF16)
            k_f32 = z_ref[rows, C_K + LANES * p:C_K + LANES * (p + 1)] * (DH ** -0.5)
            v_pair = z_ref[rows, C_V + LANES * p:C_V + LANES * (p + 1)]
            p_cat = ([], [])
            va = []
            for j in range(2):
                h = 2 * p + j
                own = lt_dh if j == 0 else jnp.logical_not(lt_dh)
                k_own = jnp.where(own, k_f32, 0.0).astype(BF16)
                s = lax.dot_general(q_pair, k_own, (((1,), (1,)), ((), ())),
                                    preferred_element_type=F32)
                va.append(jnp.concatenate([jnp.where(own, v_pair, 0.0), jnp.where(own, 1.0, 0.0)],
                                          axis=1).astype(BF16))
                for d, _, l_m, _ in dirs:
                    w = jnp.exp(jnp.where(tri[d], grow[8 * d + h:8 * d + h + 1] - m_all[:, l_m + h:l_m + h + 1],
                                          -jnp.inf))
                    p_cat[d].append((s * w).astype(BF16))
            va = jnp.concatenate(va, axis=0)
            h_sum = None
            for d, _, l_m, _ in dirs:
                tot = jnp.dot(jnp.concatenate(p_cat[d], axis=1), va, preferred_element_type=F32)
                if have_state[d][c]:
                    e_pair = _pair_cols(w_inter, l_m + 2 * p, lt_dh)
                    qc = jnp.dot(q_pair, bd_ref[d, c, p], preferred_element_type=F32)
                    tot = tot + jnp.concatenate([e_pair, e_pair], axis=1) * qc
                floor = _pair_cols(e_neg, l_m + 2 * p, lt_dh)
                h_dir = tot[:, :LANES] / jnp.maximum(jnp.abs(tot[:, LANES:]), floor)
                h_sum = h_dir if h_sum is None else h_sum + h_dir
            o_pair = z_ref[rows, C_O + LANES * p:C_O + LANES * (p + 1)]
            ycat_ref[c * CHUNK:(c + 1) * CHUNK, LANES * p:LANES * (p + 1)] = (
                _sigmoid(o_pair) * h_sum).astype(BF16)

    m_row = jnp.where(lane_row < L_BB, m_fin[0], m_fin[1])
    return fin, m_row


def _mixer_kernel(*refs, seq_len, nseq, add_pos, has_state, emit_state, cond_base, n_aliased):
    it = iter(refs)
    x_ref = next(it)
    if add_pos:
        er_ref = next(it)
        ec_ref = next(it)
    mod_ref = next(it)
    ng_ref = next(it)
    win_ref = next(it)
    gb_ref = next(it)
    ws_ref = next(it)
    gbias_ref = next(it)
    pbd_ref = next(it)
    psc_ref = next(it)
    cw_ref = next(it)
    wout_ref = next(it)
    if has_state:
        bd0_ref = next(it)
        m0_ref = next(it)
    for _ in range(n_aliased):
        next(it)
    x1_ref = next(it)
    h2_ref = next(it)
    if emit_state:
        cout_ref = next(it)
        nout_ref = next(it)
        mout_ref = next(it)
    z_all = next(it)
    gcol_all = next(it)
    grow_all = next(it)
    cols_all = next(it)
    bd_all = next(it)
    ycat_all = next(it)

    nblk = seq_len // CHUNK
    cond = cond_base if cond_base == 0 else cond_base + pl.program_id(0)
    sh1, sc1, gt1, sh2, sc2, _ = _mod_rows(mod_ref, cond)
    ng = ng_ref[0]
    seqs = range(nseq)
    units = [(q, r) for q in seqs for r in range(nblk)]

    zeros_halo = jnp.zeros((HALO, Z_COLS - C_PZ), F32)
    for q in seqs:
        z_all[q, 0:HALO, C_PZ:Z_COLS] = zeros_halo
        z_all[q, HALO + seq_len:2 * HALO + seq_len, C_PZ:Z_COLS] = zeros_halo

    g_in = ng[0:1] * (1.0 + sc1)
    hbs = []
    for q, r in units:
        rows = slice(r * CHUNK, (r + 1) * CHUNK)
        xb = x_ref[q, rows, :]
        if add_pos:
            pieces = []
            for g in range(CHUNK // GRID_W):
                gi = r * (CHUNK // GRID_W) + g
                er = jnp.broadcast_to(er_ref[gi:gi + 1, :], (GRID_W, D_MODEL // 2))
                pieces.append(jnp.concatenate([er, ec_ref[...]], axis=1))
            xb = xb + jnp.concatenate(pieces, axis=0)
            x1_ref[q, rows, :] = xb
        hbs.append((_rms(xb, g_in) + sh1).astype(BF16))
    hb = jnp.concatenate(hbs, axis=0)
    gates_all = jnp.dot(hb, win_ref[0, :, C_GATE:W_COLS], preferred_element_type=F32) + gb_ref[0]
    for cg in range(Z_COLS // Z_GROUP):
        zc = jnp.dot(hb, win_ref[0, :, cg * Z_GROUP:(cg + 1) * Z_GROUP], preferred_element_type=F32)
        for u, (q, r) in enumerate(units):
            z_all[q, HALO + r * CHUNK:HALO + (r + 1) * CHUNK, cg * Z_GROUP:(cg + 1) * Z_GROUP] = (
                zc[u * CHUNK:(u + 1) * CHUNK])
    for u, (q, r) in enumerate(units):
        gates = gates_all[u * CHUNK:(u + 1) * CHUNK]
        gcol_all[q, r] = gates
        grow, cols = _gate_stats(gates)
        grow_all[q, r] = grow
        cols_all[q, r] = cols

    for q in seqs:
        if has_state:
            state0 = [[bd0_ref[q, 0, d, p] for p in range(PAIRS)] for d in range(2)]
            m0 = m0_ref[q, 0]
        else:
            state0 = [[None] * PAIRS, [None] * PAIRS]
            m0 = jnp.zeros((1, LANES), F32)
        fin, m_row = _mlstm(z_all.at[q], gcol_all.at[q], grow_all.at[q], cols_all.at[q], bd_all.at[q],
                            ycat_all.at[q], nblk, state0, m0, need_final=emit_state)
        if emit_state:
            for d in range(2):
                for p in range(PAIRS):
                    for j in range(2):
                        cout_ref[q, 0, d, 2 * p + j] = fin[d][p][DH * j:DH * (j + 1), DH * j:DH * (j + 1)]
                    n_t = fin[d][p][:, LANES:].T
                    nout_ref[q, 0, d:d + 1, LANES * p:LANES * (p + 1)] = n_t[0:1] + n_t[DH:DH + 1]
            mout_ref[q] = m_row

    lane_g = lax.broadcasted_iota(jnp.int32, (CHUNK, D_GROUP), 1) // (D_GROUP // 4)
    lg = lax.broadcasted_iota(jnp.int32, (GMLP_CHUNK, D_GROUP), 1) // (D_GROUP // 4)
    row_c = lax.broadcasted_iota(jnp.int32, (CHUNK, D_GROUP), 0)
    half = jnp.where(lane_g == 0, 1, jnp.where(lane_g == 1, 2, jnp.where(lane_g == 2, 4, 8)))
    cw = cw_ref[0]
    for q, r in units:
        z_ref = z_all.at[q]
        ycat_ref = ycat_all.at[q]
        rows = slice(r * CHUNK, (r + 1) * CHUNK)
        zrows = slice(HALO + r * CHUNK, HALO + (r + 1) * CHUNK)
        hrows = slice(r * CHUNK, (r + 1) * CHUNK + 2 * HALO)
        inner = slice(HALO, HALO + CHUNK)

        for s in range(CHUNK // GMLP_CHUNK):
            zs = slice(HALO + r * CHUNK + s * GMLP_CHUNK, HALO + r * CHUNK + (s + 1) * GMLP_CHUNK)
            vch = z_ref[zs, C_GV:C_GV + D_GROUP].astype(BF16)
            mixed = gbias_ref[0]
            for g in range(4):
                mg = jnp.dot(ws_ref[0, g].astype(BF16), vch, preferred_element_type=F32)
                mixed = mixed + jnp.where(lg == g, mg, 0.0)
            yb = z_ref[zs, C_GU:C_GU + D_GROUP] * mixed
            ys = slice(r * CHUNK + s * GMLP_CHUNK, r * CHUNK + (s + 1) * GMLP_CHUNK)
            ycat_ref[ys, D_GROUP:2 * D_GROUP] = yb.astype(BF16)

        pz = z_ref[hrows, C_PZ:C_PZ + D_GROUP]
        a2 = pz + _shift_down(pz, 1)
        a4 = a2 + _shift_down(a2, 2)
        a8 = a4 + _shift_down(a4, 4)
        a16 = a8 + _shift_down(a8, 8)
        win = [a2[inner], _shift_up(a4, 1)[inner], _shift_up(a8, 3)[inner], _shift_up(a16, 7)[inner]]
        wsum = jnp.where(lane_g == 0, win[0], jnp.where(lane_g == 1, win[1],
                         jnp.where(lane_g == 2, win[2], win[3])))
        t_abs = row_c + r * CHUNK
        cnt = (jnp.minimum(t_abs + half, seq_len) - jnp.maximum(t_abs - half, 0)).astype(F32)
        pooled = wsum / cnt - pz[inner]
        yc = jnp.dot(pooled.astype(BF16), pbd_ref[0], preferred_element_type=F32) * psc_ref[0]
        ycat_ref[rows, 2 * D_GROUP:3 * D_GROUP] = yc.astype(BF16)

        u = z_ref[hrows, C_CC:C_CC + D_GROUP] * z_ref[hrows, C_CX:C_CX + D_GROUP]
        conv = (cw[0:1] * _shift_down(u, 1)[inner] + cw[1:2] * u[inner]
                + cw[2:3] * _shift_up(u, 1)[inner])
        yd = z_ref[zrows, C_CB:C_CB + D_GROUP] * conv
        ycat_ref[rows, 3 * D_GROUP:4 * D_GROUP] = yd.astype(BF16)

    y_all = jnp.dot(jnp.concatenate([ycat_all[q, r * CHUNK:(r + 1) * CHUNK, :] for q, r in units], axis=0),
                    wout_ref[0], preferred_element_type=F32)
    g_y = gt1 * ng[1:2]
    g_h2 = ng[2:3] * (1.0 + sc2)
    for u, (q, r) in enumerate(units):
        rows = slice(r * CHUNK, (r + 1) * CHUNK)
        xb = x1_ref[q, rows, :] if add_pos else x_ref[q, rows, :]
        x1 = xb + _rms(y_all[u * CHUNK:(u + 1) * CHUNK], g_y)
        x1_ref[q, rows, :] = x1
        h2_ref[q, rows, :] = (_rms(x1, g_h2) + sh2).astype(BF16)


def _layer_spec(shape, layer):
    nd = len(shape)
    return pl.BlockSpec((1,) + tuple(shape[1:]), lambda b: (layer,) + (0,) * (nd - 1),
                        pipeline_mode=pl.Buffered(1))


def _mixer_call(x, layer, mod_all, wts, cond_base, pos=None, state=None, emit_state=False,
                prev_states=None, nseq=1):
    bsz, seq_len, _ = x.shape
    nblk = seq_len // CHUNK
    add_pos = pos is not None
    has_state = state is not None
    aliases = {}
    args = [x]
    in_specs = [pl.BlockSpec((nseq, seq_len, D_MODEL), lambda b: (b, 0, 0))]
    if add_pos:
        args += list(pos)
        in_specs += [pl.BlockSpec(p.shape, lambda b: (0, 0), pipeline_mode=pl.Buffered(1)) for p in pos]
    args.append(mod_all)
    in_specs.append(_layer_spec(mod_all.shape, layer))
    for name in ("norm_g", "w_in", "gate_b", "gmlp_ws", "gmlp_bias", "pool_bd", "pool_scale",
                 "conv_w", "w_out"):
        args.append(wts[name])
        in_specs.append(_layer_spec(wts[name].shape, layer))
    if has_state:
        args += list(state)
        in_specs += [pl.BlockSpec((nseq, 1, 2, PAIRS, LANES, 2 * LANES), lambda b: (b, layer, 0, 0, 0, 0)),
                     pl.BlockSpec((nseq, 1, 1, LANES), lambda b: (b, layer, 0, 0))]
    out_shape = [jax.ShapeDtypeStruct((bsz, seq_len, D_MODEL), F32),
                 jax.ShapeDtypeStruct((bsz, seq_len, D_MODEL), BF16)]
    out_specs = [pl.BlockSpec((nseq, seq_len, D_MODEL), lambda b: (b, 0, 0)),
                 pl.BlockSpec((nseq, seq_len, D_MODEL), lambda b: (b, 0, 0))]
    if emit_state:
        if prev_states is not None:
            aliases = {len(args): 2, len(args) + 1: 3}
            args += list(prev_states)
            in_specs += [pl.BlockSpec(memory_space=pl.ANY)] * 2
        out_shape += [jax.ShapeDtypeStruct((bsz, DEPTH, 2, HEADS, DH, DH), F32),
                      jax.ShapeDtypeStruct((bsz, DEPTH, 2, HEADS * DH), F32),
                      jax.ShapeDtypeStruct((bsz, 1, LANES), F32)]
        out_specs += [pl.BlockSpec((nseq, 1, 2, HEADS, DH, DH), lambda b: (b, layer, 0, 0, 0, 0)),
                      pl.BlockSpec((nseq, 1, 2, HEADS * DH), lambda b: (b, layer, 0, 0)),
                      pl.BlockSpec((nseq, 1, LANES), lambda b: (b, 0, 0))]
    kern = functools.partial(_mixer_kernel, seq_len=seq_len, nseq=nseq, add_pos=add_pos,
                             has_state=has_state, emit_state=emit_state, cond_base=cond_base,
                             n_aliased=len(aliases))
    return pl.pallas_call(
        kern,
        out_shape=out_shape,
        grid=(bsz // nseq,),
        in_specs=in_specs,
        out_specs=out_specs,
        input_output_aliases=aliases,
        scratch_shapes=[pltpu.VMEM((nseq, seq_len + 2 * HALO, Z_COLS), F32),
                        pltpu.VMEM((nseq, nblk, CHUNK, LANES), F32),
                        pltpu.VMEM((nseq, nblk, 16, CHUNK), F32),
                        pltpu.VMEM((nseq, nblk, CHUNK, LANES), F32),
                        pltpu.VMEM((nseq, 2, nblk, PAIRS, LANES, 2 * LANES), BF16),
                        pltpu.VMEM((nseq, seq_len, D_MODEL), BF16)],
        compiler_params=pltpu.CompilerParams(
            dimension_semantics=("arbitrary",), vmem_limit_bytes=VMEM_LIMIT),
        name="mixer_t%d" % seq_len,
    )(*args)


def _ffn_kernel(x1_ref, h2_ref, mod_ref, ng_ref, up_ref, cw_ref, down_ref, o_ref, act_ref,
                *, seq_len, cond_base):
    rows = x1_ref.shape[0]
    half = FFN_SUB // 2
    t_in_seq = lax.broadcasted_iota(jnp.int32, (rows, FFN_SUB), 0) & (seq_len - 1)
    first = t_in_seq == 0
    last = t_in_seq == seq_len - 1
    h2 = h2_ref[...]
    for s in range(D_FF // half):
        a = jnp.dot(h2, up_ref[0, :, s * FFN_SUB:(s + 1) * FFN_SUB],
                    preferred_element_type=F32)
        cw = jnp.concatenate([cw_ref[0, :, s * half:(s + 1) * half],
                              cw_ref[0, :, D_FF + s * half:D_FF + (s + 1) * half]], axis=1)
        a_prev = jnp.where(first, 0.0, pltpu.roll(a, 1, axis=0))
        a_next = jnp.where(last, 0.0, pltpu.roll(a, rows - 1, axis=0))
        ac = cw[0:1] * a_prev + cw[1:2] * a + cw[2:3] * a_next
        hg = 0.5 * ac[:, half:]
        act = (hg * (1.0 + jnp.tanh(hg))) * ac[:, :half]
        act_ref[:, s * half:(s + 1) * half] = act.astype(BF16)
    y = jnp.dot(act_ref[...], down_ref[0], preferred_element_type=F32)
    cond = cond_base if cond_base == 0 else cond_base + pl.program_id(0)
    gt2 = _mod_rows(mod_ref, cond)[5]
    o_ref[...] = x1_ref[...] + gt2 * _rms(y, ng_ref[0][3:4])


def _ffn_call(x1, h2, layer, mod_all, wts, seq_len, cond_base):
    n = x1.shape[0]

    def resident(shape):
        nd = len(shape)
        return pl.BlockSpec((1,) + tuple(shape[1:]), lambda i: (layer,) + (0,) * (nd - 1),
                            pipeline_mode=pl.Buffered(1))

    return pl.pallas_call(
        functools.partial(_ffn_kernel, seq_len=seq_len, cond_base=cond_base),
        out_shape=jax.ShapeDtypeStruct((n, D_MODEL), F32),
        grid=(n // FFN_ROWS,),
        in_specs=[
            pl.BlockSpec((FFN_ROWS, D_MODEL), lambda i: (i, 0)),
            pl.BlockSpec((FFN_ROWS, D_MODEL), lambda i: (i, 0)),
            resident(mod_all.shape),
            resident(wts["norm_g"].shape),
            resident(wts["ffn_up"].shape),
            resident(wts["ffn_conv"].shape),
            resident(wts["ffn_down"].shape),
        ],
        out_specs=pl.BlockSpec((FFN_ROWS, D_MODEL), lambda i: (i, 0)),
        scratch_shapes=[pltpu.VMEM((FFN_ROWS, D_FF), BF16)],
        compiler_params=pltpu.CompilerParams(
            dimension_semantics=("arbitrary",), vmem_limit_bytes=VMEM_LIMIT),
        name="ffn_t%d" % seq_len,
    )(x1, h2, mod_all, wts["norm_g"], wts["ffn_up"], wts["ffn_conv"], wts["ffn_down"])


def _pos_tables(t_len):
    quarter = D_MODEL // 4
    omega = (1.0 / (np.float32(POS_BASE) ** (np.arange(quarter, dtype=np.float32) / np.float32(quarter))))
    omega = omega.astype(np.float32)

    def emb(p):
        a = p.astype(np.float32)[:, None] * omega[None, :]
        return np.concatenate([np.sin(a), np.cos(a)], axis=-1).astype(np.float32)

    return jnp.asarray(emb(np.arange(t_len // GRID_W))), jnp.asarray(emb(np.arange(GRID_W)))


def _pair_states(state_c, state_n, state_m):
    bs = state_c.shape[0]
    eye = jnp.eye(2, dtype=F32)
    c = state_c.astype(F32).reshape(bs, DEPTH, 2, PAIRS, 2, DH, DH)
    n = state_n.astype(F32).reshape(bs, DEPTH, 2, PAIRS, 2, DH)
    c_bd = jnp.einsum("...jde,jk->...jdke", c, eye).reshape(bs, DEPTH, 2, PAIRS, LANES, LANES)
    n_bd = jnp.einsum("...jd,jk,e->...jdke", n, eye, jnp.ones((DH,), F32)).reshape(
        bs, DEPTH, 2, PAIRS, LANES, LANES)
    m = jnp.pad(state_m.astype(F32), ((0, 0), (0, 0), (0, 0), (0, 16 - HEADS)))
    m = jnp.pad(m.reshape(bs, DEPTH, 1, 32), ((0, 0), (0, 0), (0, 0), (L_MF, LANES - 32 - L_MF)))
    return jnp.concatenate([c_bd, n_bd], axis=-1), m


def kernel(x_prompt, x_sample, state_C, state_n, state_m, c, c_ctx, norm_g, ada_w, ada_b, w_in,
           gate_b, gmlp_ws, gmlp_b, pool_w, pool_scale, conv_w, w_out, ffn_up, ffn_conv, ffn_down):
    bp, tp, _ = x_prompt.shape
    bs, ts, _ = x_sample.shape
    assert tp == CHUNK and ts % CHUNK == 0 and FFN_ROWS % tp == 0 and ts == FFN_ROWS

    conds = jnp.concatenate([c_ctx[None, :], c, jnp.zeros((7 - bs, D_MODEL), F32)], axis=0)
    mod_all = _modulation(conds, ada_w, ada_b)
    win_p, wout_p, up_p, down_p = _prep_weights(w_in, w_out, ffn_up, ffn_down)

    gb = jnp.pad(gate_b, ((0, 0), (0, 0), (0, 8 - HEADS))).reshape(DEPTH, 1, 32)
    dg = D_GROUP // 4
    wts = {
        "norm_g": norm_g,
        "w_in": win_p,
        "gate_b": jnp.pad(gb, ((0, 0), (0, 0), (0, LANES - 32))),
        "gmlp_ws": gmlp_ws,
        "gmlp_bias": jnp.repeat(jnp.swapaxes(gmlp_b, 1, 2), dg, axis=2),
        "pool_bd": jnp.einsum("lgcd,gh->lgchd", pool_w, jnp.eye(4, dtype=F32)).reshape(
            DEPTH, D_GROUP, D_GROUP).astype(BF16),
        "pool_scale": pool_scale.reshape(DEPTH, 1, D_GROUP),
        "conv_w": conv_w,
        "w_out": wout_p,
        "ffn_up": up_p,
        "ffn_conv": ffn_conv,
        "ffn_down": down_p,
    }
    pos = _pos_tables(ts)
    state = _pair_states(state_C, state_n, state_m)

    xp, xs = x_prompt, x_sample
    prev, ms = None, []
    for l in range(DEPTH):
        x1, h2, c_all, n_all, m_fin = _mixer_call(xp, l, mod_all, wts, 0, emit_state=True,
                                                  prev_states=prev, nseq=MIXER_ROWS // tp)
        prev = (c_all, n_all)
        xp = _ffn_call(x1.reshape(bp * tp, D_MODEL), h2.reshape(bp * tp, D_MODEL), l, mod_all, wts,
                       tp, 0).reshape(bp, tp, D_MODEL)
        ms.append(jnp.stack([m_fin[:, 0, L_MF:L_MF + HEADS], m_fin[:, 0, L_MB:L_MB + HEADS]], axis=1))

        x1, h2 = _mixer_call(xs, l, mod_all, wts, 1, pos=pos if l == 0 else None, state=state)
        xs = _ffn_call(x1.reshape(bs * ts, D_MODEL), h2.reshape(bs * ts, D_MODEL), l, mod_all, wts,
                       ts, 1).reshape(bs, ts, D_MODEL)

    return (xp, xs, c_all, n_all.reshape(bp, DEPTH, 2, HEADS, DH), jnp.stack(ms, axis=1))
```

```python
import functools

import numpy as np
import jax
import jax.numpy as jnp
from jax import lax
from jax.experimental import pallas as pl
from jax.experimental.pallas import tpu as pltpu

F32 = jnp.float32
BF16 = jnp.bfloat16

D_MODEL = 1024
DEPTH = 2
HEADS = 4
PAIRS = HEADS // 2
DH = 64
D_GROUP = 256
D_FF = 2816
N_MOD = 6
EPS = 1e-6
GRID_W = 64
POS_BASE = 10000.0
D_IN = 2576
N_GATE = 4 * HEADS

CHUNK = 256
GMLP_CHUNK = 128
HALO = 8
LANES = 128

C_Q, C_K, C_V, C_O = 0, 256, 512, 768
C_GU, C_GV, C_PZ, C_CB, C_CC, C_CX = 1024, 1280, 1536, 1792, 2048, 2304
Z_COLS = 2560
C_GATE = Z_COLS
W_COLS = Z_COLS + LANES
Z_GROUP = 512
L_BF, L_MF, L_BB, L_MB = 0, 8, 16, 24

MIXER_ROWS = 1024
FFN_ROWS = 1024
FFN_SUB = 256
VMEM_LIMIT = 60 * 1024 * 1024


def _rms(x, g):
    ms = jnp.mean(x * x, axis=-1, keepdims=True)
    return x * lax.rsqrt(ms + EPS) * g


def _sigmoid(x):
    return 1.0 / (1.0 + jnp.exp(-x))


def _log_sigmoid(x):
    return jnp.minimum(x, 0.0) - jnp.log1p(jnp.exp(-jnp.abs(x)))


def _scan_lanes(x, op, reverse, fill):
    n = x.shape[1]
    lane = lax.broadcasted_iota(jnp.int32, x.shape, 1)
    k = 1
    while k < n:
        if reverse:
            sh = jnp.where(lane < n - k, pltpu.roll(x, n - k, axis=1), fill)
        else:
            sh = jnp.where(lane >= k, pltpu.roll(x, k, axis=1), fill)
        x = op(x, sh)
        k *= 2
    return x


def _shift_down(x, k):
    return pltpu.roll(x, k, axis=0)


def _shift_up(x, k):
    return pltpu.roll(x, x.shape[0] - k, axis=0)


def _prep_win_kernel(wt_ref, sel_ref, o_ref):
    for n in range(Z_COLS // LANES):
        r0 = LANES * n if n < 4 * D_GROUP // LANES else LANES * n + N_GATE
        o_ref[0, :, LANES * n:LANES * (n + 1)] = wt_ref[0, r0:r0 + LANES, :].T.astype(BF16)
    g = wt_ref[0, 4 * D_GROUP:4 * D_GROUP + LANES, :].T.astype(BF16)
    o_ref[0, :, C_GATE:W_COLS] = jnp.dot(g, sel_ref[...], preferred_element_type=F32).astype(BF16)


def _cast_kernel(w_ref, o_ref):
    o_ref[...] = w_ref[...].astype(BF16)


def _prep_weights(w_in, w_out):
    sel = np.zeros((LANES, LANES), np.float32)
    for j in range(N_GATE):
        sel[j, 8 * (j // HEADS) + j % HEADS] = 1.0
    params = pltpu.CompilerParams(dimension_semantics=("arbitrary", "arbitrary"),
                                  vmem_limit_bytes=VMEM_LIMIT)
    win_p = pl.pallas_call(
        _prep_win_kernel, out_shape=jax.ShapeDtypeStruct((DEPTH, D_MODEL, W_COLS), BF16),
        grid=(DEPTH, 1),
        in_specs=[pl.BlockSpec((1, D_IN, D_MODEL), lambda l, i: (l, 0, 0)),
                  pl.BlockSpec((LANES, LANES), lambda l, i: (0, 0))],
        out_specs=pl.BlockSpec((1, D_MODEL, W_COLS), lambda l, i: (l, 0, 0)),
        compiler_params=params, name="prep_w_in")(jnp.swapaxes(w_in, 1, 2), jnp.asarray(sel, BF16))
    rows = 512
    wout_p = pl.pallas_call(
        _cast_kernel, out_shape=jax.ShapeDtypeStruct(w_out.shape, BF16),
        grid=(DEPTH, D_MODEL // rows),
        in_specs=[pl.BlockSpec((1, rows, D_MODEL), lambda l, i: (l, i, 0))],
        out_specs=pl.BlockSpec((1, rows, D_MODEL), lambda l, i: (l, i, 0)),
        compiler_params=params, name="prep_w_out")(w_out)
    return win_p, wout_p


def _mod_kernel(c_ref, w_ref, b_ref, o_ref):
    c = c_ref[...]
    s = (c * _sigmoid(c)).astype(BF16)
    o_ref[0] = jnp.dot(s, w_ref[0].astype(BF16), preferred_element_type=F32) + b_ref[0]


def _modulation(conds, ada_w, ada_b):
    n_out = N_MOD * D_MODEL
    bn = 1536
    return pl.pallas_call(
        _mod_kernel,
        out_shape=jax.ShapeDtypeStruct((DEPTH, 8, n_out), F32),
        grid=(DEPTH, n_out // bn),
        in_specs=[
            pl.BlockSpec((8, D_MODEL), lambda l, j: (0, 0)),
            pl.BlockSpec((1, D_MODEL, bn), lambda l, j: (l, 0, j)),
            pl.BlockSpec((1, 1, bn), lambda l, j: (l, 0, j)),
        ],
        out_specs=pl.BlockSpec((1, 8, bn), lambda l, j: (l, 0, j)),
        compiler_params=pltpu.CompilerParams(
            dimension_semantics=("arbitrary", "arbitrary"), vmem_limit_bytes=VMEM_LIMIT),
        name="adaln_mod",
    )(conds, ada_w, ada_b.reshape(DEPTH, 1, n_out))


def _mod_rows(mod_ref, cond):
    row = mod_ref[0, pl.ds(cond, 1), :]
    return [row[:, i * D_MODEL:(i + 1) * D_MODEL] for i in range(N_MOD)]


def _gate_stats(gates):
    gt = gates.T
    b_f = _scan_lanes(_log_sigmoid(gt[8:16]), jnp.add, False, 0.0)
    g_f = gt[0:8] - b_f
    cm_f = _scan_lanes(g_f, jnp.maximum, False, -jnp.inf)
    b_b = _scan_lanes(_log_sigmoid(gt[24:32]), jnp.add, True, 0.0)
    g_b = gt[16:24] - b_b
    cm_b = _scan_lanes(g_b, jnp.maximum, True, -jnp.inf)
    stack = jnp.concatenate([b_f, cm_f, b_b, cm_b, jnp.zeros((LANES - 32, CHUNK), F32)], axis=0)
    return jnp.concatenate([g_f, g_b], axis=0), stack.T


def _pair_cols(mat, lane0, lane_lt_dh):
    return jnp.where(lane_lt_dh, mat[:, lane0:lane0 + 1], mat[:, lane0 + 1:lane0 + 2])


def _mlstm(z_ref, gcol_ref, grow_ref, cols_ref, bd_ref, ycat_ref, nc, state0, m0, need_final):
    lane_row = lax.broadcasted_iota(jnp.int32, (1, LANES), 1)
    lane = lax.broadcasted_iota(jnp.int32, (CHUNK, LANES), 1)
    lt_dh = lane < DH
    row_ll = lax.broadcasted_iota(jnp.int32, (CHUNK, CHUNK), 0)
    col_ll = lax.broadcasted_iota(jnp.int32, (CHUNK, CHUNK), 1)
    tri = (col_ll <= row_ll, col_ll >= row_ll)
    bd_row = lax.broadcasted_iota(jnp.int32, (LANES, 2 * LANES), 0)
    bd_col = lax.broadcasted_iota(jnp.int32, (LANES, 2 * LANES), 1)
    bd_mask = (bd_row < DH) == ((bd_col & (LANES - 1)) < DH)
    dirs = ((0, L_BF, L_MF, CHUNK - 1), (1, L_BB, L_MB, 0))

    m_start = [[None] * nc, [None] * nc]
    m_fin = [None, None]
    for d, _, _, last in dirs:
        m = m0
        order = range(nc) if d == 0 else range(nc - 1, -1, -1)
        for step, c in enumerate(order):
            m_start[d][c] = m
            if need_final or step < nc - 1:
                cl = cols_ref[c, last:last + 1, :]
                m = pltpu.roll(cl, 8, axis=1) + jnp.maximum(m, cl)
        m_fin[d] = m

    have_state = [[False] * nc, [False] * nc]
    fin = [None, None]
    for d, l_b, l_m, last in dirs:
        bd = list(state0[d])
        order = range(nc) if d == 0 else range(nc - 1, -1, -1)
        for step, c in enumerate(order):
            if bd[0] is not None:
                have_state[d][c] = True
                for p in range(PAIRS):
                    bd_ref[d, c, p] = bd[p].astype(BF16)
            if not (need_final or step < nc - 1):
                continue
            rows = slice(HALO + c * CHUNK, HALO + (c + 1) * CHUNK)
            cols = cols_ref[c]
            m_last = jnp.maximum(cols[last:last + 1, :], m_start[d][c])
            a_col = jnp.exp(gcol_ref[c] - cols - pltpu.roll(m_last, LANES - 8, axis=1))
            a_prev = jnp.exp(m_start[d][c] - m_last)
            for p in range(PAIRS):
                a_pair = _pair_cols(a_col, l_b + 2 * p, lt_dh)
                v_pair = z_ref[rows, C_V + LANES * p:C_V + LANES * (p + 1)]
                rhs = jnp.concatenate([v_pair * a_pair, a_pair], axis=1).astype(BF16)
                k_pair = (z_ref[rows, C_K + LANES * p:C_K + LANES * (p + 1)] * (DH ** -0.5)).astype(BF16)
                upd = lax.dot_general(k_pair, rhs, (((0,), (0,)), ((), ())),
                                      preferred_element_type=F32)
                upd = jnp.where(bd_mask, upd, 0.0)
                if bd[p] is None:
                    bd[p] = upd
                else:
                    scale = jnp.where(bd_row < DH, a_prev[:, l_m + 2 * p:l_m + 2 * p + 1],
                                      a_prev[:, l_m + 2 * p + 1:l_m + 2 * p + 2])
                    bd[p] = scale * bd[p] + upd
        fin[d] = bd

    for c in range(nc):
        rows = slice(HALO + c * CHUNK, HALO + (c + 1) * CHUNK)
        cols = cols_ref[c]
        grow = grow_ref[c]
        m_prev = jnp.where(lane_row < L_BB, m_start[0][c], m_start[1][c])
        m_all = jnp.maximum(cols, m_prev)
        w_inter = jnp.exp(m_prev - m_all)
        e_neg = jnp.exp(-(pltpu.roll(cols, 8, axis=1) + m_all))
        for p in range(PAIRS):
            q_pair = z_ref[rows, C_Q + LANES * p:C_Q + LANES * (p + 1)].astype(BF16)
            k_f32 = z_ref[rows, C_K + LANES * p:C_K + LANES * (p + 1)] * (DH ** -0.5)
            v_pair = z_ref[rows, C_V + LANES * p:C_V + LANES * (p + 1)]
            p_cat = ([], [])
            va = []
            for j in range(2):
                h = 2 * p + j
                own = lt_dh if j == 0 else jnp.logical_not(lt_dh)
                k_own = jnp.where(own, k_f32, 0.0).astype(BF16)
                s = lax.dot_general(q_pair, k_own, (((1,), (1,)), ((), ())),
                                    preferred_element_type=F32)
                va.append(jnp.concatenate([jnp.where(own, v_pair, 0.0), jnp.where(own, 1.0, 0.0)],
                                          axis=1).astype(BF16))
                for d, _, l_m, _ in dirs:
                    w = jnp.exp(jnp.where(tri[d], grow[8 * d + h:8 * d + h + 1] - m_all[:, l_m + h:l_m + h + 1],
                                          -jnp.inf))
                    p_cat[d].append((s * w).astype(BF16))
            va = jnp.concatenate(va, axis=0)
            h_sum = None
            for d, _, l_m, _ in dirs:
                tot = jnp.dot(jnp.concatenate(p_cat[d], axis=1), va, preferred_element_type=F32)
                if have_state[d][c]:
                    e_pair = _pair_cols(w_inter, l_m + 2 * p, lt_dh)
                    qc = jnp.dot(q_pair, bd_ref[d, c, p], preferred_element_type=F32)
                    tot = tot + jnp.concatenate([e_pair, e_pair], axis=1) * qc
                floor = _pair_cols(e_neg, l_m + 2 * p, lt_dh)
                h_dir = tot[:, :LANES] / jnp.maximum(jnp.abs(tot[:, LANES:]), floor)
                h_sum = h_dir if h_sum is None else h_sum + h_dir
            o_pair = z_ref[rows, C_O + LANES * p:C_O + LANES * (p + 1)]
            ycat_ref[c * CHUNK:(c + 1) * CHUNK, LANES * p:LANES * (p + 1)] = (
                _sigmoid(o_pair) * h_sum).astype(BF16)

    m_row = jnp.where(lane_row < L_BB, m_fin[0], m_fin[1])
    return fin, m_row


def _mixer_kernel(*refs, seq_len, nseq, add_pos, has_state, emit_state, cond_base, n_prev):
    it = iter(refs)
    x_ref = next(it)
    if add_pos:
        er_ref = next(it)
        ec_ref = next(it)
    mod_ref = next(it)
    ng_ref = next(it)
    win_ref = next(it)
    gb_ref = next(it)
    ws_ref = next(it)
    gbias_ref = next(it)
    pbd_ref = next(it)
    psc_ref = next(it)
    cw_ref = next(it)
    wout_ref = next(it)
    if has_state:
        bd0_ref = next(it)
        m0_ref = next(it)
    if n_prev:
        cprev_ref = next(it)
        nprev_ref = next(it)
    x1_ref = next(it)
    h2_ref = next(it)
    if emit_state:
        cout_ref = next(it)
        nout_ref = next(it)
        mout_ref = next(it)
    z_all = next(it)
    gcol_all = next(it)
    grow_all = next(it)
    cols_all = next(it)
    bd_all = next(it)
    ycat_all = next(it)

    nblk = seq_len // CHUNK
    cond = cond_base if cond_base == 0 else cond_base + pl.program_id(0)
    sh1, sc1, gt1, sh2, sc2, _ = _mod_rows(mod_ref, cond)
    ng = ng_ref[0]
    seqs = range(nseq)
    units = [(q, r) for q in seqs for r in range(nblk)]

    zeros_halo = jnp.zeros((HALO, Z_COLS - C_PZ), F32)
    for q in seqs:
        z_all[q, 0:HALO, C_PZ:Z_COLS] = zeros_halo
        z_all[q, HALO + seq_len:2 * HALO + seq_len, C_PZ:Z_COLS] = zeros_halo

    g_in = ng[0:1] * (1.0 + sc1)
    hbs = []
    for q, r in units:
        rows = slice(r * CHUNK, (r + 1) * CHUNK)
        xb = x_ref[q, rows, :]
        if add_pos:
            pieces = []
            for g in range(CHUNK // GRID_W):
                gi = r * (CHUNK // GRID_W) + g
                er = jnp.broadcast_to(er_ref[gi:gi + 1, :], (GRID_W, D_MODEL // 2))
                pieces.append(jnp.concatenate([er, ec_ref[...]], axis=1))
            xb = xb + jnp.concatenate(pieces, axis=0)
            x1_ref[q, rows, :] = xb
        hbs.append((_rms(xb, g_in) + sh1).astype(BF16))
    hb = jnp.concatenate(hbs, axis=0)
    gates_all = jnp.dot(hb, win_ref[0, :, C_GATE:W_COLS], preferred_element_type=F32) + gb_ref[0]
    for cg in range(Z_COLS // Z_GROUP):
        zc = jnp.dot(hb, win_ref[0, :, cg * Z_GROUP:(cg + 1) * Z_GROUP], preferred_element_type=F32)
        for u, (q, r) in enumerate(units):
            z_all[q, HALO + r * CHUNK:HALO + (r + 1) * CHUNK, cg * Z_GROUP:(cg + 1) * Z_GROUP] = (
                zc[u * CHUNK:(u + 1) * CHUNK])
    for u, (q, r) in enumerate(units):
        gates = gates_all[u * CHUNK:(u + 1) * CHUNK]
        gcol_all[q, r] = gates
        grow, cols = _gate_stats(gates)
        grow_all[q, r] = grow
        cols_all[q, r] = cols

    for q in seqs:
        if has_state:
            state0 = [[bd0_ref[q, 0, d, p] for p in range(PAIRS)] for d in range(2)]
            m0 = m0_ref[q, 0]
        else:
            state0 = [[None] * PAIRS, [None] * PAIRS]
            m0 = jnp.zeros((1, LANES), F32)
        fin, m_row = _mlstm(z_all.at[q], gcol_all.at[q], grow_all.at[q], cols_all.at[q], bd_all.at[q],
                            ycat_all.at[q], nblk, state0, m0, need_final=emit_state)
        if emit_state:
            for d in range(2):
                for p in range(PAIRS):
                    for j in range(2):
                        cout_ref[q, n_prev, d, 2 * p + j] = fin[d][p][DH * j:DH * (j + 1), DH * j:DH * (j + 1)]
                    n_t = fin[d][p][:, LANES:].T
                    nout_ref[q, n_prev, d:d + 1, LANES * p:LANES * (p + 1)] = n_t[0:1] + n_t[DH:DH + 1]
            mout_ref[q] = m_row
    if emit_state and n_prev:
        cout_ref[:, 0:n_prev] = cprev_ref[...]
        nout_ref[:, 0:n_prev] = nprev_ref[...]

    lane_g = lax.broadcasted_iota(jnp.int32, (CHUNK, D_GROUP), 1) // (D_GROUP // 4)
    lg = lax.broadcasted_iota(jnp.int32, (GMLP_CHUNK, D_GROUP), 1) // (D_GROUP // 4)
    row_c = lax.broadcasted_iota(jnp.int32, (CHUNK, D_GROUP), 0)
    half = jnp.where(lane_g == 0, 1, jnp.where(lane_g == 1, 2, jnp.where(lane_g == 2, 4, 8)))
    cw = cw_ref[0]
    for q, r in units:
        z_ref = z_all.at[q]
        ycat_ref = ycat_all.at[q]
        rows = slice(r * CHUNK, (r + 1) * CHUNK)
        zrows = slice(HALO + r * CHUNK, HALO + (r + 1) * CHUNK)
        hrows = slice(r * CHUNK, (r + 1) * CHUNK + 2 * HALO)
        inner = slice(HALO, HALO + CHUNK)

        for s in range(CHUNK // GMLP_CHUNK):
            zs = slice(HALO + r * CHUNK + s * GMLP_CHUNK, HALO + r * CHUNK + (s + 1) * GMLP_CHUNK)
            vch = z_ref[zs, C_GV:C_GV + D_GROUP].astype(BF16)
            mixed = gbias_ref[0]
            for g in range(4):
                mg = jnp.dot(ws_ref[0, g].astype(BF16), vch, preferred_element_type=F32)
                mixed = mixed + jnp.where(lg == g, mg, 0.0)
            yb = z_ref[zs, C_GU:C_GU + D_GROUP] * mixed
            ys = slice(r * CHUNK + s * GMLP_CHUNK, r * CHUNK + (s + 1) * GMLP_CHUNK)
            ycat_ref[ys, D_GROUP:2 * D_GROUP] = yb.astype(BF16)

        pz = z_ref[hrows, C_PZ:C_PZ + D_GROUP]
        a2 = pz + _shift_down(pz, 1)
        a4 = a2 + _shift_down(a2, 2)
        a8 = a4 + _shift_down(a4, 4)
        a16 = a8 + _shift_down(a8, 8)
        win = [a2[inner], _shift_up(a4, 1)[inner], _shift_up(a8, 3)[inner], _shift_up(a16, 7)[inner]]
        wsum = jnp.where(lane_g == 0, win[0], jnp.where(lane_g == 1, win[1],
                         jnp.where(lane_g == 2, win[2], win[3])))
        t_abs = row_c + r * CHUNK
        cnt = (jnp.minimum(t_abs + half, seq_len) - jnp.maximum(t_abs - half, 0)).astype(F32)
        pooled = wsum / cnt - pz[inner]
        yc = jnp.dot(pooled.astype(BF16), pbd_ref[0], preferred_element_type=F32) * psc_ref[0]
        ycat_ref[rows, 2 * D_GROUP:3 * D_GROUP] = yc.astype(BF16)

        u = z_ref[hrows, C_CC:C_CC + D_GROUP] * z_ref[hrows, C_CX:C_CX + D_GROUP]
        conv = (cw[0:1] * _shift_down(u, 1)[inner] + cw[1:2] * u[inner]
                + cw[2:3] * _shift_up(u, 1)[inner])
        yd = z_ref[zrows, C_CB:C_CB + D_GROUP] * conv
        ycat_ref[rows, 3 * D_GROUP:4 * D_GROUP] = yd.astype(BF16)

    y_all = jnp.dot(jnp.concatenate([ycat_all[q, r * CHUNK:(r + 1) * CHUNK, :] for q, r in units], axis=0),
                    wout_ref[0], preferred_element_type=F32)
    g_y = gt1 * ng[1:2]
    g_h2 = ng[2:3] * (1.0 + sc2)
    for u, (q, r) in enumerate(units):
        rows = slice(r * CHUNK, (r + 1) * CHUNK)
        xb = x1_ref[q, rows, :] if add_pos else x_ref[q, rows, :]
        x1 = xb + _rms(y_all[u * CHUNK:(u + 1) * CHUNK], g_y)
        x1_ref[q, rows, :] = x1
        h2_ref[q, rows, :] = (_rms(x1, g_h2) + sh2).astype(BF16)


def _layer_spec(shape, layer):
    nd = len(shape)
    return pl.BlockSpec((1,) + tuple(shape[1:]), lambda b: (layer,) + (0,) * (nd - 1),
                        pipeline_mode=pl.Buffered(1))


def _mixer_call(x, layer, mod_all, wts, cond_base, pos=None, state=None, emit_state=False,
                prev_states=None, nseq=1):
    bsz, seq_len, _ = x.shape
    nblk = seq_len // CHUNK
    add_pos = pos is not None
    has_state = state is not None
    n_prev = 0 if prev_states is None else prev_states[0].shape[1]
    args = [x]
    in_specs = [pl.BlockSpec((nseq, seq_len, D_MODEL), lambda b: (b, 0, 0))]
    if add_pos:
        args += list(pos)
        in_specs += [pl.BlockSpec(p.shape, lambda b: (0, 0), pipeline_mode=pl.Buffered(1)) for p in pos]
    args.append(mod_all)
    in_specs.append(_layer_spec(mod_all.shape, layer))
    for name in ("norm_g", "w_in", "gate_b", "gmlp_ws", "gmlp_bias", "pool_bd", "pool_scale",
                 "conv_w", "w_out"):
        args.append(wts[name])
        in_specs.append(_layer_spec(wts[name].shape, layer))
    if has_state:
        args += list(state)
        in_specs += [pl.BlockSpec((nseq, 1, 2, PAIRS, LANES, 2 * LANES), lambda b: (b, layer, 0, 0, 0, 0)),
                     pl.BlockSpec((nseq, 1, 1, LANES), lambda b: (b, layer, 0, 0))]
    out_shape = [jax.ShapeDtypeStruct((bsz, seq_len, D_MODEL), F32),
                 jax.ShapeDtypeStruct((bsz, seq_len, D_MODEL), BF16)]
    out_specs = [pl.BlockSpec((nseq, seq_len, D_MODEL), lambda b: (b, 0, 0)),
                 pl.BlockSpec((nseq, seq_len, D_MODEL), lambda b: (b, 0, 0))]
    if emit_state:
        if n_prev:
            args += list(prev_states)
            in_specs += [pl.BlockSpec((nseq, n_prev, 2, HEADS, DH, DH), lambda b: (b, 0, 0, 0, 0, 0)),
                         pl.BlockSpec((nseq, n_prev, 2, HEADS * DH), lambda b: (b, 0, 0, 0))]
        out_shape += [jax.ShapeDtypeStruct((bsz, n_prev + 1, 2, HEADS, DH, DH), F32),
                      jax.ShapeDtypeStruct((bsz, n_prev + 1, 2, HEADS * DH), F32),
                      jax.ShapeDtypeStruct((bsz, 1, LANES), F32)]
        out_specs += [pl.BlockSpec((nseq, n_prev + 1, 2, HEADS, DH, DH), lambda b: (b, 0, 0, 0, 0, 0)),
                      pl.BlockSpec((nseq, n_prev + 1, 2, HEADS * DH), lambda b: (b, 0, 0, 0)),
                      pl.BlockSpec((nseq, 1, LANES), lambda b: (b, 0, 0))]
    kern = functools.partial(_mixer_kernel, seq_len=seq_len, nseq=nseq, add_pos=add_pos,
                             has_state=has_state, emit_state=emit_state, cond_base=cond_base,
                             n_prev=n_prev)
    return pl.pallas_call(
        kern,
        out_shape=out_shape,
        grid=(bsz // nseq,),
        in_specs=in_specs,
        out_specs=out_specs,
        scratch_shapes=[pltpu.VMEM((nseq, seq_len + 2 * HALO, Z_COLS), F32),
                        pltpu.VMEM((nseq, nblk, CHUNK, LANES), F32),
                        pltpu.VMEM((nseq, nblk, 16, CHUNK), F32),
                        pltpu.VMEM((nseq, nblk, CHUNK, LANES), F32),
                        pltpu.VMEM((nseq, 2, nblk, PAIRS, LANES, 2 * LANES), BF16),
                        pltpu.VMEM((nseq, seq_len, D_MODEL), BF16)],
        compiler_params=pltpu.CompilerParams(
            dimension_semantics=("arbitrary",), vmem_limit_bytes=VMEM_LIMIT),
        name="mixer_t%d" % seq_len,
    )(*args)


def _ffn_kernel(x1_ref, h2_ref, mod_ref, ng_ref, cw_ref, up_hbm, down_hbm, o_ref,
                up_bf, down_bf, act_ref, up_stage, down_stage, sems, *, seq_len, cond_base, layer):
    rows = x1_ref.shape[0]
    half = FFN_SUB // 2
    nsub = D_FF // half
    t_in_seq = lax.broadcasted_iota(jnp.int32, (rows, FFN_SUB), 0) & (seq_len - 1)
    first = t_in_seq == 0
    last = t_in_seq == seq_len - 1

    def up_copy(s, part):
        return pltpu.make_async_copy(
            up_hbm.at[layer, :, pl.ds(part * D_FF + s * half, half)],
            up_stage.at[s % 2, part], sems.at[s % 2, part])

    def down_copy(s):
        return pltpu.make_async_copy(
            down_hbm.at[layer, pl.ds(s * half, half), :], down_stage.at[s % 2], sems.at[s % 2, 2])

    def start_chunk(s):
        up_copy(s, 0).start()
        up_copy(s, 1).start()
        down_copy(s).start()

    def convert_chunk(s):
        for part in range(2):
            up_copy(s, part).wait()
            up_bf[:, s * FFN_SUB + part * half:s * FFN_SUB + (part + 1) * half] = (
                up_stage[s % 2, part].astype(BF16))
        down_copy(s).wait()
        down_bf[s * half:(s + 1) * half, :] = down_stage[s % 2].astype(BF16)

    def compute_chunk(s, h2):
        a = jnp.dot(h2, up_bf[:, s * FFN_SUB:(s + 1) * FFN_SUB],
                    preferred_element_type=F32)
        cw = jnp.concatenate([cw_ref[0, :, s * half:(s + 1) * half],
                              cw_ref[0, :, D_FF + s * half:D_FF + (s + 1) * half]], axis=1)
        a_prev = jnp.where(first, 0.0, pltpu.roll(a, 1, axis=0))
        a_next = jnp.where(last, 0.0, pltpu.roll(a, rows - 1, axis=0))
        ac = cw[0:1] * a_prev + cw[1:2] * a + cw[2:3] * a_next
        hg = 0.5 * ac[:, half:]
        act = (hg * (1.0 + jnp.tanh(hg))) * ac[:, :half]
        act_ref[:, s * half:(s + 1) * half] = act.astype(BF16)

    def finish():
        y = jnp.dot(act_ref[...], down_bf[...], preferred_element_type=F32)
        cond = cond_base if cond_base == 0 else cond_base + pl.program_id(0)
        gt2 = _mod_rows(mod_ref, cond)[5]
        o_ref[...] = x1_ref[...] + gt2 * _rms(y, ng_ref[0][3:4])

    @pl.when(pl.program_id(0) == 0)
    def _():
        h2 = h2_ref[...]
        start_chunk(0)
        for s in range(nsub):
            if s + 1 < nsub:
                start_chunk(s + 1)
            convert_chunk(s)
            compute_chunk(s, h2)
        finish()

    @pl.when(pl.program_id(0) > 0)
    def _():
        h2 = h2_ref[...]
        for s in range(nsub):
            compute_chunk(s, h2)
        finish()


def _ffn_call(x1, h2, layer, mod_all, wts, seq_len, cond_base):
    n = x1.shape[0]
    half = FFN_SUB // 2

    def resident(shape):
        nd = len(shape)
        return pl.BlockSpec((1,) + tuple(shape[1:]), lambda i: (layer,) + (0,) * (nd - 1),
                            pipeline_mode=pl.Buffered(1))

    return pl.pallas_call(
        functools.partial(_ffn_kernel, seq_len=seq_len, cond_base=cond_base, layer=layer),
        out_shape=jax.ShapeDtypeStruct((n, D_MODEL), F32),
        grid=(n // FFN_ROWS,),
        in_specs=[
            pl.BlockSpec((FFN_ROWS, D_MODEL), lambda i: (i, 0)),
            pl.BlockSpec((FFN_ROWS, D_MODEL), lambda i: (i, 0)),
            resident(mod_all.shape),
            resident(wts["norm_g"].shape),
            resident(wts["ffn_conv"].shape),
            pl.BlockSpec(memory_space=pl.ANY),
            pl.BlockSpec(memory_space=pl.ANY),
        ],
        out_specs=pl.BlockSpec((FFN_ROWS, D_MODEL), lambda i: (i, 0)),
        scratch_shapes=[pltpu.VMEM((D_MODEL, 2 * D_FF), BF16),
                        pltpu.VMEM((D_FF, D_MODEL), BF16),
                        pltpu.VMEM((FFN_ROWS, D_FF), BF16),
                        pltpu.VMEM((2, 2, D_MODEL, half), F32),
                        pltpu.VMEM((2, half, D_MODEL), F32),
                        pltpu.SemaphoreType.DMA((2, 3))],
        compiler_params=pltpu.CompilerParams(
            dimension_semantics=("arbitrary",), vmem_limit_bytes=VMEM_LIMIT),
        name="ffn_t%d" % seq_len,
    )(x1, h2, mod_all, wts["norm_g"], wts["ffn_conv"], wts["ffn_up"], wts["ffn_down"])


def _pos_tables(t_len):
    quarter = D_MODEL // 4
    omega = (1.0 / (np.float32(POS_BASE) ** (np.arange(quarter, dtype=np.float32) / np.float32(quarter))))
    omega = omega.astype(np.float32)

    def emb(p):
        a = p.astype(np.float32)[:, None] * omega[None, :]
        return np.concatenate([np.sin(a), np.cos(a)], axis=-1).astype(np.float32)

    return jnp.asarray(emb(np.arange(t_len // GRID_W))), jnp.asarray(emb(np.arange(GRID_W)))


def _pair_states(state_c, state_n, state_m):
    bs = state_c.shape[0]
    eye = jnp.eye(2, dtype=F32)
    c = state_c.astype(F32).reshape(bs, DEPTH, 2, PAIRS, 2, DH, DH)
    n = state_n.astype(F32).reshape(bs, DEPTH, 2, PAIRS, 2, DH)
    c_bd = jnp.einsum("...jde,jk->...jdke", c, eye).reshape(bs, DEPTH, 2, PAIRS, LANES, LANES)
    n_bd = jnp.einsum("...jd,jk,e->...jdke", n, eye, jnp.ones((DH,), F32)).reshape(
        bs, DEPTH, 2, PAIRS, LANES, LANES)
    m = jnp.pad(state_m.astype(F32), ((0, 0), (0, 0), (0, 0), (0, 16 - HEADS)))
    m = jnp.pad(m.reshape(bs, DEPTH, 1, 32), ((0, 0), (0, 0), (0, 0), (L_MF, LANES - 32 - L_MF)))
    return jnp.concatenate([c_bd, n_bd], axis=-1), m


def kernel(x_prompt, x_sample, state_C, state_n, state_m, c, c_ctx, norm_g, ada_w, ada_b, w_in,
           gate_b, gmlp_ws, gmlp_b, pool_w, pool_scale, conv_w, w_out, ffn_up, ffn_conv, ffn_down):
    bp, tp, _ = x_prompt.shape
    bs, ts, _ = x_sample.shape
    assert tp == CHUNK and ts % CHUNK == 0 and FFN_ROWS % tp == 0 and ts == FFN_ROWS

    conds = jnp.concatenate([c_ctx[None, :], c, jnp.zeros((7 - bs, D_MODEL), F32)], axis=0)
    mod_all = _modulation(conds, ada_w, ada_b)
    win_p, wout_p = _prep_weights(w_in, w_out)

    gb = jnp.pad(gate_b, ((0, 0), (0, 0), (0, 8 - HEADS))).reshape(DEPTH, 1, 32)
    dg = D_GROUP // 4
    wts = {
        "norm_g": norm_g,
        "w_in": win_p,
        "gate_b": jnp.pad(gb, ((0, 0), (0, 0), (0, LANES - 32))),
        "gmlp_ws": gmlp_ws,
        "gmlp_bias": jnp.repeat(jnp.swapaxes(gmlp_b, 1, 2), dg, axis=2),
        "pool_bd": jnp.einsum("lgcd,gh->lgchd", pool_w, jnp.eye(4, dtype=F32)).reshape(
            DEPTH, D_GROUP, D_GROUP).astype(BF16),
        "pool_scale": pool_scale.reshape(DEPTH, 1, D_GROUP),
        "conv_w": conv_w,
        "w_out": wout_p,
        "ffn_up": ffn_up,
        "ffn_conv": ffn_conv,
        "ffn_down": ffn_down,
    }
    pos = _pos_tables(ts)
    state = _pair_states(state_C, state_n, state_m)

    xp, xs = x_prompt, x_sample
    prev, ms = None, []
    for l in range(DEPTH):
        x1, h2, c_all, n_all, m_fin = _mixer_call(xp, l, mod_all, wts, 0, emit_state=True,
                                                  prev_states=prev, nseq=MIXER_ROWS // tp)
        prev = (c_all, n_all)
        xp = _ffn_call(x1.reshape(bp * tp, D_MODEL), h2.reshape(bp * tp, D_MODEL), l, mod_all, wts,
                       tp, 0).reshape(bp, tp, D_MODEL)
        ms.append(jnp.stack([m_fin[:, 0, L_MF:L_MF + HEADS], m_fin[:, 0, L_MB:L_MB + HEADS]], axis=1))

        x1, h2 = _mixer_call(xs, l, mod_all, wts, 1, pos=pos if l == 0 else None, state=state)
        xs = _ffn_call(x1.reshape(bs * ts, D_MODEL), h2.reshape(bs * ts, D_MODEL), l, mod_all, wts,
                       ts, 1).reshape(bs, ts, D_MODEL)

    return (xp, xs, c_all, n_all.reshape(bp, DEPTH, 2, HEADS, DH), jnp.stack(ms, axis=1))
```

```python
import functools

import numpy as np
import jax
import jax.numpy as jnp
from jax import lax
from jax.experimental import pallas as pl
from jax.experimental.pallas import tpu as pltpu

F32 = jnp.float32
BF16 = jnp.bfloat16

D_MODEL = 1024
DEPTH = 2
HEADS = 4
PAIRS = HEADS // 2
DH = 64
D_GROUP = 256
D_FF = 2816
N_MOD = 6
EPS = 1e-6
GRID_W = 64
POS_BASE = 10000.0
D_IN = 2576
N_GATE = 4 * HEADS

CHUNK = 256
GMLP_CHUNK = 128
HALO = 8
LANES = 128

C_Q, C_K, C_V, C_O = 0, 256, 512, 768
C_GU, C_GV, C_PZ, C_CB, C_CC, C_CX = 1024, 1280, 1536, 1792, 2048, 2304
Z_COLS = 2560
C_GATE = Z_COLS
W_COLS = Z_COLS + LANES
Z_GROUP = 512
L_BF, L_MF, L_BB, L_MB = 0, 8, 16, 24

MIXER_ROWS = 1024
FFN_ROWS = 1024
FFN_SUB = 256
VMEM_LIMIT = 60 * 1024 * 1024


def _rms(x, g):
    ms = jnp.mean(x * x, axis=-1, keepdims=True)
    return x * lax.rsqrt(ms + EPS) * g


def _sigmoid(x):
    return 1.0 / (1.0 + jnp.exp(-x))


def _log_sigmoid(x):
    return jnp.minimum(x, 0.0) - jnp.log1p(jnp.exp(-jnp.abs(x)))


def _scan_lanes(x, op, reverse, fill):
    n = x.shape[1]
    lane = lax.broadcasted_iota(jnp.int32, x.shape, 1)
    k = 1
    while k < n:
        if reverse:
            sh = jnp.where(lane < n - k, pltpu.roll(x, n - k, axis=1), fill)
        else:
            sh = jnp.where(lane >= k, pltpu.roll(x, k, axis=1), fill)
        x = op(x, sh)
        k *= 2
    return x


def _shift_down(x, k):
    return pltpu.roll(x, k, axis=0)


def _shift_up(x, k):
    return pltpu.roll(x, x.shape[0] - k, axis=0)


def _prep_win_kernel(wt_ref, sel_ref, o_ref):
    for n in range(Z_COLS // LANES):
        r0 = LANES * n if n < 4 * D_GROUP // LANES else LANES * n + N_GATE
        o_ref[0, :, LANES * n:LANES * (n + 1)] = wt_ref[0, r0:r0 + LANES, :].T.astype(BF16)
    g = wt_ref[0, 4 * D_GROUP:4 * D_GROUP + LANES, :].T.astype(BF16)
    o_ref[0, :, C_GATE:W_COLS] = jnp.dot(g, sel_ref[...], preferred_element_type=F32).astype(BF16)


def _prep_up_kernel(w_ref, o_ref):
    half = FFN_SUB // 2
    for m in range(D_FF // half):
        o_ref[0, :, FFN_SUB * m:FFN_SUB * m + half] = w_ref[0, :, half * m:half * (m + 1)].astype(BF16)
        o_ref[0, :, FFN_SUB * m + half:FFN_SUB * (m + 1)] = (
            w_ref[0, :, D_FF + half * m:D_FF + half * (m + 1)].astype(BF16))


def _cast_kernel(w_ref, o_ref):
    o_ref[...] = w_ref[...].astype(BF16)


def _prep_weights(w_in, w_out, ffn_up, ffn_down):
    sel = np.zeros((LANES, LANES), np.float32)
    for j in range(N_GATE):
        sel[j, 8 * (j // HEADS) + j % HEADS] = 1.0
    params = pltpu.CompilerParams(dimension_semantics=("arbitrary", "arbitrary"),
                                  vmem_limit_bytes=VMEM_LIMIT)
    win_p = pl.pallas_call(
        _prep_win_kernel, out_shape=jax.ShapeDtypeStruct((DEPTH, D_MODEL, W_COLS), BF16),
        grid=(DEPTH, 1),
        in_specs=[pl.BlockSpec((1, D_IN, D_MODEL), lambda l, i: (l, 0, 0)),
                  pl.BlockSpec((LANES, LANES), lambda l, i: (0, 0))],
        out_specs=pl.BlockSpec((1, D_MODEL, W_COLS), lambda l, i: (l, 0, 0)),
        compiler_params=params, name="prep_w_in")(jnp.swapaxes(w_in, 1, 2), jnp.asarray(sel, BF16))
    rb = 128
    up_p = pl.pallas_call(
        _prep_up_kernel, out_shape=jax.ShapeDtypeStruct((DEPTH, D_MODEL, 2 * D_FF), BF16),
        grid=(DEPTH, D_MODEL // rb),
        in_specs=[pl.BlockSpec((1, rb, 2 * D_FF), lambda l, i: (l, i, 0))],
        out_specs=pl.BlockSpec((1, rb, 2 * D_FF), lambda l, i: (l, i, 0)),
        compiler_params=params, name="prep_ffn_up")(ffn_up)

    def cast(w, rows, name):
        _, r, c = w.shape
        return pl.pallas_call(
            _cast_kernel, out_shape=jax.ShapeDtypeStruct(w.shape, BF16),
            grid=(DEPTH, r // rows),
            in_specs=[pl.BlockSpec((1, rows, c), lambda l, i: (l, i, 0))],
            out_specs=pl.BlockSpec((1, rows, c), lambda l, i: (l, i, 0)),
            compiler_params=params, name=name)(w)

    return win_p, cast(w_out, 512, "prep_w_out"), up_p, cast(ffn_down, 704, "prep_ffn_down")


def _mod_kernel(c_ref, w_ref, b_ref, o_ref):
    c = c_ref[...]
    s = (c * _sigmoid(c)).astype(BF16)
    o_ref[0] = jnp.dot(s, w_ref[0].astype(BF16), preferred_element_type=F32) + b_ref[0]


def _modulation(conds, ada_w, ada_b):
    n_out = N_MOD * D_MODEL
    bn = 1536
    return pl.pallas_call(
        _mod_kernel,
        out_shape=jax.ShapeDtypeStruct((DEPTH, 8, n_out), F32),
        grid=(DEPTH, n_out // bn),
        in_specs=[
            pl.BlockSpec((8, D_MODEL), lambda l, j: (0, 0)),
            pl.BlockSpec((1, D_MODEL, bn), lambda l, j: (l, 0, j)),
            pl.BlockSpec((1, 1, bn), lambda l, j: (l, 0, j)),
        ],
        out_specs=pl.BlockSpec((1, 8, bn), lambda l, j: (l, 0, j)),
        compiler_params=pltpu.CompilerParams(
            dimension_semantics=("arbitrary", "arbitrary"), vmem_limit_bytes=VMEM_LIMIT),
        name="adaln_mod",
    )(conds, ada_w, ada_b.reshape(DEPTH, 1, n_out))


def _mod_rows(mod_ref, cond):
    row = mod_ref[0, pl.ds(cond, 1), :]
    return [row[:, i * D_MODEL:(i + 1) * D_MODEL] for i in range(N_MOD)]


def _gate_stats(gates):
    gt = gates.T
    b_f = _scan_lanes(_log_sigmoid(gt[8:16]), jnp.add, False, 0.0)
    g_f = gt[0:8] - b_f
    cm_f = _scan_lanes(g_f, jnp.maximum, False, -jnp.inf)
    b_b = _scan_lanes(_log_sigmoid(gt[24:32]), jnp.add, True, 0.0)
    g_b = gt[16:24] - b_b
    cm_b = _scan_lanes(g_b, jnp.maximum, True, -jnp.inf)
    stack = jnp.concatenate([b_f, cm_f, b_b, cm_b, jnp.zeros((LANES - 32, CHUNK), F32)], axis=0)
    return jnp.concatenate([g_f, g_b], axis=0), stack.T


def _pair_cols(mat, lane0, lane_lt_dh):
    return jnp.where(lane_lt_dh, mat[:, lane0:lane0 + 1], mat[:, lane0 + 1:lane0 + 2])


def _mlstm(z_ref, gcol_ref, grow_ref, cols_ref, bd_ref, ycat_ref, nc, state0, m0, need_final):
    lane_row = lax.broadcasted_iota(jnp.int32, (1, LANES), 1)
    lane = lax.broadcasted_iota(jnp.int32, (CHUNK, LANES), 1)
    lt_dh = lane < DH
    row_ll = lax.broadcasted_iota(jnp.int32, (CHUNK, CHUNK), 0)
    col_ll = lax.broadcasted_iota(jnp.int32, (CHUNK, CHUNK), 1)
    tri = (col_ll <= row_ll, col_ll >= row_ll)
    bd_row = lax.broadcasted_iota(jnp.int32, (LANES, 2 * LANES), 0)
    bd_col = lax.broadcasted_iota(jnp.int32, (LANES, 2 * LANES), 1)
    bd_mask = (bd_row < DH) == ((bd_col & (LANES - 1)) < DH)
    dirs = ((0, L_BF, L_MF, CHUNK - 1), (1, L_BB, L_MB, 0))

    m_start = [[None] * nc, [None] * nc]
    m_fin = [None, None]
    for d, _, _, last in dirs:
        m = m0
        order = range(nc) if d == 0 else range(nc - 1, -1, -1)
        for step, c in enumerate(order):
            m_start[d][c] = m
            if need_final or step < nc - 1:
                cl = cols_ref[c, last:last + 1, :]
                m = pltpu.roll(cl, 8, axis=1) + jnp.maximum(m, cl)
        m_fin[d] = m

    have_state = [[False] * nc, [False] * nc]
    fin = [None, None]
    for d, l_b, l_m, last in dirs:
        bd = list(state0[d])
        order = range(nc) if d == 0 else range(nc - 1, -1, -1)
        for step, c in enumerate(order):
            if bd[0] is not None:
                have_state[d][c] = True
                for p in range(PAIRS):
                    bd_ref[d, c, p] = bd[p].astype(BF16)
            if not (need_final or step < nc - 1):
                continue
            rows = slice(HALO + c * CHUNK, HALO + (c + 1) * CHUNK)
            cols = cols_ref[c]
            m_last = jnp.maximum(cols[last:last + 1, :], m_start[d][c])
            a_col = jnp.exp(gcol_ref[c] - cols - pltpu.roll(m_last, LANES - 8, axis=1))
            a_prev = jnp.exp(m_start[d][c] - m_last)
            for p in range(PAIRS):
                a_pair = _pair_cols(a_col, l_b + 2 * p, lt_dh)
                v_pair = z_ref[rows, C_V + LANES * p:C_V + LANES * (p + 1)]
                rhs = jnp.concatenate([v_pair * a_pair, a_pair], axis=1).astype(BF16)
                k_pair = (z_ref[rows, C_K + LANES * p:C_K + LANES * (p + 1)] * (DH ** -0.5)).astype(BF16)
                upd = lax.dot_general(k_pair, rhs, (((0,), (0,)), ((), ())),
                                      preferred_element_type=F32)
                upd = jnp.where(bd_mask, upd, 0.0)
                if bd[p] is None:
                    bd[p] = upd
                else:
                    scale = jnp.where(bd_row < DH, a_prev[:, l_m + 2 * p:l_m + 2 * p + 1],
                                      a_prev[:, l_m + 2 * p + 1:l_m + 2 * p + 2])
                    bd[p] = scale * bd[p] + upd
        fin[d] = bd

    for c in range(nc):
        rows = slice(HALO + c * CHUNK, HALO + (c + 1) * CHUNK)
        cols = cols_ref[c]
        grow = grow_ref[c]
        m_prev = jnp.where(lane_row < L_BB, m_start[0][c], m_start[1][c])
        m_all = jnp.maximum(cols, m_prev)
        w_inter = jnp.exp(m_prev - m_all)
        e_neg = jnp.exp(-(pltpu.roll(cols, 8, axis=1) + m_all))
        for p in range(PAIRS):
            q_pair = z_ref[rows, C_Q + LANES * p:C_Q + LANES * (p + 1)].astype(BF16)
            k_f32 = z_ref[rows, C_K + LANES * p:C_K + LANES * (p + 1)] * (DH ** -0.5)
            v_pair = z_ref[rows, C_V + LANES * p:C_V + LANES * (p + 1)]
            p_cat = ([], [])
            va = []
            for j in range(2):
                h = 2 * p + j
                own = lt_dh if j == 0 else jnp.logical_not(lt_dh)
                k_own = jnp.where(own, k_f32, 0.0).astype(BF16)
                s = lax.dot_general(q_pair, k_own, (((1,), (1,)), ((), ())),
                                    preferred_element_type=F32)
                va.append(jnp.concatenate([jnp.where(own, v_pair, 0.0), jnp.where(own, 1.0, 0.0)],
                                          axis=1).astype(BF16))
                for d, _, l_m, _ in dirs:
                    w = jnp.exp(jnp.where(tri[d], grow[8 * d + h:8 * d + h + 1] - m_all[:, l_m + h:l_m + h + 1],
                                          -jnp.inf))
                    p_cat[d].append((s * w).astype(BF16))
            va = jnp.concatenate(va, axis=0)
            h_sum = None
            for d, _, l_m, _ in dirs:
                tot = jnp.dot(jnp.concatenate(p_cat[d], axis=1), va, preferred_element_type=F32)
                if have_state[d][c]:
                    e_pair = _pair_cols(w_inter, l_m + 2 * p, lt_dh)
                    qc = jnp.dot(q_pair, bd_ref[d, c, p], preferred_element_type=F32)
                    tot = tot + jnp.concatenate([e_pair, e_pair], axis=1) * qc
                floor = _pair_cols(e_neg, l_m + 2 * p, lt_dh)
                h_dir = tot[:, :LANES] / jnp.maximum(jnp.abs(tot[:, LANES:]), floor)
                h_sum = h_dir if h_sum is None else h_sum + h_dir
            o_pair = z_ref[rows, C_O + LANES * p:C_O + LANES * (p + 1)]
            ycat_ref[c * CHUNK:(c + 1) * CHUNK, LANES * p:LANES * (p + 1)] = (
                _sigmoid(o_pair) * h_sum).astype(BF16)

    m_row = jnp.where(lane_row < L_BB, m_fin[0], m_fin[1])
    return fin, m_row


def _mixer_kernel(*refs, seq_len, nseq, add_pos, has_state, emit_state, cond_base, n_prev):
    it = iter(refs)
    x_ref = next(it)
    if add_pos:
        er_ref = next(it)
        ec_ref = next(it)
    mod_ref = next(it)
    ng_ref = next(it)
    win_ref = next(it)
    gb_ref = next(it)
    ws_ref = next(it)
    gbias_ref = next(it)
    pbd_ref = next(it)
    psc_ref = next(it)
    cw_ref = next(it)
    wout_ref = next(it)
    if has_state:
        bd0_ref = next(it)
        m0_ref = next(it)
    if n_prev:
        cprev_ref = next(it)
        nprev_ref = next(it)
    x1_ref = next(it)
    h2_ref = next(it)
    if emit_state:
        cout_ref = next(it)
        nout_ref = next(it)
        mout_ref = next(it)
    z_all = next(it)
    gcol_all = next(it)
    grow_all = next(it)
    cols_all = next(it)
    bd_all = next(it)
    ycat_all = next(it)

    nblk = seq_len // CHUNK
    cond = cond_base if cond_base == 0 else cond_base + pl.program_id(0)
    sh1, sc1, gt1, sh2, sc2, _ = _mod_rows(mod_ref, cond)
    ng = ng_ref[0]
    seqs = range(nseq)
    units = [(q, r) for q in seqs for r in range(nblk)]

    zeros_halo = jnp.zeros((HALO, Z_COLS - C_PZ), F32)
    for q in seqs:
        z_all[q, 0:HALO, C_PZ:Z_COLS] = zeros_halo
        z_all[q, HALO + seq_len:2 * HALO + seq_len, C_PZ:Z_COLS] = zeros_halo

    g_in = ng[0:1] * (1.0 + sc1)
    hbs = []
    for q, r in units:
        rows = slice(r * CHUNK, (r + 1) * CHUNK)
        xb = x_ref[q, rows, :]
        if add_pos:
            pieces = []
            for g in range(CHUNK // GRID_W):
                gi = r * (CHUNK // GRID_W) + g
                er = jnp.broadcast_to(er_ref[gi:gi + 1, :], (GRID_W, D_MODEL // 2))
                pieces.append(jnp.concatenate([er, ec_ref[...]], axis=1))
            xb = xb + jnp.concatenate(pieces, axis=0)
            x1_ref[q, rows, :] = xb
        hbs.append((_rms(xb, g_in) + sh1).astype(BF16))
    hb = jnp.concatenate(hbs, axis=0)
    gates_all = jnp.dot(hb, win_ref[0, :, C_GATE:W_COLS], preferred_element_type=F32) + gb_ref[0]
    for cg in range(Z_COLS // Z_GROUP):
        zc = jnp.dot(hb, win_ref[0, :, cg * Z_GROUP:(cg + 1) * Z_GROUP], preferred_element_type=F32)
        for u, (q, r) in enumerate(units):
            z_all[q, HALO + r * CHUNK:HALO + (r + 1) * CHUNK, cg * Z_GROUP:(cg + 1) * Z_GROUP] = (
                zc[u * CHUNK:(u + 1) * CHUNK])
    for u, (q, r) in enumerate(units):
        gates = gates_all[u * CHUNK:(u + 1) * CHUNK]
        gcol_all[q, r] = gates
        grow, cols = _gate_stats(gates)
        grow_all[q, r] = grow
        cols_all[q, r] = cols

    for q in seqs:
        if has_state:
            state0 = [[bd0_ref[q, 0, d, p] for p in range(PAIRS)] for d in range(2)]
            m0 = m0_ref[q, 0]
        else:
            state0 = [[None] * PAIRS, [None] * PAIRS]
            m0 = jnp.zeros((1, LANES), F32)
        fin, m_row = _mlstm(z_all.at[q], gcol_all.at[q], grow_all.at[q], cols_all.at[q], bd_all.at[q],
                            ycat_all.at[q], nblk, state0, m0, need_final=emit_state)
        if emit_state:
            for d in range(2):
                for p in range(PAIRS):
                    for j in range(2):
                        cout_ref[q, n_prev, d, 2 * p + j] = fin[d][p][DH * j:DH * (j + 1), DH * j:DH * (j + 1)]
                    n_t = fin[d][p][:, LANES:].T
                    nout_ref[q, n_prev, d:d + 1, LANES * p:LANES * (p + 1)] = n_t[0:1] + n_t[DH:DH + 1]
            mout_ref[q] = m_row
    if emit_state and n_prev:
        cout_ref[:, 0:n_prev] = cprev_ref[...]
        nout_ref[:, 0:n_prev] = nprev_ref[...]

    lane_g = lax.broadcasted_iota(jnp.int32, (CHUNK, D_GROUP), 1) // (D_GROUP // 4)
    lg = lax.broadcasted_iota(jnp.int32, (GMLP_CHUNK, D_GROUP), 1) // (D_GROUP // 4)
    row_c = lax.broadcasted_iota(jnp.int32, (CHUNK, D_GROUP), 0)
    half = jnp.where(lane_g == 0, 1, jnp.where(lane_g == 1, 2, jnp.where(lane_g == 2, 4, 8)))
    cw = cw_ref[0]
    ws_cat = jnp.concatenate([ws_ref[0, g] for g in range(4)], axis=1).astype(BF16)
    for q, r in units:
        z_ref = z_all.at[q]
        ycat_ref = ycat_all.at[q]
        rows = slice(r * CHUNK, (r + 1) * CHUNK)
        zrows = slice(HALO + r * CHUNK, HALO + (r + 1) * CHUNK)
        hrows = slice(r * CHUNK, (r + 1) * CHUNK + 2 * HALO)
        inner = slice(HALO, HALO + CHUNK)

        for s in range(CHUNK // GMLP_CHUNK):
            zs = slice(HALO + r * CHUNK + s * GMLP_CHUNK, HALO + r * CHUNK + (s + 1) * GMLP_CHUNK)
            vch = z_ref[zs, C_GV:C_GV + D_GROUP]
            v_stack = jnp.concatenate([jnp.where(lg == g, vch, 0.0) for g in range(4)], axis=0)
            mixed = gbias_ref[0] + jnp.dot(ws_cat, v_stack.astype(BF16), preferred_element_type=F32)
            yb = z_ref[zs, C_GU:C_GU + D_GROUP] * mixed
            ys = slice(r * CHUNK + s * GMLP_CHUNK, r * CHUNK + (s + 1) * GMLP_CHUNK)
            ycat_ref[ys, D_GROUP:2 * D_GROUP] = yb.astype(BF16)

        pz = z_ref[hrows, C_PZ:C_PZ + D_GROUP]
        a2 = pz + _shift_down(pz, 1)
        a4 = a2 + _shift_down(a2, 2)
        a8 = a4 + _shift_down(a4, 4)
        a16 = a8 + _shift_down(a8, 8)
        win = [a2[inner], _shift_up(a4, 1)[inner], _shift_up(a8, 3)[inner], _shift_up(a16, 7)[inner]]
        wsum = jnp.where(lane_g == 0, win[0], jnp.where(lane_g == 1, win[1],
                         jnp.where(lane_g == 2, win[2], win[3])))
        t_abs = row_c + r * CHUNK
        cnt = (jnp.minimum(t_abs + half, seq_len) - jnp.maximum(t_abs - half, 0)).astype(F32)
        pooled = wsum / cnt - pz[inner]
        yc = jnp.dot(pooled.astype(BF16), pbd_ref[0], preferred_element_type=F32) * psc_ref[0]
        ycat_ref[rows, 2 * D_GROUP:3 * D_GROUP] = yc.astype(BF16)

        u = z_ref[hrows, C_CC:C_CC + D_GROUP] * z_ref[hrows, C_CX:C_CX + D_GROUP]
        conv = (cw[0:1] * _shift_down(u, 1)[inner] + cw[1:2] * u[inner]
                + cw[2:3] * _shift_up(u, 1)[inner])
        yd = z_ref[zrows, C_CB:C_CB + D_GROUP] * conv
        ycat_ref[rows, 3 * D_GROUP:4 * D_GROUP] = yd.astype(BF16)

    y_all = jnp.dot(jnp.concatenate([ycat_all[q, r * CHUNK:(r + 1) * CHUNK, :] for q, r in units], axis=0),
                    wout_ref[0], preferred_element_type=F32)
    g_y = gt1 * ng[1:2]
    g_h2 = ng[2:3] * (1.0 + sc2)
    for u, (q, r) in enumerate(units):
        rows = slice(r * CHUNK, (r + 1) * CHUNK)
        xb = x1_ref[q, rows, :] if add_pos else x_ref[q, rows, :]
        x1 = xb + _rms(y_all[u * CHUNK:(u + 1) * CHUNK], g_y)
        x1_ref[q, rows, :] = x1
        h2_ref[q, rows, :] = (_rms(x1, g_h2) + sh2).astype(BF16)


def _layer_spec(shape, layer):
    nd = len(shape)
    return pl.BlockSpec((1,) + tuple(shape[1:]), lambda b: (layer,) + (0,) * (nd - 1),
                        pipeline_mode=pl.Buffered(1))


def _mixer_call(x, layer, mod_all, wts, cond_base, pos=None, state=None, emit_state=False,
                prev_states=None, nseq=1):
    bsz, seq_len, _ = x.shape
    nblk = seq_len // CHUNK
    add_pos = pos is not None
    has_state = state is not None
    n_prev = 0 if prev_states is None else prev_states[0].shape[1]
    args = [x]
    in_specs = [pl.BlockSpec((nseq, seq_len, D_MODEL), lambda b: (b, 0, 0))]
    if add_pos:
        args += list(pos)
        in_specs += [pl.BlockSpec(p.shape, lambda b: (0, 0), pipeline_mode=pl.Buffered(1)) for p in pos]
    args.append(mod_all)
    in_specs.append(_layer_spec(mod_all.shape, layer))
    for name in ("norm_g", "w_in", "gate_b", "gmlp_ws", "gmlp_bias", "pool_bd", "pool_scale",
                 "conv_w", "w_out"):
        args.append(wts[name])
        in_specs.append(_layer_spec(wts[name].shape, layer))
    if has_state:
        args += list(state)
        in_specs += [pl.BlockSpec((nseq, 1, 2, PAIRS, LANES, 2 * LANES), lambda b: (b, layer, 0, 0, 0, 0)),
                     pl.BlockSpec((nseq, 1, 1, LANES), lambda b: (b, layer, 0, 0))]
    out_shape = [jax.ShapeDtypeStruct((bsz, seq_len, D_MODEL), F32),
                 jax.ShapeDtypeStruct((bsz, seq_len, D_MODEL), BF16)]
    out_specs = [pl.BlockSpec((nseq, seq_len, D_MODEL), lambda b: (b, 0, 0)),
                 pl.BlockSpec((nseq, seq_len, D_MODEL), lambda b: (b, 0, 0))]
    if emit_state:
        if n_prev:
            args += list(prev_states)
            in_specs += [pl.BlockSpec((nseq, n_prev, 2, HEADS, DH, DH), lambda b: (b, 0, 0, 0, 0, 0)),
                         pl.BlockSpec((nseq, n_prev, 2, HEADS * DH), lambda b: (b, 0, 0, 0))]
        out_shape += [jax.ShapeDtypeStruct((bsz, n_prev + 1, 2, HEADS, DH, DH), F32),
                      jax.ShapeDtypeStruct((bsz, n_prev + 1, 2, HEADS * DH), F32),
                      jax.ShapeDtypeStruct((bsz, 1, LANES), F32)]
        out_specs += [pl.BlockSpec((nseq, n_prev + 1, 2, HEADS, DH, DH), lambda b: (b, 0, 0, 0, 0, 0)),
                      pl.BlockSpec((nseq, n_prev + 1, 2, HEADS * DH), lambda b: (b, 0, 0, 0)),
                      pl.BlockSpec((nseq, 1, LANES), lambda b: (b, 0, 0))]
    kern = functools.partial(_mixer_kernel, seq_len=seq_len, nseq=nseq, add_pos=add_pos,
                             has_state=has_state, emit_state=emit_state, cond_base=cond_base,
                             n_prev=n_prev)
    return pl.pallas_call(
        kern,
        out_shape=out_shape,
        grid=(bsz // nseq,),
        in_specs=in_specs,
        out_specs=out_specs,
        scratch_shapes=[pltpu.VMEM((nseq, seq_len + 2 * HALO, Z_COLS), F32),
                        pltpu.VMEM((nseq, nblk, CHUNK, LANES), F32),
                        pltpu.VMEM((nseq, nblk, 16, CHUNK), F32),
                        pltpu.VMEM((nseq, nblk, CHUNK, LANES), F32),
                        pltpu.VMEM((nseq, 2, nblk, PAIRS, LANES, 2 * LANES), BF16),
                        pltpu.VMEM((nseq, seq_len, D_MODEL), BF16)],
        compiler_params=pltpu.CompilerParams(
            dimension_semantics=("arbitrary",), vmem_limit_bytes=VMEM_LIMIT),
        name="mixer_t%d" % seq_len,
    )(*args)


def _ffn_kernel(x1_ref, h2_ref, mod_ref, ng_ref, up_ref, cw_ref, down_ref, o_ref, act_ref,
                *, seq_len, cond_base):
    rows = x1_ref.shape[0]
    half = FFN_SUB // 2
    t_in_seq = lax.broadcasted_iota(jnp.int32, (rows, FFN_SUB), 0) & (seq_len - 1)
    first = t_in_seq == 0
    last = t_in_seq == seq_len - 1
    h2 = h2_ref[...]
    for s in range(D_FF // half):
        a = jnp.dot(h2, up_ref[0, :, s * FFN_SUB:(s + 1) * FFN_SUB],
                    preferred_element_type=F32)
        cw = jnp.concatenate([cw_ref[0, :, s * half:(s + 1) * half],
                              cw_ref[0, :, D_FF + s * half:D_FF + (s + 1) * half]], axis=1)
        a_prev = jnp.where(first, 0.0, pltpu.roll(a, 1, axis=0))
        a_next = jnp.where(last, 0.0, pltpu.roll(a, rows - 1, axis=0))
        ac = cw[0:1] * a_prev + cw[1:2] * a + cw[2:3] * a_next
        hg = 0.5 * ac[:, half:]
        act = (hg * (1.0 + jnp.tanh(hg))) * ac[:, :half]
        act_ref[:, s * half:(s + 1) * half] = act.astype(BF16)
    y = jnp.dot(act_ref[...], down_ref[0], preferred_element_type=F32)
    cond = cond_base if cond_base == 0 else cond_base + pl.program_id(0)
    gt2 = _mod_rows(mod_ref, cond)[5]
    o_ref[...] = x1_ref[...] + gt2 * _rms(y, ng_ref[0][3:4])


def _ffn_call(x1, h2, layer, mod_all, wts, seq_len, cond_base):
    n = x1.shape[0]

    def resident(shape):
        nd = len(shape)
        return pl.BlockSpec((1,) + tuple(shape[1:]), lambda i: (layer,) + (0,) * (nd - 1),
                            pipeline_mode=pl.Buffered(1))

    return pl.pallas_call(
        functools.partial(_ffn_kernel, seq_len=seq_len, cond_base=cond_base),
        out_shape=jax.ShapeDtypeStruct((n, D_MODEL), F32),
        grid=(n // FFN_ROWS,),
        in_specs=[
            pl.BlockSpec((FFN_ROWS, D_MODEL), lambda i: (i, 0)),
            pl.BlockSpec((FFN_ROWS, D_MODEL), lambda i: (i, 0)),
            resident(mod_all.shape),
            resident(wts["norm_g"].shape),
            resident(wts["ffn_up"].shape),
            resident(wts["ffn_conv"].shape),
            resident(wts["ffn_down"].shape),
        ],
        out_specs=pl.BlockSpec((FFN_ROWS, D_MODEL), lambda i: (i, 0)),
        scratch_shapes=[pltpu.VMEM((FFN_ROWS, D_FF), BF16)],
        compiler_params=pltpu.CompilerParams(
            dimension_semantics=("arbitrary",), vmem_limit_bytes=VMEM_LIMIT),
        name="ffn_t%d" % seq_len,
    )(x1, h2, mod_all, wts["norm_g"], wts["ffn_up"], wts["ffn_conv"], wts["ffn_down"])


def _pos_tables(t_len):
    quarter = D_MODEL // 4
    omega = (1.0 / (np.float32(POS_BASE) ** (np.arange(quarter, dtype=np.float32) / np.float32(quarter))))
    omega = omega.astype(np.float32)

    def emb(p):
        a = p.astype(np.float32)[:, None] * omega[None, :]
        return np.concatenate([np.sin(a), np.cos(a)], axis=-1).astype(np.float32)

    return jnp.asarray(emb(np.arange(t_len // GRID_W))), jnp.asarray(emb(np.arange(GRID_W)))


def _pair_states(state_c, state_n, state_m):
    bs = state_c.shape[0]
    eye = jnp.eye(2, dtype=F32)
    c = state_c.astype(F32).reshape(bs, DEPTH, 2, PAIRS, 2, DH, DH)
    n = state_n.astype(F32).reshape(bs, DEPTH, 2, PAIRS, 2, DH)
    c_bd = jnp.einsum("...jde,jk->...jdke", c, eye).reshape(bs, DEPTH, 2, PAIRS, LANES, LANES)
    n_bd = jnp.einsum("...jd,jk,e->...jdke", n, eye, jnp.ones((DH,), F32)).reshape(
        bs, DEPTH, 2, PAIRS, LANES, LANES)
    m = jnp.pad(state_m.astype(F32), ((0, 0), (0, 0), (0, 0), (0, 16 - HEADS)))
    m = jnp.pad(m.reshape(bs, DEPTH, 1, 32), ((0, 0), (0, 0), (0, 0), (L_MF, LANES - 32 - L_MF)))
    return jnp.concatenate([c_bd, n_bd], axis=-1), m


def kernel(x_prompt, x_sample, state_C, state_n, state_m, c, c_ctx, norm_g, ada_w, ada_b, w_in,
           gate_b, gmlp_ws, gmlp_b, pool_w, pool_scale, conv_w, w_out, ffn_up, ffn_conv, ffn_down):
    bp, tp, _ = x_prompt.shape
    bs, ts, _ = x_sample.shape
    assert tp == CHUNK and ts % CHUNK == 0 and FFN_ROWS % tp == 0 and ts == FFN_ROWS

    conds = jnp.concatenate([c_ctx[None, :], c, jnp.zeros((7 - bs, D_MODEL), F32)], axis=0)
    mod_all = _modulation(conds, ada_w, ada_b)
    win_p, wout_p, up_p, down_p = _prep_weights(w_in, w_out, ffn_up, ffn_down)

    gb = jnp.pad(gate_b, ((0, 0), (0, 0), (0, 8 - HEADS))).reshape(DEPTH, 1, 32)
    dg = D_GROUP // 4
    wts = {
        "norm_g": norm_g,
        "w_in": win_p,
        "gate_b": jnp.pad(gb, ((0, 0), (0, 0), (0, LANES - 32))),
        "gmlp_ws": gmlp_ws,
        "gmlp_bias": jnp.repeat(jnp.swapaxes(gmlp_b, 1, 2), dg, axis=2),
        "pool_bd": jnp.einsum("lgcd,gh->lgchd", pool_w, jnp.eye(4, dtype=F32)).reshape(
            DEPTH, D_GROUP, D_GROUP).astype(BF16),
        "pool_scale": pool_scale.reshape(DEPTH, 1, D_GROUP),
        "conv_w": conv_w,
        "w_out": wout_p,
        "ffn_up": up_p,
        "ffn_conv": ffn_conv,
        "ffn_down": down_p,
    }
    pos = _pos_tables(ts)
    state = _pair_states(state_C, state_n, state_m)

    xp, xs = x_prompt, x_sample
    prev, ms = None, []
    for l in range(DEPTH):
        x1, h2, c_all, n_all, m_fin = _mixer_call(xp, l, mod_all, wts, 0, emit_state=True,
                                                  prev_states=prev, nseq=MIXER_ROWS // tp)
        prev = (c_all, n_all)
        xp = _ffn_call(x1.reshape(bp * tp, D_MODEL), h2.reshape(bp * tp, D_MODEL), l, mod_all, wts,
                       tp, 0).reshape(bp, tp, D_MODEL)
        ms.append(jnp.stack([m_fin[:, 0, L_MF:L_MF + HEADS], m_fin[:, 0, L_MB:L_MB + HEADS]], axis=1))

        x1, h2 = _mixer_call(xs, l, mod_all, wts, 1, pos=pos if l == 0 else None, state=state)
        xs = _ffn_call(x1.reshape(bs * ts, D_MODEL), h2.reshape(bs * ts, D_MODEL), l, mod_all, wts,
                       ts, 1).reshape(bs, ts, D_MODEL)

    return (xp, xs, c_all, n_all.reshape(bp, DEPTH, 2, HEADS, DH), jnp.stack(ms, axis=1))
```

```python
import functools

import numpy as np
import jax
import jax.numpy as jnp
from jax import lax
from jax.experimental import pallas as pl
from jax.experimental.pallas import tpu as pltpu

F32 = jnp.float32
BF16 = jnp.bfloat16

D_MODEL = 1024
DEPTH = 2
HEADS = 4
PAIRS = HEADS // 2
DH = 64
D_GROUP = 256
D_FF = 2816
N_MOD = 6
EPS = 1e-6
GRID_W = 64
POS_BASE = 10000.0
D_IN = 2576
N_GATE = 4 * HEADS

CHUNK = 256
GMLP_CHUNK = 128
HALO = 8
LANES = 128

C_Q, C_K, C_V, C_O = 0, 256, 512, 768
C_GU, C_GV, C_PZ, C_CB, C_CC, C_CX = 1024, 1280, 1536, 1792, 2048, 2304
Z_COLS = 2560
C_GATE = Z_COLS
W_COLS = Z_COLS + LANES
Z_GROUP = 512
L_BF, L_MF, L_BB, L_MB = 0, 8, 16, 24

MIXER_ROWS = 1024
FFN_ROWS = 1024
FFN_SUB = 256
VMEM_LIMIT = 60 * 1024 * 1024


def _rms(x, g):
    ms = jnp.mean(x * x, axis=-1, keepdims=True)
    return x * lax.rsqrt(ms + EPS) * g


def _sigmoid(x):
    return 1.0 / (1.0 + jnp.exp(-x))


def _log_sigmoid(x):
    return jnp.minimum(x, 0.0) - jnp.log1p(jnp.exp(-jnp.abs(x)))


def _scan_lanes(x, op, reverse, fill):
    n = x.shape[1]
    lane = lax.broadcasted_iota(jnp.int32, x.shape, 1)
    k = 1
    while k < n:
        if reverse:
            sh = jnp.where(lane < n - k, pltpu.roll(x, n - k, axis=1), fill)
        else:
            sh = jnp.where(lane >= k, pltpu.roll(x, k, axis=1), fill)
        x = op(x, sh)
        k *= 2
    return x


def _shift_down(x, k):
    return pltpu.roll(x, k, axis=0)


def _shift_up(x, k):
    return pltpu.roll(x, x.shape[0] - k, axis=0)


def _prep_win_kernel(wt_ref, sel_ref, o_ref):
    for n in range(Z_COLS // LANES):
        r0 = LANES * n if n < 4 * D_GROUP // LANES else LANES * n + N_GATE
        o_ref[0, :, LANES * n:LANES * (n + 1)] = wt_ref[0, r0:r0 + LANES, :].T.astype(BF16)
    g = wt_ref[0, 4 * D_GROUP:4 * D_GROUP + LANES, :].T.astype(BF16)
    o_ref[0, :, C_GATE:W_COLS] = jnp.dot(g, sel_ref[...], preferred_element_type=F32).astype(BF16)


def _cast_kernel(w_ref, o_ref):
    o_ref[...] = w_ref[...].astype(BF16)


def _prep_weights(w_in, w_out):
    sel = np.zeros((LANES, LANES), np.float32)
    for j in range(N_GATE):
        sel[j, 8 * (j // HEADS) + j % HEADS] = 1.0
    params = pltpu.CompilerParams(dimension_semantics=("arbitrary", "arbitrary"),
                                  vmem_limit_bytes=VMEM_LIMIT)
    win_p = pl.pallas_call(
        _prep_win_kernel, out_shape=jax.ShapeDtypeStruct((DEPTH, D_MODEL, W_COLS), BF16),
        grid=(DEPTH, 1),
        in_specs=[pl.BlockSpec((1, D_IN, D_MODEL), lambda l, i: (l, 0, 0)),
                  pl.BlockSpec((LANES, LANES), lambda l, i: (0, 0))],
        out_specs=pl.BlockSpec((1, D_MODEL, W_COLS), lambda l, i: (l, 0, 0)),
        compiler_params=params, name="prep_w_in")(jnp.swapaxes(w_in, 1, 2), jnp.asarray(sel, BF16))
    rows = 512
    wout_p = pl.pallas_call(
        _cast_kernel, out_shape=jax.ShapeDtypeStruct(w_out.shape, BF16),
        grid=(DEPTH, D_MODEL // rows),
        in_specs=[pl.BlockSpec((1, rows, D_MODEL), lambda l, i: (l, i, 0))],
        out_specs=pl.BlockSpec((1, rows, D_MODEL), lambda l, i: (l, i, 0)),
        compiler_params=params, name="prep_w_out")(w_out)
    return win_p, wout_p


def _mod_kernel(c_ref, w_ref, b_ref, o_ref):
    c = c_ref[...]
    s = (c * _sigmoid(c)).astype(BF16)
    o_ref[0] = jnp.dot(s, w_ref[0].astype(BF16), preferred_element_type=F32) + b_ref[0]


def _modulation(conds, ada_w, ada_b):
    n_out = N_MOD * D_MODEL
    bn = 1536
    return pl.pallas_call(
        _mod_kernel,
        out_shape=jax.ShapeDtypeStruct((DEPTH, 8, n_out), F32),
        grid=(DEPTH, n_out // bn),
        in_specs=[
            pl.BlockSpec((8, D_MODEL), lambda l, j: (0, 0)),
            pl.BlockSpec((1, D_MODEL, bn), lambda l, j: (l, 0, j)),
            pl.BlockSpec((1, 1, bn), lambda l, j: (l, 0, j)),
        ],
        out_specs=pl.BlockSpec((1, 8, bn), lambda l, j: (l, 0, j)),
        compiler_params=pltpu.CompilerParams(
            dimension_semantics=("arbitrary", "arbitrary"), vmem_limit_bytes=VMEM_LIMIT),
        name="adaln_mod",
    )(conds, ada_w, ada_b.reshape(DEPTH, 1, n_out))


def _mod_rows(mod_ref, cond):
    row = mod_ref[0, pl.ds(cond, 1), :]
    return [row[:, i * D_MODEL:(i + 1) * D_MODEL] for i in range(N_MOD)]


def _gate_stats(gates):
    gt = gates.T
    b_f = _scan_lanes(_log_sigmoid(gt[8:16]), jnp.add, False, 0.0)
    g_f = gt[0:8] - b_f
    cm_f = _scan_lanes(g_f, jnp.maximum, False, -jnp.inf)
    b_b = _scan_lanes(_log_sigmoid(gt[24:32]), jnp.add, True, 0.0)
    g_b = gt[16:24] - b_b
    cm_b = _scan_lanes(g_b, jnp.maximum, True, -jnp.inf)
    stack = jnp.concatenate([b_f, cm_f, b_b, cm_b, jnp.zeros((LANES - 32, CHUNK), F32)], axis=0)
    return jnp.concatenate([g_f, g_b], axis=0), stack.T


def _pair_cols(mat, lane0, lane_lt_dh):
    return jnp.where(lane_lt_dh, mat[:, lane0:lane0 + 1], mat[:, lane0 + 1:lane0 + 2])


def _mlstm(z_ref, gcol_ref, grow_ref, cols_ref, bd_ref, ycat_ref, nc, state0, m0, need_final):
    lane_row = lax.broadcasted_iota(jnp.int32, (1, LANES), 1)
    lane = lax.broadcasted_iota(jnp.int32, (CHUNK, LANES), 1)
    lt_dh = lane < DH
    row_ll = lax.broadcasted_iota(jnp.int32, (CHUNK, CHUNK), 0)
    col_ll = lax.broadcasted_iota(jnp.int32, (CHUNK, CHUNK), 1)
    tri = (col_ll <= row_ll, col_ll >= row_ll)
    bd_row = lax.broadcasted_iota(jnp.int32, (LANES, 2 * LANES), 0)
    bd_col = lax.broadcasted_iota(jnp.int32, (LANES, 2 * LANES), 1)
    bd_mask = (bd_row < DH) == ((bd_col & (LANES - 1)) < DH)
    dirs = ((0, L_BF, L_MF, CHUNK - 1), (1, L_BB, L_MB, 0))

    m_start = [[None] * nc, [None] * nc]
    m_fin = [None, None]
    for d, _, _, last in dirs:
        m = m0
        order = range(nc) if d == 0 else range(nc - 1, -1, -1)
        for step, c in enumerate(order):
            m_start[d][c] = m
            if need_final or step < nc - 1:
                cl = cols_ref[c, last:last + 1, :]
                m = pltpu.roll(cl, 8, axis=1) + jnp.maximum(m, cl)
        m_fin[d] = m

    have_state = [[False] * nc, [False] * nc]
    fin = [None, None]
    for d, l_b, l_m, last in dirs:
        bd = list(state0[d])
        order = range(nc) if d == 0 else range(nc - 1, -1, -1)
        for step, c in enumerate(order):
            if bd[0] is not None:
                have_state[d][c] = True
                for p in range(PAIRS):
                    bd_ref[d, c, p] = bd[p].astype(BF16)
            if not (need_final or step < nc - 1):
                continue
            rows = slice(HALO + c * CHUNK, HALO + (c + 1) * CHUNK)
            cols = cols_ref[c]
            m_last = jnp.maximum(cols[last:last + 1, :], m_start[d][c])
            a_col = jnp.exp(gcol_ref[c] - cols - pltpu.roll(m_last, LANES - 8, axis=1))
            a_prev = jnp.exp(m_start[d][c] - m_last)
            for p in range(PAIRS):
                a_pair = _pair_cols(a_col, l_b + 2 * p, lt_dh)
                v_pair = z_ref[rows, C_V + LANES * p:C_V + LANES * (p + 1)]
                rhs = jnp.concatenate([v_pair * a_pair, a_pair], axis=1).astype(BF16)
                k_pair = (z_ref[rows, C_K + LANES * p:C_K + LANES * (p + 1)] * (DH ** -0.5)).astype(BF16)
                upd = lax.dot_general(k_pair, rhs, (((0,), (0,)), ((), ())),
                                      preferred_element_type=F32)
                upd = jnp.where(bd_mask, upd, 0.0)
                if bd[p] is None:
                    bd[p] = upd
                else:
                    scale = jnp.where(bd_row < DH, a_prev[:, l_m + 2 * p:l_m + 2 * p + 1],
                                      a_prev[:, l_m + 2 * p + 1:l_m + 2 * p + 2])
                    bd[p] = scale * bd[p] + upd
        fin[d] = bd

    for c in range(nc):
        rows = slice(HALO + c * CHUNK, HALO + (c + 1) * CHUNK)
        cols = cols_ref[c]
        grow = grow_ref[c]
        m_prev = jnp.where(lane_row < L_BB, m_start[0][c], m_start[1][c])
        m_all = jnp.maximum(cols, m_prev)
        w_inter = jnp.exp(m_prev - m_all)
        e_neg = jnp.exp(-(pltpu.roll(cols, 8, axis=1) + m_all))
        for p in range(PAIRS):
            q_pair = z_ref[rows, C_Q + LANES * p:C_Q + LANES * (p + 1)].astype(BF16)
            k_f32 = z_ref[rows, C_K + LANES * p:C_K + LANES * (p + 1)] * (DH ** -0.5)
            v_pair = z_ref[rows, C_V + LANES * p:C_V + LANES * (p + 1)]
            p_cat = ([], [])
            va = []
            for j in range(2):
                h = 2 * p + j
                own = lt_dh if j == 0 else jnp.logical_not(lt_dh)
                k_own = jnp.where(own, k_f32, 0.0).astype(BF16)
                s = lax.dot_general(q_pair, k_own, (((1,), (1,)), ((), ())),
                                    preferred_element_type=F32)
                va.append(jnp.concatenate([jnp.where(own, v_pair, 0.0), jnp.where(own, 1.0, 0.0)],
                                          axis=1).astype(BF16))
                for d, _, l_m, _ in dirs:
                    w = jnp.exp(jnp.where(tri[d], grow[8 * d + h:8 * d + h + 1] - m_all[:, l_m + h:l_m + h + 1],
                                          -jnp.inf))
                    p_cat[d].append((s * w).astype(BF16))
            va = jnp.concatenate(va, axis=0)
            h_sum = None
            for d, _, l_m, _ in dirs:
                tot = jnp.dot(jnp.concatenate(p_cat[d], axis=1), va, preferred_element_type=F32)
                if have_state[d][c]:
                    e_pair = _pair_cols(w_inter, l_m + 2 * p, lt_dh)
                    qc = jnp.dot(q_pair, bd_ref[d, c, p], preferred_element_type=F32)
                    tot = tot + jnp.concatenate([e_pair, e_pair], axis=1) * qc
                floor = _pair_cols(e_neg, l_m + 2 * p, lt_dh)
                h_dir = tot[:, :LANES] / jnp.maximum(jnp.abs(tot[:, LANES:]), floor)
                h_sum = h_dir if h_sum is None else h_sum + h_dir
            o_pair = z_ref[rows, C_O + LANES * p:C_O + LANES * (p + 1)]
            ycat_ref[c * CHUNK:(c + 1) * CHUNK, LANES * p:LANES * (p + 1)] = (
                _sigmoid(o_pair) * h_sum).astype(BF16)

    m_row = jnp.where(lane_row < L_BB, m_fin[0], m_fin[1])
    return fin, m_row


def _mixer_kernel(*refs, seq_len, nseq, add_pos, has_state, emit_state, cond_base, n_prev):
    it = iter(refs)
    x_ref = next(it)
    if add_pos:
        er_ref = next(it)
        ec_ref = next(it)
    mod_ref = next(it)
    ng_ref = next(it)
    win_ref = next(it)
    gb_ref = next(it)
    ws_ref = next(it)
    gbias_ref = next(it)
    pbd_ref = next(it)
    psc_ref = next(it)
    cw_ref = next(it)
    wout_ref = next(it)
    if has_state:
        bd0_ref = next(it)
        m0_ref = next(it)
    if n_prev:
        cprev_ref = next(it)
        nprev_ref = next(it)
    x1_ref = next(it)
    h2_ref = next(it)
    if emit_state:
        cout_ref = next(it)
        nout_ref = next(it)
        mout_ref = next(it)
    z_all = next(it)
    gcol_all = next(it)
    grow_all = next(it)
    cols_all = next(it)
    bd_all = next(it)
    ycat_all = next(it)

    nblk = seq_len // CHUNK
    cond = cond_base if cond_base == 0 else cond_base + pl.program_id(0)
    sh1, sc1, gt1, sh2, sc2, _ = _mod_rows(mod_ref, cond)
    ng = ng_ref[0]
    seqs = range(nseq)
    units = [(q, r) for q in seqs for r in range(nblk)]

    zeros_halo = jnp.zeros((HALO, Z_COLS - C_PZ), F32)
    for q in seqs:
        z_all[q, 0:HALO, C_PZ:Z_COLS] = zeros_halo
        z_all[q, HALO + seq_len:2 * HALO + seq_len, C_PZ:Z_COLS] = zeros_halo

    g_in = ng[0:1] * (1.0 + sc1)
    hbs = []
    for q, r in units:
        rows = slice(r * CHUNK, (r + 1) * CHUNK)
        xb = x_ref[q, rows, :]
        if add_pos:
            pieces = []
            for g in range(CHUNK // GRID_W):
                gi = r * (CHUNK // GRID_W) + g
                er = jnp.broadcast_to(er_ref[gi:gi + 1, :], (GRID_W, D_MODEL // 2))
                pieces.append(jnp.concatenate([er, ec_ref[...]], axis=1))
            xb = xb + jnp.concatenate(pieces, axis=0)
            x1_ref[q, rows, :] = xb
        hbs.append((_rms(xb, g_in) + sh1).astype(BF16))
    hb = jnp.concatenate(hbs, axis=0)
    gates_all = jnp.dot(hb, win_ref[0, :, C_GATE:W_COLS], preferred_element_type=F32) + gb_ref[0]
    for cg in range(Z_COLS // Z_GROUP):
        zc = jnp.dot(hb, win_ref[0, :, cg * Z_GROUP:(cg + 1) * Z_GROUP], preferred_element_type=F32)
        for u, (q, r) in enumerate(units):
            z_all[q, HALO + r * CHUNK:HALO + (r + 1) * CHUNK, cg * Z_GROUP:(cg + 1) * Z_GROUP] = (
                zc[u * CHUNK:(u + 1) * CHUNK])
    for u, (q, r) in enumerate(units):
        gates = gates_all[u * CHUNK:(u + 1) * CHUNK]
        gcol_all[q, r] = gates
        grow, cols = _gate_stats(gates)
        grow_all[q, r] = grow
        cols_all[q, r] = cols

    for q in seqs:
        if has_state:
            state0 = [[bd0_ref[q, 0, d, p] for p in range(PAIRS)] for d in range(2)]
            m0 = m0_ref[q, 0]
        else:
            state0 = [[None] * PAIRS, [None] * PAIRS]
            m0 = jnp.zeros((1, LANES), F32)
        fin, m_row = _mlstm(z_all.at[q], gcol_all.at[q], grow_all.at[q], cols_all.at[q], bd_all.at[q],
                            ycat_all.at[q], nblk, state0, m0, need_final=emit_state)
        if emit_state:
            for d in range(2):
                for p in range(PAIRS):
                    for j in range(2):
                        cout_ref[q, n_prev, d, 2 * p + j] = fin[d][p][DH * j:DH * (j + 1), DH * j:DH * (j + 1)]
                    n_t = fin[d][p][:, LANES:].T
                    nout_ref[q, n_prev, d:d + 1, LANES * p:LANES * (p + 1)] = n_t[0:1] + n_t[DH:DH + 1]
            mout_ref[q] = m_row
    if emit_state and n_prev:
        cout_ref[:, 0:n_prev] = cprev_ref[...]
        nout_ref[:, 0:n_prev] = nprev_ref[...]

    lane_g = lax.broadcasted_iota(jnp.int32, (CHUNK, D_GROUP), 1) // (D_GROUP // 4)
    lg = lax.broadcasted_iota(jnp.int32, (GMLP_CHUNK, D_GROUP), 1) // (D_GROUP // 4)
    row_c = lax.broadcasted_iota(jnp.int32, (CHUNK, D_GROUP), 0)
    half = jnp.where(lane_g == 0, 1, jnp.where(lane_g == 1, 2, jnp.where(lane_g == 2, 4, 8)))
    cw = cw_ref[0]
    ws_cat = jnp.concatenate([ws_ref[0, g] for g in range(4)], axis=1).astype(BF16)
    for q, r in units:
        z_ref = z_all.at[q]
        ycat_ref = ycat_all.at[q]
        rows = slice(r * CHUNK, (r + 1) * CHUNK)
        zrows = slice(HALO + r * CHUNK, HALO + (r + 1) * CHUNK)
        hrows = slice(r * CHUNK, (r + 1) * CHUNK + 2 * HALO)
        inner = slice(HALO, HALO + CHUNK)

        for s in range(CHUNK // GMLP_CHUNK):
            zs = slice(HALO + r * CHUNK + s * GMLP_CHUNK, HALO + r * CHUNK + (s + 1) * GMLP_CHUNK)
            vch = z_ref[zs, C_GV:C_GV + D_GROUP]
            v_stack = jnp.concatenate([jnp.where(lg == g, vch, 0.0) for g in range(4)], axis=0)
            mixed = gbias_ref[0] + jnp.dot(ws_cat, v_stack.astype(BF16), preferred_element_type=F32)
            yb = z_ref[zs, C_GU:C_GU + D_GROUP] * mixed
            ys = slice(r * CHUNK + s * GMLP_CHUNK, r * CHUNK + (s + 1) * GMLP_CHUNK)
            ycat_ref[ys, D_GROUP:2 * D_GROUP] = yb.astype(BF16)

        pz = z_ref[hrows, C_PZ:C_PZ + D_GROUP]
        a2 = pz + _shift_down(pz, 1)
        a4 = a2 + _shift_down(a2, 2)
        a8 = a4 + _shift_down(a4, 4)
        a16 = a8 + _shift_down(a8, 8)
        win = [a2[inner], _shift_up(a4, 1)[inner], _shift_up(a8, 3)[inner], _shift_up(a16, 7)[inner]]
        wsum = jnp.where(lane_g == 0, win[0], jnp.where(lane_g == 1, win[1],
                         jnp.where(lane_g == 2, win[2], win[3])))
        t_abs = row_c + r * CHUNK
        cnt = (jnp.minimum(t_abs + half, seq_len) - jnp.maximum(t_abs - half, 0)).astype(F32)
        pooled = wsum / cnt - pz[inner]
        yc = jnp.dot(pooled.astype(BF16), pbd_ref[0], preferred_element_type=F32) * psc_ref[0]
        ycat_ref[rows, 2 * D_GROUP:3 * D_GROUP] = yc.astype(BF16)

        u = z_ref[hrows, C_CC:C_CC + D_GROUP] * z_ref[hrows, C_CX:C_CX + D_GROUP]
        conv = (cw[0:1] * _shift_down(u, 1)[inner] + cw[1:2] * u[inner]
                + cw[2:3] * _shift_up(u, 1)[inner])
        yd = z_ref[zrows, C_CB:C_CB + D_GROUP] * conv
        ycat_ref[rows, 3 * D_GROUP:4 * D_GROUP] = yd.astype(BF16)

    y_all = jnp.dot(jnp.concatenate([ycat_all[q, r * CHUNK:(r + 1) * CHUNK, :] for q, r in units], axis=0),
                    wout_ref[0], preferred_element_type=F32)
    g_y = gt1 * ng[1:2]
    g_h2 = ng[2:3] * (1.0 + sc2)
    for u, (q, r) in enumerate(units):
        rows = slice(r * CHUNK, (r + 1) * CHUNK)
        xb = x1_ref[q, rows, :] if add_pos else x_ref[q, rows, :]
        x1 = xb + _rms(y_all[u * CHUNK:(u + 1) * CHUNK], g_y)
        x1_ref[q, rows, :] = x1
        h2_ref[q, rows, :] = (_rms(x1, g_h2) + sh2).astype(BF16)


def _layer_spec(shape, layer):
    nd = len(shape)
    return pl.BlockSpec((1,) + tuple(shape[1:]), lambda b: (layer,) + (0,) * (nd - 1),
                        pipeline_mode=pl.Buffered(1))


def _mixer_call(x, layer, mod_all, wts, cond_base, pos=None, state=None, emit_state=False,
                prev_states=None, nseq=1):
    bsz, seq_len, _ = x.shape
    nblk = seq_len // CHUNK
    add_pos = pos is not None
    has_state = state is not None
    n_prev = 0 if prev_states is None else prev_states[0].shape[1]
    args = [x]
    in_specs = [pl.BlockSpec((nseq, seq_len, D_MODEL), lambda b: (b, 0, 0))]
    if add_pos:
        args += list(pos)
        in_specs += [pl.BlockSpec(p.shape, lambda b: (0, 0), pipeline_mode=pl.Buffered(1)) for p in pos]
    args.append(mod_all)
    in_specs.append(_layer_spec(mod_all.shape, layer))
    for name in ("norm_g", "w_in", "gate_b", "gmlp_ws", "gmlp_bias", "pool_bd", "pool_scale",
                 "conv_w", "w_out"):
        args.append(wts[name])
        in_specs.append(_layer_spec(wts[name].shape, layer))
    if has_state:
        args += list(state)
        in_specs += [pl.BlockSpec((nseq, 1, 2, PAIRS, LANES, 2 * LANES), lambda b: (b, layer, 0, 0, 0, 0)),
                     pl.BlockSpec((nseq, 1, 1, LANES), lambda b: (b, layer, 0, 0))]
    out_shape = [jax.ShapeDtypeStruct((bsz, seq_len, D_MODEL), F32),
                 jax.ShapeDtypeStruct((bsz, seq_len, D_MODEL), BF16)]
    out_specs = [pl.BlockSpec((nseq, seq_len, D_MODEL), lambda b: (b, 0, 0)),
                 pl.BlockSpec((nseq, seq_len, D_MODEL), lambda b: (b, 0, 0))]
    if emit_state:
        if n_prev:
            args += list(prev_states)
            in_specs += [pl.BlockSpec((nseq, n_prev, 2, HEADS, DH, DH), lambda b: (b, 0, 0, 0, 0, 0)),
                         pl.BlockSpec((nseq, n_prev, 2, HEADS * DH), lambda b: (b, 0, 0, 0))]
        out_shape += [jax.ShapeDtypeStruct((bsz, n_prev + 1, 2, HEADS, DH, DH), F32),
                      jax.ShapeDtypeStruct((bsz, n_prev + 1, 2, HEADS * DH), F32),
                      jax.ShapeDtypeStruct((bsz, 1, LANES), F32)]
        out_specs += [pl.BlockSpec((nseq, n_prev + 1, 2, HEADS, DH, DH), lambda b: (b, 0, 0, 0, 0, 0)),
                      pl.BlockSpec((nseq, n_prev + 1, 2, HEADS * DH), lambda b: (b, 0, 0, 0)),
                      pl.BlockSpec((nseq, 1, LANES), lambda b: (b, 0, 0))]
    kern = functools.partial(_mixer_kernel, seq_len=seq_len, nseq=nseq, add_pos=add_pos,
                             has_state=has_state, emit_state=emit_state, cond_base=cond_base,
                             n_prev=n_prev)
    return pl.pallas_call(
        kern,
        out_shape=out_shape,
        grid=(bsz // nseq,),
        in_specs=in_specs,
        out_specs=out_specs,
        scratch_shapes=[pltpu.VMEM((nseq, seq_len + 2 * HALO, Z_COLS), F32),
                        pltpu.VMEM((nseq, nblk, CHUNK, LANES), F32),
                        pltpu.VMEM((nseq, nblk, 16, CHUNK), F32),
                        pltpu.VMEM((nseq, nblk, CHUNK, LANES), F32),
                        pltpu.VMEM((nseq, 2, nblk, PAIRS, LANES, 2 * LANES), BF16),
                        pltpu.VMEM((nseq, seq_len, D_MODEL), BF16)],
        compiler_params=pltpu.CompilerParams(
            dimension_semantics=("arbitrary",), vmem_limit_bytes=VMEM_LIMIT),
        name="mixer_t%d" % seq_len,
    )(*args)


def _ffn_kernel(mod_ref, ng_ref, cw_ref, x1p_hbm, h2p_hbm, x1s_hbm, h2s_hbm, up_hbm, down_hbm,
                outp_hbm, outs_hbm,
                up_bf, down_bf, act_ref, h2_buf, x1_buf, o_buf, up_stage, down_stage,
                in_sems, out_sems, w_sems, *, layer, tiles_p, tiles_s, seq_p, seq_s):
    i = pl.program_id(0)
    n_tiles = tiles_p + tiles_s
    rows = FFN_ROWS
    half = FFN_SUB // 2
    nsub = D_FF // half
    slot = i % 2
    is_prompt = i < tiles_p

    def tile_rows(tile, first_tile):
        if isinstance(tile, int):
            return pl.ds((tile - first_tile) * rows, rows)
        return pl.ds(pl.multiple_of((tile - first_tile) * rows, rows), rows)

    def fetch_from(h2_hbm, x1_hbm, r, buf):
        pltpu.make_async_copy(h2_hbm.at[r], h2_buf.at[buf], in_sems.at[0, buf]).start()
        pltpu.make_async_copy(x1_hbm.at[r], x1_buf.at[buf], in_sems.at[1, buf]).start()

    def fetch(tile, buf):
        @pl.when(tile < tiles_p)
        def _():
            fetch_from(h2p_hbm, x1p_hbm, tile_rows(tile, 0), buf)

        @pl.when(tile >= tiles_p)
        def _():
            fetch_from(h2s_hbm, x1s_hbm, tile_rows(tile, tiles_p), buf)

    def await_fetch(buf):
        r = pl.ds(0, rows)
        pltpu.make_async_copy(h2p_hbm.at[r], h2_buf.at[buf], in_sems.at[0, buf]).wait()
        pltpu.make_async_copy(x1p_hbm.at[r], x1_buf.at[buf], in_sems.at[1, buf]).wait()

    def await_store(buf):
        pltpu.make_async_copy(o_buf.at[buf], outp_hbm.at[pl.ds(0, rows)], out_sems.at[buf]).wait()

    def weight_copies(c, buf):
        aligned = (lambda v: v) if isinstance(c, int) else (lambda v: pl.multiple_of(v, half))
        cv = pl.ds(aligned(c * half), half)
        cg = pl.ds(aligned(D_FF + c * half), half)
        return (pltpu.make_async_copy(up_hbm.at[layer, :, cv], up_stage.at[buf, 0], w_sems.at[buf, 0]),
                pltpu.make_async_copy(up_hbm.at[layer, :, cg], up_stage.at[buf, 1], w_sems.at[buf, 1]),
                pltpu.make_async_copy(down_hbm.at[layer, cv, :], down_stage.at[buf], w_sems.at[buf, 2]))

    @pl.when(i == 0)
    def _():
        fetch_from(h2p_hbm, x1p_hbm, tile_rows(0, 0), 0)
        for cp in weight_copies(0, 0):
            cp.start()

        def convert(c, carry):
            buf = c % 2

            @pl.when(c + 1 < nsub)
            def _():
                for cp in weight_copies(c + 1, 1 - buf):
                    cp.start()

            for cp in weight_copies(c, buf):
                cp.wait()
            up_bf[c, :, 0:half] = up_stage[buf, 0].astype(BF16)
            up_bf[c, :, half:FFN_SUB] = up_stage[buf, 1].astype(BF16)
            down_bf[pl.ds(pl.multiple_of(c * half, half), half), :] = down_stage[buf].astype(BF16)
            return carry

        lax.fori_loop(0, nsub, convert, 0)

    await_fetch(slot)

    @pl.when(i + 1 < n_tiles)
    def _():
        fetch(i + 1, 1 - slot)

    seq_len = jnp.where(is_prompt, seq_p, seq_s)
    t_in_seq = lax.broadcasted_iota(jnp.int32, (rows, FFN_SUB), 0) & (seq_len - 1)
    first = t_in_seq == 0
    last = t_in_seq == seq_len - 1
    h2 = h2_buf[slot]
    for s in range(nsub):
        a = jnp.dot(h2, up_bf[s], preferred_element_type=F32)
        cw = jnp.concatenate([cw_ref[0, :, s * half:(s + 1) * half],
                              cw_ref[0, :, D_FF + s * half:D_FF + (s + 1) * half]], axis=1)
        a_prev = jnp.where(first, 0.0, pltpu.roll(a, 1, axis=0))
        a_next = jnp.where(last, 0.0, pltpu.roll(a, rows - 1, axis=0))
        ac = cw[0:1] * a_prev + cw[1:2] * a + cw[2:3] * a_next
        hg = 0.5 * ac[:, half:]
        act = (hg * (1.0 + jnp.tanh(hg))) * ac[:, :half]
        act_ref[:, s * half:(s + 1) * half] = act.astype(BF16)
    y = jnp.dot(act_ref[...], down_bf[...], preferred_element_type=F32)
    cond = jnp.where(is_prompt, 0, i - tiles_p + 1)
    gt2 = _mod_rows(mod_ref, cond)[5]

    @pl.when(i >= 2)
    def _():
        await_store(slot)

    o_buf[slot] = x1_buf[slot] + gt2 * _rms(y, ng_ref[0][3:4])

    @pl.when(is_prompt)
    def _():
        pltpu.make_async_copy(o_buf.at[slot], outp_hbm.at[tile_rows(i, 0)], out_sems.at[slot]).start()

    @pl.when(jnp.logical_not(is_prompt))
    def _():
        pltpu.make_async_copy(o_buf.at[slot], outs_hbm.at[tile_rows(i, tiles_p)], out_sems.at[slot]).start()

    @pl.when(i == n_tiles - 1)
    def _():
        await_store(1 - slot)
        await_store(slot)


def _ffn_call(x1p, h2p, x1s, h2s, layer, mod_all, wts, seq_p, seq_s):
    tiles_p = x1p.shape[0] // FFN_ROWS
    tiles_s = x1s.shape[0] // FFN_ROWS
    assert tiles_p >= 1 and tiles_s >= 1 and tiles_p + tiles_s >= 2
    half = FFN_SUB // 2

    def resident(shape):
        nd = len(shape)
        return pl.BlockSpec((1,) + tuple(shape[1:]), lambda i: (layer,) + (0,) * (nd - 1),
                            pipeline_mode=pl.Buffered(1))

    hbm = pl.BlockSpec(memory_space=pl.ANY)
    return pl.pallas_call(
        functools.partial(_ffn_kernel, layer=layer, tiles_p=tiles_p, tiles_s=tiles_s,
                          seq_p=seq_p, seq_s=seq_s),
        out_shape=[jax.ShapeDtypeStruct(x1p.shape, F32), jax.ShapeDtypeStruct(x1s.shape, F32)],
        grid=(tiles_p + tiles_s,),
        in_specs=[resident(mod_all.shape), resident(wts["norm_g"].shape), resident(wts["ffn_conv"].shape),
                  hbm, hbm, hbm, hbm, hbm, hbm],
        out_specs=[hbm, hbm],
        scratch_shapes=[pltpu.VMEM((D_FF // half, D_MODEL, FFN_SUB), BF16),
                        pltpu.VMEM((D_FF, D_MODEL), BF16),
                        pltpu.VMEM((FFN_ROWS, D_FF), BF16),
                        pltpu.VMEM((2, FFN_ROWS, D_MODEL), BF16),
                        pltpu.VMEM((2, FFN_ROWS, D_MODEL), F32),
                        pltpu.VMEM((2, FFN_ROWS, D_MODEL), F32),
                        pltpu.VMEM((2, 2, D_MODEL, half), F32),
                        pltpu.VMEM((2, half, D_MODEL), F32),
                        pltpu.SemaphoreType.DMA((2, 2)),
                        pltpu.SemaphoreType.DMA((2,)),
                        pltpu.SemaphoreType.DMA((2, 3))],
        compiler_params=pltpu.CompilerParams(
            dimension_semantics=("arbitrary",), vmem_limit_bytes=VMEM_LIMIT),
        name="ffn",
    )(mod_all, wts["norm_g"], wts["ffn_conv"], x1p, h2p, x1s, h2s, wts["ffn_up"], wts["ffn_down"])


def _pos_tables(t_len):
    quarter = D_MODEL // 4
    omega = (1.0 / (np.float32(POS_BASE) ** (np.arange(quarter, dtype=np.float32) / np.float32(quarter))))
    omega = omega.astype(np.float32)

    def emb(p):
        a = p.astype(np.float32)[:, None] * omega[None, :]
        return np.concatenate([np.sin(a), np.cos(a)], axis=-1).astype(np.float32)

    return jnp.asarray(emb(np.arange(t_len // GRID_W))), jnp.asarray(emb(np.arange(GRID_W)))


def _pair_states(state_c, state_n, state_m):
    bs = state_c.shape[0]
    eye = jnp.eye(2, dtype=F32)
    c = state_c.astype(F32).reshape(bs, DEPTH, 2, PAIRS, 2, DH, DH)
    n = state_n.astype(F32).reshape(bs, DEPTH, 2, PAIRS, 2, DH)
    c_bd = jnp.einsum("...jde,jk->...jdke", c, eye).reshape(bs, DEPTH, 2, PAIRS, LANES, LANES)
    n_bd = jnp.einsum("...jd,jk,e->...jdke", n, eye, jnp.ones((DH,), F32)).reshape(
        bs, DEPTH, 2, PAIRS, LANES, LANES)
    m = jnp.pad(state_m.astype(F32), ((0, 0), (0, 0), (0, 0), (0, 16 - HEADS)))
    m = jnp.pad(m.reshape(bs, DEPTH, 1, 32), ((0, 0), (0, 0), (0, 0), (L_MF, LANES - 32 - L_MF)))
    return jnp.concatenate([c_bd, n_bd], axis=-1), m


def kernel(x_prompt, x_sample, state_C, state_n, state_m, c, c_ctx, norm_g, ada_w, ada_b, w_in,
           gate_b, gmlp_ws, gmlp_b, pool_w, pool_scale, conv_w, w_out, ffn_up, ffn_conv, ffn_down):
    bp, tp, _ = x_prompt.shape
    bs, ts, _ = x_sample.shape
    assert tp == CHUNK and ts % CHUNK == 0 and FFN_ROWS % tp == 0 and ts == FFN_ROWS

    conds = jnp.concatenate([c_ctx[None, :], c, jnp.zeros((7 - bs, D_MODEL), F32)], axis=0)
    mod_all = _modulation(conds, ada_w, ada_b)
    win_p, wout_p = _prep_weights(w_in, w_out)

    gb = jnp.pad(gate_b, ((0, 0), (0, 0), (0, 8 - HEADS))).reshape(DEPTH, 1, 32)
    dg = D_GROUP // 4
    wts = {
        "norm_g": norm_g,
        "w_in": win_p,
        "gate_b": jnp.pad(gb, ((0, 0), (0, 0), (0, LANES - 32))),
        "gmlp_ws": gmlp_ws,
        "gmlp_bias": jnp.repeat(jnp.swapaxes(gmlp_b, 1, 2), dg, axis=2),
        "pool_bd": jnp.einsum("lgcd,gh->lgchd", pool_w, jnp.eye(4, dtype=F32)).reshape(
            DEPTH, D_GROUP, D_GROUP).astype(BF16),
        "pool_scale": pool_scale.reshape(DEPTH, 1, D_GROUP),
        "conv_w": conv_w,
        "w_out": wout_p,
        "ffn_up": ffn_up,
        "ffn_conv": ffn_conv,
        "ffn_down": ffn_down,
    }
    pos = _pos_tables(ts)
    state = _pair_states(state_C, state_n, state_m)

    xp, xs = x_prompt, x_sample
    prev, ms = None, []
    for l in range(DEPTH):
        x1p, h2p, c_all, n_all, m_fin = _mixer_call(xp, l, mod_all, wts, 0, emit_state=True,
                                                    prev_states=prev, nseq=MIXER_ROWS // tp)
        prev = (c_all, n_all)
        ms.append(jnp.stack([m_fin[:, 0, L_MF:L_MF + HEADS], m_fin[:, 0, L_MB:L_MB + HEADS]], axis=1))

        x1s, h2s = _mixer_call(xs, l, mod_all, wts, 1, pos=pos if l == 0 else None, state=state)

        xp, xs = _ffn_call(x1p.reshape(bp * tp, D_MODEL), h2p.reshape(bp * tp, D_MODEL),
                           x1s.reshape(bs * ts, D_MODEL), h2s.reshape(bs * ts, D_MODEL),
                           l, mod_all, wts, tp, ts)
        xp = xp.reshape(bp, tp, D_MODEL)
        xs = xs.reshape(bs, ts, D_MODEL)

    return (xp, xs, c_all, n_all.reshape(bp, DEPTH, 2, HEADS, DH), jnp.stack(ms, axis=1))
```

```python
import functools

import numpy as np
import jax
import jax.numpy as jnp
from jax import lax
from jax.experimental import pallas as pl
from jax.experimental.pallas import tpu as pltpu

F32 = jnp.float32
BF16 = jnp.bfloat16

D_MODEL = 1024
DEPTH = 2
HEADS = 4
PAIRS = HEADS // 2
DH = 64
D_GROUP = 256
D_FF = 2816
N_MOD = 6
EPS = 1e-6
GRID_W = 64
POS_BASE = 10000.0
D_IN = 2576
N_GATE = 4 * HEADS

CHUNK = 256
GMLP_CHUNK = 128
HALO = 8
LANES = 128

C_Q, C_K, C_V, C_O = 0, 256, 512, 768
C_GU, C_GV, C_PZ, C_CB, C_CC, C_CX = 1024, 1280, 1536, 1792, 2048, 2304
Z_COLS = 2560
C_GATE = Z_COLS
W_COLS = Z_COLS + LANES
Z_GROUP = 512
L_BF, L_MF, L_BB, L_MB = 0, 8, 16, 24

MIXER_ROWS = 1024
FFN_ROWS = 1024
FFN_SUB = 256
VMEM_LIMIT = 60 * 1024 * 1024


def _rms(x, g):
    ms = jnp.mean(x * x, axis=-1, keepdims=True)
    return x * lax.rsqrt(ms + EPS) * g


def _sigmoid(x):
    return 1.0 / (1.0 + jnp.exp(-x))


def _log_sigmoid(x):
    return jnp.minimum(x, 0.0) - jnp.log1p(jnp.exp(-jnp.abs(x)))


def _scan_lanes(x, op, reverse, fill):
    n = x.shape[1]
    lane = lax.broadcasted_iota(jnp.int32, x.shape, 1)
    k = 1
    while k < n:
        if reverse:
            sh = jnp.where(lane < n - k, pltpu.roll(x, n - k, axis=1), fill)
        else:
            sh = jnp.where(lane >= k, pltpu.roll(x, k, axis=1), fill)
        x = op(x, sh)
        k *= 2
    return x


def _shift_down(x, k):
    return pltpu.roll(x, k, axis=0)


def _shift_up(x, k):
    return pltpu.roll(x, x.shape[0] - k, axis=0)


def _prep_win_kernel(wt_ref, sel_ref, o_ref):
    for n in range(Z_COLS // LANES):
        r0 = LANES * n if n < 4 * D_GROUP // LANES else LANES * n + N_GATE
        o_ref[0, :, LANES * n:LANES * (n + 1)] = wt_ref[0, r0:r0 + LANES, :].T.astype(BF16)
    g = wt_ref[0, 4 * D_GROUP:4 * D_GROUP + LANES, :].T.astype(BF16)
    o_ref[0, :, C_GATE:W_COLS] = jnp.dot(g, sel_ref[...], preferred_element_type=F32).astype(BF16)


def _cast_kernel(w_ref, o_ref):
    o_ref[...] = w_ref[...].astype(BF16)


def _prep_weights(w_in, w_out):
    sel = np.zeros((LANES, LANES), np.float32)
    for j in range(N_GATE):
        sel[j, 8 * (j // HEADS) + j % HEADS] = 1.0
    params = pltpu.CompilerParams(dimension_semantics=("arbitrary", "arbitrary"),
                                  vmem_limit_bytes=VMEM_LIMIT)
    win_p = pl.pallas_call(
        _prep_win_kernel, out_shape=jax.ShapeDtypeStruct((DEPTH, D_MODEL, W_COLS), BF16),
        grid=(DEPTH, 1),
        in_specs=[pl.BlockSpec((1, D_IN, D_MODEL), lambda l, i: (l, 0, 0)),
                  pl.BlockSpec((LANES, LANES), lambda l, i: (0, 0))],
        out_specs=pl.BlockSpec((1, D_MODEL, W_COLS), lambda l, i: (l, 0, 0)),
        compiler_params=params, name="prep_w_in")(jnp.swapaxes(w_in, 1, 2), jnp.asarray(sel, BF16))
    rows = 512
    wout_p = pl.pallas_call(
        _cast_kernel, out_shape=jax.ShapeDtypeStruct(w_out.shape, BF16),
        grid=(DEPTH, D_MODEL // rows),
        in_specs=[pl.BlockSpec((1, rows, D_MODEL), lambda l, i: (l, i, 0))],
        out_specs=pl.BlockSpec((1, rows, D_MODEL), lambda l, i: (l, i, 0)),
        compiler_params=params, name="prep_w_out")(w_out)
    return win_p, wout_p


def _mod_kernel(c_ref, w_ref, b_ref, o_ref):
    c = c_ref[...]
    s = (c * _sigmoid(c)).astype(BF16)
    o_ref[0] = jnp.dot(s, w_ref[0].astype(BF16), preferred_element_type=F32) + b_ref[0]


def _modulation(conds, ada_w, ada_b):
    n_out = N_MOD * D_MODEL
    bn = 1536
    return pl.pallas_call(
        _mod_kernel,
        out_shape=jax.ShapeDtypeStruct((DEPTH, 8, n_out), F32),
        grid=(DEPTH, n_out // bn),
        in_specs=[
            pl.BlockSpec((8, D_MODEL), lambda l, j: (0, 0)),
            pl.BlockSpec((1, D_MODEL, bn), lambda l, j: (l, 0, j)),
            pl.BlockSpec((1, 1, bn), lambda l, j: (l, 0, j)),
        ],
        out_specs=pl.BlockSpec((1, 8, bn), lambda l, j: (l, 0, j)),
        compiler_params=pltpu.CompilerParams(
            dimension_semantics=("arbitrary", "arbitrary"), vmem_limit_bytes=VMEM_LIMIT),
        name="adaln_mod",
    )(conds, ada_w, ada_b.reshape(DEPTH, 1, n_out))


def _mod_rows(mod_ref, cond):
    row = mod_ref[0, pl.ds(cond, 1), :]
    return [row[:, i * D_MODEL:(i + 1) * D_MODEL] for i in range(N_MOD)]


def _gate_stats(gates):
    gt = gates.T
    b_f = _scan_lanes(_log_sigmoid(gt[8:16]), jnp.add, False, 0.0)
    g_f = gt[0:8] - b_f
    cm_f = _scan_lanes(g_f, jnp.maximum, False, -jnp.inf)
    b_b = _scan_lanes(_log_sigmoid(gt[24:32]), jnp.add, True, 0.0)
    g_b = gt[16:24] - b_b
    cm_b = _scan_lanes(g_b, jnp.maximum, True, -jnp.inf)
    stack = jnp.concatenate([b_f, cm_f, b_b, cm_b, jnp.zeros((LANES - 32, CHUNK), F32)], axis=0)
    return jnp.concatenate([g_f, g_b], axis=0), stack.T


def _pair_cols(mat, lane0, lane_lt_dh):
    return jnp.where(lane_lt_dh, mat[:, lane0:lane0 + 1], mat[:, lane0 + 1:lane0 + 2])


def _mlstm(z_ref, gcol_ref, grow_ref, cols_ref, bd_ref, ycat_ref, nc, state0, m0, need_final):
    lane_row = lax.broadcasted_iota(jnp.int32, (1, LANES), 1)
    lane = lax.broadcasted_iota(jnp.int32, (CHUNK, LANES), 1)
    lt_dh = lane < DH
    row_ll = lax.broadcasted_iota(jnp.int32, (CHUNK, CHUNK), 0)
    col_ll = lax.broadcasted_iota(jnp.int32, (CHUNK, CHUNK), 1)
    tri = (col_ll <= row_ll, col_ll >= row_ll)
    bd_row = lax.broadcasted_iota(jnp.int32, (LANES, 2 * LANES), 0)
    bd_col = lax.broadcasted_iota(jnp.int32, (LANES, 2 * LANES), 1)
    bd_mask = (bd_row < DH) == ((bd_col & (LANES - 1)) < DH)
    dirs = ((0, L_BF, L_MF, CHUNK - 1), (1, L_BB, L_MB, 0))

    m_start = [[None] * nc, [None] * nc]
    m_fin = [None, None]
    for d, _, _, last in dirs:
        m = m0
        order = range(nc) if d == 0 else range(nc - 1, -1, -1)
        for step, c in enumerate(order):
            m_start[d][c] = m
            if need_final or step < nc - 1:
                cl = cols_ref[c, last:last + 1, :]
                m = pltpu.roll(cl, 8, axis=1) + jnp.maximum(m, cl)
        m_fin[d] = m

    have_state = [[False] * nc, [False] * nc]
    fin = [None, None]
    for d, l_b, l_m, last in dirs:
        bd = list(state0[d])
        order = range(nc) if d == 0 else range(nc - 1, -1, -1)
        for step, c in enumerate(order):
            if bd[0] is not None:
                have_state[d][c] = True
                for p in range(PAIRS):
                    bd_ref[d, c, p] = bd[p].astype(BF16)
            if not (need_final or step < nc - 1):
                continue
            rows = slice(HALO + c * CHUNK, HALO + (c + 1) * CHUNK)
            cols = cols_ref[c]
            m_last = jnp.maximum(cols[last:last + 1, :], m_start[d][c])
            a_col = jnp.exp(gcol_ref[c] - cols - pltpu.roll(m_last, LANES - 8, axis=1))
            a_prev = jnp.exp(m_start[d][c] - m_last)
            for p in range(PAIRS):
                a_pair = _pair_cols(a_col, l_b + 2 * p, lt_dh)
                v_pair = z_ref[rows, C_V + LANES * p:C_V + LANES * (p + 1)]
                rhs = jnp.concatenate([v_pair * a_pair, a_pair], axis=1).astype(BF16)
                k_pair = (z_ref[rows, C_K + LANES * p:C_K + LANES * (p + 1)] * (DH ** -0.5)).astype(BF16)
                upd = lax.dot_general(k_pair, rhs, (((0,), (0,)), ((), ())),
                                      preferred_element_type=F32)
                upd = jnp.where(bd_mask, upd, 0.0)
                if bd[p] is None:
                    bd[p] = upd
                else:
                    scale = jnp.where(bd_row < DH, a_prev[:, l_m + 2 * p:l_m + 2 * p + 1],
                                      a_prev[:, l_m + 2 * p + 1:l_m + 2 * p + 2])
                    bd[p] = scale * bd[p] + upd
        fin[d] = bd

    for c in range(nc):
        rows = slice(HALO + c * CHUNK, HALO + (c + 1) * CHUNK)
        cols = cols_ref[c]
        grow = grow_ref[c]
        m_prev = jnp.where(lane_row < L_BB, m_start[0][c], m_start[1][c])
        m_all = jnp.maximum(cols, m_prev)
        w_inter = jnp.exp(m_prev - m_all)
        e_neg = jnp.exp(-(pltpu.roll(cols, 8, axis=1) + m_all))
        for p in range(PAIRS):
            q_pair = z_ref[rows, C_Q + LANES * p:C_Q + LANES * (p + 1)].astype(BF16)
            k_f32 = z_ref[rows, C_K + LANES * p:C_K + LANES * (p + 1)] * (DH ** -0.5)
            v_pair = z_ref[rows, C_V + LANES * p:C_V + LANES * (p + 1)]
            p_cat = ([], [])
            va = []
            for j in range(2):
                h = 2 * p + j
                own = lt_dh if j == 0 else jnp.logical_not(lt_dh)
                k_own = jnp.where(own, k_f32, 0.0).astype(BF16)
                s = lax.dot_general(q_pair, k_own, (((1,), (1,)), ((), ())),
                                    preferred_element_type=F32)
                va.append(jnp.concatenate([jnp.where(own, v_pair, 0.0), jnp.where(own, 1.0, 0.0)],
                                          axis=1).astype(BF16))
                for d, _, l_m, _ in dirs:
                    w = jnp.exp(jnp.where(tri[d], grow[8 * d + h:8 * d + h + 1] - m_all[:, l_m + h:l_m + h + 1],
                                          -jnp.inf))
                    p_cat[d].append((s * w).astype(BF16))
            va = jnp.concatenate(va, axis=0)
            h_sum = None
            for d, _, l_m, _ in dirs:
                tot = jnp.dot(jnp.concatenate(p_cat[d], axis=1), va, preferred_element_type=F32)
                if have_state[d][c]:
                    e_pair = _pair_cols(w_inter, l_m + 2 * p, lt_dh)
                    qc = jnp.dot(q_pair, bd_ref[d, c, p], preferred_element_type=F32)
                    tot = tot + jnp.concatenate([e_pair, e_pair], axis=1) * qc
                floor = _pair_cols(e_neg, l_m + 2 * p, lt_dh)
                h_dir = tot[:, :LANES] / jnp.maximum(jnp.abs(tot[:, LANES:]), floor)
                h_sum = h_dir if h_sum is None else h_sum + h_dir
            o_pair = z_ref[rows, C_O + LANES * p:C_O + LANES * (p + 1)]
            ycat_ref[c * CHUNK:(c + 1) * CHUNK, LANES * p:LANES * (p + 1)] = (
                _sigmoid(o_pair) * h_sum).astype(BF16)

    m_row = jnp.where(lane_row < L_BB, m_fin[0], m_fin[1])
    return fin, m_row


def _mixer_kernel(*refs, seq_len, nseq, add_pos, has_state, emit_state, cond_base, n_prev):
    it = iter(refs)
    x_ref = next(it)
    if add_pos:
        er_ref = next(it)
        ec_ref = next(it)
    mod_ref = next(it)
    ng_ref = next(it)
    win_ref = next(it)
    gb_ref = next(it)
    ws_ref = next(it)
    gbias_ref = next(it)
    pbd_ref = next(it)
    psc_ref = next(it)
    cw_ref = next(it)
    wout_ref = next(it)
    if has_state:
        bd0_ref = next(it)
        m0_ref = next(it)
    if n_prev:
        cprev_ref = next(it)
        nprev_ref = next(it)
    x1_ref = next(it)
    h2_ref = next(it)
    if emit_state:
        cout_ref = next(it)
        nout_ref = next(it)
        mout_ref = next(it)
    z_all = next(it)
    gcol_all = next(it)
    grow_all = next(it)
    cols_all = next(it)
    bd_all = next(it)
    ycat_all = next(it)

    nblk = seq_len // CHUNK
    cond = cond_base if cond_base == 0 else cond_base + pl.program_id(0)
    sh1, sc1, gt1, sh2, sc2, _ = _mod_rows(mod_ref, cond)
    ng = ng_ref[0]
    seqs = range(nseq)
    units = [(q, r) for q in seqs for r in range(nblk)]

    zeros_halo = jnp.zeros((HALO, Z_COLS - C_PZ), F32)
    for q in seqs:
        z_all[q, 0:HALO, C_PZ:Z_COLS] = zeros_halo
        z_all[q, HALO + seq_len:2 * HALO + seq_len, C_PZ:Z_COLS] = zeros_halo

    g_in = ng[0:1] * (1.0 + sc1)
    hbs = []
    for q, r in units:
        rows = slice(r * CHUNK, (r + 1) * CHUNK)
        xb = x_ref[q, rows, :]
        if add_pos:
            pieces = []
            for g in range(CHUNK // GRID_W):
                gi = r * (CHUNK // GRID_W) + g
                er = jnp.broadcast_to(er_ref[gi:gi + 1, :], (GRID_W, D_MODEL // 2))
                pieces.append(jnp.concatenate([er, ec_ref[...]], axis=1))
            xb = xb + jnp.concatenate(pieces, axis=0)
            x1_ref[q, rows, :] = xb
        hbs.append((_rms(xb, g_in) + sh1).astype(BF16))
    hb = jnp.concatenate(hbs, axis=0)
    gates_all = jnp.dot(hb, win_ref[0, :, C_GATE:W_COLS], preferred_element_type=F32) + gb_ref[0]
    for cg in range(Z_COLS // Z_GROUP):
        zc = jnp.dot(hb, win_ref[0, :, cg * Z_GROUP:(cg + 1) * Z_GROUP], preferred_element_type=F32)
        for u, (q, r) in enumerate(units):
            z_all[q, HALO + r * CHUNK:HALO + (r + 1) * CHUNK, cg * Z_GROUP:(cg + 1) * Z_GROUP] = (
                zc[u * CHUNK:(u + 1) * CHUNK])
    for u, (q, r) in enumerate(units):
        gates = gates_all[u * CHUNK:(u + 1) * CHUNK]
        gcol_all[q, r] = gates
        grow, cols = _gate_stats(gates)
        grow_all[q, r] = grow
        cols_all[q, r] = cols

    for q in seqs:
        if has_state:
            state0 = [[bd0_ref[q, 0, d, p] for p in range(PAIRS)] for d in range(2)]
            m0 = m0_ref[q, 0]
        else:
            state0 = [[None] * PAIRS, [None] * PAIRS]
            m0 = jnp.zeros((1, LANES), F32)
        fin, m_row = _mlstm(z_all.at[q], gcol_all.at[q], grow_all.at[q], cols_all.at[q], bd_all.at[q],
                            ycat_all.at[q], nblk, state0, m0, need_final=emit_state)
        if emit_state:
            for d in range(2):
                for p in range(PAIRS):
                    for j in range(2):
                        cout_ref[q, n_prev, d, 2 * p + j] = fin[d][p][DH * j:DH * (j + 1), DH * j:DH * (j + 1)]
                    n_t = fin[d][p][:, LANES:].T
                    nout_ref[q, n_prev, d:d + 1, LANES * p:LANES * (p + 1)] = n_t[0:1] + n_t[DH:DH + 1]
            mout_ref[q] = m_row
    if emit_state and n_prev:
        cout_ref[:, 0:n_prev] = cprev_ref[...]
        nout_ref[:, 0:n_prev] = nprev_ref[...]

    lane_g = lax.broadcasted_iota(jnp.int32, (CHUNK, D_GROUP), 1) // (D_GROUP // 4)
    lg = lax.broadcasted_iota(jnp.int32, (GMLP_CHUNK, D_GROUP), 1) // (D_GROUP // 4)
    row_c = lax.broadcasted_iota(jnp.int32, (CHUNK, D_GROUP), 0)
    half = jnp.where(lane_g == 0, 1, jnp.where(lane_g == 1, 2, jnp.where(lane_g == 2, 4, 8)))
    cw = cw_ref[0]
    ws_cat = jnp.concatenate([ws_ref[0, g] for g in range(4)], axis=1).astype(BF16)
    for q, r in units:
        z_ref = z_all.at[q]
        ycat_ref = ycat_all.at[q]
        rows = slice(r * CHUNK, (r + 1) * CHUNK)
        zrows = slice(HALO + r * CHUNK, HALO + (r + 1) * CHUNK)
        hrows = slice(r * CHUNK, (r + 1) * CHUNK + 2 * HALO)
        inner = slice(HALO, HALO + CHUNK)

        for s in range(CHUNK // GMLP_CHUNK):
            zs = slice(HALO + r * CHUNK + s * GMLP_CHUNK, HALO + r * CHUNK + (s + 1) * GMLP_CHUNK)
            vch = z_ref[zs, C_GV:C_GV + D_GROUP]
            v_stack = jnp.concatenate([jnp.where(lg == g, vch, 0.0) for g in range(4)], axis=0)
            mixed = gbias_ref[0] + jnp.dot(ws_cat, v_stack.astype(BF16), preferred_element_type=F32)
            yb = z_ref[zs, C_GU:C_GU + D_GROUP] * mixed
            ys = slice(r * CHUNK + s * GMLP_CHUNK, r * CHUNK + (s + 1) * GMLP_CHUNK)
            ycat_ref[ys, D_GROUP:2 * D_GROUP] = yb.astype(BF16)

        pz = z_ref[hrows, C_PZ:C_PZ + D_GROUP]
        a2 = pz + _shift_down(pz, 1)
        a4 = a2 + _shift_down(a2, 2)
        a8 = a4 + _shift_down(a4, 4)
        a16 = a8 + _shift_down(a8, 8)
        win = [a2[inner], _shift_up(a4, 1)[inner], _shift_up(a8, 3)[inner], _shift_up(a16, 7)[inner]]
        wsum = jnp.where(lane_g == 0, win[0], jnp.where(lane_g == 1, win[1],
                         jnp.where(lane_g == 2, win[2], win[3])))
        t_abs = row_c + r * CHUNK
        cnt = (jnp.minimum(t_abs + half, seq_len) - jnp.maximum(t_abs - half, 0)).astype(F32)
        pooled = wsum / cnt - pz[inner]
        yc = jnp.dot(pooled.astype(BF16), pbd_ref[0], preferred_element_type=F32) * psc_ref[0]
        ycat_ref[rows, 2 * D_GROUP:3 * D_GROUP] = yc.astype(BF16)

        u = z_ref[hrows, C_CC:C_CC + D_GROUP] * z_ref[hrows, C_CX:C_CX + D_GROUP]
        conv = (cw[0:1] * _shift_down(u, 1)[inner] + cw[1:2] * u[inner]
                + cw[2:3] * _shift_up(u, 1)[inner])
        yd = z_ref[zrows, C_CB:C_CB + D_GROUP] * conv
        ycat_ref[rows, 3 * D_GROUP:4 * D_GROUP] = yd.astype(BF16)

    y_all = jnp.dot(jnp.concatenate([ycat_all[q, r * CHUNK:(r + 1) * CHUNK, :] for q, r in units], axis=0),
                    wout_ref[0], preferred_element_type=F32)
    g_y = gt1 * ng[1:2]
    g_h2 = ng[2:3] * (1.0 + sc2)
    for u, (q, r) in enumerate(units):
        rows = slice(r * CHUNK, (r + 1) * CHUNK)
        xb = x1_ref[q, rows, :] if add_pos else x_ref[q, rows, :]
        x1 = xb + _rms(y_all[u * CHUNK:(u + 1) * CHUNK], g_y)
        x1_ref[q, rows, :] = x1
        h2_ref[q, rows, :] = (_rms(x1, g_h2) + sh2).astype(BF16)


def _layer_spec(shape, layer):
    nd = len(shape)
    return pl.BlockSpec((1,) + tuple(shape[1:]), lambda b: (layer,) + (0,) * (nd - 1),
                        pipeline_mode=pl.Buffered(1))


def _mixer_call(x, layer, mod_all, wts, cond_base, pos=None, state=None, emit_state=False,
                prev_states=None, nseq=1):
    bsz, seq_len, _ = x.shape
    nblk = seq_len // CHUNK
    add_pos = pos is not None
    has_state = state is not None
    n_prev = 0 if prev_states is None else prev_states[0].shape[1]
    args = [x]
    in_specs = [pl.BlockSpec((nseq, seq_len, D_MODEL), lambda b: (b, 0, 0))]
    if add_pos:
        args += list(pos)
        in_specs += [pl.BlockSpec(p.shape, lambda b: (0, 0), pipeline_mode=pl.Buffered(1)) for p in pos]
    args.append(mod_all)
    in_specs.append(_layer_spec(mod_all.shape, layer))
    for name in ("norm_g", "w_in", "gate_b", "gmlp_ws", "gmlp_bias", "pool_bd", "pool_scale",
                 "conv_w", "w_out"):
        args.append(wts[name])
        in_specs.append(_layer_spec(wts[name].shape, layer))
    if has_state:
        args += list(state)
        in_specs += [pl.BlockSpec((nseq, 1, 2, PAIRS, LANES, 2 * LANES), lambda b: (b, layer, 0, 0, 0, 0)),
                     pl.BlockSpec((nseq, 1, 1, LANES), lambda b: (b, layer, 0, 0))]
    out_shape = [jax.ShapeDtypeStruct((bsz, seq_len, D_MODEL), F32),
                 jax.ShapeDtypeStruct((bsz, seq_len, D_MODEL), BF16)]
    out_specs = [pl.BlockSpec((nseq, seq_len, D_MODEL), lambda b: (b, 0, 0)),
                 pl.BlockSpec((nseq, seq_len, D_MODEL), lambda b: (b, 0, 0))]
    if emit_state:
        if n_prev:
            args += list(prev_states)
            in_specs += [pl.BlockSpec((nseq, n_prev, 2, HEADS, DH, DH), lambda b: (b, 0, 0, 0, 0, 0)),
                         pl.BlockSpec((nseq, n_prev, 2, HEADS * DH), lambda b: (b, 0, 0, 0))]
        out_shape += [jax.ShapeDtypeStruct((bsz, n_prev + 1, 2, HEADS, DH, DH), F32),
                      jax.ShapeDtypeStruct((bsz, n_prev + 1, 2, HEADS * DH), F32),
                      jax.ShapeDtypeStruct((bsz, 1, LANES), F32)]
        out_specs += [pl.BlockSpec((nseq, n_prev + 1, 2, HEADS, DH, DH), lambda b: (b, 0, 0, 0, 0, 0)),
                      pl.BlockSpec((nseq, n_prev + 1, 2, HEADS * DH), lambda b: (b, 0, 0, 0)),
                      pl.BlockSpec((nseq, 1, LANES), lambda b: (b, 0, 0))]
    kern = functools.partial(_mixer_kernel, seq_len=seq_len, nseq=nseq, add_pos=add_pos,
                             has_state=has_state, emit_state=emit_state, cond_base=cond_base,
                             n_prev=n_prev)
    return pl.pallas_call(
        kern,
        out_shape=out_shape,
        grid=(bsz // nseq,),
        in_specs=in_specs,
        out_specs=out_specs,
        scratch_shapes=[pltpu.VMEM((nseq, seq_len + 2 * HALO, Z_COLS), F32),
                        pltpu.VMEM((nseq, nblk, CHUNK, LANES), F32),
                        pltpu.VMEM((nseq, nblk, 16, CHUNK), F32),
                        pltpu.VMEM((nseq, nblk, CHUNK, LANES), F32),
                        pltpu.VMEM((nseq, 2, nblk, PAIRS, LANES, 2 * LANES), BF16),
                        pltpu.VMEM((nseq, seq_len, D_MODEL), BF16)],
        compiler_params=pltpu.CompilerParams(
            dimension_semantics=("arbitrary",), vmem_limit_bytes=VMEM_LIMIT),
        name="mixer_t%d" % seq_len,
    )(*args)


def _ffn_kernel(mod_ref, ng_ref, cw_ref, x1p_hbm, h2p_hbm, x1s_hbm, h2s_hbm, up_hbm, down_hbm,
                outp_hbm, outs_hbm,
                up_bf, down_bf, act_ref, h2_buf, x1_buf, o_buf, up_stage, down_stage,
                in_sems, out_sems, w_sems, *, layer, tiles_p, tiles_s, seq_p, seq_s):
    i = pl.program_id(0)
    n_tiles = tiles_p + tiles_s
    rows = FFN_ROWS
    half = FFN_SUB // 2
    nsub = D_FF // half
    slot = i % 2
    is_prompt = i < tiles_p

    def tile_rows(tile, first_tile):
        if isinstance(tile, int):
            return pl.ds((tile - first_tile) * rows, rows)
        return pl.ds(pl.multiple_of((tile - first_tile) * rows, rows), rows)

    def fetch_from(h2_hbm, x1_hbm, r, buf):
        pltpu.make_async_copy(h2_hbm.at[r], h2_buf.at[buf], in_sems.at[0, buf]).start()
        pltpu.make_async_copy(x1_hbm.at[r], x1_buf.at[buf], in_sems.at[1, buf]).start()

    def fetch(tile, buf):
        @pl.when(tile < tiles_p)
        def _():
            fetch_from(h2p_hbm, x1p_hbm, tile_rows(tile, 0), buf)

        @pl.when(tile >= tiles_p)
        def _():
            fetch_from(h2s_hbm, x1s_hbm, tile_rows(tile, tiles_p), buf)

    def await_fetch(buf):
        r = pl.ds(0, rows)
        pltpu.make_async_copy(h2p_hbm.at[r], h2_buf.at[buf], in_sems.at[0, buf]).wait()
        pltpu.make_async_copy(x1p_hbm.at[r], x1_buf.at[buf], in_sems.at[1, buf]).wait()

    def await_store(buf):
        pltpu.make_async_copy(o_buf.at[buf], outp_hbm.at[pl.ds(0, rows)], out_sems.at[buf]).wait()

    def weight_copies(c, buf):
        aligned = (lambda v: v) if isinstance(c, int) else (lambda v: pl.multiple_of(v, half))
        cv = pl.ds(aligned(c * half), half)
        cg = pl.ds(aligned(D_FF + c * half), half)
        return (pltpu.make_async_copy(up_hbm.at[layer, :, cv], up_stage.at[buf, 0], w_sems.at[buf, 0]),
                pltpu.make_async_copy(up_hbm.at[layer, :, cg], up_stage.at[buf, 1], w_sems.at[buf, 1]),
                pltpu.make_async_copy(down_hbm.at[layer, cv, :], down_stage.at[buf], w_sems.at[buf, 2]))

    @pl.when(i == 0)
    def _():
        fetch_from(h2p_hbm, x1p_hbm, tile_rows(0, 0), 0)
        for cp in weight_copies(0, 0):
            cp.start()

        def convert(c, carry):
            buf = c % 2

            @pl.when(c + 1 < nsub)
            def _():
                for cp in weight_copies(c + 1, 1 - buf):
                    cp.start()

            for cp in weight_copies(c, buf):
                cp.wait()
            up_bf[c, :, 0:half] = up_stage[buf, 0].astype(BF16)
            up_bf[c, :, half:FFN_SUB] = up_stage[buf, 1].astype(BF16)
            down_bf[pl.ds(pl.multiple_of(c * half, half), half), :] = down_stage[buf].astype(BF16)
            return carry

        lax.fori_loop(0, nsub, convert, 0)

    await_fetch(slot)

    @pl.when(i + 1 < n_tiles)
    def _():
        fetch(i + 1, 1 - slot)

    row = lax.broadcasted_iota(jnp.int32, (rows, FFN_SUB), 0)
    first = ((row & (seq_s - 1)) == 0) | (((row & (seq_p - 1)) == 0) & is_prompt)
    last = ((row & (seq_s - 1)) == seq_s - 1) | (((row & (seq_p - 1)) == seq_p - 1) & is_prompt)
    h2 = h2_buf[slot]
    for s in range(nsub):
        a = jnp.dot(h2, up_bf[s], preferred_element_type=F32)
        cw = jnp.concatenate([cw_ref[0, :, s * half:(s + 1) * half],
                              cw_ref[0, :, D_FF + s * half:D_FF + (s + 1) * half]], axis=1)
        a_prev = jnp.where(first, 0.0, pltpu.roll(a, 1, axis=0))
        a_next = jnp.where(last, 0.0, pltpu.roll(a, rows - 1, axis=0))
        ac = cw[0:1] * a_prev + cw[1:2] * a + cw[2:3] * a_next
        hg = 0.5 * ac[:, half:]
        act = (hg * (1.0 + jnp.tanh(hg))) * ac[:, :half]
        act_ref[:, s * half:(s + 1) * half] = act.astype(BF16)
    y = jnp.dot(act_ref[...], down_bf[...], preferred_element_type=F32)
    cond = jnp.where(is_prompt, 0, i - tiles_p + 1)
    gt2 = _mod_rows(mod_ref, cond)[5]

    @pl.when(i >= 2)
    def _():
        await_store(slot)

    o_buf[slot] = x1_buf[slot] + gt2 * _rms(y, ng_ref[0][3:4])

    @pl.when(is_prompt)
    def _():
        pltpu.make_async_copy(o_buf.at[slot], outp_hbm.at[tile_rows(i, 0)], out_sems.at[slot]).start()

    @pl.when(jnp.logical_not(is_prompt))
    def _():
        pltpu.make_async_copy(o_buf.at[slot], outs_hbm.at[tile_rows(i, tiles_p)], out_sems.at[slot]).start()

    @pl.when(i == n_tiles - 1)
    def _():
        await_store(1 - slot)
        await_store(slot)


def _ffn_call(x1p, h2p, x1s, h2s, layer, mod_all, wts, seq_p, seq_s):
    tiles_p = x1p.shape[0] // FFN_ROWS
    tiles_s = x1s.shape[0] // FFN_ROWS
    assert tiles_p >= 1 and tiles_s >= 1 and tiles_p + tiles_s >= 2
    half = FFN_SUB // 2

    def resident(shape):
        nd = len(shape)
        return pl.BlockSpec((1,) + tuple(shape[1:]), lambda i: (layer,) + (0,) * (nd - 1),
                            pipeline_mode=pl.Buffered(1))

    hbm = pl.BlockSpec(memory_space=pl.ANY)
    return pl.pallas_call(
        functools.partial(_ffn_kernel, layer=layer, tiles_p=tiles_p, tiles_s=tiles_s,
                          seq_p=seq_p, seq_s=seq_s),
        out_shape=[jax.ShapeDtypeStruct(x1p.shape, F32), jax.ShapeDtypeStruct(x1s.shape, F32)],
        grid=(tiles_p + tiles_s,),
        in_specs=[resident(mod_all.shape), resident(wts["norm_g"].shape), resident(wts["ffn_conv"].shape),
                  hbm, hbm, hbm, hbm, hbm, hbm],
        out_specs=[hbm, hbm],
        scratch_shapes=[pltpu.VMEM((D_FF // half, D_MODEL, FFN_SUB), BF16),
                        pltpu.VMEM((D_FF, D_MODEL), BF16),
                        pltpu.VMEM((FFN_ROWS, D_FF), BF16),
                        pltpu.VMEM((2, FFN_ROWS, D_MODEL), BF16),
                        pltpu.VMEM((2, FFN_ROWS, D_MODEL), F32),
                        pltpu.VMEM((2, FFN_ROWS, D_MODEL), F32),
                        pltpu.VMEM((2, 2, D_MODEL, half), F32),
                        pltpu.VMEM((2, half, D_MODEL), F32),
                        pltpu.SemaphoreType.DMA((2, 2)),
                        pltpu.SemaphoreType.DMA((2,)),
                        pltpu.SemaphoreType.DMA((2, 3))],
        compiler_params=pltpu.CompilerParams(
            dimension_semantics=("arbitrary",), vmem_limit_bytes=VMEM_LIMIT),
        name="ffn",
    )(mod_all, wts["norm_g"], wts["ffn_conv"], x1p, h2p, x1s, h2s, wts["ffn_up"], wts["ffn_down"])


def _pos_tables(t_len):
    quarter = D_MODEL // 4
    omega = (1.0 / (np.float32(POS_BASE) ** (np.arange(quarter, dtype=np.float32) / np.float32(quarter))))
    omega = omega.astype(np.float32)

    def emb(p):
        a = p.astype(np.float32)[:, None] * omega[None, :]
        return np.concatenate([np.sin(a), np.cos(a)], axis=-1).astype(np.float32)

    return jnp.asarray(emb(np.arange(t_len // GRID_W))), jnp.asarray(emb(np.arange(GRID_W)))


def _pair_states(state_c, state_n, state_m):
    bs = state_c.shape[0]
    eye = jnp.eye(2, dtype=F32)
    c = state_c.astype(F32).reshape(bs, DEPTH, 2, PAIRS, 2, DH, DH)
    n = state_n.astype(F32).reshape(bs, DEPTH, 2, PAIRS, 2, DH)
    c_bd = jnp.einsum("...jde,jk->...jdke", c, eye).reshape(bs, DEPTH, 2, PAIRS, LANES, LANES)
    n_bd = jnp.einsum("...jd,jk,e->...jdke", n, eye, jnp.ones((DH,), F32)).reshape(
        bs, DEPTH, 2, PAIRS, LANES, LANES)
    m = jnp.pad(state_m.astype(F32), ((0, 0), (0, 0), (0, 0), (0, 16 - HEADS)))
    m = jnp.pad(m.reshape(bs, DEPTH, 1, 32), ((0, 0), (0, 0), (0, 0), (L_MF, LANES - 32 - L_MF)))
    return jnp.concatenate([c_bd, n_bd], axis=-1), m


def kernel(x_prompt, x_sample, state_C, state_n, state_m, c, c_ctx, norm_g, ada_w, ada_b, w_in,
           gate_b, gmlp_ws, gmlp_b, pool_w, pool_scale, conv_w, w_out, ffn_up, ffn_conv, ffn_down):
    bp, tp, _ = x_prompt.shape
    bs, ts, _ = x_sample.shape
    assert tp == CHUNK and ts % CHUNK == 0 and FFN_ROWS % tp == 0 and ts == FFN_ROWS

    conds = jnp.concatenate([c_ctx[None, :], c, jnp.zeros((7 - bs, D_MODEL), F32)], axis=0)
    mod_all = _modulation(conds, ada_w, ada_b)
    win_p, wout_p = _prep_weights(w_in, w_out)

    gb = jnp.pad(gate_b, ((0, 0), (0, 0), (0, 8 - HEADS))).reshape(DEPTH, 1, 32)
    dg = D_GROUP // 4
    wts = {
        "norm_g": norm_g,
        "w_in": win_p,
        "gate_b": jnp.pad(gb, ((0, 0), (0, 0), (0, LANES - 32))),
        "gmlp_ws": gmlp_ws,
        "gmlp_bias": jnp.repeat(jnp.swapaxes(gmlp_b, 1, 2), dg, axis=2),
        "pool_bd": jnp.einsum("lgcd,gh->lgchd", pool_w, jnp.eye(4, dtype=F32)).reshape(
            DEPTH, D_GROUP, D_GROUP).astype(BF16),
        "pool_scale": pool_scale.reshape(DEPTH, 1, D_GROUP),
        "conv_w": conv_w,
        "w_out": wout_p,
        "ffn_up": ffn_up,
        "ffn_conv": ffn_conv,
        "ffn_down": ffn_down,
    }
    pos = _pos_tables(ts)
    state = _pair_states(state_C, state_n, state_m)

    xp, xs = x_prompt, x_sample
    prev, ms = None, []
    for l in range(DEPTH):
        x1p, h2p, c_all, n_all, m_fin = _mixer_call(xp, l, mod_all, wts, 0, emit_state=True,
                                                    prev_states=prev, nseq=MIXER_ROWS // tp)
        prev = (c_all, n_all)
        ms.append(jnp.stack([m_fin[:, 0, L_MF:L_MF + HEADS], m_fin[:, 0, L_MB:L_MB + HEADS]], axis=1))

        x1s, h2s = _mixer_call(xs, l, mod_all, wts, 1, pos=pos if l == 0 else None, state=state)

        xp, xs = _ffn_call(x1p.reshape(bp * tp, D_MODEL), h2p.reshape(bp * tp, D_MODEL),
                           x1s.reshape(bs * ts, D_MODEL), h2s.reshape(bs * ts, D_MODEL),
                           l, mod_all, wts, tp, ts)
        xp = xp.reshape(bp, tp, D_MODEL)
        xs = xs.reshape(bs, ts, D_MODEL)

    return (xp, xs, c_all, n_all.reshape(bp, DEPTH, 2, HEADS, DH), jnp.stack(ms, axis=1))
```

```python
import functools

import numpy as np
import jax
import jax.numpy as jnp
from jax import lax
from jax.experimental import pallas as pl
from jax.experimental.pallas import tpu as pltpu

F32 = jnp.float32
BF16 = jnp.bfloat16

D_MODEL = 1024
DEPTH = 2
HEADS = 4
PAIRS = HEADS // 2
DH = 64
D_GROUP = 256
D_FF = 2816
N_MOD = 6
EPS = 1e-6
GRID_W = 64
POS_BASE = 10000.0
D_IN = 2576
N_GATE = 4 * HEADS

CHUNK = 256
GMLP_CHUNK = 128
HALO = 8
LANES = 128

C_Q, C_K, C_V, C_O = 0, 256, 512, 768
C_GU, C_GV, C_PZ, C_CB, C_CC, C_CX = 1024, 1280, 1536, 1792, 2048, 2304
Z_COLS = 2560
C_GATE = Z_COLS
W_COLS = Z_COLS + LANES
Z_GROUP = 512
L_BF, L_MF, L_BB, L_MB = 0, 8, 16, 24

MIXER_ROWS = 1024
FFN_ROWS = 1024
FFN_SUB = 256
FFN_WEIGHT_BUFS = 4
VMEM_LIMIT = 60 * 1024 * 1024


def _rms(x, g):
    ms = jnp.mean(x * x, axis=-1, keepdims=True)
    return x * lax.rsqrt(ms + EPS) * g


def _sigmoid(x):
    return 1.0 / (1.0 + jnp.exp(-x))


def _log_sigmoid(x):
    return jnp.minimum(x, 0.0) - jnp.log1p(jnp.exp(-jnp.abs(x)))


def _scan_lanes(x, op, reverse, fill):
    n = x.shape[1]
    lane = lax.broadcasted_iota(jnp.int32, x.shape, 1)
    k = 1
    while k < n:
        if reverse:
            sh = jnp.where(lane < n - k, pltpu.roll(x, n - k, axis=1), fill)
        else:
            sh = jnp.where(lane >= k, pltpu.roll(x, k, axis=1), fill)
        x = op(x, sh)
        k *= 2
    return x


def _shift_down(x, k):
    return pltpu.roll(x, k, axis=0)


def _shift_up(x, k):
    return pltpu.roll(x, x.shape[0] - k, axis=0)


def _prep_win_kernel(wt_ref, sel_ref, o_ref):
    for n in range(Z_COLS // LANES):
        r0 = LANES * n if n < 4 * D_GROUP // LANES else LANES * n + N_GATE
        o_ref[0, :, LANES * n:LANES * (n + 1)] = wt_ref[0, r0:r0 + LANES, :].T.astype(BF16)
    g = wt_ref[0, 4 * D_GROUP:4 * D_GROUP + LANES, :].T.astype(BF16)
    o_ref[0, :, C_GATE:W_COLS] = jnp.dot(g, sel_ref[...], preferred_element_type=F32).astype(BF16)


def _cast_kernel(w_ref, o_ref):
    o_ref[...] = w_ref[...].astype(BF16)


def _prep_weights(w_in, w_out):
    sel = np.zeros((LANES, LANES), np.float32)
    for j in range(N_GATE):
        sel[j, 8 * (j // HEADS) + j % HEADS] = 1.0
    params = pltpu.CompilerParams(dimension_semantics=("arbitrary", "arbitrary"),
                                  vmem_limit_bytes=VMEM_LIMIT)
    win_p = pl.pallas_call(
        _prep_win_kernel, out_shape=jax.ShapeDtypeStruct((DEPTH, D_MODEL, W_COLS), BF16),
        grid=(DEPTH, 1),
        in_specs=[pl.BlockSpec((1, D_IN, D_MODEL), lambda l, i: (l, 0, 0)),
                  pl.BlockSpec((LANES, LANES), lambda l, i: (0, 0))],
        out_specs=pl.BlockSpec((1, D_MODEL, W_COLS), lambda l, i: (l, 0, 0)),
        compiler_params=params, name="prep_w_in")(jnp.swapaxes(w_in, 1, 2), jnp.asarray(sel, BF16))
    rows = 512
    wout_p = pl.pallas_call(
        _cast_kernel, out_shape=jax.ShapeDtypeStruct(w_out.shape, BF16),
        grid=(DEPTH, D_MODEL // rows),
        in_specs=[pl.BlockSpec((1, rows, D_MODEL), lambda l, i: (l, i, 0))],
        out_specs=pl.BlockSpec((1, rows, D_MODEL), lambda l, i: (l, i, 0)),
        compiler_params=params, name="prep_w_out")(w_out)
    return win_p, wout_p


def _mod_kernel(c_ref, w_ref, b_ref, o_ref):
    c = c_ref[...]
    s = (c * _sigmoid(c)).astype(BF16)
    o_ref[0] = jnp.dot(s, w_ref[0].astype(BF16), preferred_element_type=F32) + b_ref[0]


def _modulation(conds, ada_w, ada_b):
    n_out = N_MOD * D_MODEL
    bn = 1536
    return pl.pallas_call(
        _mod_kernel,
        out_shape=jax.ShapeDtypeStruct((DEPTH, 8, n_out), F32),
        grid=(DEPTH, n_out // bn),
        in_specs=[
            pl.BlockSpec((8, D_MODEL), lambda l, j: (0, 0)),
            pl.BlockSpec((1, D_MODEL, bn), lambda l, j: (l, 0, j)),
            pl.BlockSpec((1, 1, bn), lambda l, j: (l, 0, j)),
        ],
        out_specs=pl.BlockSpec((1, 8, bn), lambda l, j: (l, 0, j)),
        compiler_params=pltpu.CompilerParams(
            dimension_semantics=("arbitrary", "arbitrary"), vmem_limit_bytes=VMEM_LIMIT),
        name="adaln_mod",
    )(conds, ada_w, ada_b.reshape(DEPTH, 1, n_out))


def _mod_rows(mod_ref, cond):
    row = mod_ref[0, pl.ds(cond, 1), :]
    return [row[:, i * D_MODEL:(i + 1) * D_MODEL] for i in range(N_MOD)]


def _gate_stats(gates):
    gt = gates.T
    b_f = _scan_lanes(_log_sigmoid(gt[8:16]), jnp.add, False, 0.0)
    g_f = gt[0:8] - b_f
    cm_f = _scan_lanes(g_f, jnp.maximum, False, -jnp.inf)
    b_b = _scan_lanes(_log_sigmoid(gt[24:32]), jnp.add, True, 0.0)
    g_b = gt[16:24] - b_b
    cm_b = _scan_lanes(g_b, jnp.maximum, True, -jnp.inf)
    stack = jnp.concatenate([b_f, cm_f, b_b, cm_b, jnp.zeros((LANES - 32, CHUNK), F32)], axis=0)
    return jnp.concatenate([g_f, g_b], axis=0), stack.T


def _pair_cols(mat, lane0, lane_lt_dh):
    return jnp.where(lane_lt_dh, mat[:, lane0:lane0 + 1], mat[:, lane0 + 1:lane0 + 2])


def _mlstm(z_ref, gcol_ref, grow_ref, cols_ref, bd_ref, ycat_ref, nc, state0, m0, need_final):
    lane_row = lax.broadcasted_iota(jnp.int32, (1, LANES), 1)
    lane = lax.broadcasted_iota(jnp.int32, (CHUNK, LANES), 1)
    lt_dh = lane < DH
    row_ll = lax.broadcasted_iota(jnp.int32, (CHUNK, CHUNK), 0)
    col_ll = lax.broadcasted_iota(jnp.int32, (CHUNK, CHUNK), 1)
    tri = (col_ll <= row_ll, col_ll >= row_ll)
    bd_row = lax.broadcasted_iota(jnp.int32, (LANES, 2 * LANES), 0)
    bd_col = lax.broadcasted_iota(jnp.int32, (LANES, 2 * LANES), 1)
    bd_mask = (bd_row < DH) == ((bd_col & (LANES - 1)) < DH)
    dirs = ((0, L_BF, L_MF, CHUNK - 1), (1, L_BB, L_MB, 0))

    m_start = [[None] * nc, [None] * nc]
    m_fin = [None, None]
    for d, _, _, last in dirs:
        m = m0
        order = range(nc) if d == 0 else range(nc - 1, -1, -1)
        for step, c in enumerate(order):
            m_start[d][c] = m
            if need_final or step < nc - 1:
                cl = cols_ref[c, last:last + 1, :]
                m = pltpu.roll(cl, 8, axis=1) + jnp.maximum(m, cl)
        m_fin[d] = m

    have_state = [[False] * nc, [False] * nc]
    fin = [None, None]
    for d, l_b, l_m, last in dirs:
        bd = list(state0[d])
        order = range(nc) if d == 0 else range(nc - 1, -1, -1)
        for step, c in enumerate(order):
            if bd[0] is not None:
                have_state[d][c] = True
                for p in range(PAIRS):
                    bd_ref[d, c, p] = bd[p].astype(BF16)
            if not (need_final or step < nc - 1):
                continue
            rows = slice(HALO + c * CHUNK, HALO + (c + 1) * CHUNK)
            cols = cols_ref[c]
            m_last = jnp.maximum(cols[last:last + 1, :], m_start[d][c])
            a_col = jnp.exp(gcol_ref[c] - cols - pltpu.roll(m_last, LANES - 8, axis=1))
            a_prev = jnp.exp(m_start[d][c] - m_last)
            for p in range(PAIRS):
                a_pair = _pair_cols(a_col, l_b + 2 * p, lt_dh)
                v_pair = z_ref[rows, C_V + LANES * p:C_V + LANES * (p + 1)]
                rhs = jnp.concatenate([v_pair * a_pair, a_pair], axis=1).astype(BF16)
                k_pair = (z_ref[rows, C_K + LANES * p:C_K + LANES * (p + 1)] * (DH ** -0.5)).astype(BF16)
                upd = lax.dot_general(k_pair, rhs, (((0,), (0,)), ((), ())),
                                      preferred_element_type=F32)
                upd = jnp.where(bd_mask, upd, 0.0)
                if bd[p] is None:
                    bd[p] = upd
                else:
                    scale = jnp.where(bd_row < DH, a_prev[:, l_m + 2 * p:l_m + 2 * p + 1],
                                      a_prev[:, l_m + 2 * p + 1:l_m + 2 * p + 2])
                    bd[p] = scale * bd[p] + upd
        fin[d] = bd

    for c in range(nc):
        rows = slice(HALO + c * CHUNK, HALO + (c + 1) * CHUNK)
        cols = cols_ref[c]
        grow = grow_ref[c]
        m_prev = jnp.where(lane_row < L_BB, m_start[0][c], m_start[1][c])
        m_all = jnp.maximum(cols, m_prev)
        w_inter = jnp.exp(m_prev - m_all)
        e_neg = jnp.exp(-(pltpu.roll(cols, 8, axis=1) + m_all))
        for p in range(PAIRS):
            q_pair = z_ref[rows, C_Q + LANES * p:C_Q + LANES * (p + 1)].astype(BF16)
            k_f32 = z_ref[rows, C_K + LANES * p:C_K + LANES * (p + 1)] * (DH ** -0.5)
            v_pair = z_ref[rows, C_V + LANES * p:C_V + LANES * (p + 1)]
            p_cat = ([], [])
            va = []
            for j in range(2):
                h = 2 * p + j
                own = lt_dh if j == 0 else jnp.logical_not(lt_dh)
                k_own = jnp.where(own, k_f32, 0.0).astype(BF16)
                s = lax.dot_general(q_pair, k_own, (((1,), (1,)), ((), ())),
                                    preferred_element_type=F32)
                va.append(jnp.concatenate([jnp.where(own, v_pair, 0.0), jnp.where(own, 1.0, 0.0)],
                                          axis=1).astype(BF16))
                for d, _, l_m, _ in dirs:
                    w = jnp.exp(jnp.where(tri[d], grow[8 * d + h:8 * d + h + 1] - m_all[:, l_m + h:l_m + h + 1],
                                          -jnp.inf))
                    p_cat[d].append((s * w).astype(BF16))
            va = jnp.concatenate(va, axis=0)
            h_sum = None
            for d, _, l_m, _ in dirs:
                tot = jnp.dot(jnp.concatenate(p_cat[d], axis=1), va, preferred_element_type=F32)
                if have_state[d][c]:
                    e_pair = _pair_cols(w_inter, l_m + 2 * p, lt_dh)
                    qc = jnp.dot(q_pair, bd_ref[d, c, p], preferred_element_type=F32)
                    tot = tot + jnp.concatenate([e_pair, e_pair], axis=1) * qc
                floor = _pair_cols(e_neg, l_m + 2 * p, lt_dh)
                h_dir = tot[:, :LANES] / jnp.maximum(jnp.abs(tot[:, LANES:]), floor)
                h_sum = h_dir if h_sum is None else h_sum + h_dir
            o_pair = z_ref[rows, C_O + LANES * p:C_O + LANES * (p + 1)]
            ycat_ref[c * CHUNK:(c + 1) * CHUNK, LANES * p:LANES * (p + 1)] = (
                _sigmoid(o_pair) * h_sum).astype(BF16)

    m_row = jnp.where(lane_row < L_BB, m_fin[0], m_fin[1])
    return fin, m_row


def _mixer_kernel(*refs, seq_len, nseq, add_pos, has_state, emit_state, cond_base, n_prev):
    it = iter(refs)
    x_ref = next(it)
    if add_pos:
        er_ref = next(it)
        ec_ref = next(it)
    mod_ref = next(it)
    ng_ref = next(it)
    win_ref = next(it)
    gb_ref = next(it)
    ws_ref = next(it)
    gbias_ref = next(it)
    pbd_ref = next(it)
    psc_ref = next(it)
    cw_ref = next(it)
    wout_ref = next(it)
    if has_state:
        bd0_ref = next(it)
        m0_ref = next(it)
    if n_prev:
        cprev_ref = next(it)
        nprev_ref = next(it)
    x1_ref = next(it)
    h2_ref = next(it)
    if emit_state:
        cout_ref = next(it)
        nout_ref = next(it)
        mout_ref = next(it)
    z_all = next(it)
    gcol_all = next(it)
    grow_all = next(it)
    cols_all = next(it)
    bd_all = next(it)
    ycat_all = next(it)

    nblk = seq_len // CHUNK
    cond = cond_base if cond_base == 0 else cond_base + pl.program_id(0)
    sh1, sc1, gt1, sh2, sc2, _ = _mod_rows(mod_ref, cond)
    ng = ng_ref[0]
    seqs = range(nseq)
    units = [(q, r) for q in seqs for r in range(nblk)]

    zeros_halo = jnp.zeros((HALO, Z_COLS - C_PZ), F32)
    for q in seqs:
        z_all[q, 0:HALO, C_PZ:Z_COLS] = zeros_halo
        z_all[q, HALO + seq_len:2 * HALO + seq_len, C_PZ:Z_COLS] = zeros_halo

    g_in = ng[0:1] * (1.0 + sc1)
    hbs = []
    for q, r in units:
        rows = slice(r * CHUNK, (r + 1) * CHUNK)
        xb = x_ref[q, rows, :]
        if add_pos:
            pieces = []
            for g in range(CHUNK // GRID_W):
                gi = r * (CHUNK // GRID_W) + g
                er = jnp.broadcast_to(er_ref[gi:gi + 1, :], (GRID_W, D_MODEL // 2))
                pieces.append(jnp.concatenate([er, ec_ref[...]], axis=1))
            xb = xb + jnp.concatenate(pieces, axis=0)
            x1_ref[q, rows, :] = xb
        hbs.append((_rms(xb, g_in) + sh1).astype(BF16))
    hb = jnp.concatenate(hbs, axis=0)
    gates_all = jnp.dot(hb, win_ref[0, :, C_GATE:W_COLS], preferred_element_type=F32) + gb_ref[0]
    for cg in range(Z_COLS // Z_GROUP):
        zc = jnp.dot(hb, win_ref[0, :, cg * Z_GROUP:(cg + 1) * Z_GROUP], preferred_element_type=F32)
        for u, (q, r) in enumerate(units):
            z_all[q, HALO + r * CHUNK:HALO + (r + 1) * CHUNK, cg * Z_GROUP:(cg + 1) * Z_GROUP] = (
                zc[u * CHUNK:(u + 1) * CHUNK])
    for u, (q, r) in enumerate(units):
        gates = gates_all[u * CHUNK:(u + 1) * CHUNK]
        gcol_all[q, r] = gates
        grow, cols = _gate_stats(gates)
        grow_all[q, r] = grow
        cols_all[q, r] = cols

    for q in seqs:
        if has_state:
            state0 = [[bd0_ref[q, 0, d, p] for p in range(PAIRS)] for d in range(2)]
            m0 = m0_ref[q, 0]
        else:
            state0 = [[None] * PAIRS, [None] * PAIRS]
            m0 = jnp.zeros((1, LANES), F32)
        fin, m_row = _mlstm(z_all.at[q], gcol_all.at[q], grow_all.at[q], cols_all.at[q], bd_all.at[q],
                            ycat_all.at[q], nblk, state0, m0, need_final=emit_state)
        if emit_state:
            for d in range(2):
                for p in range(PAIRS):
                    for j in range(2):
                        cout_ref[q, n_prev, d, 2 * p + j] = fin[d][p][DH * j:DH * (j + 1), DH * j:DH * (j + 1)]
                    n_t = fin[d][p][:, LANES:].T
                    nout_ref[q, n_prev, d:d + 1, LANES * p:LANES * (p + 1)] = n_t[0:1] + n_t[DH:DH + 1]
            mout_ref[q] = m_row
    if emit_state and n_prev:
        cout_ref[:, 0:n_prev] = cprev_ref[...]
        nout_ref[:, 0:n_prev] = nprev_ref[...]

    lane_g = lax.broadcasted_iota(jnp.int32, (CHUNK, D_GROUP), 1) // (D_GROUP // 4)
    lg = lax.broadcasted_iota(jnp.int32, (GMLP_CHUNK, D_GROUP), 1) // (D_GROUP // 4)
    row_c = lax.broadcasted_iota(jnp.int32, (CHUNK, D_GROUP), 0)
    half = jnp.where(lane_g == 0, 1, jnp.where(lane_g == 1, 2, jnp.where(lane_g == 2, 4, 8)))
    cw = cw_ref[0]
    ws_cat = jnp.concatenate([ws_ref[0, g] for g in range(4)], axis=1).astype(BF16)
    for q, r in units:
        z_ref = z_all.at[q]
        ycat_ref = ycat_all.at[q]
        rows = slice(r * CHUNK, (r + 1) * CHUNK)
        zrows = slice(HALO + r * CHUNK, HALO + (r + 1) * CHUNK)
        hrows = slice(r * CHUNK, (r + 1) * CHUNK + 2 * HALO)
        inner = slice(HALO, HALO + CHUNK)

        for s in range(CHUNK // GMLP_CHUNK):
            zs = slice(HALO + r * CHUNK + s * GMLP_CHUNK, HALO + r * CHUNK + (s + 1) * GMLP_CHUNK)
            vch = z_ref[zs, C_GV:C_GV + D_GROUP]
            v_stack = jnp.concatenate([jnp.where(lg == g, vch, 0.0) for g in range(4)], axis=0)
            mixed = gbias_ref[0] + jnp.dot(ws_cat, v_stack.astype(BF16), preferred_element_type=F32)
            yb = z_ref[zs, C_GU:C_GU + D_GROUP] * mixed
            ys = slice(r * CHUNK + s * GMLP_CHUNK, r * CHUNK + (s + 1) * GMLP_CHUNK)
            ycat_ref[ys, D_GROUP:2 * D_GROUP] = yb.astype(BF16)

        pz = z_ref[hrows, C_PZ:C_PZ + D_GROUP]
        a2 = pz + _shift_down(pz, 1)
        a4 = a2 + _shift_down(a2, 2)
        a8 = a4 + _shift_down(a4, 4)
        a16 = a8 + _shift_down(a8, 8)
        win = [a2[inner], _shift_up(a4, 1)[inner], _shift_up(a8, 3)[inner], _shift_up(a16, 7)[inner]]
        wsum = jnp.where(lane_g == 0, win[0], jnp.where(lane_g == 1, win[1],
                         jnp.where(lane_g == 2, win[2], win[3])))
        t_abs = row_c + r * CHUNK
        cnt = (jnp.minimum(t_abs + half, seq_len) - jnp.maximum(t_abs - half, 0)).astype(F32)
        pooled = wsum / cnt - pz[inner]
        yc = jnp.dot(pooled.astype(BF16), pbd_ref[0], preferred_element_type=F32) * psc_ref[0]
        ycat_ref[rows, 2 * D_GROUP:3 * D_GROUP] = yc.astype(BF16)

        u = z_ref[hrows, C_CC:C_CC + D_GROUP] * z_ref[hrows, C_CX:C_CX + D_GROUP]
        conv = (cw[0:1] * _shift_down(u, 1)[inner] + cw[1:2] * u[inner]
                + cw[2:3] * _shift_up(u, 1)[inner])
        yd = z_ref[zrows, C_CB:C_CB + D_GROUP] * conv
        ycat_ref[rows, 3 * D_GROUP:4 * D_GROUP] = yd.astype(BF16)

    y_all = jnp.dot(jnp.concatenate([ycat_all[q, r * CHUNK:(r + 1) * CHUNK, :] for q, r in units], axis=0),
                    wout_ref[0], preferred_element_type=F32)
    g_y = gt1 * ng[1:2]
    g_h2 = ng[2:3] * (1.0 + sc2)
    for u, (q, r) in enumerate(units):
        rows = slice(r * CHUNK, (r + 1) * CHUNK)
        xb = x1_ref[q, rows, :] if add_pos else x_ref[q, rows, :]
        x1 = xb + _rms(y_all[u * CHUNK:(u + 1) * CHUNK], g_y)
        x1_ref[q, rows, :] = x1
        h2_ref[q, rows, :] = (_rms(x1, g_h2) + sh2).astype(BF16)


def _layer_spec(shape, layer):
    nd = len(shape)
    return pl.BlockSpec((1,) + tuple(shape[1:]), lambda b: (layer,) + (0,) * (nd - 1),
                        pipeline_mode=pl.Buffered(1))


def _mixer_call(x, layer, mod_all, wts, cond_base, pos=None, state=None, emit_state=False,
                prev_states=None, nseq=1):
    bsz, seq_len, _ = x.shape
    nblk = seq_len // CHUNK
    add_pos = pos is not None
    has_state = state is not None
    n_prev = 0 if prev_states is None else prev_states[0].shape[1]
    args = [x]
    in_specs = [pl.BlockSpec((nseq, seq_len, D_MODEL), lambda b: (b, 0, 0))]
    if add_pos:
        args += list(pos)
        in_specs += [pl.BlockSpec(p.shape, lambda b: (0, 0), pipeline_mode=pl.Buffered(1)) for p in pos]
    args.append(mod_all)
    in_specs.append(_layer_spec(mod_all.shape, layer))
    for name in ("norm_g", "w_in", "gate_b", "gmlp_ws", "gmlp_bias", "pool_bd", "pool_scale",
                 "conv_w", "w_out"):
        args.append(wts[name])
        in_specs.append(_layer_spec(wts[name].shape, layer))
    if has_state:
        args += list(state)
        in_specs += [pl.BlockSpec((nseq, 1, 2, PAIRS, LANES, 2 * LANES), lambda b: (b, layer, 0, 0, 0, 0)),
                     pl.BlockSpec((nseq, 1, 1, LANES), lambda b: (b, layer, 0, 0))]
    out_shape = [jax.ShapeDtypeStruct((bsz, seq_len, D_MODEL), F32),
                 jax.ShapeDtypeStruct((bsz, seq_len, D_MODEL), BF16)]
    out_specs = [pl.BlockSpec((nseq, seq_len, D_MODEL), lambda b: (b, 0, 0)),
                 pl.BlockSpec((nseq, seq_len, D_MODEL), lambda b: (b, 0, 0))]
    if emit_state:
        if n_prev:
            args += list(prev_states)
            in_specs += [pl.BlockSpec((nseq, n_prev, 2, HEADS, DH, DH), lambda b: (b, 0, 0, 0, 0, 0)),
                         pl.BlockSpec((nseq, n_prev, 2, HEADS * DH), lambda b: (b, 0, 0, 0))]
        out_shape += [jax.ShapeDtypeStruct((bsz, n_prev + 1, 2, HEADS, DH, DH), F32),
                      jax.ShapeDtypeStruct((bsz, n_prev + 1, 2, HEADS * DH), F32),
                      jax.ShapeDtypeStruct((bsz, 1, LANES), F32)]
        out_specs += [pl.BlockSpec((nseq, n_prev + 1, 2, HEADS, DH, DH), lambda b: (b, 0, 0, 0, 0, 0)),
                      pl.BlockSpec((nseq, n_prev + 1, 2, HEADS * DH), lambda b: (b, 0, 0, 0)),
                      pl.BlockSpec((nseq, 1, LANES), lambda b: (b, 0, 0))]
    kern = functools.partial(_mixer_kernel, seq_len=seq_len, nseq=nseq, add_pos=add_pos,
                             has_state=has_state, emit_state=emit_state, cond_base=cond_base,
                             n_prev=n_prev)
    return pl.pallas_call(
        kern,
        out_shape=out_shape,
        grid=(bsz // nseq,),
        in_specs=in_specs,
        out_specs=out_specs,
        scratch_shapes=[pltpu.VMEM((nseq, seq_len + 2 * HALO, Z_COLS), F32),
                        pltpu.VMEM((nseq, nblk, CHUNK, LANES), F32),
                        pltpu.VMEM((nseq, nblk, 16, CHUNK), F32),
                        pltpu.VMEM((nseq, nblk, CHUNK, LANES), F32),
                        pltpu.VMEM((nseq, 2, nblk, PAIRS, LANES, 2 * LANES), BF16),
                        pltpu.VMEM((nseq, seq_len, D_MODEL), BF16)],
        compiler_params=pltpu.CompilerParams(
            dimension_semantics=("arbitrary",), vmem_limit_bytes=VMEM_LIMIT),
        name="mixer_t%d" % seq_len,
    )(*args)


def _ffn_kernel(mod_ref, ng_ref, cw_ref, x1p_hbm, h2p_hbm, x1s_hbm, h2s_hbm, up_hbm, down_hbm,
                outp_hbm, outs_hbm,
                up_bf, down_bf, act_ref, h2_buf, x1_buf, o_buf, up_stage, down_stage,
                in_sems, out_sems, w_sems, *, layer, tiles_p, tiles_s, seq_p, seq_s):
    i = pl.program_id(0)
    n_tiles = tiles_p + tiles_s
    rows = FFN_ROWS
    half = FFN_SUB // 2
    nsub = D_FF // half
    slot = i % 2
    is_prompt = i < tiles_p

    def tile_rows(tile, first_tile):
        if isinstance(tile, int):
            return pl.ds((tile - first_tile) * rows, rows)
        return pl.ds(pl.multiple_of((tile - first_tile) * rows, rows), rows)

    def fetch_from(h2_hbm, x1_hbm, r, buf):
        pltpu.make_async_copy(h2_hbm.at[r], h2_buf.at[buf], in_sems.at[0, buf]).start()
        pltpu.make_async_copy(x1_hbm.at[r], x1_buf.at[buf], in_sems.at[1, buf]).start()

    def fetch(tile, buf):
        @pl.when(tile < tiles_p)
        def _():
            fetch_from(h2p_hbm, x1p_hbm, tile_rows(tile, 0), buf)

        @pl.when(tile >= tiles_p)
        def _():
            fetch_from(h2s_hbm, x1s_hbm, tile_rows(tile, tiles_p), buf)

    def await_fetch(buf):
        r = pl.ds(0, rows)
        pltpu.make_async_copy(h2p_hbm.at[r], h2_buf.at[buf], in_sems.at[0, buf]).wait()
        pltpu.make_async_copy(x1p_hbm.at[r], x1_buf.at[buf], in_sems.at[1, buf]).wait()

    def await_store(buf):
        pltpu.make_async_copy(o_buf.at[buf], outp_hbm.at[pl.ds(0, rows)], out_sems.at[buf]).wait()

    def weight_copies(c, buf):
        aligned = (lambda v: v) if isinstance(c, int) else (lambda v: pl.multiple_of(v, half))
        cv = pl.ds(aligned(c * half), half)
        cg = pl.ds(aligned(D_FF + c * half), half)
        return (pltpu.make_async_copy(up_hbm.at[layer, :, cv], up_stage.at[buf, 0], w_sems.at[buf, 0]),
                pltpu.make_async_copy(up_hbm.at[layer, :, cg], up_stage.at[buf, 1], w_sems.at[buf, 1]),
                pltpu.make_async_copy(down_hbm.at[layer, cv, :], down_stage.at[buf], w_sems.at[buf, 2]))

    @pl.when(i == 0)
    def _():
        fetch_from(h2p_hbm, x1p_hbm, tile_rows(0, 0), 0)
        depth = up_stage.shape[0]
        for c0 in range(depth - 1):
            for cp in weight_copies(c0, c0):
                cp.start()

        def convert(c, carry):
            buf = c % depth
            ahead = c + depth - 1

            @pl.when(ahead < nsub)
            def _():
                for cp in weight_copies(ahead, ahead % depth):
                    cp.start()

            for cp in weight_copies(c, buf):
                cp.wait()
            up_bf[c, :, 0:half] = up_stage[buf, 0].astype(BF16)
            up_bf[c, :, half:FFN_SUB] = up_stage[buf, 1].astype(BF16)
            down_bf[pl.ds(pl.multiple_of(c * half, half), half), :] = down_stage[buf].astype(BF16)
            return carry

        lax.fori_loop(0, nsub, convert, 0)

    await_fetch(slot)

    @pl.when(i + 1 < n_tiles)
    def _():
        fetch(i + 1, 1 - slot)

    row = lax.broadcasted_iota(jnp.int32, (rows, FFN_SUB), 0)
    first = ((row & (seq_s - 1)) == 0) | (((row & (seq_p - 1)) == 0) & is_prompt)
    last = ((row & (seq_s - 1)) == seq_s - 1) | (((row & (seq_p - 1)) == seq_p - 1) & is_prompt)
    h2 = h2_buf[slot]
    for s in range(nsub):
        a = jnp.dot(h2, up_bf[s], preferred_element_type=F32)
        cw = jnp.concatenate([cw_ref[0, :, s * half:(s + 1) * half],
                              cw_ref[0, :, D_FF + s * half:D_FF + (s + 1) * half]], axis=1)
        a_prev = jnp.where(first, 0.0, pltpu.roll(a, 1, axis=0))
        a_next = jnp.where(last, 0.0, pltpu.roll(a, rows - 1, axis=0))
        ac = cw[0:1] * a_prev + cw[1:2] * a + cw[2:3] * a_next
        hg = 0.5 * ac[:, half:]
        act = (hg * (1.0 + jnp.tanh(hg))) * ac[:, :half]
        act_ref[:, s * half:(s + 1) * half] = act.astype(BF16)
    y = jnp.dot(act_ref[...], down_bf[...], preferred_element_type=F32)
    cond = jnp.where(is_prompt, 0, i - tiles_p + 1)
    gt2 = _mod_rows(mod_ref, cond)[5]

    @pl.when(i >= 2)
    def _():
        await_store(slot)

    o_buf[slot] = x1_buf[slot] + gt2 * _rms(y, ng_ref[0][3:4])

    @pl.when(is_prompt)
    def _():
        pltpu.make_async_copy(o_buf.at[slot], outp_hbm.at[tile_rows(i, 0)], out_sems.at[slot]).start()

    @pl.when(jnp.logical_not(is_prompt))
    def _():
        pltpu.make_async_copy(o_buf.at[slot], outs_hbm.at[tile_rows(i, tiles_p)], out_sems.at[slot]).start()

    @pl.when(i == n_tiles - 1)
    def _():
        await_store(1 - slot)
        await_store(slot)


def _ffn_call(x1p, h2p, x1s, h2s, layer, mod_all, wts, seq_p, seq_s):
    tiles_p = x1p.shape[0] // FFN_ROWS
    tiles_s = x1s.shape[0] // FFN_ROWS
    assert tiles_p >= 1 and tiles_s >= 1 and tiles_p + tiles_s >= 2
    half = FFN_SUB // 2

    def resident(shape):
        nd = len(shape)
        return pl.BlockSpec((1,) + tuple(shape[1:]), lambda i: (layer,) + (0,) * (nd - 1),
                            pipeline_mode=pl.Buffered(1))

    hbm = pl.BlockSpec(memory_space=pl.ANY)
    return pl.pallas_call(
        functools.partial(_ffn_kernel, layer=layer, tiles_p=tiles_p, tiles_s=tiles_s,
                          seq_p=seq_p, seq_s=seq_s),
        out_shape=[jax.ShapeDtypeStruct(x1p.shape, F32), jax.ShapeDtypeStruct(x1s.shape, F32)],
        grid=(tiles_p + tiles_s,),
        in_specs=[resident(mod_all.shape), resident(wts["norm_g"].shape), resident(wts["ffn_conv"].shape),
                  hbm, hbm, hbm, hbm, hbm, hbm],
        out_specs=[hbm, hbm],
        scratch_shapes=[pltpu.VMEM((D_FF // half, D_MODEL, FFN_SUB), BF16),
                        pltpu.VMEM((D_FF, D_MODEL), BF16),
                        pltpu.VMEM((FFN_ROWS, D_FF), BF16),
                        pltpu.VMEM((2, FFN_ROWS, D_MODEL), BF16),
                        pltpu.VMEM((2, FFN_ROWS, D_MODEL), F32),
                        pltpu.VMEM((2, FFN_ROWS, D_MODEL), F32),
                        pltpu.VMEM((FFN_WEIGHT_BUFS, 2, D_MODEL, half), F32),
                        pltpu.VMEM((FFN_WEIGHT_BUFS, half, D_MODEL), F32),
                        pltpu.SemaphoreType.DMA((2, 2)),
                        pltpu.SemaphoreType.DMA((2,)),
                        pltpu.SemaphoreType.DMA((FFN_WEIGHT_BUFS, 3))],
        compiler_params=pltpu.CompilerParams(
            dimension_semantics=("arbitrary",), vmem_limit_bytes=VMEM_LIMIT),
        name="ffn",
    )(mod_all, wts["norm_g"], wts["ffn_conv"], x1p, h2p, x1s, h2s, wts["ffn_up"], wts["ffn_down"])


def _pos_tables(t_len):
    quarter = D_MODEL // 4
    omega = (1.0 / (np.float32(POS_BASE) ** (np.arange(quarter, dtype=np.float32) / np.float32(quarter))))
    omega = omega.astype(np.float32)

    def emb(p):
        a = p.astype(np.float32)[:, None] * omega[None, :]
        return np.concatenate([np.sin(a), np.cos(a)], axis=-1).astype(np.float32)

    return jnp.asarray(emb(np.arange(t_len // GRID_W))), jnp.asarray(emb(np.arange(GRID_W)))


def _pair_states(state_c, state_n, state_m):
    bs = state_c.shape[0]
    eye = jnp.eye(2, dtype=F32)
    c = state_c.astype(F32).reshape(bs, DEPTH, 2, PAIRS, 2, DH, DH)
    n = state_n.astype(F32).reshape(bs, DEPTH, 2, PAIRS, 2, DH)
    c_bd = jnp.einsum("...jde,jk->...jdke", c, eye).reshape(bs, DEPTH, 2, PAIRS, LANES, LANES)
    n_bd = jnp.einsum("...jd,jk,e->...jdke", n, eye, jnp.ones((DH,), F32)).reshape(
        bs, DEPTH, 2, PAIRS, LANES, LANES)
    m = jnp.pad(state_m.astype(F32), ((0, 0), (0, 0), (0, 0), (0, 16 - HEADS)))
    m = jnp.pad(m.reshape(bs, DEPTH, 1, 32), ((0, 0), (0, 0), (0, 0), (L_MF, LANES - 32 - L_MF)))
    return jnp.concatenate([c_bd, n_bd], axis=-1), m


def kernel(x_prompt, x_sample, state_C, state_n, state_m, c, c_ctx, norm_g, ada_w, ada_b, w_in,
           gate_b, gmlp_ws, gmlp_b, pool_w, pool_scale, conv_w, w_out, ffn_up, ffn_conv, ffn_down):
    bp, tp, _ = x_prompt.shape
    bs, ts, _ = x_sample.shape
    assert tp == CHUNK and ts % CHUNK == 0 and FFN_ROWS % tp == 0 and ts == FFN_ROWS

    conds = jnp.concatenate([c_ctx[None, :], c, jnp.zeros((7 - bs, D_MODEL), F32)], axis=0)
    mod_all = _modulation(conds, ada_w, ada_b)
    win_p, wout_p = _prep_weights(w_in, w_out)

    gb = jnp.pad(gate_b, ((0, 0), (0, 0), (0, 8 - HEADS))).reshape(DEPTH, 1, 32)
    dg = D_GROUP // 4
    wts = {
        "norm_g": norm_g,
        "w_in": win_p,
        "gate_b": jnp.pad(gb, ((0, 0), (0, 0), (0, LANES - 32))),
        "gmlp_ws": gmlp_ws,
        "gmlp_bias": jnp.repeat(jnp.swapaxes(gmlp_b, 1, 2), dg, axis=2),
        "pool_bd": jnp.einsum("lgcd,gh->lgchd", pool_w, jnp.eye(4, dtype=F32)).reshape(
            DEPTH, D_GROUP, D_GROUP).astype(BF16),
        "pool_scale": pool_scale.reshape(DEPTH, 1, D_GROUP),
        "conv_w": conv_w,
        "w_out": wout_p,
        "ffn_up": ffn_up,
        "ffn_conv": ffn_conv,
        "ffn_down": ffn_down,
    }
    pos = _pos_tables(ts)
    state = _pair_states(state_C, state_n, state_m)

    xp, xs = x_prompt, x_sample
    prev, ms = None, []
    for l in range(DEPTH):
        x1p, h2p, c_all, n_all, m_fin = _mixer_call(xp, l, mod_all, wts, 0, emit_state=True,
                                                    prev_states=prev, nseq=MIXER_ROWS // tp)
        prev = (c_all, n_all)
        ms.append(jnp.stack([m_fin[:, 0, L_MF:L_MF + HEADS], m_fin[:, 0, L_MB:L_MB + HEADS]], axis=1))

        x1s, h2s = _mixer_call(xs, l, mod_all, wts, 1, pos=pos if l == 0 else None, state=state)

        xp, xs = _ffn_call(x1p.reshape(bp * tp, D_MODEL), h2p.reshape(bp * tp, D_MODEL),
                           x1s.reshape(bs * ts, D_MODEL), h2s.reshape(bs * ts, D_MODEL),
                           l, mod_all, wts, tp, ts)
        xp = xp.reshape(bp, tp, D_MODEL)
        xs = xs.reshape(bs, ts, D_MODEL)

    return (xp, xs, c_all, n_all.reshape(bp, DEPTH, 2, HEADS, DH), jnp.stack(ms, axis=1))
```

```python
import functools

import numpy as np
import jax
import jax.numpy as jnp
from jax import lax
from jax.experimental import pallas as pl
from jax.experimental.pallas import tpu as pltpu

F32 = jnp.float32
BF16 = jnp.bfloat16

D_MODEL = 1024
DEPTH = 2
HEADS = 4
PAIRS = HEADS // 2
DH = 64
D_GROUP = 256
D_FF = 2816
N_MOD = 6
EPS = 1e-6
GRID_W = 64
POS_BASE = 10000.0
D_IN = 2576
N_GATE = 4 * HEADS

CHUNK = 256
GMLP_CHUNK = 128
HALO = 8
LANES = 128

C_Q, C_K, C_V, C_O = 0, 256, 512, 768
C_GU, C_GV, C_PZ, C_CB, C_CC, C_CX = 1024, 1280, 1536, 1792, 2048, 2304
Z_COLS = 2560
C_GATE = Z_COLS
W_COLS = Z_COLS + LANES
Z_GROUP = 512
L_BF, L_MF, L_BB, L_MB = 0, 8, 16, 24

MIXER_ROWS = 1024
FFN_ROWS = 1024
FFN_SUB = 256
FFN_WEIGHT_BUFS = 8
VMEM_LIMIT = 60 * 1024 * 1024


def _rms(x, g):
    ms = jnp.mean(x * x, axis=-1, keepdims=True)
    return x * lax.rsqrt(ms + EPS) * g


def _sigmoid(x):
    return 1.0 / (1.0 + jnp.exp(-x))


def _log_sigmoid(x):
    return jnp.minimum(x, 0.0) - jnp.log1p(jnp.exp(-jnp.abs(x)))


def _scan_lanes(x, op, reverse, fill):
    n = x.shape[1]
    lane = lax.broadcasted_iota(jnp.int32, x.shape, 1)
    k = 1
    while k < n:
        if reverse:
            sh = jnp.where(lane < n - k, pltpu.roll(x, n - k, axis=1), fill)
        else:
            sh = jnp.where(lane >= k, pltpu.roll(x, k, axis=1), fill)
        x = op(x, sh)
        k *= 2
    return x


def _shift_down(x, k):
    return pltpu.roll(x, k, axis=0)


def _shift_up(x, k):
    return pltpu.roll(x, x.shape[0] - k, axis=0)


def _prep_win_kernel(wt_ref, sel_ref, o_ref):
    for n in range(Z_COLS // LANES):
        r0 = LANES * n if n < 4 * D_GROUP // LANES else LANES * n + N_GATE
        o_ref[0, :, LANES * n:LANES * (n + 1)] = wt_ref[0, r0:r0 + LANES, :].T.astype(BF16)
    g = wt_ref[0, 4 * D_GROUP:4 * D_GROUP + LANES, :].T.astype(BF16)
    o_ref[0, :, C_GATE:W_COLS] = jnp.dot(g, sel_ref[...], preferred_element_type=F32).astype(BF16)


def _cast_kernel(w_ref, o_ref):
    o_ref[...] = w_ref[...].astype(BF16)


def _prep_weights(w_in, w_out):
    sel = np.zeros((LANES, LANES), np.float32)
    for j in range(N_GATE):
        sel[j, 8 * (j // HEADS) + j % HEADS] = 1.0
    params = pltpu.CompilerParams(dimension_semantics=("arbitrary", "arbitrary"),
                                  vmem_limit_bytes=VMEM_LIMIT)
    win_p = pl.pallas_call(
        _prep_win_kernel, out_shape=jax.ShapeDtypeStruct((DEPTH, D_MODEL, W_COLS), BF16),
        grid=(DEPTH, 1),
        in_specs=[pl.BlockSpec((1, D_IN, D_MODEL), lambda l, i: (l, 0, 0)),
                  pl.BlockSpec((LANES, LANES), lambda l, i: (0, 0))],
        out_specs=pl.BlockSpec((1, D_MODEL, W_COLS), lambda l, i: (l, 0, 0)),
        compiler_params=params, name="prep_w_in")(jnp.swapaxes(w_in, 1, 2), jnp.asarray(sel, BF16))
    rows = 512
    wout_p = pl.pallas_call(
        _cast_kernel, out_shape=jax.ShapeDtypeStruct(w_out.shape, BF16),
        grid=(DEPTH, D_MODEL // rows),
        in_specs=[pl.BlockSpec((1, rows, D_MODEL), lambda l, i: (l, i, 0))],
        out_specs=pl.BlockSpec((1, rows, D_MODEL), lambda l, i: (l, i, 0)),
        compiler_params=params, name="prep_w_out")(w_out)
    return win_p, wout_p


def _mod_kernel(c_ref, w_ref, b_ref, o_ref):
    c = c_ref[...]
    s = (c * _sigmoid(c)).astype(BF16)
    o_ref[0] = jnp.dot(s, w_ref[0].astype(BF16), preferred_element_type=F32) + b_ref[0]


def _modulation(conds, ada_w, ada_b):
    n_out = N_MOD * D_MODEL
    bn = 1536
    return pl.pallas_call(
        _mod_kernel,
        out_shape=jax.ShapeDtypeStruct((DEPTH, 8, n_out), F32),
        grid=(DEPTH, n_out // bn),
        in_specs=[
            pl.BlockSpec((8, D_MODEL), lambda l, j: (0, 0)),
            pl.BlockSpec((1, D_MODEL, bn), lambda l, j: (l, 0, j)),
            pl.BlockSpec((1, 1, bn), lambda l, j: (l, 0, j)),
        ],
        out_specs=pl.BlockSpec((1, 8, bn), lambda l, j: (l, 0, j)),
        compiler_params=pltpu.CompilerParams(
            dimension_semantics=("arbitrary", "arbitrary"), vmem_limit_bytes=VMEM_LIMIT),
        name="adaln_mod",
    )(conds, ada_w, ada_b.reshape(DEPTH, 1, n_out))


def _mod_rows(mod_ref, cond):
    row = mod_ref[0, pl.ds(cond, 1), :]
    return [row[:, i * D_MODEL:(i + 1) * D_MODEL] for i in range(N_MOD)]


def _gate_stats(gates):
    gt = gates.T
    b_f = _scan_lanes(_log_sigmoid(gt[8:16]), jnp.add, False, 0.0)
    g_f = gt[0:8] - b_f
    cm_f = _scan_lanes(g_f, jnp.maximum, False, -jnp.inf)
    b_b = _scan_lanes(_log_sigmoid(gt[24:32]), jnp.add, True, 0.0)
    g_b = gt[16:24] - b_b
    cm_b = _scan_lanes(g_b, jnp.maximum, True, -jnp.inf)
    stack = jnp.concatenate([b_f, cm_f, b_b, cm_b, jnp.zeros((LANES - 32, CHUNK), F32)], axis=0)
    return jnp.concatenate([g_f, g_b], axis=0), stack.T


def _pair_cols(mat, lane0, lane_lt_dh):
    return jnp.where(lane_lt_dh, mat[:, lane0:lane0 + 1], mat[:, lane0 + 1:lane0 + 2])


def _mlstm(z_ref, gcol_ref, grow_ref, cols_ref, bd_ref, ycat_ref, nc, state0, m0, need_final):
    lane_row = lax.broadcasted_iota(jnp.int32, (1, LANES), 1)
    lane = lax.broadcasted_iota(jnp.int32, (CHUNK, LANES), 1)
    lt_dh = lane < DH
    row_ll = lax.broadcasted_iota(jnp.int32, (CHUNK, CHUNK), 0)
    col_ll = lax.broadcasted_iota(jnp.int32, (CHUNK, CHUNK), 1)
    tri = (col_ll <= row_ll, col_ll >= row_ll)
    bd_row = lax.broadcasted_iota(jnp.int32, (LANES, 2 * LANES), 0)
    bd_col = lax.broadcasted_iota(jnp.int32, (LANES, 2 * LANES), 1)
    bd_mask = (bd_row < DH) == ((bd_col & (LANES - 1)) < DH)
    dirs = ((0, L_BF, L_MF, CHUNK - 1), (1, L_BB, L_MB, 0))

    m_start = [[None] * nc, [None] * nc]
    m_fin = [None, None]
    for d, _, _, last in dirs:
        m = m0
        order = range(nc) if d == 0 else range(nc - 1, -1, -1)
        for step, c in enumerate(order):
            m_start[d][c] = m
            if need_final or step < nc - 1:
                cl = cols_ref[c, last:last + 1, :]
                m = pltpu.roll(cl, 8, axis=1) + jnp.maximum(m, cl)
        m_fin[d] = m

    have_state = [[False] * nc, [False] * nc]
    fin = [None, None]
    for d, l_b, l_m, last in dirs:
        bd = list(state0[d])
        order = range(nc) if d == 0 else range(nc - 1, -1, -1)
        for step, c in enumerate(order):
            if bd[0] is not None:
                have_state[d][c] = True
                for p in range(PAIRS):
                    bd_ref[d, c, p] = bd[p].astype(BF16)
            if not (need_final or step < nc - 1):
                continue
            rows = slice(HALO + c * CHUNK, HALO + (c + 1) * CHUNK)
            cols = cols_ref[c]
            m_last = jnp.maximum(cols[last:last + 1, :], m_start[d][c])
            a_col = jnp.exp(gcol_ref[c] - cols - pltpu.roll(m_last, LANES - 8, axis=1))
            a_prev = jnp.exp(m_start[d][c] - m_last)
            for p in range(PAIRS):
                a_pair = _pair_cols(a_col, l_b + 2 * p, lt_dh)
                v_pair = z_ref[rows, C_V + LANES * p:C_V + LANES * (p + 1)]
                rhs = jnp.concatenate([v_pair * a_pair, a_pair], axis=1).astype(BF16)
                k_pair = (z_ref[rows, C_K + LANES * p:C_K + LANES * (p + 1)] * (DH ** -0.5)).astype(BF16)
                upd = lax.dot_general(k_pair, rhs, (((0,), (0,)), ((), ())),
                                      preferred_element_type=F32)
                upd = jnp.where(bd_mask, upd, 0.0)
                if bd[p] is None:
                    bd[p] = upd
                else:
                    scale = jnp.where(bd_row < DH, a_prev[:, l_m + 2 * p:l_m + 2 * p + 1],
                                      a_prev[:, l_m + 2 * p + 1:l_m + 2 * p + 2])
                    bd[p] = scale * bd[p] + upd
        fin[d] = bd

    for c in range(nc):
        rows = slice(HALO + c * CHUNK, HALO + (c + 1) * CHUNK)
        cols = cols_ref[c]
        grow = grow_ref[c]
        m_prev = jnp.where(lane_row < L_BB, m_start[0][c], m_start[1][c])
        m_all = jnp.maximum(cols, m_prev)
        w_inter = jnp.exp(m_prev - m_all)
        e_neg = jnp.exp(-(pltpu.roll(cols, 8, axis=1) + m_all))
        for p in range(PAIRS):
            q_pair = z_ref[rows, C_Q + LANES * p:C_Q + LANES * (p + 1)].astype(BF16)
            k_f32 = z_ref[rows, C_K + LANES * p:C_K + LANES * (p + 1)] * (DH ** -0.5)
            v_pair = z_ref[rows, C_V + LANES * p:C_V + LANES * (p + 1)]
            p_cat = ([], [])
            va = []
            for j in range(2):
                h = 2 * p + j
                own = lt_dh if j == 0 else jnp.logical_not(lt_dh)
                k_own = jnp.where(own, k_f32, 0.0).astype(BF16)
                s = lax.dot_general(q_pair, k_own, (((1,), (1,)), ((), ())),
                                    preferred_element_type=F32)
                va.append(jnp.concatenate([jnp.where(own, v_pair, 0.0), jnp.where(own, 1.0, 0.0)],
                                          axis=1).astype(BF16))
                for d, _, l_m, _ in dirs:
                    w = jnp.exp(jnp.where(tri[d], grow[8 * d + h:8 * d + h + 1] - m_all[:, l_m + h:l_m + h + 1],
                                          -jnp.inf))
                    p_cat[d].append((s * w).astype(BF16))
            va = jnp.concatenate(va, axis=0)
            h_sum = None
            for d, _, l_m, _ in dirs:
                tot = jnp.dot(jnp.concatenate(p_cat[d], axis=1), va, preferred_element_type=F32)
                if have_state[d][c]:
                    e_pair = _pair_cols(w_inter, l_m + 2 * p, lt_dh)
                    qc = jnp.dot(q_pair, bd_ref[d, c, p], preferred_element_type=F32)
                    tot = tot + jnp.concatenate([e_pair, e_pair], axis=1) * qc
                floor = _pair_cols(e_neg, l_m + 2 * p, lt_dh)
                h_dir = tot[:, :LANES] / jnp.maximum(jnp.abs(tot[:, LANES:]), floor)
                h_sum = h_dir if h_sum is None else h_sum + h_dir
            o_pair = z_ref[rows, C_O + LANES * p:C_O + LANES * (p + 1)]
            ycat_ref[c * CHUNK:(c + 1) * CHUNK, LANES * p:LANES * (p + 1)] = (
                _sigmoid(o_pair) * h_sum).astype(BF16)

    m_row = jnp.where(lane_row < L_BB, m_fin[0], m_fin[1])
    return fin, m_row


def _mixer_kernel(*refs, seq_len, nseq, add_pos, has_state, emit_state, cond_base, n_prev):
    it = iter(refs)
    x_ref = next(it)
    if add_pos:
        er_ref = next(it)
        ec_ref = next(it)
    mod_ref = next(it)
    ng_ref = next(it)
    win_ref = next(it)
    gb_ref = next(it)
    ws_ref = next(it)
    gbias_ref = next(it)
    pbd_ref = next(it)
    psc_ref = next(it)
    cw_ref = next(it)
    wout_ref = next(it)
    if has_state:
        bd0_ref = next(it)
        m0_ref = next(it)
    if n_prev:
        cprev_ref = next(it)
        nprev_ref = next(it)
    x1_ref = next(it)
    h2_ref = next(it)
    if emit_state:
        cout_ref = next(it)
        nout_ref = next(it)
        mout_ref = next(it)
    z_all = next(it)
    gcol_all = next(it)
    grow_all = next(it)
    cols_all = next(it)
    bd_all = next(it)
    ycat_all = next(it)

    nblk = seq_len // CHUNK
    cond = cond_base if cond_base == 0 else cond_base + pl.program_id(0)
    sh1, sc1, gt1, sh2, sc2, _ = _mod_rows(mod_ref, cond)
    ng = ng_ref[0]
    seqs = range(nseq)
    units = [(q, r) for q in seqs for r in range(nblk)]

    zeros_halo = jnp.zeros((HALO, Z_COLS - C_PZ), F32)
    for q in seqs:
        z_all[q, 0:HALO, C_PZ:Z_COLS] = zeros_halo
        z_all[q, HALO + seq_len:2 * HALO + seq_len, C_PZ:Z_COLS] = zeros_halo

    g_in = ng[0:1] * (1.0 + sc1)
    hbs = []
    for q, r in units:
        rows = slice(r * CHUNK, (r + 1) * CHUNK)
        xb = x_ref[q, rows, :]
        if add_pos:
            pieces = []
            for g in range(CHUNK // GRID_W):
                gi = r * (CHUNK // GRID_W) + g
                er = jnp.broadcast_to(er_ref[gi:gi + 1, :], (GRID_W, D_MODEL // 2))
                pieces.append(jnp.concatenate([er, ec_ref[...]], axis=1))
            xb = xb + jnp.concatenate(pieces, axis=0)
            x1_ref[q, rows, :] = xb
        hbs.append((_rms(xb, g_in) + sh1).astype(BF16))
    hb = jnp.concatenate(hbs, axis=0)
    gates_all = jnp.dot(hb, win_ref[0, :, C_GATE:W_COLS], preferred_element_type=F32) + gb_ref[0]
    for cg in range(Z_COLS // Z_GROUP):
        zc = jnp.dot(hb, win_ref[0, :, cg * Z_GROUP:(cg + 1) * Z_GROUP], preferred_element_type=F32)
        for u, (q, r) in enumerate(units):
            z_all[q, HALO + r * CHUNK:HALO + (r + 1) * CHUNK, cg * Z_GROUP:(cg + 1) * Z_GROUP] = (
                zc[u * CHUNK:(u + 1) * CHUNK])
    for u, (q, r) in enumerate(units):
        gates = gates_all[u * CHUNK:(u + 1) * CHUNK]
        gcol_all[q, r] = gates
        grow, cols = _gate_stats(gates)
        grow_all[q, r] = grow
        cols_all[q, r] = cols

    for q in seqs:
        if has_state:
            state0 = [[bd0_ref[q, 0, d, p] for p in range(PAIRS)] for d in range(2)]
            m0 = m0_ref[q, 0]
        else:
            state0 = [[None] * PAIRS, [None] * PAIRS]
            m0 = jnp.zeros((1, LANES), F32)
        fin, m_row = _mlstm(z_all.at[q], gcol_all.at[q], grow_all.at[q], cols_all.at[q], bd_all.at[q],
                            ycat_all.at[q], nblk, state0, m0, need_final=emit_state)
        if emit_state:
            for d in range(2):
                for p in range(PAIRS):
                    for j in range(2):
                        cout_ref[q, n_prev, d, 2 * p + j] = fin[d][p][DH * j:DH * (j + 1), DH * j:DH * (j + 1)]
                    n_t = fin[d][p][:, LANES:].T
                    nout_ref[q, n_prev, d:d + 1, LANES * p:LANES * (p + 1)] = n_t[0:1] + n_t[DH:DH + 1]
            mout_ref[q] = m_row
    if emit_state and n_prev:
        cout_ref[:, 0:n_prev] = cprev_ref[...]
        nout_ref[:, 0:n_prev] = nprev_ref[...]

    lane_g = lax.broadcasted_iota(jnp.int32, (CHUNK, D_GROUP), 1) // (D_GROUP // 4)
    lg = lax.broadcasted_iota(jnp.int32, (GMLP_CHUNK, D_GROUP), 1) // (D_GROUP // 4)
    row_c = lax.broadcasted_iota(jnp.int32, (CHUNK, D_GROUP), 0)
    half = jnp.where(lane_g == 0, 1, jnp.where(lane_g == 1, 2, jnp.where(lane_g == 2, 4, 8)))
    cw = cw_ref[0]
    ws_cat = jnp.concatenate([ws_ref[0, g] for g in range(4)], axis=1).astype(BF16)
    for q, r in units:
        z_ref = z_all.at[q]
        ycat_ref = ycat_all.at[q]
        rows = slice(r * CHUNK, (r + 1) * CHUNK)
        zrows = slice(HALO + r * CHUNK, HALO + (r + 1) * CHUNK)
        hrows = slice(r * CHUNK, (r + 1) * CHUNK + 2 * HALO)
        inner = slice(HALO, HALO + CHUNK)

        for s in range(CHUNK // GMLP_CHUNK):
            zs = slice(HALO + r * CHUNK + s * GMLP_CHUNK, HALO + r * CHUNK + (s + 1) * GMLP_CHUNK)
            vch = z_ref[zs, C_GV:C_GV + D_GROUP]
            v_stack = jnp.concatenate([jnp.where(lg == g, vch, 0.0) for g in range(4)], axis=0)
            mixed = gbias_ref[0] + jnp.dot(ws_cat, v_stack.astype(BF16), preferred_element_type=F32)
            yb = z_ref[zs, C_GU:C_GU + D_GROUP] * mixed
            ys = slice(r * CHUNK + s * GMLP_CHUNK, r * CHUNK + (s + 1) * GMLP_CHUNK)
            ycat_ref[ys, D_GROUP:2 * D_GROUP] = yb.astype(BF16)

        pz = z_ref[hrows, C_PZ:C_PZ + D_GROUP]
        a2 = pz + _shift_down(pz, 1)
        a4 = a2 + _shift_down(a2, 2)
        a8 = a4 + _shift_down(a4, 4)
        a16 = a8 + _shift_down(a8, 8)
        win = [a2[inner], _shift_up(a4, 1)[inner], _shift_up(a8, 3)[inner], _shift_up(a16, 7)[inner]]
        wsum = jnp.where(lane_g == 0, win[0], jnp.where(lane_g == 1, win[1],
                         jnp.where(lane_g == 2, win[2], win[3])))
        t_abs = row_c + r * CHUNK
        cnt = (jnp.minimum(t_abs + half, seq_len) - jnp.maximum(t_abs - half, 0)).astype(F32)
        pooled = wsum / cnt - pz[inner]
        yc = jnp.dot(pooled.astype(BF16), pbd_ref[0], preferred_element_type=F32) * psc_ref[0]
        ycat_ref[rows, 2 * D_GROUP:3 * D_GROUP] = yc.astype(BF16)

        u = z_ref[hrows, C_CC:C_CC + D_GROUP] * z_ref[hrows, C_CX:C_CX + D_GROUP]
        conv = (cw[0:1] * _shift_down(u, 1)[inner] + cw[1:2] * u[inner]
                + cw[2:3] * _shift_up(u, 1)[inner])
        yd = z_ref[zrows, C_CB:C_CB + D_GROUP] * conv
        ycat_ref[rows, 3 * D_GROUP:4 * D_GROUP] = yd.astype(BF16)

    y_all = jnp.dot(jnp.concatenate([ycat_all[q, r * CHUNK:(r + 1) * CHUNK, :] for q, r in units], axis=0),
                    wout_ref[0], preferred_element_type=F32)
    g_y = gt1 * ng[1:2]
    g_h2 = ng[2:3] * (1.0 + sc2)
    for u, (q, r) in enumerate(units):
        rows = slice(r * CHUNK, (r + 1) * CHUNK)
        xb = x1_ref[q, rows, :] if add_pos else x_ref[q, rows, :]
        x1 = xb + _rms(y_all[u * CHUNK:(u + 1) * CHUNK], g_y)
        x1_ref[q, rows, :] = x1
        h2_ref[q, rows, :] = (_rms(x1, g_h2) + sh2).astype(BF16)


def _layer_spec(shape, layer):
    nd = len(shape)
    return pl.BlockSpec((1,) + tuple(shape[1:]), lambda b: (layer,) + (0,) * (nd - 1),
                        pipeline_mode=pl.Buffered(1))


def _mixer_call(x, layer, mod_all, wts, cond_base, pos=None, state=None, emit_state=False,
                prev_states=None, nseq=1):
    bsz, seq_len, _ = x.shape
    nblk = seq_len // CHUNK
    add_pos = pos is not None
    has_state = state is not None
    n_prev = 0 if prev_states is None else prev_states[0].shape[1]
    args = [x]
    in_specs = [pl.BlockSpec((nseq, seq_len, D_MODEL), lambda b: (b, 0, 0))]
    if add_pos:
        args += list(pos)
        in_specs += [pl.BlockSpec(p.shape, lambda b: (0, 0), pipeline_mode=pl.Buffered(1)) for p in pos]
    args.append(mod_all)
    in_specs.append(_layer_spec(mod_all.shape, layer))
    for name in ("norm_g", "w_in", "gate_b", "gmlp_ws", "gmlp_bias", "pool_bd", "pool_scale",
                 "conv_w", "w_out"):
        args.append(wts[name])
        in_specs.append(_layer_spec(wts[name].shape, layer))
    if has_state:
        args += list(state)
        in_specs += [pl.BlockSpec((nseq, 1, 2, PAIRS, LANES, 2 * LANES), lambda b: (b, layer, 0, 0, 0, 0)),
                     pl.BlockSpec((nseq, 1, 1, LANES), lambda b: (b, layer, 0, 0))]
    out_shape = [jax.ShapeDtypeStruct((bsz, seq_len, D_MODEL), F32),
                 jax.ShapeDtypeStruct((bsz, seq_len, D_MODEL), BF16)]
    out_specs = [pl.BlockSpec((nseq, seq_len, D_MODEL), lambda b: (b, 0, 0)),
                 pl.BlockSpec((nseq, seq_len, D_MODEL), lambda b: (b, 0, 0))]
    if emit_state:
        if n_prev:
            args += list(prev_states)
            in_specs += [pl.BlockSpec((nseq, n_prev, 2, HEADS, DH, DH), lambda b: (b, 0, 0, 0, 0, 0)),
                         pl.BlockSpec((nseq, n_prev, 2, HEADS * DH), lambda b: (b, 0, 0, 0))]
        out_shape += [jax.ShapeDtypeStruct((bsz, n_prev + 1, 2, HEADS, DH, DH), F32),
                      jax.ShapeDtypeStruct((bsz, n_prev + 1, 2, HEADS * DH), F32),
                      jax.ShapeDtypeStruct((bsz, 1, LANES), F32)]
        out_specs += [pl.BlockSpec((nseq, n_prev + 1, 2, HEADS, DH, DH), lambda b: (b, 0, 0, 0, 0, 0)),
                      pl.BlockSpec((nseq, n_prev + 1, 2, HEADS * DH), lambda b: (b, 0, 0, 0)),
                      pl.BlockSpec((nseq, 1, LANES), lambda b: (b, 0, 0))]
    kern = functools.partial(_mixer_kernel, seq_len=seq_len, nseq=nseq, add_pos=add_pos,
                             has_state=has_state, emit_state=emit_state, cond_base=cond_base,
                             n_prev=n_prev)
    return pl.pallas_call(
        kern,
        out_shape=out_shape,
        grid=(bsz // nseq,),
        in_specs=in_specs,
        out_specs=out_specs,
        scratch_shapes=[pltpu.VMEM((nseq, seq_len + 2 * HALO, Z_COLS), F32),
                        pltpu.VMEM((nseq, nblk, CHUNK, LANES), F32),
                        pltpu.VMEM((nseq, nblk, 16, CHUNK), F32),
                        pltpu.VMEM((nseq, nblk, CHUNK, LANES), F32),
                        pltpu.VMEM((nseq, 2, nblk, PAIRS, LANES, 2 * LANES), BF16),
                        pltpu.VMEM((nseq, seq_len, D_MODEL), BF16)],
        compiler_params=pltpu.CompilerParams(
            dimension_semantics=("arbitrary",), vmem_limit_bytes=VMEM_LIMIT),
        name="mixer_t%d" % seq_len,
    )(*args)


def _ffn_kernel(mod_ref, ng_ref, cw_ref, x1p_hbm, h2p_hbm, x1s_hbm, h2s_hbm, up_hbm, down_hbm,
                outp_hbm, outs_hbm,
                up_bf, down_bf, act_ref, h2_buf, x1_buf, o_buf, up_stage, down_stage,
                in_sems, out_sems, w_sems, *, layer, tiles_p, tiles_s, seq_p, seq_s):
    i = pl.program_id(0)
    n_tiles = tiles_p + tiles_s
    rows = FFN_ROWS
    half = FFN_SUB // 2
    nsub = D_FF // half
    slot = i % 2
    is_prompt = i < tiles_p

    def tile_rows(tile, first_tile):
        if isinstance(tile, int):
            return pl.ds((tile - first_tile) * rows, rows)
        return pl.ds(pl.multiple_of((tile - first_tile) * rows, rows), rows)

    def h2_copy(h2_hbm, r, buf):
        return pltpu.make_async_copy(h2_hbm.at[r], h2_buf.at[buf], in_sems.at[buf])

    def x1_copy(x1_hbm, r):
        return pltpu.make_async_copy(x1_hbm.at[r], x1_buf, in_sems.at[2])

    def fetch_h2(tile, buf):
        @pl.when(tile < tiles_p)
        def _():
            h2_copy(h2p_hbm, tile_rows(tile, 0), buf).start()

        @pl.when(tile >= tiles_p)
        def _():
            h2_copy(h2s_hbm, tile_rows(tile, tiles_p), buf).start()

    def fetch_x1(tile):
        @pl.when(tile < tiles_p)
        def _():
            x1_copy(x1p_hbm, tile_rows(tile, 0)).start()

        @pl.when(tile >= tiles_p)
        def _():
            x1_copy(x1s_hbm, tile_rows(tile, tiles_p)).start()

    def await_store(buf):
        pltpu.make_async_copy(o_buf.at[buf], outp_hbm.at[pl.ds(0, rows)], out_sems.at[buf]).wait()

    def weight_copies(c, buf):
        aligned = (lambda v: v) if isinstance(c, int) else (lambda v: pl.multiple_of(v, half))
        cv = pl.ds(aligned(c * half), half)
        cg = pl.ds(aligned(D_FF + c * half), half)
        return (pltpu.make_async_copy(up_hbm.at[layer, :, cv], up_stage.at[buf, 0], w_sems.at[buf, 0]),
                pltpu.make_async_copy(up_hbm.at[layer, :, cg], up_stage.at[buf, 1], w_sems.at[buf, 1]),
                pltpu.make_async_copy(down_hbm.at[layer, cv, :], down_stage.at[buf], w_sems.at[buf, 2]))

    @pl.when(i == 0)
    def _():
        h2_copy(h2p_hbm, tile_rows(0, 0), 0).start()
        depth = up_stage.shape[0]
        for c0 in range(depth - 1):
            for cp in weight_copies(c0, c0):
                cp.start()

        def convert(c, carry):
            buf = c % depth
            ahead = c + depth - 1

            @pl.when(ahead < nsub)
            def _():
                for cp in weight_copies(ahead, ahead % depth):
                    cp.start()

            for cp in weight_copies(c, buf):
                cp.wait()
            up_bf[c, :, 0:half] = up_stage[buf, 0].astype(BF16)
            up_bf[c, :, half:FFN_SUB] = up_stage[buf, 1].astype(BF16)
            down_bf[pl.ds(pl.multiple_of(c * half, half), half), :] = down_stage[buf].astype(BF16)
            return carry

        lax.fori_loop(0, nsub, convert, 0)

    h2_copy(h2p_hbm, pl.ds(0, rows), slot).wait()

    @pl.when(i >= 2)
    def _():
        await_store(slot)

    fetch_x1(i)

    @pl.when(i + 1 < n_tiles)
    def _():
        fetch_h2(i + 1, 1 - slot)

    row = lax.broadcasted_iota(jnp.int32, (rows, FFN_SUB), 0)
    first = ((row & (seq_s - 1)) == 0) | (((row & (seq_p - 1)) == 0) & is_prompt)
    last = ((row & (seq_s - 1)) == seq_s - 1) | (((row & (seq_p - 1)) == seq_p - 1) & is_prompt)
    h2 = h2_buf[slot]
    for s in range(nsub):
        a = jnp.dot(h2, up_bf[s], preferred_element_type=F32)
        cw = jnp.concatenate([cw_ref[0, :, s * half:(s + 1) * half],
                              cw_ref[0, :, D_FF + s * half:D_FF + (s + 1) * half]], axis=1)
        a_prev = jnp.where(first, 0.0, pltpu.roll(a, 1, axis=0))
        a_next = jnp.where(last, 0.0, pltpu.roll(a, rows - 1, axis=0))
        ac = cw[0:1] * a_prev + cw[1:2] * a + cw[2:3] * a_next
        hg = 0.5 * ac[:, half:]
        act = (hg * (1.0 + jnp.tanh(hg))) * ac[:, :half]
        act_ref[:, s * half:(s + 1) * half] = act.astype(BF16)
    y = jnp.dot(act_ref[...], down_bf[...], preferred_element_type=F32)
    cond = jnp.where(is_prompt, 0, i - tiles_p + 1)
    gt2 = _mod_rows(mod_ref, cond)[5]

    res = gt2 * _rms(y, ng_ref[0][3:4])
    x1_copy(x1p_hbm, pl.ds(0, rows)).wait()
    o_buf[slot] = x1_buf[...] + res

    @pl.when(is_prompt)
    def _():
        pltpu.make_async_copy(o_buf.at[slot], outp_hbm.at[tile_rows(i, 0)], out_sems.at[slot]).start()

    @pl.when(jnp.logical_not(is_prompt))
    def _():
        pltpu.make_async_copy(o_buf.at[slot], outs_hbm.at[tile_rows(i, tiles_p)], out_sems.at[slot]).start()

    @pl.when(i == n_tiles - 1)
    def _():
        await_store(1 - slot)
        await_store(slot)


def _ffn_call(x1p, h2p, x1s, h2s, layer, mod_all, wts, seq_p, seq_s):
    tiles_p = x1p.shape[0] // FFN_ROWS
    tiles_s = x1s.shape[0] // FFN_ROWS
    assert tiles_p >= 1 and tiles_s >= 1 and tiles_p + tiles_s >= 2
    half = FFN_SUB // 2

    def resident(shape):
        nd = len(shape)
        return pl.BlockSpec((1,) + tuple(shape[1:]), lambda i: (layer,) + (0,) * (nd - 1),
                            pipeline_mode=pl.Buffered(1))

    hbm = pl.BlockSpec(memory_space=pl.ANY)
    return pl.pallas_call(
        functools.partial(_ffn_kernel, layer=layer, tiles_p=tiles_p, tiles_s=tiles_s,
                          seq_p=seq_p, seq_s=seq_s),
        out_shape=[jax.ShapeDtypeStruct(x1p.shape, F32), jax.ShapeDtypeStruct(x1s.shape, F32)],
        grid=(tiles_p + tiles_s,),
        in_specs=[resident(mod_all.shape), resident(wts["norm_g"].shape), resident(wts["ffn_conv"].shape),
                  hbm, hbm, hbm, hbm, hbm, hbm],
        out_specs=[hbm, hbm],
        scratch_shapes=[pltpu.VMEM((D_FF // half, D_MODEL, FFN_SUB), BF16),
                        pltpu.VMEM((D_FF, D_MODEL), BF16),
                        pltpu.VMEM((FFN_ROWS, D_FF), BF16),
                        pltpu.VMEM((2, FFN_ROWS, D_MODEL), BF16),
                        pltpu.VMEM((FFN_ROWS, D_MODEL), F32),
                        pltpu.VMEM((2, FFN_ROWS, D_MODEL), F32),
                        pltpu.VMEM((FFN_WEIGHT_BUFS, 2, D_MODEL, half), F32),
                        pltpu.VMEM((FFN_WEIGHT_BUFS, half, D_MODEL), F32),
                        pltpu.SemaphoreType.DMA((3,)),
                        pltpu.SemaphoreType.DMA((2,)),
                        pltpu.SemaphoreType.DMA((FFN_WEIGHT_BUFS, 3))],
        compiler_params=pltpu.CompilerParams(
            dimension_semantics=("arbitrary",), vmem_limit_bytes=VMEM_LIMIT),
        name="ffn",
    )(mod_all, wts["norm_g"], wts["ffn_conv"], x1p, h2p, x1s, h2s, wts["ffn_up"], wts["ffn_down"])


def _pos_tables(t_len):
    quarter = D_MODEL // 4
    omega = (1.0 / (np.float32(POS_BASE) ** (np.arange(quarter, dtype=np.float32) / np.float32(quarter))))
    omega = omega.astype(np.float32)

    def emb(p):
        a = p.astype(np.float32)[:, None] * omega[None, :]
        return np.concatenate([np.sin(a), np.cos(a)], axis=-1).astype(np.float32)

    return jnp.asarray(emb(np.arange(t_len // GRID_W))), jnp.asarray(emb(np.arange(GRID_W)))


def _pair_states(state_c, state_n, state_m):
    bs = state_c.shape[0]
    eye = jnp.eye(2, dtype=F32)
    c = state_c.astype(F32).reshape(bs, DEPTH, 2, PAIRS, 2, DH, DH)
    n = state_n.astype(F32).reshape(bs, DEPTH, 2, PAIRS, 2, DH)
    c_bd = jnp.einsum("...jde,jk->...jdke", c, eye).reshape(bs, DEPTH, 2, PAIRS, LANES, LANES)
    n_bd = jnp.einsum("...jd,jk,e->...jdke", n, eye, jnp.ones((DH,), F32)).reshape(
        bs, DEPTH, 2, PAIRS, LANES, LANES)
    m = jnp.pad(state_m.astype(F32), ((0, 0), (0, 0), (0, 0), (0, 16 - HEADS)))
    m = jnp.pad(m.reshape(bs, DEPTH, 1, 32), ((0, 0), (0, 0), (0, 0), (L_MF, LANES - 32 - L_MF)))
    return jnp.concatenate([c_bd, n_bd], axis=-1), m


def kernel(x_prompt, x_sample, state_C, state_n, state_m, c, c_ctx, norm_g, ada_w, ada_b, w_in,
           gate_b, gmlp_ws, gmlp_b, pool_w, pool_scale, conv_w, w_out, ffn_up, ffn_conv, ffn_down):
    bp, tp, _ = x_prompt.shape
    bs, ts, _ = x_sample.shape
    assert tp == CHUNK and ts % CHUNK == 0 and FFN_ROWS % tp == 0 and ts == FFN_ROWS

    conds = jnp.concatenate([c_ctx[None, :], c, jnp.zeros((7 - bs, D_MODEL), F32)], axis=0)
    mod_all = _modulation(conds, ada_w, ada_b)
    win_p, wout_p = _prep_weights(w_in, w_out)

    gb = jnp.pad(gate_b, ((0, 0), (0, 0), (0, 8 - HEADS))).reshape(DEPTH, 1, 32)
    dg = D_GROUP // 4
    wts = {
        "norm_g": norm_g,
        "w_in": win_p,
        "gate_b": jnp.pad(gb, ((0, 0), (0, 0), (0, LANES - 32))),
        "gmlp_ws": gmlp_ws,
        "gmlp_bias": jnp.repeat(jnp.swapaxes(gmlp_b, 1, 2), dg, axis=2),
        "pool_bd": jnp.einsum("lgcd,gh->lgchd", pool_w, jnp.eye(4, dtype=F32)).reshape(
            DEPTH, D_GROUP, D_GROUP).astype(BF16),
        "pool_scale": pool_scale.reshape(DEPTH, 1, D_GROUP),
        "conv_w": conv_w,
        "w_out": wout_p,
        "ffn_up": ffn_up,
        "ffn_conv": ffn_conv,
        "ffn_down": ffn_down,
    }
    pos = _pos_tables(ts)
    state = _pair_states(state_C, state_n, state_m)

    xp, xs = x_prompt, x_sample
    prev, ms = None, []
    for l in range(DEPTH):
        x1p, h2p, c_all, n_all, m_fin = _mixer_call(xp, l, mod_all, wts, 0, emit_state=True,
                                                    prev_states=prev, nseq=MIXER_ROWS // tp)
        prev = (c_all, n_all)
        ms.append(jnp.stack([m_fin[:, 0, L_MF:L_MF + HEADS], m_fin[:, 0, L_MB:L_MB + HEADS]], axis=1))

        x1s, h2s = _mixer_call(xs, l, mod_all, wts, 1, pos=pos if l == 0 else None, state=state)

        xp, xs = _ffn_call(x1p.reshape(bp * tp, D_MODEL), h2p.reshape(bp * tp, D_MODEL),
                           x1s.reshape(bs * ts, D_MODEL), h2s.reshape(bs * ts, D_MODEL),
                           l, mod_all, wts, tp, ts)
        xp = xp.reshape(bp, tp, D_MODEL)
        xs = xs.reshape(bs, ts, D_MODEL)

    return (xp, xs, c_all, n_all.reshape(bp, DEPTH, 2, HEADS, DH), jnp.stack(ms, axis=1))
```

```python
import functools

import numpy as np
import jax
import jax.numpy as jnp
from jax import lax
from jax.experimental import pallas as pl
from jax.experimental.pallas import tpu as pltpu

F32 = jnp.float32
BF16 = jnp.bfloat16

D_MODEL = 1024
DEPTH = 2
HEADS = 4
PAIRS = HEADS // 2
DH = 64
D_GROUP = 256
D_FF = 2816
N_MOD = 6
EPS = 1e-6
GRID_W = 64
POS_BASE = 10000.0
D_IN = 2576
N_GATE = 4 * HEADS

CHUNK = 256
GMLP_CHUNK = 128
HALO = 8
LANES = 128

C_Q, C_K, C_V, C_O = 0, 256, 512, 768
C_GU, C_GV, C_PZ, C_CB, C_CC, C_CX = 1024, 1280, 1536, 1792, 2048, 2304
Z_COLS = 2560
C_GATE = Z_COLS
W_COLS = Z_COLS + LANES
Z_GROUP = 512
L_BF, L_MF, L_BB, L_MB = 0, 8, 16, 24

MIXER_ROWS = 1024
FFN_ROWS = 1024
FFN_SUB = 256
FFN_WEIGHT_BUFS = 8
VMEM_LIMIT = 60 * 1024 * 1024


def _rms(x, g):
    ms = jnp.mean(x * x, axis=-1, keepdims=True)
    return x * lax.rsqrt(ms + EPS) * g


def _sigmoid(x):
    return 1.0 / (1.0 + jnp.exp(-x))


def _log_sigmoid(x):
    return jnp.minimum(x, 0.0) - jnp.log1p(jnp.exp(-jnp.abs(x)))


def _scan_lanes(x, op, reverse, fill):
    n = x.shape[1]
    lane = lax.broadcasted_iota(jnp.int32, x.shape, 1)
    k = 1
    while k < n:
        if reverse:
            sh = jnp.where(lane < n - k, pltpu.roll(x, n - k, axis=1), fill)
        else:
            sh = jnp.where(lane >= k, pltpu.roll(x, k, axis=1), fill)
        x = op(x, sh)
        k *= 2
    return x


def _shift_down(x, k):
    return pltpu.roll(x, k, axis=0)


def _shift_up(x, k):
    return pltpu.roll(x, x.shape[0] - k, axis=0)


def _prep_win_kernel(wt_ref, sel_ref, o_ref):
    for n in range(Z_COLS // LANES):
        r0 = LANES * n if n < 4 * D_GROUP // LANES else LANES * n + N_GATE
        o_ref[0, :, LANES * n:LANES * (n + 1)] = wt_ref[0, r0:r0 + LANES, :].T.astype(BF16)
    g = wt_ref[0, 4 * D_GROUP:4 * D_GROUP + LANES, :].T.astype(BF16)
    o_ref[0, :, C_GATE:W_COLS] = jnp.dot(g, sel_ref[...], preferred_element_type=F32).astype(BF16)


def _cast_kernel(w_ref, o_ref):
    o_ref[...] = w_ref[...].astype(BF16)


def _prep_weights(w_in, w_out):
    sel = np.zeros((LANES, LANES), np.float32)
    for j in range(N_GATE):
        sel[j, 8 * (j // HEADS) + j % HEADS] = 1.0
    params = pltpu.CompilerParams(dimension_semantics=("arbitrary", "arbitrary"),
                                  vmem_limit_bytes=VMEM_LIMIT)
    win_p = pl.pallas_call(
        _prep_win_kernel, out_shape=jax.ShapeDtypeStruct((DEPTH, D_MODEL, W_COLS), BF16),
        grid=(DEPTH, 1),
        in_specs=[pl.BlockSpec((1, D_IN, D_MODEL), lambda l, i: (l, 0, 0)),
                  pl.BlockSpec((LANES, LANES), lambda l, i: (0, 0))],
        out_specs=pl.BlockSpec((1, D_MODEL, W_COLS), lambda l, i: (l, 0, 0)),
        compiler_params=params, name="prep_w_in")(jnp.swapaxes(w_in, 1, 2), jnp.asarray(sel, BF16))
    rows = 512
    wout_p = pl.pallas_call(
        _cast_kernel, out_shape=jax.ShapeDtypeStruct(w_out.shape, BF16),
        grid=(DEPTH, D_MODEL // rows),
        in_specs=[pl.BlockSpec((1, rows, D_MODEL), lambda l, i: (l, i, 0))],
        out_specs=pl.BlockSpec((1, rows, D_MODEL), lambda l, i: (l, i, 0)),
        compiler_params=params, name="prep_w_out")(w_out)
    return win_p, wout_p


def _mod_kernel(c_ref, w_ref, b_ref, o_ref):
    c = c_ref[...]
    s = (c * _sigmoid(c)).astype(BF16)
    bias = b_ref[pl.ds(pl.program_id(0), 1), :]
    o_ref[0] = jnp.dot(s, w_ref[0].astype(BF16), preferred_element_type=F32) + bias


def _modulation(conds, ada_w, ada_b):
    n_out = N_MOD * D_MODEL
    bn = 1536
    return pl.pallas_call(
        _mod_kernel,
        out_shape=jax.ShapeDtypeStruct((DEPTH, 8, n_out), F32),
        grid=(DEPTH, n_out // bn),
        in_specs=[
            pl.BlockSpec((8, D_MODEL), lambda l, j: (0, 0)),
            pl.BlockSpec((1, D_MODEL, bn), lambda l, j: (l, 0, j)),
            pl.BlockSpec((DEPTH, bn), lambda l, j: (0, j)),
        ],
        out_specs=pl.BlockSpec((1, 8, bn), lambda l, j: (l, 0, j)),
        compiler_params=pltpu.CompilerParams(
            dimension_semantics=("arbitrary", "arbitrary"), vmem_limit_bytes=VMEM_LIMIT),
        name="adaln_mod",
    )(conds, ada_w, ada_b)


def _mod_rows(mod_ref, cond):
    row = mod_ref[0, pl.ds(cond, 1), :]
    return [row[:, i * D_MODEL:(i + 1) * D_MODEL] for i in range(N_MOD)]


def _gate_stats(gates):
    gt = gates.T
    b_f = _scan_lanes(_log_sigmoid(gt[8:16]), jnp.add, False, 0.0)
    g_f = gt[0:8] - b_f
    cm_f = _scan_lanes(g_f, jnp.maximum, False, -jnp.inf)
    b_b = _scan_lanes(_log_sigmoid(gt[24:32]), jnp.add, True, 0.0)
    g_b = gt[16:24] - b_b
    cm_b = _scan_lanes(g_b, jnp.maximum, True, -jnp.inf)
    stack = jnp.concatenate([b_f, cm_f, b_b, cm_b, jnp.zeros((LANES - 32, CHUNK), F32)], axis=0)
    return jnp.concatenate([g_f, g_b], axis=0), stack.T


def _pair_cols(mat, lane0, lane_lt_dh):
    return jnp.where(lane_lt_dh, mat[:, lane0:lane0 + 1], mat[:, lane0 + 1:lane0 + 2])


def _mlstm(z_ref, gcol_ref, grow_ref, cols_ref, bd_ref, ycat_ref, nc, state0, m0, need_final):
    lane_row = lax.broadcasted_iota(jnp.int32, (1, LANES), 1)
    lane = lax.broadcasted_iota(jnp.int32, (CHUNK, LANES), 1)
    lt_dh = lane < DH
    row_ll = lax.broadcasted_iota(jnp.int32, (CHUNK, CHUNK), 0)
    col_ll = lax.broadcasted_iota(jnp.int32, (CHUNK, CHUNK), 1)
    tri = (col_ll <= row_ll, col_ll >= row_ll)
    bd_row = lax.broadcasted_iota(jnp.int32, (LANES, 2 * LANES), 0)
    bd_col = lax.broadcasted_iota(jnp.int32, (LANES, 2 * LANES), 1)
    bd_mask = (bd_row < DH) == ((bd_col & (LANES - 1)) < DH)
    dirs = ((0, L_BF, L_MF, CHUNK - 1), (1, L_BB, L_MB, 0))

    m_start = [[None] * nc, [None] * nc]
    m_fin = [None, None]
    for d, _, _, last in dirs:
        m = m0
        order = range(nc) if d == 0 else range(nc - 1, -1, -1)
        for step, c in enumerate(order):
            m_start[d][c] = m
            if need_final or step < nc - 1:
                cl = cols_ref[c, last:last + 1, :]
                m = pltpu.roll(cl, 8, axis=1) + jnp.maximum(m, cl)
        m_fin[d] = m

    have_state = [[False] * nc, [False] * nc]
    fin = [None, None]
    for d, l_b, l_m, last in dirs:
        bd = list(state0[d])
        order = range(nc) if d == 0 else range(nc - 1, -1, -1)
        for step, c in enumerate(order):
            if bd[0] is not None:
                have_state[d][c] = True
                for p in range(PAIRS):
                    bd_ref[d, c, p] = bd[p].astype(BF16)
            if not (need_final or step < nc - 1):
                continue
            rows = slice(HALO + c * CHUNK, HALO + (c + 1) * CHUNK)
            cols = cols_ref[c]
            m_last = jnp.maximum(cols[last:last + 1, :], m_start[d][c])
            a_col = jnp.exp(gcol_ref[c] - cols - pltpu.roll(m_last, LANES - 8, axis=1))
            a_prev = jnp.exp(m_start[d][c] - m_last)
            for p in range(PAIRS):
                a_pair = _pair_cols(a_col, l_b + 2 * p, lt_dh)
                v_pair = z_ref[rows, C_V + LANES * p:C_V + LANES * (p + 1)]
                rhs = jnp.concatenate([v_pair * a_pair, a_pair], axis=1).astype(BF16)
                k_pair = (z_ref[rows, C_K + LANES * p:C_K + LANES * (p + 1)] * (DH ** -0.5)).astype(BF16)
                upd = lax.dot_general(k_pair, rhs, (((0,), (0,)), ((), ())),
                                      preferred_element_type=F32)
                upd = jnp.where(bd_mask, upd, 0.0)
                if bd[p] is None:
                    bd[p] = upd
                else:
                    scale = jnp.where(bd_row < DH, a_prev[:, l_m + 2 * p:l_m + 2 * p + 1],
                                      a_prev[:, l_m + 2 * p + 1:l_m + 2 * p + 2])
                    bd[p] = scale * bd[p] + upd
        fin[d] = bd

    for c in range(nc):
        rows = slice(HALO + c * CHUNK, HALO + (c + 1) * CHUNK)
        cols = cols_ref[c]
        grow = grow_ref[c]
        m_prev = jnp.where(lane_row < L_BB, m_start[0][c], m_start[1][c])
        m_all = jnp.maximum(cols, m_prev)
        w_inter = jnp.exp(m_prev - m_all)
        e_neg = jnp.exp(-(pltpu.roll(cols, 8, axis=1) + m_all))
        for p in range(PAIRS):
            q_pair = z_ref[rows, C_Q + LANES * p:C_Q + LANES * (p + 1)].astype(BF16)
            k_f32 = z_ref[rows, C_K + LANES * p:C_K + LANES * (p + 1)] * (DH ** -0.5)
            v_pair = z_ref[rows, C_V + LANES * p:C_V + LANES * (p + 1)]
            p_cat = ([], [])
            va = []
            for j in range(2):
                h = 2 * p + j
                own = lt_dh if j == 0 else jnp.logical_not(lt_dh)
                k_own = jnp.where(own, k_f32, 0.0).astype(BF16)
                s = lax.dot_general(q_pair, k_own, (((1,), (1,)), ((), ())),
                                    preferred_element_type=F32)
                va.append(jnp.concatenate([jnp.where(own, v_pair, 0.0), jnp.where(own, 1.0, 0.0)],
                                          axis=1).astype(BF16))
                for d, _, l_m, _ in dirs:
                    w = jnp.exp(jnp.where(tri[d], grow[8 * d + h:8 * d + h + 1] - m_all[:, l_m + h:l_m + h + 1],
                                          -jnp.inf))
                    p_cat[d].append((s * w).astype(BF16))
            va = jnp.concatenate(va, axis=0)
            h_sum = None
            for d, _, l_m, _ in dirs:
                tot = jnp.dot(jnp.concatenate(p_cat[d], axis=1), va, preferred_element_type=F32)
                if have_state[d][c]:
                    e_pair = _pair_cols(w_inter, l_m + 2 * p, lt_dh)
                    qc = jnp.dot(q_pair, bd_ref[d, c, p], preferred_element_type=F32)
                    tot = tot + jnp.concatenate([e_pair, e_pair], axis=1) * qc
                floor = _pair_cols(e_neg, l_m + 2 * p, lt_dh)
                h_dir = tot[:, :LANES] / jnp.maximum(jnp.abs(tot[:, LANES:]), floor)
                h_sum = h_dir if h_sum is None else h_sum + h_dir
            o_pair = z_ref[rows, C_O + LANES * p:C_O + LANES * (p + 1)]
            ycat_ref[c * CHUNK:(c + 1) * CHUNK, LANES * p:LANES * (p + 1)] = (
                _sigmoid(o_pair) * h_sum).astype(BF16)

    m_row = jnp.where(lane_row < L_BB, m_fin[0], m_fin[1])
    return fin, m_row


def _pair_state(c_ref, n_ref, q, d, p):
    eye = (lax.broadcasted_iota(jnp.int32, (DH, DH), 0) == lax.broadcasted_iota(jnp.int32, (DH, DH), 1))
    zero = jnp.zeros((DH, DH), F32)
    blocks = []
    for j in range(2):
        h = 2 * p + j
        n_row = n_ref[q, 0, d, h:h + 1, :]
        n_col = jnp.sum(jnp.where(eye, n_row, 0.0), axis=1, keepdims=True)
        row = [zero] * 4
        row[j] = c_ref[q, 0, d, h]
        row[2 + j] = jnp.broadcast_to(n_col, (DH, DH))
        blocks.append(jnp.concatenate(row, axis=1))
    return jnp.concatenate(blocks, axis=0)


def _mixer_kernel(*refs, layer, seq_len, nseq, add_pos, has_state, emit_state, cond_base, n_prev):
    it = iter(refs)
    x_ref = next(it)
    if add_pos:
        er_ref = next(it)
        ec_ref = next(it)
    mod_ref = next(it)
    ng_ref = next(it)
    win_ref = next(it)
    gb_ref = next(it)
    ws_ref = next(it)
    gbias_ref = next(it)
    pbd_ref = next(it)
    psc_ref = next(it)
    cw_ref = next(it)
    wout_ref = next(it)
    if has_state:
        c0_ref = next(it)
        n0_ref = next(it)
        m0_ref = next(it)
    if n_prev:
        cprev_ref = next(it)
        nprev_ref = next(it)
    x1_ref = next(it)
    h2_ref = next(it)
    if emit_state:
        cout_ref = next(it)
        nout_ref = next(it)
        mout_ref = next(it)
    z_all = next(it)
    gcol_all = next(it)
    grow_all = next(it)
    cols_all = next(it)
    bd_all = next(it)
    ycat_all = next(it)

    nblk = seq_len // CHUNK
    cond = cond_base if cond_base == 0 else cond_base + pl.program_id(0)
    sh1, sc1, gt1, sh2, sc2, _ = _mod_rows(mod_ref, cond)
    ng = ng_ref[0]
    seqs = range(nseq)
    units = [(q, r) for q in seqs for r in range(nblk)]

    zeros_halo = jnp.zeros((HALO, Z_COLS - C_PZ), F32)
    for q in seqs:
        z_all[q, 0:HALO, C_PZ:Z_COLS] = zeros_halo
        z_all[q, HALO + seq_len:2 * HALO + seq_len, C_PZ:Z_COLS] = zeros_halo

    g_in = ng[0:1] * (1.0 + sc1)
    hbs = []
    for q, r in units:
        rows = slice(r * CHUNK, (r + 1) * CHUNK)
        xb = x_ref[q, rows, :]
        if add_pos:
            pieces = []
            for g in range(CHUNK // GRID_W):
                gi = r * (CHUNK // GRID_W) + g
                er = jnp.broadcast_to(er_ref[gi:gi + 1, :], (GRID_W, D_MODEL // 2))
                pieces.append(jnp.concatenate([er, ec_ref[...]], axis=1))
            xb = xb + jnp.concatenate(pieces, axis=0)
            x1_ref[q, rows, :] = xb
        hbs.append((_rms(xb, g_in) + sh1).astype(BF16))
    hb = jnp.concatenate(hbs, axis=0)
    gates_all = jnp.dot(hb, win_ref[0, :, C_GATE:W_COLS], preferred_element_type=F32) + gb_ref[0]
    for cg in range(Z_COLS // Z_GROUP):
        zc = jnp.dot(hb, win_ref[0, :, cg * Z_GROUP:(cg + 1) * Z_GROUP], preferred_element_type=F32)
        for u, (q, r) in enumerate(units):
            z_all[q, HALO + r * CHUNK:HALO + (r + 1) * CHUNK, cg * Z_GROUP:(cg + 1) * Z_GROUP] = (
                zc[u * CHUNK:(u + 1) * CHUNK])
    for u, (q, r) in enumerate(units):
        gates = gates_all[u * CHUNK:(u + 1) * CHUNK]
        gcol_all[q, r] = gates
        grow, cols = _gate_stats(gates)
        grow_all[q, r] = grow
        cols_all[q, r] = cols

    for q in seqs:
        if has_state:
            state0 = [[_pair_state(c0_ref, n0_ref, q, d, p) for p in range(PAIRS)] for d in range(2)]
            m0 = m0_ref[q, 0]
        else:
            state0 = [[None] * PAIRS, [None] * PAIRS]
            m0 = jnp.zeros((1, LANES), F32)
        fin, m_row = _mlstm(z_all.at[q], gcol_all.at[q], grow_all.at[q], cols_all.at[q], bd_all.at[q],
                            ycat_all.at[q], nblk, state0, m0, need_final=emit_state)
        if emit_state:
            for d in range(2):
                for p in range(PAIRS):
                    for j in range(2):
                        cout_ref[q, n_prev, d, 2 * p + j] = fin[d][p][DH * j:DH * (j + 1), DH * j:DH * (j + 1)]
                    n_t = fin[d][p][:, LANES:].T
                    nout_ref[q, n_prev, d, 2 * p:2 * p + 1, :] = n_t[0:1, 0:DH]
                    nout_ref[q, n_prev, d, 2 * p + 1:2 * p + 2, :] = n_t[DH:DH + 1, DH:LANES]
            mout_ref[q] = m_row
    if emit_state and n_prev:
        cout_ref[:, 0:n_prev] = cprev_ref[...]
        nout_ref[:, 0:n_prev] = nprev_ref[...]

    lane_g = lax.broadcasted_iota(jnp.int32, (CHUNK, D_GROUP), 1) // (D_GROUP // 4)
    lg = lax.broadcasted_iota(jnp.int32, (GMLP_CHUNK, D_GROUP), 1) // (D_GROUP // 4)
    row_c = lax.broadcasted_iota(jnp.int32, (CHUNK, D_GROUP), 0)
    half = jnp.where(lane_g == 0, 1, jnp.where(lane_g == 1, 2, jnp.where(lane_g == 2, 4, 8)))
    cw = cw_ref[0]
    ws_cat = jnp.concatenate([ws_ref[0, g] for g in range(4)], axis=1).astype(BF16)
    for q, r in units:
        z_ref = z_all.at[q]
        ycat_ref = ycat_all.at[q]
        rows = slice(r * CHUNK, (r + 1) * CHUNK)
        zrows = slice(HALO + r * CHUNK, HALO + (r + 1) * CHUNK)
        hrows = slice(r * CHUNK, (r + 1) * CHUNK + 2 * HALO)
        inner = slice(HALO, HALO + CHUNK)

        for s in range(CHUNK // GMLP_CHUNK):
            zs = slice(HALO + r * CHUNK + s * GMLP_CHUNK, HALO + r * CHUNK + (s + 1) * GMLP_CHUNK)
            vch = z_ref[zs, C_GV:C_GV + D_GROUP]
            v_stack = jnp.concatenate([jnp.where(lg == g, vch, 0.0) for g in range(4)], axis=0)
            mixed = gbias_ref[0] + jnp.dot(ws_cat, v_stack.astype(BF16), preferred_element_type=F32)
            yb = z_ref[zs, C_GU:C_GU + D_GROUP] * mixed
            ys = slice(r * CHUNK + s * GMLP_CHUNK, r * CHUNK + (s + 1) * GMLP_CHUNK)
            ycat_ref[ys, D_GROUP:2 * D_GROUP] = yb.astype(BF16)

        pz = z_ref[hrows, C_PZ:C_PZ + D_GROUP]
        a2 = pz + _shift_down(pz, 1)
        a4 = a2 + _shift_down(a2, 2)
        a8 = a4 + _shift_down(a4, 4)
        a16 = a8 + _shift_down(a8, 8)
        win = [a2[inner], _shift_up(a4, 1)[inner], _shift_up(a8, 3)[inner], _shift_up(a16, 7)[inner]]
        wsum = jnp.where(lane_g == 0, win[0], jnp.where(lane_g == 1, win[1],
                         jnp.where(lane_g == 2, win[2], win[3])))
        t_abs = row_c + r * CHUNK
        cnt = (jnp.minimum(t_abs + half, seq_len) - jnp.maximum(t_abs - half, 0)).astype(F32)
        pooled = wsum / cnt - pz[inner]
        yc = (jnp.dot(pooled.astype(BF16), pbd_ref[0], preferred_element_type=F32)
              * psc_ref[layer:layer + 1, :])
        ycat_ref[rows, 2 * D_GROUP:3 * D_GROUP] = yc.astype(BF16)

        u = z_ref[hrows, C_CC:C_CC + D_GROUP] * z_ref[hrows, C_CX:C_CX + D_GROUP]
        conv = (cw[0:1] * _shift_down(u, 1)[inner] + cw[1:2] * u[inner]
                + cw[2:3] * _shift_up(u, 1)[inner])
        yd = z_ref[zrows, C_CB:C_CB + D_GROUP] * conv
        ycat_ref[rows, 3 * D_GROUP:4 * D_GROUP] = yd.astype(BF16)

    y_all = jnp.dot(jnp.concatenate([ycat_all[q, r * CHUNK:(r + 1) * CHUNK, :] for q, r in units], axis=0),
                    wout_ref[0], preferred_element_type=F32)
    g_y = gt1 * ng[1:2]
    g_h2 = ng[2:3] * (1.0 + sc2)
    for u, (q, r) in enumerate(units):
        rows = slice(r * CHUNK, (r + 1) * CHUNK)
        xb = x1_ref[q, rows, :] if add_pos else x_ref[q, rows, :]
        x1 = xb + _rms(y_all[u * CHUNK:(u + 1) * CHUNK], g_y)
        x1_ref[q, rows, :] = x1
        h2_ref[q, rows, :] = (_rms(x1, g_h2) + sh2).astype(BF16)


def _layer_spec(shape, layer):
    nd = len(shape)
    return pl.BlockSpec((1,) + tuple(shape[1:]), lambda b: (layer,) + (0,) * (nd - 1),
                        pipeline_mode=pl.Buffered(1))


def _mixer_call(x, layer, mod_all, wts, cond_base, pos=None, state=None, emit_state=False,
                prev_states=None, nseq=1):
    bsz, seq_len, _ = x.shape
    nblk = seq_len // CHUNK
    add_pos = pos is not None
    has_state = state is not None
    n_prev = 0 if prev_states is None else prev_states[0].shape[1]
    args = [x]
    in_specs = [pl.BlockSpec((nseq, seq_len, D_MODEL), lambda b: (b, 0, 0))]
    if add_pos:
        args += list(pos)
        in_specs += [pl.BlockSpec(p.shape, lambda b: (0, 0), pipeline_mode=pl.Buffered(1)) for p in pos]
    args.append(mod_all)
    in_specs.append(_layer_spec(mod_all.shape, layer))
    for name in ("norm_g", "w_in", "gate_b", "gmlp_ws", "gmlp_bias", "pool_bd", "pool_scale",
                 "conv_w", "w_out"):
        args.append(wts[name])
        if wts[name].ndim == 2:
            in_specs.append(pl.BlockSpec(wts[name].shape, lambda b: (0, 0), pipeline_mode=pl.Buffered(1)))
        else:
            in_specs.append(_layer_spec(wts[name].shape, layer))
    if has_state:
        args += list(state)
        in_specs += [pl.BlockSpec((nseq, 1, 2, HEADS, DH, DH), lambda b: (b, layer, 0, 0, 0, 0)),
                     pl.BlockSpec((nseq, 1, 2, HEADS, DH), lambda b: (b, layer, 0, 0, 0)),
                     pl.BlockSpec((nseq, 1, 1, LANES), lambda b: (b, layer, 0, 0))]
    out_shape = [jax.ShapeDtypeStruct((bsz, seq_len, D_MODEL), F32),
                 jax.ShapeDtypeStruct((bsz, seq_len, D_MODEL), BF16)]
    out_specs = [pl.BlockSpec((nseq, seq_len, D_MODEL), lambda b: (b, 0, 0)),
                 pl.BlockSpec((nseq, seq_len, D_MODEL), lambda b: (b, 0, 0))]
    if emit_state:
        if n_prev:
            args += list(prev_states)
            in_specs += [pl.BlockSpec((nseq, n_prev, 2, HEADS, DH, DH), lambda b: (b, 0, 0, 0, 0, 0)),
                         pl.BlockSpec((nseq, n_prev, 2, HEADS, DH), lambda b: (b, 0, 0, 0, 0))]
        out_shape += [jax.ShapeDtypeStruct((bsz, n_prev + 1, 2, HEADS, DH, DH), F32),
                      jax.ShapeDtypeStruct((bsz, n_prev + 1, 2, HEADS, DH), F32),
                      jax.ShapeDtypeStruct((bsz, 1, LANES), F32)]
        out_specs += [pl.BlockSpec((nseq, n_prev + 1, 2, HEADS, DH, DH), lambda b: (b, 0, 0, 0, 0, 0)),
                      pl.BlockSpec((nseq, n_prev + 1, 2, HEADS, DH), lambda b: (b, 0, 0, 0, 0)),
                      pl.BlockSpec((nseq, 1, LANES), lambda b: (b, 0, 0))]
    kern = functools.partial(_mixer_kernel, layer=layer, seq_len=seq_len, nseq=nseq, add_pos=add_pos,
                             has_state=has_state, emit_state=emit_state, cond_base=cond_base,
                             n_prev=n_prev)
    return pl.pallas_call(
        kern,
        out_shape=out_shape,
        grid=(bsz // nseq,),
        in_specs=in_specs,
        out_specs=out_specs,
        scratch_shapes=[pltpu.VMEM((nseq, seq_len + 2 * HALO, Z_COLS), F32),
                        pltpu.VMEM((nseq, nblk, CHUNK, LANES), F32),
                        pltpu.VMEM((nseq, nblk, 16, CHUNK), F32),
                        pltpu.VMEM((nseq, nblk, CHUNK, LANES), F32),
                        pltpu.VMEM((nseq, 2, nblk, PAIRS, LANES, 2 * LANES), BF16),
                        pltpu.VMEM((nseq, seq_len, D_MODEL), BF16)],
        compiler_params=pltpu.CompilerParams(
            dimension_semantics=("arbitrary",), vmem_limit_bytes=VMEM_LIMIT),
        name="mixer_t%d" % seq_len,
    )(*args)


def _ffn_kernel(mod_ref, ng_ref, cw_ref, x1p_hbm, h2p_hbm, x1s_hbm, h2s_hbm, up_hbm, down_hbm,
                outp_hbm, outs_hbm,
                up_bf, down_bf, act_ref, h2_buf, x1_buf, o_buf, up_stage, down_stage,
                in_sems, out_sems, w_sems, *, layer, tiles_p, tiles_s, seq_p, seq_s):
    i = pl.program_id(0)
    n_tiles = tiles_p + tiles_s
    rows = FFN_ROWS
    half = FFN_SUB // 2
    nsub = D_FF // half
    slot = i % 2
    is_prompt = i < tiles_p

    def tile_rows(tile, first_tile):
        if isinstance(tile, int):
            return pl.ds((tile - first_tile) * rows, rows)
        return pl.ds(pl.multiple_of((tile - first_tile) * rows, rows), rows)

    def h2_copy(h2_hbm, r, buf):
        return pltpu.make_async_copy(h2_hbm.at[r], h2_buf.at[buf], in_sems.at[buf])

    def x1_copy(x1_hbm, r):
        return pltpu.make_async_copy(x1_hbm.at[r], x1_buf, in_sems.at[2])

    def fetch_h2(tile, buf):
        @pl.when(tile < tiles_p)
        def _():
            h2_copy(h2p_hbm, tile_rows(tile, 0), buf).start()

        @pl.when(tile >= tiles_p)
        def _():
            h2_copy(h2s_hbm, tile_rows(tile, tiles_p), buf).start()

    def fetch_x1(tile):
        @pl.when(tile < tiles_p)
        def _():
            x1_copy(x1p_hbm, tile_rows(tile, 0)).start()

        @pl.when(tile >= tiles_p)
        def _():
            x1_copy(x1s_hbm, tile_rows(tile, tiles_p)).start()

    def await_store(buf):
        pltpu.make_async_copy(o_buf.at[buf], outp_hbm.at[pl.ds(0, rows)], out_sems.at[buf]).wait()

    def weight_copies(c, buf):
        aligned = (lambda v: v) if isinstance(c, int) else (lambda v: pl.multiple_of(v, half))
        cv = pl.ds(aligned(c * half), half)
        cg = pl.ds(aligned(D_FF + c * half), half)
        return (pltpu.make_async_copy(up_hbm.at[layer, :, cv], up_stage.at[buf, 0], w_sems.at[buf, 0]),
                pltpu.make_async_copy(up_hbm.at[layer, :, cg], up_stage.at[buf, 1], w_sems.at[buf, 1]),
                pltpu.make_async_copy(down_hbm.at[layer, cv, :], down_stage.at[buf], w_sems.at[buf, 2]))

    @pl.when(i == 0)
    def _():
        h2_copy(h2p_hbm, tile_rows(0, 0), 0).start()
        depth = up_stage.shape[0]
        for c0 in range(depth - 1):
            for cp in weight_copies(c0, c0):
                cp.start()

        def convert(c, carry):
            buf = c % depth
            ahead = c + depth - 1

            @pl.when(ahead < nsub)
            def _():
                for cp in weight_copies(ahead, ahead % depth):
                    cp.start()

            for cp in weight_copies(c, buf):
                cp.wait()
            up_bf[c, :, 0:half] = up_stage[buf, 0].astype(BF16)
            up_bf[c, :, half:FFN_SUB] = up_stage[buf, 1].astype(BF16)
            down_bf[pl.ds(pl.multiple_of(c * half, half), half), :] = down_stage[buf].astype(BF16)
            return carry

        lax.fori_loop(0, nsub, convert, 0)

    h2_copy(h2p_hbm, pl.ds(0, rows), slot).wait()

    @pl.when(i >= 2)
    def _():
        await_store(slot)

    fetch_x1(i)

    @pl.when(i + 1 < n_tiles)
    def _():
        fetch_h2(i + 1, 1 - slot)

    row = lax.broadcasted_iota(jnp.int32, (rows, FFN_SUB), 0)
    first = ((row & (seq_s - 1)) == 0) | (((row & (seq_p - 1)) == 0) & is_prompt)
    last = ((row & (seq_s - 1)) == seq_s - 1) | (((row & (seq_p - 1)) == seq_p - 1) & is_prompt)
    h2 = h2_buf[slot]
    for s in range(nsub):
        a = jnp.dot(h2, up_bf[s], preferred_element_type=F32)
        cw = jnp.concatenate([cw_ref[0, :, s * half:(s + 1) * half],
                              cw_ref[0, :, D_FF + s * half:D_FF + (s + 1) * half]], axis=1)
        a_prev = jnp.where(first, 0.0, pltpu.roll(a, 1, axis=0))
        a_next = jnp.where(last, 0.0, pltpu.roll(a, rows - 1, axis=0))
        ac = cw[0:1] * a_prev + cw[1:2] * a + cw[2:3] * a_next
        hg = 0.5 * ac[:, half:]
        act = (hg * (1.0 + jnp.tanh(hg))) * ac[:, :half]
        act_ref[:, s * half:(s + 1) * half] = act.astype(BF16)
    y = jnp.dot(act_ref[...], down_bf[...], preferred_element_type=F32)
    cond = jnp.where(is_prompt, 0, i - tiles_p + 1)
    gt2 = _mod_rows(mod_ref, cond)[5]

    res = gt2 * _rms(y, ng_ref[0][3:4])
    x1_copy(x1p_hbm, pl.ds(0, rows)).wait()
    o_buf[slot] = x1_buf[...] + res

    @pl.when(is_prompt)
    def _():
        pltpu.make_async_copy(o_buf.at[slot], outp_hbm.at[tile_rows(i, 0)], out_sems.at[slot]).start()

    @pl.when(jnp.logical_not(is_prompt))
    def _():
        pltpu.make_async_copy(o_buf.at[slot], outs_hbm.at[tile_rows(i, tiles_p)], out_sems.at[slot]).start()

    @pl.when(i == n_tiles - 1)
    def _():
        await_store(1 - slot)
        await_store(slot)


def _ffn_call(x1p, h2p, x1s, h2s, layer, mod_all, wts, seq_p, seq_s):
    tiles_p = x1p.shape[0] // FFN_ROWS
    tiles_s = x1s.shape[0] // FFN_ROWS
    assert tiles_p >= 1 and tiles_s >= 1 and tiles_p + tiles_s >= 2
    half = FFN_SUB // 2

    def resident(shape):
        nd = len(shape)
        return pl.BlockSpec((1,) + tuple(shape[1:]), lambda i: (layer,) + (0,) * (nd - 1),
                            pipeline_mode=pl.Buffered(1))

    hbm = pl.BlockSpec(memory_space=pl.ANY)
    return pl.pallas_call(
        functools.partial(_ffn_kernel, layer=layer, tiles_p=tiles_p, tiles_s=tiles_s,
                          seq_p=seq_p, seq_s=seq_s),
        out_shape=[jax.ShapeDtypeStruct(x1p.shape, F32), jax.ShapeDtypeStruct(x1s.shape, F32)],
        grid=(tiles_p + tiles_s,),
        in_specs=[resident(mod_all.shape), resident(wts["norm_g"].shape), resident(wts["ffn_conv"].shape),
                  hbm, hbm, hbm, hbm, hbm, hbm],
        out_specs=[hbm, hbm],
        scratch_shapes=[pltpu.VMEM((D_FF // half, D_MODEL, FFN_SUB), BF16),
                        pltpu.VMEM((D_FF, D_MODEL), BF16),
                        pltpu.VMEM((FFN_ROWS, D_FF), BF16),
                        pltpu.VMEM((2, FFN_ROWS, D_MODEL), BF16),
                        pltpu.VMEM((FFN_ROWS, D_MODEL), F32),
                        pltpu.VMEM((2, FFN_ROWS, D_MODEL), F32),
                        pltpu.VMEM((FFN_WEIGHT_BUFS, 2, D_MODEL, half), F32),
                        pltpu.VMEM((FFN_WEIGHT_BUFS, half, D_MODEL), F32),
                        pltpu.SemaphoreType.DMA((3,)),
                        pltpu.SemaphoreType.DMA((2,)),
                        pltpu.SemaphoreType.DMA((FFN_WEIGHT_BUFS, 3))],
        compiler_params=pltpu.CompilerParams(
            dimension_semantics=("arbitrary",), vmem_limit_bytes=VMEM_LIMIT),
        name="ffn",
    )(mod_all, wts["norm_g"], wts["ffn_conv"], x1p, h2p, x1s, h2s, wts["ffn_up"], wts["ffn_down"])


def _pos_tables(t_len):
    quarter = D_MODEL // 4
    omega = (1.0 / (np.float32(POS_BASE) ** (np.arange(quarter, dtype=np.float32) / np.float32(quarter))))
    omega = omega.astype(np.float32)

    def emb(p):
        a = p.astype(np.float32)[:, None] * omega[None, :]
        return np.concatenate([np.sin(a), np.cos(a)], axis=-1).astype(np.float32)

    return jnp.asarray(emb(np.arange(t_len // GRID_W))), jnp.asarray(emb(np.arange(GRID_W)))


def _stabiliser_rows(state_m):
    bs = state_m.shape[0]
    m = jnp.pad(state_m.astype(F32), ((0, 0), (0, 0), (0, 0), (0, 16 - HEADS)))
    return jnp.pad(m.reshape(bs, DEPTH, 1, 32), ((0, 0), (0, 0), (0, 0), (L_MF, LANES - 32 - L_MF)))


def kernel(x_prompt, x_sample, state_C, state_n, state_m, c, c_ctx, norm_g, ada_w, ada_b, w_in,
           gate_b, gmlp_ws, gmlp_b, pool_w, pool_scale, conv_w, w_out, ffn_up, ffn_conv, ffn_down):
    bp, tp, _ = x_prompt.shape
    bs, ts, _ = x_sample.shape
    assert tp == CHUNK and ts % CHUNK == 0 and FFN_ROWS % tp == 0 and ts == FFN_ROWS

    conds = jnp.concatenate([c_ctx[None, :], c, jnp.zeros((7 - bs, D_MODEL), F32)], axis=0)
    mod_all = _modulation(conds, ada_w, ada_b)
    win_p, wout_p = _prep_weights(w_in, w_out)

    gb = jnp.pad(gate_b, ((0, 0), (0, 0), (0, 8 - HEADS))).reshape(DEPTH, 1, 32)
    dg = D_GROUP // 4
    wts = {
        "norm_g": norm_g,
        "w_in": win_p,
        "gate_b": jnp.pad(gb, ((0, 0), (0, 0), (0, LANES - 32))),
        "gmlp_ws": gmlp_ws,
        "gmlp_bias": jnp.repeat(jnp.swapaxes(gmlp_b, 1, 2), dg, axis=2),
        "pool_bd": jnp.einsum("lgcd,gh->lgchd", pool_w, jnp.eye(4, dtype=F32)).reshape(
            DEPTH, D_GROUP, D_GROUP).astype(BF16),
        "pool_scale": pool_scale,
        "conv_w": conv_w,
        "w_out": wout_p,
        "ffn_up": ffn_up,
        "ffn_conv": ffn_conv,
        "ffn_down": ffn_down,
    }
    pos = _pos_tables(ts)
    state = (state_C.astype(F32), state_n.astype(F32), _stabiliser_rows(state_m))

    xp, xs = x_prompt, x_sample
    prev, ms = None, []
    for l in range(DEPTH):
        x1p, h2p, c_all, n_all, m_fin = _mixer_call(xp, l, mod_all, wts, 0, emit_state=True,
                                                    prev_states=prev, nseq=MIXER_ROWS // tp)
        prev = (c_all, n_all)
        ms.append(jnp.stack([m_fin[:, 0, L_MF:L_MF + HEADS], m_fin[:, 0, L_MB:L_MB + HEADS]], axis=1))

        x1s, h2s = _mixer_call(xs, l, mod_all, wts, 1, pos=pos if l == 0 else None, state=state)

        xp, xs = _ffn_call(x1p.reshape(bp * tp, D_MODEL), h2p.reshape(bp * tp, D_MODEL),
                           x1s.reshape(bs * ts, D_MODEL), h2s.reshape(bs * ts, D_MODEL),
                           l, mod_all, wts, tp, ts)
        xp = xp.reshape(bp, tp, D_MODEL)
        xs = xs.reshape(bs, ts, D_MODEL)

    return (xp, xs, c_all, n_all, jnp.stack(ms, axis=1))
```

```python
import functools

import numpy as np
import jax
import jax.numpy as jnp
from jax import lax
from jax.experimental import pallas as pl
from jax.experimental.pallas import tpu as pltpu

F32 = jnp.float32
BF16 = jnp.bfloat16

D_MODEL = 1024
DEPTH = 2
HEADS = 4
PAIRS = HEADS // 2
DH = 64
D_GROUP = 256
D_FF = 2816
N_MOD = 6
EPS = 1e-6
GRID_W = 64
POS_BASE = 10000.0
D_IN = 2576
N_GATE = 4 * HEADS

CHUNK = 256
GMLP_CHUNK = 128
HALO = 8
LANES = 128

C_Q, C_K, C_V, C_O = 0, 256, 512, 768
C_GU, C_GV, C_PZ, C_CB, C_CC, C_CX = 1024, 1280, 1536, 1792, 2048, 2304
Z_COLS = 2560
C_GATE = Z_COLS
W_COLS = Z_COLS + LANES
Z_GROUP = 512
L_BF, L_MF, L_BB, L_MB = 0, 8, 16, 24

MIXER_ROWS = 1024
FFN_ROWS = 1024
FFN_SUB = 256
FFN_WEIGHT_BUFS = 8
VMEM_LIMIT = 60 * 1024 * 1024


def _rms(x, g):
    ms = jnp.mean(x * x, axis=-1, keepdims=True)
    return x * lax.rsqrt(ms + EPS) * g


def _sigmoid(x):
    return 1.0 / (1.0 + jnp.exp(-x))


def _log_sigmoid(x):
    return jnp.minimum(x, 0.0) - jnp.log1p(jnp.exp(-jnp.abs(x)))


def _scan_lanes(x, op, reverse, fill):
    n = x.shape[1]
    lane = lax.broadcasted_iota(jnp.int32, x.shape, 1)
    k = 1
    while k < n:
        if reverse:
            sh = jnp.where(lane < n - k, pltpu.roll(x, n - k, axis=1), fill)
        else:
            sh = jnp.where(lane >= k, pltpu.roll(x, k, axis=1), fill)
        x = op(x, sh)
        k *= 2
    return x


def _shift_down(x, k):
    return pltpu.roll(x, k, axis=0)


def _shift_up(x, k):
    return pltpu.roll(x, x.shape[0] - k, axis=0)


def _prep_win_kernel(wt_ref, sel_ref, o_ref):
    for n in range(Z_COLS // LANES):
        r0 = LANES * n if n < 4 * D_GROUP // LANES else LANES * n + N_GATE
        o_ref[0, :, LANES * n:LANES * (n + 1)] = wt_ref[0, r0:r0 + LANES, :].T.astype(BF16)
    g = wt_ref[0, 4 * D_GROUP:4 * D_GROUP + LANES, :].T.astype(BF16)
    o_ref[0, :, C_GATE:W_COLS] = jnp.dot(g, sel_ref[...], preferred_element_type=F32).astype(BF16)


def _prep_w_in(w_in):
    sel = np.zeros((LANES, LANES), np.float32)
    for j in range(N_GATE):
        sel[j, 8 * (j // HEADS) + j % HEADS] = 1.0
    params = pltpu.CompilerParams(dimension_semantics=("arbitrary", "arbitrary"),
                                  vmem_limit_bytes=VMEM_LIMIT)
    return pl.pallas_call(
        _prep_win_kernel, out_shape=jax.ShapeDtypeStruct((DEPTH, D_MODEL, W_COLS), BF16),
        grid=(DEPTH, 1),
        in_specs=[pl.BlockSpec((1, D_IN, D_MODEL), lambda l, i: (l, 0, 0)),
                  pl.BlockSpec((LANES, LANES), lambda l, i: (0, 0))],
        out_specs=pl.BlockSpec((1, D_MODEL, W_COLS), lambda l, i: (l, 0, 0)),
        compiler_params=params, name="prep_w_in")(jnp.swapaxes(w_in, 1, 2), jnp.asarray(sel, BF16))


def _mod_kernel(c_ref, w_ref, b_ref, o_ref):
    c = c_ref[...]
    s = (c * _sigmoid(c)).astype(BF16)
    bias = b_ref[pl.ds(pl.program_id(0), 1), :]
    o_ref[0] = jnp.dot(s, w_ref[0].astype(BF16), preferred_element_type=F32) + bias


def _modulation(conds, ada_w, ada_b):
    n_out = N_MOD * D_MODEL
    bn = 3072
    return pl.pallas_call(
        _mod_kernel,
        out_shape=jax.ShapeDtypeStruct((DEPTH, 8, n_out), F32),
        grid=(DEPTH, n_out // bn),
        in_specs=[
            pl.BlockSpec((8, D_MODEL), lambda l, j: (0, 0)),
            pl.BlockSpec((1, D_MODEL, bn), lambda l, j: (l, 0, j)),
            pl.BlockSpec((DEPTH, bn), lambda l, j: (0, j)),
        ],
        out_specs=pl.BlockSpec((1, 8, bn), lambda l, j: (l, 0, j)),
        compiler_params=pltpu.CompilerParams(
            dimension_semantics=("arbitrary", "arbitrary"), vmem_limit_bytes=VMEM_LIMIT),
        name="adaln_mod",
    )(conds, ada_w, ada_b)


def _mod_rows(mod_ref, cond):
    row = mod_ref[0, pl.ds(cond, 1), :]
    return [row[:, i * D_MODEL:(i + 1) * D_MODEL] for i in range(N_MOD)]


def _gate_stats(gates):
    gt = gates.T
    b_f = _scan_lanes(_log_sigmoid(gt[8:16]), jnp.add, False, 0.0)
    g_f = gt[0:8] - b_f
    cm_f = _scan_lanes(g_f, jnp.maximum, False, -jnp.inf)
    b_b = _scan_lanes(_log_sigmoid(gt[24:32]), jnp.add, True, 0.0)
    g_b = gt[16:24] - b_b
    cm_b = _scan_lanes(g_b, jnp.maximum, True, -jnp.inf)
    stack = jnp.concatenate([b_f, cm_f, b_b, cm_b, jnp.zeros((LANES - 32, CHUNK), F32)], axis=0)
    return jnp.concatenate([g_f, g_b], axis=0), stack.T


def _pair_cols(mat, lane0, lane_lt_dh):
    return jnp.where(lane_lt_dh, mat[:, lane0:lane0 + 1], mat[:, lane0 + 1:lane0 + 2])


def _mlstm(z_ref, gcol_ref, grow_ref, cols_ref, bd_ref, ycat_ref, nc, state0, m0, need_final):
    lane_row = lax.broadcasted_iota(jnp.int32, (1, LANES), 1)
    lane = lax.broadcasted_iota(jnp.int32, (CHUNK, LANES), 1)
    lt_dh = lane < DH
    row_ll = lax.broadcasted_iota(jnp.int32, (CHUNK, CHUNK), 0)
    col_ll = lax.broadcasted_iota(jnp.int32, (CHUNK, CHUNK), 1)
    tri = (col_ll <= row_ll, col_ll >= row_ll)
    bd_row = lax.broadcasted_iota(jnp.int32, (LANES, 2 * LANES), 0)
    bd_col = lax.broadcasted_iota(jnp.int32, (LANES, 2 * LANES), 1)
    bd_mask = (bd_row < DH) == ((bd_col & (LANES - 1)) < DH)
    dirs = ((0, L_BF, L_MF, CHUNK - 1), (1, L_BB, L_MB, 0))

    m_start = [[None] * nc, [None] * nc]
    m_fin = [None, None]
    for d, _, _, last in dirs:
        m = m0
        order = range(nc) if d == 0 else range(nc - 1, -1, -1)
        for step, c in enumerate(order):
            m_start[d][c] = m
            if need_final or step < nc - 1:
                cl = cols_ref[c, last:last + 1, :]
                m = pltpu.roll(cl, 8, axis=1) + jnp.maximum(m, cl)
        m_fin[d] = m

    have_state = [[False] * nc, [False] * nc]
    fin = [None, None]
    for d, l_b, l_m, last in dirs:
        bd = list(state0[d])
        order = range(nc) if d == 0 else range(nc - 1, -1, -1)
        for step, c in enumerate(order):
            if bd[0] is not None:
                have_state[d][c] = True
                for p in range(PAIRS):
                    bd_ref[d, c, p] = bd[p].astype(BF16)
            if not (need_final or step < nc - 1):
                continue
            rows = slice(HALO + c * CHUNK, HALO + (c + 1) * CHUNK)
            cols = cols_ref[c]
            m_last = jnp.maximum(cols[last:last + 1, :], m_start[d][c])
            a_col = jnp.exp(gcol_ref[c] - cols - pltpu.roll(m_last, LANES - 8, axis=1))
            a_prev = jnp.exp(m_start[d][c] - m_last)
            for p in range(PAIRS):
                a_pair = _pair_cols(a_col, l_b + 2 * p, lt_dh)
                v_pair = z_ref[rows, C_V + LANES * p:C_V + LANES * (p + 1)]
                rhs = jnp.concatenate([v_pair * a_pair, a_pair], axis=1).astype(BF16)
                k_pair = (z_ref[rows, C_K + LANES * p:C_K + LANES * (p + 1)] * (DH ** -0.5)).astype(BF16)
                upd = lax.dot_general(k_pair, rhs, (((0,), (0,)), ((), ())),
                                      preferred_element_type=F32)
                upd = jnp.where(bd_mask, upd, 0.0)
                if bd[p] is None:
                    bd[p] = upd
                else:
                    scale = jnp.where(bd_row < DH, a_prev[:, l_m + 2 * p:l_m + 2 * p + 1],
                                      a_prev[:, l_m + 2 * p + 1:l_m + 2 * p + 2])
                    bd[p] = scale * bd[p] + upd
        fin[d] = bd

    for c in range(nc):
        rows = slice(HALO + c * CHUNK, HALO + (c + 1) * CHUNK)
        cols = cols_ref[c]
        grow = grow_ref[c]
        m_prev = jnp.where(lane_row < L_BB, m_start[0][c], m_start[1][c])
        m_all = jnp.maximum(cols, m_prev)
        w_inter = jnp.exp(m_prev - m_all)
        e_neg = jnp.exp(-(pltpu.roll(cols, 8, axis=1) + m_all))
        for p in range(PAIRS):
            q_pair = z_ref[rows, C_Q + LANES * p:C_Q + LANES * (p + 1)].astype(BF16)
            k_f32 = z_ref[rows, C_K + LANES * p:C_K + LANES * (p + 1)] * (DH ** -0.5)
            v_pair = z_ref[rows, C_V + LANES * p:C_V + LANES * (p + 1)]
            p_cat = ([], [])
            va = []
            for j in range(2):
                h = 2 * p + j
                own = lt_dh if j == 0 else jnp.logical_not(lt_dh)
                k_own = jnp.where(own, k_f32, 0.0).astype(BF16)
                s = lax.dot_general(q_pair, k_own, (((1,), (1,)), ((), ())),
                                    preferred_element_type=F32)
                va.append(jnp.concatenate([jnp.where(own, v_pair, 0.0), jnp.where(own, 1.0, 0.0)],
                                          axis=1).astype(BF16))
                for d, _, l_m, _ in dirs:
                    w = jnp.exp(jnp.where(tri[d], grow[8 * d + h:8 * d + h + 1] - m_all[:, l_m + h:l_m + h + 1],
                                          -jnp.inf))
                    p_cat[d].append((s * w).astype(BF16))
            va = jnp.concatenate(va, axis=0)
            h_sum = None
            for d, _, l_m, _ in dirs:
                tot = jnp.dot(jnp.concatenate(p_cat[d], axis=1), va, preferred_element_type=F32)
                if have_state[d][c]:
                    e_pair = _pair_cols(w_inter, l_m + 2 * p, lt_dh)
                    qc = jnp.dot(q_pair, bd_ref[d, c, p], preferred_element_type=F32)
                    tot = tot + jnp.concatenate([e_pair, e_pair], axis=1) * qc
                floor = _pair_cols(e_neg, l_m + 2 * p, lt_dh)
                h_dir = tot[:, :LANES] / jnp.maximum(jnp.abs(tot[:, LANES:]), floor)
                h_sum = h_dir if h_sum is None else h_sum + h_dir
            o_pair = z_ref[rows, C_O + LANES * p:C_O + LANES * (p + 1)]
            ycat_ref[c * CHUNK:(c + 1) * CHUNK, LANES * p:LANES * (p + 1)] = (
                _sigmoid(o_pair) * h_sum).astype(BF16)

    m_row = jnp.where(lane_row < L_BB, m_fin[0], m_fin[1])
    return fin, m_row


def _pair_state(c_ref, n_ref, q, d, p):
    eye = (lax.broadcasted_iota(jnp.int32, (DH, DH), 0) == lax.broadcasted_iota(jnp.int32, (DH, DH), 1))
    zero = jnp.zeros((DH, DH), F32)
    blocks = []
    for j in range(2):
        h = 2 * p + j
        n_row = n_ref[q, 0, d, h:h + 1, :]
        n_col = jnp.sum(jnp.where(eye, n_row, 0.0), axis=1, keepdims=True)
        row = [zero] * 4
        row[j] = c_ref[q, 0, d, h]
        row[2 + j] = jnp.broadcast_to(n_col, (DH, DH))
        blocks.append(jnp.concatenate(row, axis=1))
    return jnp.concatenate(blocks, axis=0)


def _mixer_kernel(*refs, layer, seq_len, nseq, add_pos, has_state, emit_state, cond_base, n_prev):
    it = iter(refs)
    x_ref = next(it)
    if add_pos:
        er_ref = next(it)
        ec_ref = next(it)
    mod_ref = next(it)
    ng_ref = next(it)
    win_ref = next(it)
    gb_ref = next(it)
    ws_ref = next(it)
    gbias_ref = next(it)
    pbd_ref = next(it)
    psc_ref = next(it)
    cw_ref = next(it)
    wout_ref = next(it)
    if has_state:
        c0_ref = next(it)
        n0_ref = next(it)
        m0_ref = next(it)
    if n_prev:
        cprev_ref = next(it)
        nprev_ref = next(it)
    x1_ref = next(it)
    h2_ref = next(it)
    if emit_state:
        cout_ref = next(it)
        nout_ref = next(it)
        mout_ref = next(it)
    z_all = next(it)
    gcol_all = next(it)
    grow_all = next(it)
    cols_all = next(it)
    bd_all = next(it)
    ycat_all = next(it)
    wout_stage = next(it)
    wout_bf = next(it)
    wout_sem = next(it)

    first_step = pl.program_id(0) == 0
    wout_copy = pltpu.make_async_copy(wout_ref.at[layer], wout_stage, wout_sem.at[0])

    @pl.when(first_step)
    def _():
        wout_copy.start()

    nblk = seq_len // CHUNK
    cond = cond_base if cond_base == 0 else cond_base + pl.program_id(0)
    sh1, sc1, gt1, sh2, sc2, _ = _mod_rows(mod_ref, cond)
    ng = ng_ref[0]
    seqs = range(nseq)
    units = [(q, r) for q in seqs for r in range(nblk)]

    zeros_halo = jnp.zeros((HALO, Z_COLS - C_PZ), F32)
    for q in seqs:
        z_all[q, 0:HALO, C_PZ:Z_COLS] = zeros_halo
        z_all[q, HALO + seq_len:2 * HALO + seq_len, C_PZ:Z_COLS] = zeros_halo

    g_in = ng[0:1] * (1.0 + sc1)
    hbs = []
    for q, r in units:
        rows = slice(r * CHUNK, (r + 1) * CHUNK)
        xb = x_ref[q, rows, :]
        if add_pos:
            pieces = []
            for g in range(CHUNK // GRID_W):
                gi = r * (CHUNK // GRID_W) + g
                er = jnp.broadcast_to(er_ref[gi:gi + 1, :], (GRID_W, D_MODEL // 2))
                pieces.append(jnp.concatenate([er, ec_ref[...]], axis=1))
            xb = xb + jnp.concatenate(pieces, axis=0)
            x1_ref[q, rows, :] = xb
        hbs.append((_rms(xb, g_in) + sh1).astype(BF16))
    hb = jnp.concatenate(hbs, axis=0)
    gates_all = jnp.dot(hb, win_ref[0, :, C_GATE:W_COLS], preferred_element_type=F32) + gb_ref[0]
    for cg in range(Z_COLS // Z_GROUP):
        zc = jnp.dot(hb, win_ref[0, :, cg * Z_GROUP:(cg + 1) * Z_GROUP], preferred_element_type=F32)
        for u, (q, r) in enumerate(units):
            z_all[q, HALO + r * CHUNK:HALO + (r + 1) * CHUNK, cg * Z_GROUP:(cg + 1) * Z_GROUP] = (
                zc[u * CHUNK:(u + 1) * CHUNK])
    for u, (q, r) in enumerate(units):
        gates = gates_all[u * CHUNK:(u + 1) * CHUNK]
        gcol_all[q, r] = gates
        grow, cols = _gate_stats(gates)
        grow_all[q, r] = grow
        cols_all[q, r] = cols

    for q in seqs:
        if has_state:
            state0 = [[_pair_state(c0_ref, n0_ref, q, d, p) for p in range(PAIRS)] for d in range(2)]
            m0 = m0_ref[q, 0]
        else:
            state0 = [[None] * PAIRS, [None] * PAIRS]
            m0 = jnp.zeros((1, LANES), F32)
        fin, m_row = _mlstm(z_all.at[q], gcol_all.at[q], grow_all.at[q], cols_all.at[q], bd_all.at[q],
                            ycat_all.at[q], nblk, state0, m0, need_final=emit_state)
        if emit_state:
            for d in range(2):
                for p in range(PAIRS):
                    for j in range(2):
                        cout_ref[q, n_prev, d, 2 * p + j] = fin[d][p][DH * j:DH * (j + 1), DH * j:DH * (j + 1)]
                    n_t = fin[d][p][:, LANES:].T
                    nout_ref[q, n_prev, d, 2 * p:2 * p + 1, :] = n_t[0:1, 0:DH]
                    nout_ref[q, n_prev, d, 2 * p + 1:2 * p + 2, :] = n_t[DH:DH + 1, DH:LANES]
            mout_ref[q] = m_row
    if emit_state and n_prev:
        cout_ref[:, 0:n_prev] = cprev_ref[...]
        nout_ref[:, 0:n_prev] = nprev_ref[...]

    lane_g = lax.broadcasted_iota(jnp.int32, (CHUNK, D_GROUP), 1) // (D_GROUP // 4)
    lg = lax.broadcasted_iota(jnp.int32, (GMLP_CHUNK, D_GROUP), 1) // (D_GROUP // 4)
    row_c = lax.broadcasted_iota(jnp.int32, (CHUNK, D_GROUP), 0)
    half = jnp.where(lane_g == 0, 1, jnp.where(lane_g == 1, 2, jnp.where(lane_g == 2, 4, 8)))
    cw = cw_ref[0]
    ws_cat = jnp.concatenate([ws_ref[0, g] for g in range(4)], axis=1).astype(BF16)
    for q, r in units:
        z_ref = z_all.at[q]
        ycat_ref = ycat_all.at[q]
        rows = slice(r * CHUNK, (r + 1) * CHUNK)
        zrows = slice(HALO + r * CHUNK, HALO + (r + 1) * CHUNK)
        hrows = slice(r * CHUNK, (r + 1) * CHUNK + 2 * HALO)
        inner = slice(HALO, HALO + CHUNK)

        for s in range(CHUNK // GMLP_CHUNK):
            zs = slice(HALO + r * CHUNK + s * GMLP_CHUNK, HALO + r * CHUNK + (s + 1) * GMLP_CHUNK)
            vch = z_ref[zs, C_GV:C_GV + D_GROUP]
            v_stack = jnp.concatenate([jnp.where(lg == g, vch, 0.0) for g in range(4)], axis=0)
            mixed = gbias_ref[0] + jnp.dot(ws_cat, v_stack.astype(BF16), preferred_element_type=F32)
            yb = z_ref[zs, C_GU:C_GU + D_GROUP] * mixed
            ys = slice(r * CHUNK + s * GMLP_CHUNK, r * CHUNK + (s + 1) * GMLP_CHUNK)
            ycat_ref[ys, D_GROUP:2 * D_GROUP] = yb.astype(BF16)

        pz = z_ref[hrows, C_PZ:C_PZ + D_GROUP]
        a2 = pz + _shift_down(pz, 1)
        a4 = a2 + _shift_down(a2, 2)
        a8 = a4 + _shift_down(a4, 4)
        a16 = a8 + _shift_down(a8, 8)
        win = [a2[inner], _shift_up(a4, 1)[inner], _shift_up(a8, 3)[inner], _shift_up(a16, 7)[inner]]
        wsum = jnp.where(lane_g == 0, win[0], jnp.where(lane_g == 1, win[1],
                         jnp.where(lane_g == 2, win[2], win[3])))
        t_abs = row_c + r * CHUNK
        cnt = (jnp.minimum(t_abs + half, seq_len) - jnp.maximum(t_abs - half, 0)).astype(F32)
        pooled = wsum / cnt - pz[inner]
        yc = (jnp.dot(pooled.astype(BF16), pbd_ref[0], preferred_element_type=F32)
              * psc_ref[layer:layer + 1, :])
        ycat_ref[rows, 2 * D_GROUP:3 * D_GROUP] = yc.astype(BF16)

        u = z_ref[hrows, C_CC:C_CC + D_GROUP] * z_ref[hrows, C_CX:C_CX + D_GROUP]
        conv = (cw[0:1] * _shift_down(u, 1)[inner] + cw[1:2] * u[inner]
                + cw[2:3] * _shift_up(u, 1)[inner])
        yd = z_ref[zrows, C_CB:C_CB + D_GROUP] * conv
        ycat_ref[rows, 3 * D_GROUP:4 * D_GROUP] = yd.astype(BF16)

    @pl.when(first_step)
    def _():
        wout_copy.wait()
        wout_bf[...] = wout_stage[...].astype(BF16)

    y_all = jnp.dot(jnp.concatenate([ycat_all[q, r * CHUNK:(r + 1) * CHUNK, :] for q, r in units], axis=0),
                    wout_bf[...], preferred_element_type=F32)
    g_y = gt1 * ng[1:2]
    g_h2 = ng[2:3] * (1.0 + sc2)
    for u, (q, r) in enumerate(units):
        rows = slice(r * CHUNK, (r + 1) * CHUNK)
        xb = x1_ref[q, rows, :] if add_pos else x_ref[q, rows, :]
        x1 = xb + _rms(y_all[u * CHUNK:(u + 1) * CHUNK], g_y)
        x1_ref[q, rows, :] = x1
        h2_ref[q, rows, :] = (_rms(x1, g_h2) + sh2).astype(BF16)


def _layer_spec(shape, layer):
    nd = len(shape)
    return pl.BlockSpec((1,) + tuple(shape[1:]), lambda b: (layer,) + (0,) * (nd - 1),
                        pipeline_mode=pl.Buffered(1))


def _mixer_call(x, layer, mod_all, wts, cond_base, pos=None, state=None, emit_state=False,
                prev_states=None, nseq=1):
    bsz, seq_len, _ = x.shape
    nblk = seq_len // CHUNK
    add_pos = pos is not None
    has_state = state is not None
    n_prev = 0 if prev_states is None else prev_states[0].shape[1]
    args = [x]
    in_specs = [pl.BlockSpec((nseq, seq_len, D_MODEL), lambda b: (b, 0, 0))]
    if add_pos:
        args += list(pos)
        in_specs += [pl.BlockSpec(p.shape, lambda b: (0, 0), pipeline_mode=pl.Buffered(1)) for p in pos]
    args.append(mod_all)
    in_specs.append(_layer_spec(mod_all.shape, layer))
    for name in ("norm_g", "w_in", "gate_b", "gmlp_ws", "gmlp_bias", "pool_bd", "pool_scale",
                 "conv_w", "w_out"):
        args.append(wts[name])
        if name == "w_out":
            in_specs.append(pl.BlockSpec(memory_space=pl.ANY))
        elif wts[name].ndim == 2:
            in_specs.append(pl.BlockSpec(wts[name].shape, lambda b: (0, 0), pipeline_mode=pl.Buffered(1)))
        else:
            in_specs.append(_layer_spec(wts[name].shape, layer))
    if has_state:
        args += list(state)
        in_specs += [pl.BlockSpec((nseq, 1, 2, HEADS, DH, DH), lambda b: (b, layer, 0, 0, 0, 0)),
                     pl.BlockSpec((nseq, 1, 2, HEADS, DH), lambda b: (b, layer, 0, 0, 0)),
                     pl.BlockSpec((nseq, 1, 1, LANES), lambda b: (b, layer, 0, 0))]
    out_shape = [jax.ShapeDtypeStruct((bsz, seq_len, D_MODEL), F32),
                 jax.ShapeDtypeStruct((bsz, seq_len, D_MODEL), BF16)]
    out_specs = [pl.BlockSpec((nseq, seq_len, D_MODEL), lambda b: (b, 0, 0)),
                 pl.BlockSpec((nseq, seq_len, D_MODEL), lambda b: (b, 0, 0))]
    if emit_state:
        if n_prev:
            args += list(prev_states)
            in_specs += [pl.BlockSpec((nseq, n_prev, 2, HEADS, DH, DH), lambda b: (b, 0, 0, 0, 0, 0)),
                         pl.BlockSpec((nseq, n_prev, 2, HEADS, DH), lambda b: (b, 0, 0, 0, 0))]
        out_shape += [jax.ShapeDtypeStruct((bsz, n_prev + 1, 2, HEADS, DH, DH), F32),
                      jax.ShapeDtypeStruct((bsz, n_prev + 1, 2, HEADS, DH), F32),
                      jax.ShapeDtypeStruct((bsz, 1, LANES), F32)]
        out_specs += [pl.BlockSpec((nseq, n_prev + 1, 2, HEADS, DH, DH), lambda b: (b, 0, 0, 0, 0, 0)),
                      pl.BlockSpec((nseq, n_prev + 1, 2, HEADS, DH), lambda b: (b, 0, 0, 0, 0)),
                      pl.BlockSpec((nseq, 1, LANES), lambda b: (b, 0, 0))]
    kern = functools.partial(_mixer_kernel, layer=layer, seq_len=seq_len, nseq=nseq, add_pos=add_pos,
                             has_state=has_state, emit_state=emit_state, cond_base=cond_base,
                             n_prev=n_prev)
    return pl.pallas_call(
        kern,
        out_shape=out_shape,
        grid=(bsz // nseq,),
        in_specs=in_specs,
        out_specs=out_specs,
        scratch_shapes=[pltpu.VMEM((nseq, seq_len + 2 * HALO, Z_COLS), F32),
                        pltpu.VMEM((nseq, nblk, CHUNK, LANES), F32),
                        pltpu.VMEM((nseq, nblk, 16, CHUNK), F32),
                        pltpu.VMEM((nseq, nblk, CHUNK, LANES), F32),
                        pltpu.VMEM((nseq, 2, nblk, PAIRS, LANES, 2 * LANES), BF16),
                        pltpu.VMEM((nseq, seq_len, D_MODEL), BF16),
                        pltpu.VMEM((D_MODEL, D_MODEL), F32),
                        pltpu.VMEM((D_MODEL, D_MODEL), BF16),
                        pltpu.SemaphoreType.DMA((1,))],
        compiler_params=pltpu.CompilerParams(
            dimension_semantics=("arbitrary",), vmem_limit_bytes=VMEM_LIMIT),
        name="mixer_t%d" % seq_len,
    )(*args)


def _ffn_kernel(mod_ref, ng_ref, cw_ref, x1p_hbm, h2p_hbm, x1s_hbm, h2s_hbm, up_hbm, down_hbm,
                outp_hbm, outs_hbm,
                up_bf, down_bf, act_ref, h2_buf, x1_buf, o_buf, up_stage, down_stage,
                in_sems, out_sems, w_sems, *, layer, tiles_p, tiles_s, seq_p, seq_s):
    i = pl.program_id(0)
    n_tiles = tiles_p + tiles_s
    rows = FFN_ROWS
    half = FFN_SUB // 2
    nsub = D_FF // half
    slot = i % 2
    is_prompt = i < tiles_p

    def tile_rows(tile, first_tile):
        if isinstance(tile, int):
            return pl.ds((tile - first_tile) * rows, rows)
        return pl.ds(pl.multiple_of((tile - first_tile) * rows, rows), rows)

    def h2_copy(h2_hbm, r, buf):
        return pltpu.make_async_copy(h2_hbm.at[r], h2_buf.at[buf], in_sems.at[buf])

    def x1_copy(x1_hbm, r):
        return pltpu.make_async_copy(x1_hbm.at[r], x1_buf, in_sems.at[2])

    def fetch_h2(tile, buf):
        @pl.when(tile < tiles_p)
        def _():
            h2_copy(h2p_hbm, tile_rows(tile, 0), buf).start()

        @pl.when(tile >= tiles_p)
        def _():
            h2_copy(h2s_hbm, tile_rows(tile, tiles_p), buf).start()

    def fetch_x1(tile):
        @pl.when(tile < tiles_p)
        def _():
            x1_copy(x1p_hbm, tile_rows(tile, 0)).start()

        @pl.when(tile >= tiles_p)
        def _():
            x1_copy(x1s_hbm, tile_rows(tile, tiles_p)).start()

    def await_store(buf):
        pltpu.make_async_copy(o_buf.at[buf], outp_hbm.at[pl.ds(0, rows)], out_sems.at[buf]).wait()

    def weight_copies(c, buf):
        aligned = (lambda v: v) if isinstance(c, int) else (lambda v: pl.multiple_of(v, half))
        cv = pl.ds(aligned(c * half), half)
        cg = pl.ds(aligned(D_FF + c * half), half)
        return (pltpu.make_async_copy(up_hbm.at[layer, :, cv], up_stage.at[buf, 0], w_sems.at[buf, 0]),
                pltpu.make_async_copy(up_hbm.at[layer, :, cg], up_stage.at[buf, 1], w_sems.at[buf, 1]),
                pltpu.make_async_copy(down_hbm.at[layer, cv, :], down_stage.at[buf], w_sems.at[buf, 2]))

    @pl.when(i == 0)
    def _():
        h2_copy(h2p_hbm, tile_rows(0, 0), 0).start()
        depth = up_stage.shape[0]
        for c0 in range(depth - 1):
            for cp in weight_copies(c0, c0):
                cp.start()

        def convert(c, carry):
            buf = c % depth
            ahead = c + depth - 1

            @pl.when(ahead < nsub)
            def _():
                for cp in weight_copies(ahead, ahead % depth):
                    cp.start()

            for cp in weight_copies(c, buf):
                cp.wait()
            up_bf[c, :, 0:half] = up_stage[buf, 0].astype(BF16)
            up_bf[c, :, half:FFN_SUB] = up_stage[buf, 1].astype(BF16)
            down_bf[pl.ds(pl.multiple_of(c * half, half), half), :] = down_stage[buf].astype(BF16)
            return carry

        lax.fori_loop(0, nsub, convert, 0)

    h2_copy(h2p_hbm, pl.ds(0, rows), slot).wait()

    @pl.when(i >= 2)
    def _():
        await_store(slot)

    fetch_x1(i)

    @pl.when(i + 1 < n_tiles)
    def _():
        fetch_h2(i + 1, 1 - slot)

    row = lax.broadcasted_iota(jnp.int32, (rows, FFN_SUB), 0)
    first = ((row & (seq_s - 1)) == 0) | (((row & (seq_p - 1)) == 0) & is_prompt)
    last = ((row & (seq_s - 1)) == seq_s - 1) | (((row & (seq_p - 1)) == seq_p - 1) & is_prompt)
    h2 = h2_buf[slot]
    for s in range(nsub):
        a = jnp.dot(h2, up_bf[s], preferred_element_type=F32)
        cw = jnp.concatenate([cw_ref[0, :, s * half:(s + 1) * half],
                              cw_ref[0, :, D_FF + s * half:D_FF + (s + 1) * half]], axis=1)
        a_prev = jnp.where(first, 0.0, pltpu.roll(a, 1, axis=0))
        a_next = jnp.where(last, 0.0, pltpu.roll(a, rows - 1, axis=0))
        ac = cw[0:1] * a_prev + cw[1:2] * a + cw[2:3] * a_next
        hg = 0.5 * ac[:, half:]
        act = (hg * (1.0 + jnp.tanh(hg))) * ac[:, :half]
        act_ref[:, s * half:(s + 1) * half] = act.astype(BF16)
    y = jnp.dot(act_ref[...], down_bf[...], preferred_element_type=F32)
    cond = jnp.where(is_prompt, 0, i - tiles_p + 1)
    gt2 = _mod_rows(mod_ref, cond)[5]

    res = gt2 * _rms(y, ng_ref[0][3:4])
    x1_copy(x1p_hbm, pl.ds(0, rows)).wait()
    o_buf[slot] = x1_buf[...] + res

    @pl.when(is_prompt)
    def _():
        pltpu.make_async_copy(o_buf.at[slot], outp_hbm.at[tile_rows(i, 0)], out_sems.at[slot]).start()

    @pl.when(jnp.logical_not(is_prompt))
    def _():
        pltpu.make_async_copy(o_buf.at[slot], outs_hbm.at[tile_rows(i, tiles_p)], out_sems.at[slot]).start()

    @pl.when(i == n_tiles - 1)
    def _():
        await_store(1 - slot)
        await_store(slot)


def _ffn_call(x1p, h2p, x1s, h2s, layer, mod_all, wts, seq_p, seq_s):
    tiles_p = x1p.shape[0] // FFN_ROWS
    tiles_s = x1s.shape[0] // FFN_ROWS
    assert tiles_p >= 1 and tiles_s >= 1 and tiles_p + tiles_s >= 2
    half = FFN_SUB // 2

    def resident(shape):
        nd = len(shape)
        return pl.BlockSpec((1,) + tuple(shape[1:]), lambda i: (layer,) + (0,) * (nd - 1),
                            pipeline_mode=pl.Buffered(1))

    hbm = pl.BlockSpec(memory_space=pl.ANY)
    return pl.pallas_call(
        functools.partial(_ffn_kernel, layer=layer, tiles_p=tiles_p, tiles_s=tiles_s,
                          seq_p=seq_p, seq_s=seq_s),
        out_shape=[jax.ShapeDtypeStruct(x1p.shape, F32), jax.ShapeDtypeStruct(x1s.shape, F32)],
        grid=(tiles_p + tiles_s,),
        in_specs=[resident(mod_all.shape), resident(wts["norm_g"].shape), resident(wts["ffn_conv"].shape),
                  hbm, hbm, hbm, hbm, hbm, hbm],
        out_specs=[hbm, hbm],
        scratch_shapes=[pltpu.VMEM((D_FF // half, D_MODEL, FFN_SUB), BF16),
                        pltpu.VMEM((D_FF, D_MODEL), BF16),
                        pltpu.VMEM((FFN_ROWS, D_FF), BF16),
                        pltpu.VMEM((2, FFN_ROWS, D_MODEL), BF16),
                        pltpu.VMEM((FFN_ROWS, D_MODEL), F32),
                        pltpu.VMEM((2, FFN_ROWS, D_MODEL), F32),
                        pltpu.VMEM((FFN_WEIGHT_BUFS, 2, D_MODEL, half), F32),
                        pltpu.VMEM((FFN_WEIGHT_BUFS, half, D_MODEL), F32),
                        pltpu.SemaphoreType.DMA((3,)),
                        pltpu.SemaphoreType.DMA((2,)),
                        pltpu.SemaphoreType.DMA((FFN_WEIGHT_BUFS, 3))],
        compiler_params=pltpu.CompilerParams(
            dimension_semantics=("arbitrary",), vmem_limit_bytes=VMEM_LIMIT),
        name="ffn",
    )(mod_all, wts["norm_g"], wts["ffn_conv"], x1p, h2p, x1s, h2s, wts["ffn_up"], wts["ffn_down"])


def _pos_tables(t_len):
    quarter = D_MODEL // 4
    omega = (1.0 / (np.float32(POS_BASE) ** (np.arange(quarter, dtype=np.float32) / np.float32(quarter))))
    omega = omega.astype(np.float32)

    def emb(p):
        a = p.astype(np.float32)[:, None] * omega[None, :]
        return np.concatenate([np.sin(a), np.cos(a)], axis=-1).astype(np.float32)

    return jnp.asarray(emb(np.arange(t_len // GRID_W))), jnp.asarray(emb(np.arange(GRID_W)))


def _stabiliser_rows(state_m):
    bs = state_m.shape[0]
    m = jnp.pad(state_m.astype(F32), ((0, 0), (0, 0), (0, 0), (0, 16 - HEADS)))
    return jnp.pad(m.reshape(bs, DEPTH, 1, 32), ((0, 0), (0, 0), (0, 0), (L_MF, LANES - 32 - L_MF)))


def kernel(x_prompt, x_sample, state_C, state_n, state_m, c, c_ctx, norm_g, ada_w, ada_b, w_in,
           gate_b, gmlp_ws, gmlp_b, pool_w, pool_scale, conv_w, w_out, ffn_up, ffn_conv, ffn_down):
    bp, tp, _ = x_prompt.shape
    bs, ts, _ = x_sample.shape
    assert tp == CHUNK and ts % CHUNK == 0 and FFN_ROWS % tp == 0 and ts == FFN_ROWS

    conds = jnp.concatenate([c_ctx[None, :], c, jnp.zeros((7 - bs, D_MODEL), F32)], axis=0)
    mod_all = _modulation(conds, ada_w, ada_b)
    win_p = _prep_w_in(w_in)

    gb = jnp.pad(gate_b, ((0, 0), (0, 0), (0, 8 - HEADS))).reshape(DEPTH, 1, 32)
    dg = D_GROUP // 4
    wts = {
        "norm_g": norm_g,
        "w_in": win_p,
        "gate_b": jnp.pad(gb, ((0, 0), (0, 0), (0, LANES - 32))),
        "gmlp_ws": gmlp_ws,
        "gmlp_bias": jnp.repeat(jnp.swapaxes(gmlp_b, 1, 2), dg, axis=2),
        "pool_bd": jnp.einsum("lgcd,gh->lgchd", pool_w, jnp.eye(4, dtype=F32)).reshape(
            DEPTH, D_GROUP, D_GROUP).astype(BF16),
        "pool_scale": pool_scale,
        "conv_w": conv_w,
        "w_out": w_out,
        "ffn_up": ffn_up,
        "ffn_conv": ffn_conv,
        "ffn_down": ffn_down,
    }
    pos = _pos_tables(ts)
    state = (state_C.astype(F32), state_n.astype(F32), _stabiliser_rows(state_m))

    xp, xs = x_prompt, x_sample
    prev, ms = None, []
    for l in range(DEPTH):
        x1p, h2p, c_all, n_all, m_fin = _mixer_call(xp, l, mod_all, wts, 0, emit_state=True,
                                                    prev_states=prev, nseq=MIXER_ROWS // tp)
        prev = (c_all, n_all)
        ms.append(jnp.stack([m_fin[:, 0, L_MF:L_MF + HEADS], m_fin[:, 0, L_MB:L_MB + HEADS]], axis=1))

        x1s, h2s = _mixer_call(xs, l, mod_all, wts, 1, pos=pos if l == 0 else None, state=state)

        xp, xs = _ffn_call(x1p.reshape(bp * tp, D_MODEL), h2p.reshape(bp * tp, D_MODEL),
                           x1s.reshape(bs * ts, D_MODEL), h2s.reshape(bs * ts, D_MODEL),
                           l, mod_all, wts, tp, ts)
        xp = xp.reshape(bp, tp, D_MODEL)
        xs = xs.reshape(bs, ts, D_MODEL)

    return (xp, xs, c_all, n_all, jnp.stack(ms, axis=1))
```

```python
import functools

import numpy as np
import jax
import jax.numpy as jnp
from jax import lax
from jax.experimental import pallas as pl
from jax.experimental.pallas import tpu as pltpu

F32 = jnp.float32
BF16 = jnp.bfloat16

D_MODEL = 1024
DEPTH = 2
HEADS = 4
PAIRS = HEADS // 2
DH = 64
D_GROUP = 256
D_FF = 2816
N_MOD = 6
EPS = 1e-6
GRID_W = 64
POS_BASE = 10000.0
D_IN = 2576
N_GATE = 4 * HEADS

CHUNK = 256
GMLP_CHUNK = 128
HALO = 8
LANES = 128

C_Q, C_K, C_V, C_O = 0, 256, 512, 768
C_GU, C_GV, C_PZ, C_CB, C_CC, C_CX = 1024, 1280, 1536, 1792, 2048, 2304
Z_COLS = 2560
C_GATE = Z_COLS
W_COLS = Z_COLS + LANES
Z_GROUP = 512
L_BF, L_MF, L_BB, L_MB = 0, 8, 16, 24

MIXER_ROWS = 1024
FFN_ROWS = 1024
FFN_SUB = 256
FFN_WEIGHT_BUFS = 8
VMEM_LIMIT = 60 * 1024 * 1024


def _rms(x, g):
    ms = jnp.mean(x * x, axis=-1, keepdims=True)
    return x * lax.rsqrt(ms + EPS) * g


def _sigmoid(x):
    return 1.0 / (1.0 + jnp.exp(-x))


def _log_sigmoid(x):
    return jnp.minimum(x, 0.0) - jnp.log1p(jnp.exp(-jnp.abs(x)))


def _scan_lanes(x, op, reverse, fill):
    n = x.shape[1]
    lane = lax.broadcasted_iota(jnp.int32, x.shape, 1)
    k = 1
    while k < n:
        if reverse:
            sh = jnp.where(lane < n - k, pltpu.roll(x, n - k, axis=1), fill)
        else:
            sh = jnp.where(lane >= k, pltpu.roll(x, k, axis=1), fill)
        x = op(x, sh)
        k *= 2
    return x


def _shift_down(x, k):
    return pltpu.roll(x, k, axis=0)


def _shift_up(x, k):
    return pltpu.roll(x, x.shape[0] - k, axis=0)


def _prep_kernel(wt_ref, sel_ref, wout_ref, win_o_ref, wout_o_ref):
    for n in range(Z_COLS // LANES):
        r0 = LANES * n if n < 4 * D_GROUP // LANES else LANES * n + N_GATE
        win_o_ref[0, :, LANES * n:LANES * (n + 1)] = wt_ref[0, r0:r0 + LANES, :].T.astype(BF16)
    g = wt_ref[0, 4 * D_GROUP:4 * D_GROUP + LANES, :].T.astype(BF16)
    win_o_ref[0, :, C_GATE:W_COLS] = jnp.dot(g, sel_ref[...], preferred_element_type=F32).astype(BF16)
    wout_o_ref[...] = wout_ref[...].astype(BF16)


def _prep_weights(w_in, w_out):
    sel = np.zeros((LANES, LANES), np.float32)
    for j in range(N_GATE):
        sel[j, 8 * (j // HEADS) + j % HEADS] = 1.0
    rows = D_MODEL // 2
    return pl.pallas_call(
        _prep_kernel,
        out_shape=[jax.ShapeDtypeStruct((DEPTH, D_MODEL, W_COLS), BF16),
                   jax.ShapeDtypeStruct(w_out.shape, BF16)],
        grid=(DEPTH, D_MODEL // rows),
        in_specs=[pl.BlockSpec((1, D_IN, rows), lambda l, i: (l, 0, i)),
                  pl.BlockSpec((LANES, LANES), lambda l, i: (0, 0)),
                  pl.BlockSpec((1, rows, D_MODEL), lambda l, i: (l, i, 0))],
        out_specs=[pl.BlockSpec((1, rows, W_COLS), lambda l, i: (l, i, 0)),
                   pl.BlockSpec((1, rows, D_MODEL), lambda l, i: (l, i, 0))],
        compiler_params=pltpu.CompilerParams(dimension_semantics=("arbitrary", "arbitrary"),
                                             vmem_limit_bytes=VMEM_LIMIT),
        name="prep_mixer_w")(jnp.swapaxes(w_in, 1, 2), jnp.asarray(sel, BF16), w_out)


def _mod_kernel(c_ref, w_ref, b_ref, o_ref):
    c = c_ref[...]
    s = (c * _sigmoid(c)).astype(BF16)
    bias = b_ref[pl.ds(pl.program_id(0), 1), :]
    o_ref[0] = jnp.dot(s, w_ref[0].astype(BF16), preferred_element_type=F32) + bias


def _modulation(conds, ada_w, ada_b):
    n_out = N_MOD * D_MODEL
    bn = 1536
    return pl.pallas_call(
        _mod_kernel,
        out_shape=jax.ShapeDtypeStruct((DEPTH, 8, n_out), F32),
        grid=(DEPTH, n_out // bn),
        in_specs=[
            pl.BlockSpec((8, D_MODEL), lambda l, j: (0, 0)),
            pl.BlockSpec((1, D_MODEL, bn), lambda l, j: (l, 0, j)),
            pl.BlockSpec((DEPTH, bn), lambda l, j: (0, j)),
        ],
        out_specs=pl.BlockSpec((1, 8, bn), lambda l, j: (l, 0, j)),
        compiler_params=pltpu.CompilerParams(
            dimension_semantics=("arbitrary", "arbitrary"), vmem_limit_bytes=VMEM_LIMIT),
        name="adaln_mod",
    )(conds, ada_w, ada_b)


def _mod_rows(mod_ref, cond):
    row = mod_ref[0, pl.ds(cond, 1), :]
    return [row[:, i * D_MODEL:(i + 1) * D_MODEL] for i in range(N_MOD)]


def _gate_stats(gates):
    gt = gates.T
    b_f = _scan_lanes(_log_sigmoid(gt[8:16]), jnp.add, False, 0.0)
    g_f = gt[0:8] - b_f
    cm_f = _scan_lanes(g_f, jnp.maximum, False, -jnp.inf)
    b_b = _scan_lanes(_log_sigmoid(gt[24:32]), jnp.add, True, 0.0)
    g_b = gt[16:24] - b_b
    cm_b = _scan_lanes(g_b, jnp.maximum, True, -jnp.inf)
    stack = jnp.concatenate([b_f, cm_f, b_b, cm_b, jnp.zeros((LANES - 32, CHUNK), F32)], axis=0)
    return jnp.concatenate([g_f, g_b], axis=0), stack.T


def _pair_cols(mat, lane0, lane_lt_dh):
    return jnp.where(lane_lt_dh, mat[:, lane0:lane0 + 1], mat[:, lane0 + 1:lane0 + 2])


def _mlstm(z_ref, gcol_ref, grow_ref, cols_ref, bd_ref, ycat_ref, nc, state0, m0, need_final):
    lane_row = lax.broadcasted_iota(jnp.int32, (1, LANES), 1)
    lane = lax.broadcasted_iota(jnp.int32, (CHUNK, LANES), 1)
    lt_dh = lane < DH
    row_ll = lax.broadcasted_iota(jnp.int32, (CHUNK, CHUNK), 0)
    col_ll = lax.broadcasted_iota(jnp.int32, (CHUNK, CHUNK), 1)
    tri = (col_ll <= row_ll, col_ll >= row_ll)
    bd_row = lax.broadcasted_iota(jnp.int32, (LANES, 2 * LANES), 0)
    bd_col = lax.broadcasted_iota(jnp.int32, (LANES, 2 * LANES), 1)
    bd_mask = (bd_row < DH) == ((bd_col & (LANES - 1)) < DH)
    dirs = ((0, L_BF, L_MF, CHUNK - 1), (1, L_BB, L_MB, 0))

    m_start = [[None] * nc, [None] * nc]
    m_fin = [None, None]
    for d, _, _, last in dirs:
        m = m0
        order = range(nc) if d == 0 else range(nc - 1, -1, -1)
        for step, c in enumerate(order):
            m_start[d][c] = m
            if need_final or step < nc - 1:
                cl = cols_ref[c, last:last + 1, :]
                m = pltpu.roll(cl, 8, axis=1) + jnp.maximum(m, cl)
        m_fin[d] = m

    have_state = [[False] * nc, [False] * nc]
    fin = [None, None]
    for d, l_b, l_m, last in dirs:
        bd = list(state0[d])
        order = range(nc) if d == 0 else range(nc - 1, -1, -1)
        for step, c in enumerate(order):
            if bd[0] is not None:
                have_state[d][c] = True
                for p in range(PAIRS):
                    bd_ref[d, c, p] = bd[p].astype(BF16)
            if not (need_final or step < nc - 1):
                continue
            rows = slice(HALO + c * CHUNK, HALO + (c + 1) * CHUNK)
            cols = cols_ref[c]
            m_last = jnp.maximum(cols[last:last + 1, :], m_start[d][c])
            a_col = jnp.exp(gcol_ref[c] - cols - pltpu.roll(m_last, LANES - 8, axis=1))
            a_prev = jnp.exp(m_start[d][c] - m_last)
            for p in range(PAIRS):
                a_pair = _pair_cols(a_col, l_b + 2 * p, lt_dh)
                v_pair = z_ref[rows, C_V + LANES * p:C_V + LANES * (p + 1)]
                rhs = jnp.concatenate([v_pair * a_pair, a_pair], axis=1).astype(BF16)
                k_pair = (z_ref[rows, C_K + LANES * p:C_K + LANES * (p + 1)] * (DH ** -0.5)).astype(BF16)
                upd = lax.dot_general(k_pair, rhs, (((0,), (0,)), ((), ())),
                                      preferred_element_type=F32)
                upd = jnp.where(bd_mask, upd, 0.0)
                if bd[p] is None:
                    bd[p] = upd
                else:
                    scale = jnp.where(bd_row < DH, a_prev[:, l_m + 2 * p:l_m + 2 * p + 1],
                                      a_prev[:, l_m + 2 * p + 1:l_m + 2 * p + 2])
                    bd[p] = scale * bd[p] + upd
        fin[d] = bd

    for c in range(nc):
        rows = slice(HALO + c * CHUNK, HALO + (c + 1) * CHUNK)
        cols = cols_ref[c]
        grow = grow_ref[c]
        m_prev = jnp.where(lane_row < L_BB, m_start[0][c], m_start[1][c])
        m_all = jnp.maximum(cols, m_prev)
        w_inter = jnp.exp(m_prev - m_all)
        e_neg = jnp.exp(-(pltpu.roll(cols, 8, axis=1) + m_all))
        for p in range(PAIRS):
            q_pair = z_ref[rows, C_Q + LANES * p:C_Q + LANES * (p + 1)].astype(BF16)
            k_f32 = z_ref[rows, C_K + LANES * p:C_K + LANES * (p + 1)] * (DH ** -0.5)
            v_pair = z_ref[rows, C_V + LANES * p:C_V + LANES * (p + 1)]
            p_cat = ([], [])
            va = []
            for j in range(2):
                h = 2 * p + j
                own = lt_dh if j == 0 else jnp.logical_not(lt_dh)
                k_own = jnp.where(own, k_f32, 0.0).astype(BF16)
                s = lax.dot_general(q_pair, k_own, (((1,), (1,)), ((), ())),
                                    preferred_element_type=F32)
                va.append(jnp.concatenate([jnp.where(own, v_pair, 0.0), jnp.where(own, 1.0, 0.0)],
                                          axis=1).astype(BF16))
                for d, _, l_m, _ in dirs:
                    w = jnp.exp(jnp.where(tri[d], grow[8 * d + h:8 * d + h + 1] - m_all[:, l_m + h:l_m + h + 1],
                                          -jnp.inf))
                    p_cat[d].append((s * w).astype(BF16))
            va = jnp.concatenate(va, axis=0)
            h_sum = None
            for d, _, l_m, _ in dirs:
                tot = jnp.dot(jnp.concatenate(p_cat[d], axis=1), va, preferred_element_type=F32)
                if have_state[d][c]:
                    e_pair = _pair_cols(w_inter, l_m + 2 * p, lt_dh)
                    qc = jnp.dot(q_pair, bd_ref[d, c, p], preferred_element_type=F32)
                    tot = tot + jnp.concatenate([e_pair, e_pair], axis=1) * qc
                floor = _pair_cols(e_neg, l_m + 2 * p, lt_dh)
                h_dir = tot[:, :LANES] / jnp.maximum(jnp.abs(tot[:, LANES:]), floor)
                h_sum = h_dir if h_sum is None else h_sum + h_dir
            o_pair = z_ref[rows, C_O + LANES * p:C_O + LANES * (p + 1)]
            ycat_ref[c * CHUNK:(c + 1) * CHUNK, LANES * p:LANES * (p + 1)] = (
                _sigmoid(o_pair) * h_sum).astype(BF16)

    m_row = jnp.where(lane_row < L_BB, m_fin[0], m_fin[1])
    return fin, m_row


def _pair_state(c_ref, n_ref, q, d, p):
    eye = (lax.broadcasted_iota(jnp.int32, (DH, DH), 0) == lax.broadcasted_iota(jnp.int32, (DH, DH), 1))
    zero = jnp.zeros((DH, DH), F32)
    blocks = []
    for j in range(2):
        h = 2 * p + j
        n_row = n_ref[q, 0, d, h:h + 1, :]
        n_col = jnp.sum(jnp.where(eye, n_row, 0.0), axis=1, keepdims=True)
        row = [zero] * 4
        row[j] = c_ref[q, 0, d, h]
        row[2 + j] = jnp.broadcast_to(n_col, (DH, DH))
        blocks.append(jnp.concatenate(row, axis=1))
    return jnp.concatenate(blocks, axis=0)


def _mixer_kernel(*refs, layer, seq_len, nseq, add_pos, has_state, emit_state, cond_base, n_prev):
    it = iter(refs)
    x_ref = next(it)
    if add_pos:
        er_ref = next(it)
        ec_ref = next(it)
    mod_ref = next(it)
    ng_ref = next(it)
    win_ref = next(it)
    gb_ref = next(it)
    ws_ref = next(it)
    gbias_ref = next(it)
    pbd_ref = next(it)
    psc_ref = next(it)
    cw_ref = next(it)
    wout_ref = next(it)
    if has_state:
        c0_ref = next(it)
        n0_ref = next(it)
        m0_ref = next(it)
    if n_prev:
        cprev_ref = next(it)
        nprev_ref = next(it)
    x1_ref = next(it)
    h2_ref = next(it)
    if emit_state:
        cout_ref = next(it)
        nout_ref = next(it)
        mout_ref = next(it)
    z_all = next(it)
    gcol_all = next(it)
    grow_all = next(it)
    cols_all = next(it)
    bd_all = next(it)
    ycat_all = next(it)

    nblk = seq_len // CHUNK
    cond = cond_base if cond_base == 0 else cond_base + pl.program_id(0)
    sh1, sc1, gt1, sh2, sc2, _ = _mod_rows(mod_ref, cond)
    ng = ng_ref[0]
    seqs = range(nseq)
    units = [(q, r) for q in seqs for r in range(nblk)]

    zeros_halo = jnp.zeros((HALO, Z_COLS - C_PZ), F32)
    for q in seqs:
        z_all[q, 0:HALO, C_PZ:Z_COLS] = zeros_halo
        z_all[q, HALO + seq_len:2 * HALO + seq_len, C_PZ:Z_COLS] = zeros_halo

    g_in = ng[0:1] * (1.0 + sc1)
    hbs = []
    for q, r in units:
        rows = slice(r * CHUNK, (r + 1) * CHUNK)
        xb = x_ref[q, rows, :]
        if add_pos:
            pieces = []
            for g in range(CHUNK // GRID_W):
                gi = r * (CHUNK // GRID_W) + g
                er = jnp.broadcast_to(er_ref[gi:gi + 1, :], (GRID_W, D_MODEL // 2))
                pieces.append(jnp.concatenate([er, ec_ref[...]], axis=1))
            xb = xb + jnp.concatenate(pieces, axis=0)
            x1_ref[q, rows, :] = xb
        hbs.append((_rms(xb, g_in) + sh1).astype(BF16))
    hb = jnp.concatenate(hbs, axis=0)
    gates_all = jnp.dot(hb, win_ref[0, :, C_GATE:W_COLS], preferred_element_type=F32) + gb_ref[0]
    for cg in range(Z_COLS // Z_GROUP):
        zc = jnp.dot(hb, win_ref[0, :, cg * Z_GROUP:(cg + 1) * Z_GROUP], preferred_element_type=F32)
        for u, (q, r) in enumerate(units):
            z_all[q, HALO + r * CHUNK:HALO + (r + 1) * CHUNK, cg * Z_GROUP:(cg + 1) * Z_GROUP] = (
                zc[u * CHUNK:(u + 1) * CHUNK])
    for u, (q, r) in enumerate(units):
        gates = gates_all[u * CHUNK:(u + 1) * CHUNK]
        gcol_all[q, r] = gates
        grow, cols = _gate_stats(gates)
        grow_all[q, r] = grow
        cols_all[q, r] = cols

    for q in seqs:
        if has_state:
            state0 = [[_pair_state(c0_ref, n0_ref, q, d, p) for p in range(PAIRS)] for d in range(2)]
            m0 = m0_ref[q, 0]
        else:
            state0 = [[None] * PAIRS, [None] * PAIRS]
            m0 = jnp.zeros((1, LANES), F32)
        fin, m_row = _mlstm(z_all.at[q], gcol_all.at[q], grow_all.at[q], cols_all.at[q], bd_all.at[q],
                            ycat_all.at[q], nblk, state0, m0, need_final=emit_state)
        if emit_state:
            for d in range(2):
                for p in range(PAIRS):
                    for j in range(2):
                        cout_ref[q, n_prev, d, 2 * p + j] = fin[d][p][DH * j:DH * (j + 1), DH * j:DH * (j + 1)]
                    n_t = fin[d][p][:, LANES:].T
                    nout_ref[q, n_prev, d, 2 * p:2 * p + 1, :] = n_t[0:1, 0:DH]
                    nout_ref[q, n_prev, d, 2 * p + 1:2 * p + 2, :] = n_t[DH:DH + 1, DH:LANES]
            mout_ref[q] = m_row
    if emit_state and n_prev:
        cout_ref[:, 0:n_prev] = cprev_ref[...]
        nout_ref[:, 0:n_prev] = nprev_ref[...]

    lane_g = lax.broadcasted_iota(jnp.int32, (CHUNK, D_GROUP), 1) // (D_GROUP // 4)
    lg = lax.broadcasted_iota(jnp.int32, (GMLP_CHUNK, D_GROUP), 1) // (D_GROUP // 4)
    row_c = lax.broadcasted_iota(jnp.int32, (CHUNK, D_GROUP), 0)
    half = jnp.where(lane_g == 0, 1, jnp.where(lane_g == 1, 2, jnp.where(lane_g == 2, 4, 8)))
    cw = cw_ref[0]
    ws_cat = jnp.concatenate([ws_ref[0, g] for g in range(4)], axis=1).astype(BF16)
    for q, r in units:
        z_ref = z_all.at[q]
        ycat_ref = ycat_all.at[q]
        rows = slice(r * CHUNK, (r + 1) * CHUNK)
        zrows = slice(HALO + r * CHUNK, HALO + (r + 1) * CHUNK)
        hrows = slice(r * CHUNK, (r + 1) * CHUNK + 2 * HALO)
        inner = slice(HALO, HALO + CHUNK)

        for s in range(CHUNK // GMLP_CHUNK):
            zs = slice(HALO + r * CHUNK + s * GMLP_CHUNK, HALO + r * CHUNK + (s + 1) * GMLP_CHUNK)
            vch = z_ref[zs, C_GV:C_GV + D_GROUP]
            v_stack = jnp.concatenate([jnp.where(lg == g, vch, 0.0) for g in range(4)], axis=0)
            mixed = gbias_ref[0] + jnp.dot(ws_cat, v_stack.astype(BF16), preferred_element_type=F32)
            yb = z_ref[zs, C_GU:C_GU + D_GROUP] * mixed
            ys = slice(r * CHUNK + s * GMLP_CHUNK, r * CHUNK + (s + 1) * GMLP_CHUNK)
            ycat_ref[ys, D_GROUP:2 * D_GROUP] = yb.astype(BF16)

        pz = z_ref[hrows, C_PZ:C_PZ + D_GROUP]
        a2 = pz + _shift_down(pz, 1)
        a4 = a2 + _shift_down(a2, 2)
        a8 = a4 + _shift_down(a4, 4)
        a16 = a8 + _shift_down(a8, 8)
        win = [a2[inner], _shift_up(a4, 1)[inner], _shift_up(a8, 3)[inner], _shift_up(a16, 7)[inner]]
        wsum = jnp.where(lane_g == 0, win[0], jnp.where(lane_g == 1, win[1],
                         jnp.where(lane_g == 2, win[2], win[3])))
        t_abs = row_c + r * CHUNK
        cnt = (jnp.minimum(t_abs + half, seq_len) - jnp.maximum(t_abs - half, 0)).astype(F32)
        pooled = wsum / cnt - pz[inner]
        yc = (jnp.dot(pooled.astype(BF16), pbd_ref[0], preferred_element_type=F32)
              * psc_ref[layer:layer + 1, :])
        ycat_ref[rows, 2 * D_GROUP:3 * D_GROUP] = yc.astype(BF16)

        u = z_ref[hrows, C_CC:C_CC + D_GROUP] * z_ref[hrows, C_CX:C_CX + D_GROUP]
        conv = (cw[0:1] * _shift_down(u, 1)[inner] + cw[1:2] * u[inner]
                + cw[2:3] * _shift_up(u, 1)[inner])
        yd = z_ref[zrows, C_CB:C_CB + D_GROUP] * conv
        ycat_ref[rows, 3 * D_GROUP:4 * D_GROUP] = yd.astype(BF16)

    y_all = jnp.dot(jnp.concatenate([ycat_all[q, r * CHUNK:(r + 1) * CHUNK, :] for q, r in units], axis=0),
                    wout_ref[0], preferred_element_type=F32)
    g_y = gt1 * ng[1:2]
    g_h2 = ng[2:3] * (1.0 + sc2)
    for u, (q, r) in enumerate(units):
        rows = slice(r * CHUNK, (r + 1) * CHUNK)
        xb = x1_ref[q, rows, :] if add_pos else x_ref[q, rows, :]
        x1 = xb + _rms(y_all[u * CHUNK:(u + 1) * CHUNK], g_y)
        x1_ref[q, rows, :] = x1
        h2_ref[q, rows, :] = (_rms(x1, g_h2) + sh2).astype(BF16)


def _layer_spec(shape, layer):
    nd = len(shape)
    return pl.BlockSpec((1,) + tuple(shape[1:]), lambda b: (layer,) + (0,) * (nd - 1),
                        pipeline_mode=pl.Buffered(1))


def _mixer_call(x, layer, mod_all, wts, cond_base, pos=None, state=None, emit_state=False,
                prev_states=None, nseq=1):
    bsz, seq_len, _ = x.shape
    nblk = seq_len // CHUNK
    add_pos = pos is not None
    has_state = state is not None
    n_prev = 0 if prev_states is None else prev_states[0].shape[1]
    args = [x]
    in_specs = [pl.BlockSpec((nseq, seq_len, D_MODEL), lambda b: (b, 0, 0))]
    if add_pos:
        args += list(pos)
        in_specs += [pl.BlockSpec(p.shape, lambda b: (0, 0), pipeline_mode=pl.Buffered(1)) for p in pos]
    args.append(mod_all)
    in_specs.append(_layer_spec(mod_all.shape, layer))
    for name in ("norm_g", "w_in", "gate_b", "gmlp_ws", "gmlp_bias", "pool_bd", "pool_scale",
                 "conv_w", "w_out"):
        args.append(wts[name])
        if wts[name].ndim == 2:
            in_specs.append(pl.BlockSpec(wts[name].shape, lambda b: (0, 0), pipeline_mode=pl.Buffered(1)))
        else:
            in_specs.append(_layer_spec(wts[name].shape, layer))
    if has_state:
        args += list(state)
        in_specs += [pl.BlockSpec((nseq, 1, 2, HEADS, DH, DH), lambda b: (b, layer, 0, 0, 0, 0)),
                     pl.BlockSpec((nseq, 1, 2, HEADS, DH), lambda b: (b, layer, 0, 0, 0)),
                     pl.BlockSpec((nseq, 1, 1, LANES), lambda b: (b, layer, 0, 0))]
    out_shape = [jax.ShapeDtypeStruct((bsz, seq_len, D_MODEL), F32),
                 jax.ShapeDtypeStruct((bsz, seq_len, D_MODEL), BF16)]
    out_specs = [pl.BlockSpec((nseq, seq_len, D_MODEL), lambda b: (b, 0, 0)),
                 pl.BlockSpec((nseq, seq_len, D_MODEL), lambda b: (b, 0, 0))]
    if emit_state:
        if n_prev:
            args += list(prev_states)
            in_specs += [pl.BlockSpec((nseq, n_prev, 2, HEADS, DH, DH), lambda b: (b, 0, 0, 0, 0, 0)),
                         pl.BlockSpec((nseq, n_prev, 2, HEADS, DH), lambda b: (b, 0, 0, 0, 0))]
        out_shape += [jax.ShapeDtypeStruct((bsz, n_prev + 1, 2, HEADS, DH, DH), F32),
                      jax.ShapeDtypeStruct((bsz, n_prev + 1, 2, HEADS, DH), F32),
                      jax.ShapeDtypeStruct((bsz, 1, LANES), F32)]
        out_specs += [pl.BlockSpec((nseq, n_prev + 1, 2, HEADS, DH, DH), lambda b: (b, 0, 0, 0, 0, 0)),
                      pl.BlockSpec((nseq, n_prev + 1, 2, HEADS, DH), lambda b: (b, 0, 0, 0, 0)),
                      pl.BlockSpec((nseq, 1, LANES), lambda b: (b, 0, 0))]
    kern = functools.partial(_mixer_kernel, layer=layer, seq_len=seq_len, nseq=nseq, add_pos=add_pos,
                             has_state=has_state, emit_state=emit_state, cond_base=cond_base,
                             n_prev=n_prev)
    return pl.pallas_call(
        kern,
        out_shape=out_shape,
        grid=(bsz // nseq,),
        in_specs=in_specs,
        out_specs=out_specs,
        scratch_shapes=[pltpu.VMEM((nseq, seq_len + 2 * HALO, Z_COLS), F32),
                        pltpu.VMEM((nseq, nblk, CHUNK, LANES), F32),
                        pltpu.VMEM((nseq, nblk, 16, CHUNK), F32),
                        pltpu.VMEM((nseq, nblk, CHUNK, LANES), F32),
                        pltpu.VMEM((nseq, 2, nblk, PAIRS, LANES, 2 * LANES), BF16),
                        pltpu.VMEM((nseq, seq_len, D_MODEL), BF16)],
        compiler_params=pltpu.CompilerParams(
            dimension_semantics=("arbitrary",), vmem_limit_bytes=VMEM_LIMIT),
        name="mixer_t%d" % seq_len,
    )(*args)


def _ffn_kernel(mod_ref, ng_ref, cw_ref, x1p_hbm, h2p_hbm, x1s_hbm, h2s_hbm, up_hbm, down_hbm,
                outp_hbm, outs_hbm,
                up_bf, down_bf, act_ref, h2_buf, x1_buf, o_buf, up_stage, down_stage,
                in_sems, out_sems, w_sems, *, layer, tiles_p, tiles_s, seq_p, seq_s):
    i = pl.program_id(0)
    n_tiles = tiles_p + tiles_s
    rows = FFN_ROWS
    half = FFN_SUB // 2
    nsub = D_FF // half
    slot = i % 2
    is_prompt = i < tiles_p

    def tile_rows(tile, first_tile):
        if isinstance(tile, int):
            return pl.ds((tile - first_tile) * rows, rows)
        return pl.ds(pl.multiple_of((tile - first_tile) * rows, rows), rows)

    def h2_copy(h2_hbm, r, buf):
        return pltpu.make_async_copy(h2_hbm.at[r], h2_buf.at[buf], in_sems.at[buf])

    def x1_copy(x1_hbm, r):
        return pltpu.make_async_copy(x1_hbm.at[r], x1_buf, in_sems.at[2])

    def fetch_h2(tile, buf):
        @pl.when(tile < tiles_p)
        def _():
            h2_copy(h2p_hbm, tile_rows(tile, 0), buf).start()

        @pl.when(tile >= tiles_p)
        def _():
            h2_copy(h2s_hbm, tile_rows(tile, tiles_p), buf).start()

    def fetch_x1(tile):
        @pl.when(tile < tiles_p)
        def _():
            x1_copy(x1p_hbm, tile_rows(tile, 0)).start()

        @pl.when(tile >= tiles_p)
        def _():
            x1_copy(x1s_hbm, tile_rows(tile, tiles_p)).start()

    def await_store(buf):
        pltpu.make_async_copy(o_buf.at[buf], outp_hbm.at[pl.ds(0, rows)], out_sems.at[buf]).wait()

    def weight_copies(c, buf):
        aligned = (lambda v: v) if isinstance(c, int) else (lambda v: pl.multiple_of(v, half))
        cv = pl.ds(aligned(c * half), half)
        cg = pl.ds(aligned(D_FF + c * half), half)
        return (pltpu.make_async_copy(up_hbm.at[layer, :, cv], up_stage.at[buf, 0], w_sems.at[buf, 0]),
                pltpu.make_async_copy(up_hbm.at[layer, :, cg], up_stage.at[buf, 1], w_sems.at[buf, 1]),
                pltpu.make_async_copy(down_hbm.at[layer, cv, :], down_stage.at[buf], w_sems.at[buf, 2]))

    @pl.when(i == 0)
    def _():
        h2_copy(h2p_hbm, tile_rows(0, 0), 0).start()
        depth = up_stage.shape[0]
        for c0 in range(depth - 1):
            for cp in weight_copies(c0, c0):
                cp.start()

        def convert(c, carry):
            buf = c % depth
            ahead = c + depth - 1

            @pl.when(ahead < nsub)
            def _():
                for cp in weight_copies(ahead, ahead % depth):
                    cp.start()

            for cp in weight_copies(c, buf):
                cp.wait()
            up_bf[c, :, 0:half] = up_stage[buf, 0].astype(BF16)
            up_bf[c, :, half:FFN_SUB] = up_stage[buf, 1].astype(BF16)
            down_bf[pl.ds(pl.multiple_of(c * half, half), half), :] = down_stage[buf].astype(BF16)
            return carry

        lax.fori_loop(0, nsub, convert, 0)

    h2_copy(h2p_hbm, pl.ds(0, rows), slot).wait()

    @pl.when(i >= 2)
    def _():
        await_store(slot)

    fetch_x1(i)

    @pl.when(i + 1 < n_tiles)
    def _():
        fetch_h2(i + 1, 1 - slot)

    row = lax.broadcasted_iota(jnp.int32, (rows, FFN_SUB), 0)
    first = ((row & (seq_s - 1)) == 0) | (((row & (seq_p - 1)) == 0) & is_prompt)
    last = ((row & (seq_s - 1)) == seq_s - 1) | (((row & (seq_p - 1)) == seq_p - 1) & is_prompt)
    h2 = h2_buf[slot]
    for s in range(nsub):
        a = jnp.dot(h2, up_bf[s], preferred_element_type=F32)
        cw = jnp.concatenate([cw_ref[0, :, s * half:(s + 1) * half],
                              0.5 * cw_ref[0, :, D_FF + s * half:D_FF + (s + 1) * half]], axis=1)
        a_prev = jnp.where(first, 0.0, pltpu.roll(a, 1, axis=0))
        a_next = jnp.where(last, 0.0, pltpu.roll(a, rows - 1, axis=0))
        ac = cw[0:1] * a_prev + cw[1:2] * a + cw[2:3] * a_next
        hg = ac[:, half:]
        act = (hg * (1.0 + jnp.tanh(hg))) * ac[:, :half]
        act_ref[:, s * half:(s + 1) * half] = act.astype(BF16)
    y = jnp.dot(act_ref[...], down_bf[...], preferred_element_type=F32)
    cond = jnp.where(is_prompt, 0, i - tiles_p + 1)
    gt2 = _mod_rows(mod_ref, cond)[5]

    res = gt2 * _rms(y, ng_ref[0][3:4])
    x1_copy(x1p_hbm, pl.ds(0, rows)).wait()
    o_buf[slot] = x1_buf[...] + res

    @pl.when(is_prompt)
    def _():
        pltpu.make_async_copy(o_buf.at[slot], outp_hbm.at[tile_rows(i, 0)], out_sems.at[slot]).start()

    @pl.when(jnp.logical_not(is_prompt))
    def _():
        pltpu.make_async_copy(o_buf.at[slot], outs_hbm.at[tile_rows(i, tiles_p)], out_sems.at[slot]).start()

    @pl.when(i == n_tiles - 1)
    def _():
        await_store(1 - slot)
        await_store(slot)


def _ffn_call(x1p, h2p, x1s, h2s, layer, mod_all, wts, seq_p, seq_s):
    tiles_p = x1p.shape[0] // FFN_ROWS
    tiles_s = x1s.shape[0] // FFN_ROWS
    assert tiles_p >= 1 and tiles_s >= 1 and tiles_p + tiles_s >= 2
    half = FFN_SUB // 2

    def resident(shape):
        nd = len(shape)
        return pl.BlockSpec((1,) + tuple(shape[1:]), lambda i: (layer,) + (0,) * (nd - 1),
                            pipeline_mode=pl.Buffered(1))

    hbm = pl.BlockSpec(memory_space=pl.ANY)
    return pl.pallas_call(
        functools.partial(_ffn_kernel, layer=layer, tiles_p=tiles_p, tiles_s=tiles_s,
                          seq_p=seq_p, seq_s=seq_s),
        out_shape=[jax.ShapeDtypeStruct(x1p.shape, F32), jax.ShapeDtypeStruct(x1s.shape, F32)],
        grid=(tiles_p + tiles_s,),
        in_specs=[resident(mod_all.shape), resident(wts["norm_g"].shape), resident(wts["ffn_conv"].shape),
                  hbm, hbm, hbm, hbm, hbm, hbm],
        out_specs=[hbm, hbm],
        scratch_shapes=[pltpu.VMEM((D_FF // half, D_MODEL, FFN_SUB), BF16),
                        pltpu.VMEM((D_FF, D_MODEL), BF16),
                        pltpu.VMEM((FFN_ROWS, D_FF), BF16),
                        pltpu.VMEM((2, FFN_ROWS, D_MODEL), BF16),
                        pltpu.VMEM((FFN_ROWS, D_MODEL), F32),
                        pltpu.VMEM((2, FFN_ROWS, D_MODEL), F32),
                        pltpu.VMEM((FFN_WEIGHT_BUFS, 2, D_MODEL, half), F32),
                        pltpu.VMEM((FFN_WEIGHT_BUFS, half, D_MODEL), F32),
                        pltpu.SemaphoreType.DMA((3,)),
                        pltpu.SemaphoreType.DMA((2,)),
                        pltpu.SemaphoreType.DMA((FFN_WEIGHT_BUFS, 3))],
        compiler_params=pltpu.CompilerParams(
            dimension_semantics=("arbitrary",), vmem_limit_bytes=VMEM_LIMIT),
        name="ffn",
    )(mod_all, wts["norm_g"], wts["ffn_conv"], x1p, h2p, x1s, h2s, wts["ffn_up"], wts["ffn_down"])


def _pos_tables(t_len):
    quarter = D_MODEL // 4
    omega = (1.0 / (np.float32(POS_BASE) ** (np.arange(quarter, dtype=np.float32) / np.float32(quarter))))
    omega = omega.astype(np.float32)

    def emb(p):
        a = p.astype(np.float32)[:, None] * omega[None, :]
        return np.concatenate([np.sin(a), np.cos(a)], axis=-1).astype(np.float32)

    return jnp.asarray(emb(np.arange(t_len // GRID_W))), jnp.asarray(emb(np.arange(GRID_W)))


def _stabiliser_rows(state_m):
    bs = state_m.shape[0]
    m = jnp.pad(state_m.astype(F32), ((0, 0), (0, 0), (0, 0), (0, 16 - HEADS)))
    return jnp.pad(m.reshape(bs, DEPTH, 1, 32), ((0, 0), (0, 0), (0, 0), (L_MF, LANES - 32 - L_MF)))


def kernel(x_prompt, x_sample, state_C, state_n, state_m, c, c_ctx, norm_g, ada_w, ada_b, w_in,
           gate_b, gmlp_ws, gmlp_b, pool_w, pool_scale, conv_w, w_out, ffn_up, ffn_conv, ffn_down):
    bp, tp, _ = x_prompt.shape
    bs, ts, _ = x_sample.shape
    assert tp == CHUNK and ts % CHUNK == 0 and FFN_ROWS % tp == 0 and ts == FFN_ROWS

    conds = jnp.concatenate([c_ctx[None, :], c, jnp.zeros((7 - bs, D_MODEL), F32)], axis=0)
    mod_all = _modulation(conds, ada_w, ada_b)
    win_p, wout_p = _prep_weights(w_in, w_out)

    gb = jnp.pad(gate_b, ((0, 0), (0, 0), (0, 8 - HEADS))).reshape(DEPTH, 1, 32)
    dg = D_GROUP // 4
    wts = {
        "norm_g": norm_g,
        "w_in": win_p,
        "gate_b": jnp.pad(gb, ((0, 0), (0, 0), (0, LANES - 32))),
        "gmlp_ws": gmlp_ws,
        "gmlp_bias": jnp.repeat(jnp.swapaxes(gmlp_b, 1, 2), dg, axis=2),
        "pool_bd": jnp.einsum("lgcd,gh->lgchd", pool_w, jnp.eye(4, dtype=F32)).reshape(
            DEPTH, D_GROUP, D_GROUP).astype(BF16),
        "pool_scale": pool_scale,
        "conv_w": conv_w,
        "w_out": wout_p,
        "ffn_up": ffn_up,
        "ffn_conv": ffn_conv,
        "ffn_down": ffn_down,
    }
    pos = _pos_tables(ts)
    state = (state_C.astype(F32), state_n.astype(F32), _stabiliser_rows(state_m))

    xp, xs = x_prompt, x_sample
    prev, ms = None, []
    for l in range(DEPTH):
        x1p, h2p, c_all, n_all, m_fin = _mixer_call(xp, l, mod_all, wts, 0, emit_state=True,
                                                    prev_states=prev, nseq=MIXER_ROWS // tp)
        prev = (c_all, n_all)
        ms.append(jnp.stack([m_fin[:, 0, L_MF:L_MF + HEADS], m_fin[:, 0, L_MB:L_MB + HEADS]], axis=1))

        x1s, h2s = _mixer_call(xs, l, mod_all, wts, 1, pos=pos if l == 0 else None, state=state)

        xp, xs = _ffn_call(x1p.reshape(bp * tp, D_MODEL), h2p.reshape(bp * tp, D_MODEL),
                           x1s.reshape(bs * ts, D_MODEL), h2s.reshape(bs * ts, D_MODEL),
                           l, mod_all, wts, tp, ts)
        xp = xp.reshape(bp, tp, D_MODEL)
        xs = xs.reshape(bs, ts, D_MODEL)

    return (xp, xs, c_all, n_all, jnp.stack(ms, axis=1))
```

```python
import functools

import numpy as np
import jax
import jax.numpy as jnp
from jax import lax
from jax.experimental import pallas as pl
from jax.experimental.pallas import tpu as pltpu

F32 = jnp.float32
BF16 = jnp.bfloat16

D_MODEL = 1024
DEPTH = 2
HEADS = 4
PAIRS = HEADS // 2
DH = 64
D_GROUP = 256
D_FF = 2816
N_MOD = 6
EPS = 1e-6
GRID_W = 64
POS_BASE = 10000.0
D_IN = 2576
N_GATE = 4 * HEADS

CHUNK = 256
GMLP_CHUNK = 128
HALO = 8
LANES = 128

C_Q, C_K, C_V, C_O = 0, 256, 512, 768
C_GU, C_GV, C_PZ, C_CB, C_CC, C_CX = 1024, 1280, 1536, 1792, 2048, 2304
Z_COLS = 2560
C_GATE = Z_COLS
W_COLS = Z_COLS + LANES
Z_GROUP = 512
L_BF, L_MF, L_BB, L_MB = 0, 8, 16, 24

MIXER_ROWS = 1024
FFN_ROWS = 1024
FFN_SUB = 256
FFN_WEIGHT_BUFS = 8
VMEM_LIMIT = 60 * 1024 * 1024


def _rms(x, g):
    ms = jnp.mean(x * x, axis=-1, keepdims=True)
    return x * lax.rsqrt(ms + EPS) * g


def _sigmoid(x):
    return 1.0 / (1.0 + jnp.exp(-x))


def _log_sigmoid(x):
    return jnp.minimum(x, 0.0) - jnp.log1p(jnp.exp(-jnp.abs(x)))


def _scan_lanes(x, op, reverse, fill):
    n = x.shape[1]
    lane = lax.broadcasted_iota(jnp.int32, x.shape, 1)
    k = 1
    while k < n:
        if reverse:
            sh = jnp.where(lane < n - k, pltpu.roll(x, n - k, axis=1), fill)
        else:
            sh = jnp.where(lane >= k, pltpu.roll(x, k, axis=1), fill)
        x = op(x, sh)
        k *= 2
    return x


def _shift_down(x, k):
    return pltpu.roll(x, k, axis=0)


def _shift_up(x, k):
    return pltpu.roll(x, x.shape[0] - k, axis=0)


def _prep_kernel(wt_ref, sel_ref, wout_ref, win_o_ref, wout_o_ref):
    for n in range(Z_COLS // LANES):
        r0 = LANES * n if n < 4 * D_GROUP // LANES else LANES * n + N_GATE
        win_o_ref[0, :, LANES * n:LANES * (n + 1)] = wt_ref[0, r0:r0 + LANES, :].T.astype(BF16)
    g = wt_ref[0, 4 * D_GROUP:4 * D_GROUP + LANES, :].T.astype(BF16)
    win_o_ref[0, :, C_GATE:W_COLS] = jnp.dot(g, sel_ref[...], preferred_element_type=F32).astype(BF16)
    wout_o_ref[...] = wout_ref[...].astype(BF16)


def _prep_weights(w_in, w_out):
    sel = np.zeros((LANES, LANES), np.float32)
    for j in range(N_GATE):
        sel[j, 8 * (j // HEADS) + j % HEADS] = 1.0
    rows = D_MODEL // 2
    return pl.pallas_call(
        _prep_kernel,
        out_shape=[jax.ShapeDtypeStruct((DEPTH, D_MODEL, W_COLS), BF16),
                   jax.ShapeDtypeStruct(w_out.shape, BF16)],
        grid=(DEPTH, D_MODEL // rows),
        in_specs=[pl.BlockSpec((1, D_IN, rows), lambda l, i: (l, 0, i)),
                  pl.BlockSpec((LANES, LANES), lambda l, i: (0, 0)),
                  pl.BlockSpec((1, rows, D_MODEL), lambda l, i: (l, i, 0))],
        out_specs=[pl.BlockSpec((1, rows, W_COLS), lambda l, i: (l, i, 0)),
                   pl.BlockSpec((1, rows, D_MODEL), lambda l, i: (l, i, 0))],
        compiler_params=pltpu.CompilerParams(dimension_semantics=("arbitrary", "arbitrary"),
                                             vmem_limit_bytes=VMEM_LIMIT),
        name="prep_mixer_w")(jnp.swapaxes(w_in, 1, 2), jnp.asarray(sel, BF16), w_out)


def _mod_kernel(ctx_ref, c_ref, w_ref, b_ref, o_ref):
    n_c = c_ref.shape[0]
    c = jnp.concatenate([ctx_ref[...], c_ref[...], jnp.zeros((7 - n_c, D_MODEL), F32)], axis=0)
    s = (c * _sigmoid(c)).astype(BF16)
    bias = b_ref[pl.ds(pl.program_id(0), 1), :]
    o_ref[0] = jnp.dot(s, w_ref[0].astype(BF16), preferred_element_type=F32) + bias


def _modulation(c_ctx, c, ada_w, ada_b):
    n_out = N_MOD * D_MODEL
    bn = 1536
    return pl.pallas_call(
        _mod_kernel,
        out_shape=jax.ShapeDtypeStruct((DEPTH, 8, n_out), F32),
        grid=(DEPTH, n_out // bn),
        in_specs=[
            pl.BlockSpec((1, D_MODEL), lambda l, j: (0, 0)),
            pl.BlockSpec(c.shape, lambda l, j: (0, 0)),
            pl.BlockSpec((1, D_MODEL, bn), lambda l, j: (l, 0, j)),
            pl.BlockSpec((DEPTH, bn), lambda l, j: (0, j)),
        ],
        out_specs=pl.BlockSpec((1, 8, bn), lambda l, j: (l, 0, j)),
        compiler_params=pltpu.CompilerParams(
            dimension_semantics=("arbitrary", "arbitrary"), vmem_limit_bytes=VMEM_LIMIT),
        name="adaln_mod",
    )(c_ctx.reshape(1, D_MODEL), c, ada_w, ada_b)


def _mod_rows(mod_ref, cond):
    row = mod_ref[0, pl.ds(cond, 1), :]
    return [row[:, i * D_MODEL:(i + 1) * D_MODEL] for i in range(N_MOD)]


def _gate_stats(gates):
    gt = gates.T
    b_f = _scan_lanes(_log_sigmoid(gt[8:16]), jnp.add, False, 0.0)
    g_f = gt[0:8] - b_f
    cm_f = _scan_lanes(g_f, jnp.maximum, False, -jnp.inf)
    b_b = _scan_lanes(_log_sigmoid(gt[24:32]), jnp.add, True, 0.0)
    g_b = gt[16:24] - b_b
    cm_b = _scan_lanes(g_b, jnp.maximum, True, -jnp.inf)
    stack = jnp.concatenate([b_f, cm_f, b_b, cm_b, jnp.zeros((LANES - 32, CHUNK), F32)], axis=0)
    return jnp.concatenate([g_f, g_b], axis=0), stack.T


def _pair_cols(mat, lane0, lane_lt_dh):
    return jnp.where(lane_lt_dh, mat[:, lane0:lane0 + 1], mat[:, lane0 + 1:lane0 + 2])


def _mlstm(z_ref, gcol_ref, grow_ref, cols_ref, bd_ref, ycat_ref, nc, state0, m0, need_final):
    lane_row = lax.broadcasted_iota(jnp.int32, (1, LANES), 1)
    lane = lax.broadcasted_iota(jnp.int32, (CHUNK, LANES), 1)
    lt_dh = lane < DH
    row_ll = lax.broadcasted_iota(jnp.int32, (CHUNK, CHUNK), 0)
    col_ll = lax.broadcasted_iota(jnp.int32, (CHUNK, CHUNK), 1)
    tri = (col_ll <= row_ll, col_ll >= row_ll)
    bd_row = lax.broadcasted_iota(jnp.int32, (LANES, 2 * LANES), 0)
    bd_col = lax.broadcasted_iota(jnp.int32, (LANES, 2 * LANES), 1)
    bd_mask = (bd_row < DH) == ((bd_col & (LANES - 1)) < DH)
    dirs = ((0, L_BF, L_MF, CHUNK - 1), (1, L_BB, L_MB, 0))

    m_start = [[None] * nc, [None] * nc]
    m_fin = [None, None]
    for d, _, _, last in dirs:
        m = m0
        order = range(nc) if d == 0 else range(nc - 1, -1, -1)
        for step, c in enumerate(order):
            m_start[d][c] = m
            if need_final or step < nc - 1:
                cl = cols_ref[c, last:last + 1, :]
                m = pltpu.roll(cl, 8, axis=1) + jnp.maximum(m, cl)
        m_fin[d] = m

    have_state = [[False] * nc, [False] * nc]
    fin = [None, None]
    for d, l_b, l_m, last in dirs:
        bd = list(state0[d])
        order = range(nc) if d == 0 else range(nc - 1, -1, -1)
        for step, c in enumerate(order):
            if bd[0] is not None:
                have_state[d][c] = True
                for p in range(PAIRS):
                    bd_ref[d, c, p] = bd[p].astype(BF16)
            if not (need_final or step < nc - 1):
                continue
            rows = slice(HALO + c * CHUNK, HALO + (c + 1) * CHUNK)
            cols = cols_ref[c]
            m_last = jnp.maximum(cols[last:last + 1, :], m_start[d][c])
            a_col = jnp.exp(gcol_ref[c] - cols - pltpu.roll(m_last, LANES - 8, axis=1))
            a_prev = jnp.exp(m_start[d][c] - m_last)
            for p in range(PAIRS):
                a_pair = _pair_cols(a_col, l_b + 2 * p, lt_dh)
                v_pair = z_ref[rows, C_V + LANES * p:C_V + LANES * (p + 1)]
                rhs = jnp.concatenate([v_pair * a_pair, a_pair], axis=1).astype(BF16)
                k_pair = (z_ref[rows, C_K + LANES * p:C_K + LANES * (p + 1)] * (DH ** -0.5)).astype(BF16)
                upd = lax.dot_general(k_pair, rhs, (((0,), (0,)), ((), ())),
                                      preferred_element_type=F32)
                upd = jnp.where(bd_mask, upd, 0.0)
                if bd[p] is None:
                    bd[p] = upd
                else:
                    scale = jnp.where(bd_row < DH, a_prev[:, l_m + 2 * p:l_m + 2 * p + 1],
                                      a_prev[:, l_m + 2 * p + 1:l_m + 2 * p + 2])
                    bd[p] = scale * bd[p] + upd
        fin[d] = bd

    for c in range(nc):
        rows = slice(HALO + c * CHUNK, HALO + (c + 1) * CHUNK)
        cols = cols_ref[c]
        grow = grow_ref[c]
        m_prev = jnp.where(lane_row < L_BB, m_start[0][c], m_start[1][c])
        m_all = jnp.maximum(cols, m_prev)
        w_inter = jnp.exp(m_prev - m_all)
        e_neg = jnp.exp(-(pltpu.roll(cols, 8, axis=1) + m_all))
        for p in range(PAIRS):
            q_pair = z_ref[rows, C_Q + LANES * p:C_Q + LANES * (p + 1)].astype(BF16)
            k_f32 = z_ref[rows, C_K + LANES * p:C_K + LANES * (p + 1)] * (DH ** -0.5)
            v_pair = z_ref[rows, C_V + LANES * p:C_V + LANES * (p + 1)]
            p_cat = ([], [])
            va = []
            for j in range(2):
                h = 2 * p + j
                own = lt_dh if j == 0 else jnp.logical_not(lt_dh)
                k_own = jnp.where(own, k_f32, 0.0).astype(BF16)
                s = lax.dot_general(q_pair, k_own, (((1,), (1,)), ((), ())),
                                    preferred_element_type=F32)
                va.append(jnp.concatenate([jnp.where(own, v_pair, 0.0), jnp.where(own, 1.0, 0.0)],
                                          axis=1).astype(BF16))
                for d, _, l_m, _ in dirs:
                    w = jnp.exp(jnp.where(tri[d], grow[8 * d + h:8 * d + h + 1] - m_all[:, l_m + h:l_m + h + 1],
                                          -jnp.inf))
                    p_cat[d].append((s * w).astype(BF16))
            va = jnp.concatenate(va, axis=0)
            h_sum = None
            for d, _, l_m, _ in dirs:
                tot = jnp.dot(jnp.concatenate(p_cat[d], axis=1), va, preferred_element_type=F32)
                if have_state[d][c]:
                    e_pair = _pair_cols(w_inter, l_m + 2 * p, lt_dh)
                    qc = jnp.dot(q_pair, bd_ref[d, c, p], preferred_element_type=F32)
                    tot = tot + jnp.concatenate([e_pair, e_pair], axis=1) * qc
                floor = _pair_cols(e_neg, l_m + 2 * p, lt_dh)
                h_dir = tot[:, :LANES] / jnp.maximum(jnp.abs(tot[:, LANES:]), floor)
                h_sum = h_dir if h_sum is None else h_sum + h_dir
            o_pair = z_ref[rows, C_O + LANES * p:C_O + LANES * (p + 1)]
            ycat_ref[c * CHUNK:(c + 1) * CHUNK, LANES * p:LANES * (p + 1)] = (
                _sigmoid(o_pair) * h_sum).astype(BF16)

    m_row = jnp.where(lane_row < L_BB, m_fin[0], m_fin[1])
    return fin, m_row


def _pair_state(c_ref, n_ref, q, d, p):
    eye = (lax.broadcasted_iota(jnp.int32, (DH, DH), 0) == lax.broadcasted_iota(jnp.int32, (DH, DH), 1))
    zero = jnp.zeros((DH, DH), F32)
    blocks = []
    for j in range(2):
        h = 2 * p + j
        n_row = n_ref[q, 0, d, h:h + 1, :]
        n_col = jnp.sum(jnp.where(eye, n_row, 0.0), axis=1, keepdims=True)
        row = [zero] * 4
        row[j] = c_ref[q, 0, d, h]
        row[2 + j] = jnp.broadcast_to(n_col, (DH, DH))
        blocks.append(jnp.concatenate(row, axis=1))
    return jnp.concatenate(blocks, axis=0)


def _mixer_kernel(*refs, layer, seq_len, nseq, add_pos, has_state, emit_state, cond_base, n_prev):
    it = iter(refs)
    x_ref = next(it)
    if add_pos:
        er_ref = next(it)
        ec_ref = next(it)
    mod_ref = next(it)
    ng_ref = next(it)
    win_ref = next(it)
    gb_ref = next(it)
    ws_ref = next(it)
    gbias_ref = next(it)
    pw_ref = next(it)
    psc_ref = next(it)
    cw_ref = next(it)
    wout_ref = next(it)
    if has_state:
        c0_ref = next(it)
        n0_ref = next(it)
        m0_ref = next(it)
    if n_prev:
        cprev_ref = next(it)
        nprev_ref = next(it)
    x1_ref = next(it)
    h2_ref = next(it)
    if emit_state:
        cout_ref = next(it)
        nout_ref = next(it)
        mout_ref = next(it)
    z_all = next(it)
    gcol_all = next(it)
    grow_all = next(it)
    cols_all = next(it)
    bd_all = next(it)
    ycat_all = next(it)

    nblk = seq_len // CHUNK
    cond = cond_base if cond_base == 0 else cond_base + pl.program_id(0)
    sh1, sc1, gt1, sh2, sc2, _ = _mod_rows(mod_ref, cond)
    ng = ng_ref[0]
    seqs = range(nseq)
    units = [(q, r) for q in seqs for r in range(nblk)]

    zeros_halo = jnp.zeros((HALO, Z_COLS - C_PZ), F32)
    for q in seqs:
        z_all[q, 0:HALO, C_PZ:Z_COLS] = zeros_halo
        z_all[q, HALO + seq_len:2 * HALO + seq_len, C_PZ:Z_COLS] = zeros_halo

    g_in = ng[0:1] * (1.0 + sc1)
    hbs = []
    for q, r in units:
        rows = slice(r * CHUNK, (r + 1) * CHUNK)
        xb = x_ref[q, rows, :]
        if add_pos:
            pieces = []
            for g in range(CHUNK // GRID_W):
                gi = r * (CHUNK // GRID_W) + g
                er = jnp.broadcast_to(er_ref[gi:gi + 1, :], (GRID_W, D_MODEL // 2))
                pieces.append(jnp.concatenate([er, ec_ref[...]], axis=1))
            xb = xb + jnp.concatenate(pieces, axis=0)
            x1_ref[q, rows, :] = xb
        hbs.append((_rms(xb, g_in) + sh1).astype(BF16))
    hb = jnp.concatenate(hbs, axis=0)
    gates_all = jnp.dot(hb, win_ref[0, :, C_GATE:W_COLS], preferred_element_type=F32) + gb_ref[0]
    for cg in range(Z_COLS // Z_GROUP):
        zc = jnp.dot(hb, win_ref[0, :, cg * Z_GROUP:(cg + 1) * Z_GROUP], preferred_element_type=F32)
        for u, (q, r) in enumerate(units):
            z_all[q, HALO + r * CHUNK:HALO + (r + 1) * CHUNK, cg * Z_GROUP:(cg + 1) * Z_GROUP] = (
                zc[u * CHUNK:(u + 1) * CHUNK])
    for u, (q, r) in enumerate(units):
        gates = gates_all[u * CHUNK:(u + 1) * CHUNK]
        gcol_all[q, r] = gates
        grow, cols = _gate_stats(gates)
        grow_all[q, r] = grow
        cols_all[q, r] = cols

    for q in seqs:
        if has_state:
            state0 = [[_pair_state(c0_ref, n0_ref, q, d, p) for p in range(PAIRS)] for d in range(2)]
            m0 = m0_ref[q, 0]
        else:
            state0 = [[None] * PAIRS, [None] * PAIRS]
            m0 = jnp.zeros((1, LANES), F32)
        fin, m_row = _mlstm(z_all.at[q], gcol_all.at[q], grow_all.at[q], cols_all.at[q], bd_all.at[q],
                            ycat_all.at[q], nblk, state0, m0, need_final=emit_state)
        if emit_state:
            for d in range(2):
                for p in range(PAIRS):
                    for j in range(2):
                        cout_ref[q, n_prev, d, 2 * p + j] = fin[d][p][DH * j:DH * (j + 1), DH * j:DH * (j + 1)]
                    n_t = fin[d][p][:, LANES:].T
                    nout_ref[q, n_prev, d, 2 * p:2 * p + 1, :] = n_t[0:1, 0:DH]
                    nout_ref[q, n_prev, d, 2 * p + 1:2 * p + 2, :] = n_t[DH:DH + 1, DH:LANES]
            mout_ref[q] = m_row
    if emit_state and n_prev:
        cout_ref[:, 0:n_prev] = cprev_ref[...]
        nout_ref[:, 0:n_prev] = nprev_ref[...]

    lane_g = lax.broadcasted_iota(jnp.int32, (CHUNK, D_GROUP), 1) // (D_GROUP // 4)
    lg = lax.broadcasted_iota(jnp.int32, (GMLP_CHUNK, D_GROUP), 1) // (D_GROUP // 4)
    row_c = lax.broadcasted_iota(jnp.int32, (CHUNK, D_GROUP), 0)
    half = jnp.where(lane_g == 0, 1, jnp.where(lane_g == 1, 2, jnp.where(lane_g == 2, 4, 8)))
    cw = cw_ref[0]
    ws_cat = jnp.concatenate([ws_ref[0, g] for g in range(4)], axis=1).astype(BF16)
    dg = D_GROUP // 4
    pool_zero = jnp.zeros((dg, dg), F32)
    pool_bd = jnp.concatenate(
        [jnp.concatenate([pw_ref[0, g] if h == g else pool_zero for h in range(4)], axis=1)
         for g in range(4)], axis=0).astype(BF16)
    for q, r in units:
        z_ref = z_all.at[q]
        ycat_ref = ycat_all.at[q]
        rows = slice(r * CHUNK, (r + 1) * CHUNK)
        zrows = slice(HALO + r * CHUNK, HALO + (r + 1) * CHUNK)
        hrows = slice(r * CHUNK, (r + 1) * CHUNK + 2 * HALO)
        inner = slice(HALO, HALO + CHUNK)

        for s in range(CHUNK // GMLP_CHUNK):
            zs = slice(HALO + r * CHUNK + s * GMLP_CHUNK, HALO + r * CHUNK + (s + 1) * GMLP_CHUNK)
            vch = z_ref[zs, C_GV:C_GV + D_GROUP]
            v_stack = jnp.concatenate([jnp.where(lg == g, vch, 0.0) for g in range(4)], axis=0)
            mixed = gbias_ref[0] + jnp.dot(ws_cat, v_stack.astype(BF16), preferred_element_type=F32)
            yb = z_ref[zs, C_GU:C_GU + D_GROUP] * mixed
            ys = slice(r * CHUNK + s * GMLP_CHUNK, r * CHUNK + (s + 1) * GMLP_CHUNK)
            ycat_ref[ys, D_GROUP:2 * D_GROUP] = yb.astype(BF16)

        pz = z_ref[hrows, C_PZ:C_PZ + D_GROUP]
        a2 = pz + _shift_down(pz, 1)
        a4 = a2 + _shift_down(a2, 2)
        a8 = a4 + _shift_down(a4, 4)
        a16 = a8 + _shift_down(a8, 8)
        win = [a2[inner], _shift_up(a4, 1)[inner], _shift_up(a8, 3)[inner], _shift_up(a16, 7)[inner]]
        wsum = jnp.where(lane_g == 0, win[0], jnp.where(lane_g == 1, win[1],
                         jnp.where(lane_g == 2, win[2], win[3])))
        t_abs = row_c + r * CHUNK
        cnt = (jnp.minimum(t_abs + half, seq_len) - jnp.maximum(t_abs - half, 0)).astype(F32)
        pooled = wsum / cnt - pz[inner]
        yc = (jnp.dot(pooled.astype(BF16), pool_bd, preferred_element_type=F32)
              * psc_ref[layer:layer + 1, :])
        ycat_ref[rows, 2 * D_GROUP:3 * D_GROUP] = yc.astype(BF16)

        u = z_ref[hrows, C_CC:C_CC + D_GROUP] * z_ref[hrows, C_CX:C_CX + D_GROUP]
        conv = (cw[0:1] * _shift_down(u, 1)[inner] + cw[1:2] * u[inner]
                + cw[2:3] * _shift_up(u, 1)[inner])
        yd = z_ref[zrows, C_CB:C_CB + D_GROUP] * conv
        ycat_ref[rows, 3 * D_GROUP:4 * D_GROUP] = yd.astype(BF16)

    y_all = jnp.dot(jnp.concatenate([ycat_all[q, r * CHUNK:(r + 1) * CHUNK, :] for q, r in units], axis=0),
                    wout_ref[0], preferred_element_type=F32)
    g_y = gt1 * ng[1:2]
    g_h2 = ng[2:3] * (1.0 + sc2)
    for u, (q, r) in enumerate(units):
        rows = slice(r * CHUNK, (r + 1) * CHUNK)
        xb = x1_ref[q, rows, :] if add_pos else x_ref[q, rows, :]
        x1 = xb + _rms(y_all[u * CHUNK:(u + 1) * CHUNK], g_y)
        x1_ref[q, rows, :] = x1
        h2_ref[q, rows, :] = (_rms(x1, g_h2) + sh2).astype(BF16)


def _layer_spec(shape, layer):
    nd = len(shape)
    return pl.BlockSpec((1,) + tuple(shape[1:]), lambda b: (layer,) + (0,) * (nd - 1),
                        pipeline_mode=pl.Buffered(1))


def _mixer_call(x, layer, mod_all, wts, cond_base, pos=None, state=None, emit_state=False,
                prev_states=None, nseq=1):
    bsz, seq_len, _ = x.shape
    nblk = seq_len // CHUNK
    add_pos = pos is not None
    has_state = state is not None
    n_prev = 0 if prev_states is None else prev_states[0].shape[1]
    args = [x]
    in_specs = [pl.BlockSpec((nseq, seq_len, D_MODEL), lambda b: (b, 0, 0))]
    if add_pos:
        args += list(pos)
        in_specs += [pl.BlockSpec(p.shape, lambda b: (0, 0), pipeline_mode=pl.Buffered(1)) for p in pos]
    args.append(mod_all)
    in_specs.append(_layer_spec(mod_all.shape, layer))
    for name in ("norm_g", "w_in", "gate_b", "gmlp_ws", "gmlp_bias", "pool_w", "pool_scale",
                 "conv_w", "w_out"):
        args.append(wts[name])
        if wts[name].ndim == 2:
            in_specs.append(pl.BlockSpec(wts[name].shape, lambda b: (0, 0), pipeline_mode=pl.Buffered(1)))
        else:
            in_specs.append(_layer_spec(wts[name].shape, layer))
    if has_state:
        args += list(state)
        in_specs += [pl.BlockSpec((nseq, 1, 2, HEADS, DH, DH), lambda b: (b, layer, 0, 0, 0, 0)),
                     pl.BlockSpec((nseq, 1, 2, HEADS, DH), lambda b: (b, layer, 0, 0, 0)),
                     pl.BlockSpec((nseq, 1, 1, LANES), lambda b: (b, layer, 0, 0))]
    out_shape = [jax.ShapeDtypeStruct((bsz, seq_len, D_MODEL), F32),
                 jax.ShapeDtypeStruct((bsz, seq_len, D_MODEL), BF16)]
    out_specs = [pl.BlockSpec((nseq, seq_len, D_MODEL), lambda b: (b, 0, 0)),
                 pl.BlockSpec((nseq, seq_len, D_MODEL), lambda b: (b, 0, 0))]
    if emit_state:
        if n_prev:
            args += list(prev_states)
            in_specs += [pl.BlockSpec((nseq, n_prev, 2, HEADS, DH, DH), lambda b: (b, 0, 0, 0, 0, 0)),
                         pl.BlockSpec((nseq, n_prev, 2, HEADS, DH), lambda b: (b, 0, 0, 0, 0))]
        out_shape += [jax.ShapeDtypeStruct((bsz, n_prev + 1, 2, HEADS, DH, DH), F32),
                      jax.ShapeDtypeStruct((bsz, n_prev + 1, 2, HEADS, DH), F32),
                      jax.ShapeDtypeStruct((bsz, 1, LANES), F32)]
        out_specs += [pl.BlockSpec((nseq, n_prev + 1, 2, HEADS, DH, DH), lambda b: (b, 0, 0, 0, 0, 0)),
                      pl.BlockSpec((nseq, n_prev + 1, 2, HEADS, DH), lambda b: (b, 0, 0, 0, 0)),
                      pl.BlockSpec((nseq, 1, LANES), lambda b: (b, 0, 0))]
    kern = functools.partial(_mixer_kernel, layer=layer, seq_len=seq_len, nseq=nseq, add_pos=add_pos,
                             has_state=has_state, emit_state=emit_state, cond_base=cond_base,
                             n_prev=n_prev)
    return pl.pallas_call(
        kern,
        out_shape=out_shape,
        grid=(bsz // nseq,),
        in_specs=in_specs,
        out_specs=out_specs,
        scratch_shapes=[pltpu.VMEM((nseq, seq_len + 2 * HALO, Z_COLS), F32),
                        pltpu.VMEM((nseq, nblk, CHUNK, LANES), F32),
                        pltpu.VMEM((nseq, nblk, 16, CHUNK), F32),
                        pltpu.VMEM((nseq, nblk, CHUNK, LANES), F32),
                        pltpu.VMEM((nseq, 2, nblk, PAIRS, LANES, 2 * LANES), BF16),
                        pltpu.VMEM((nseq, seq_len, D_MODEL), BF16)],
        compiler_params=pltpu.CompilerParams(
            dimension_semantics=("arbitrary",), vmem_limit_bytes=VMEM_LIMIT),
        name="mixer_t%d" % seq_len,
    )(*args)


def _ffn_kernel(mod_ref, ng_ref, cw_ref, x1p_hbm, h2p_hbm, x1s_hbm, h2s_hbm, up_hbm, down_hbm,
                outp_hbm, outs_hbm,
                up_bf, down_bf, act_ref, h2_buf, x1_buf, o_buf, up_stage, down_stage,
                in_sems, out_sems, w_sems, *, layer, tiles_p, tiles_s, seq_p, seq_s):
    i = pl.program_id(0)
    n_tiles = tiles_p + tiles_s
    rows = FFN_ROWS
    half = FFN_SUB // 2
    nsub = D_FF // half
    slot = i % 2
    is_prompt = i < tiles_p

    def tile_rows(tile, first_tile):
        if isinstance(tile, int):
            return pl.ds((tile - first_tile) * rows, rows)
        return pl.ds(pl.multiple_of((tile - first_tile) * rows, rows), rows)

    def h2_copy(h2_hbm, r, buf):
        return pltpu.make_async_copy(h2_hbm.at[r], h2_buf.at[buf], in_sems.at[buf])

    def x1_copy(x1_hbm, r):
        return pltpu.make_async_copy(x1_hbm.at[r], x1_buf, in_sems.at[2])

    def fetch_h2(tile, buf):
        @pl.when(tile < tiles_p)
        def _():
            h2_copy(h2p_hbm, tile_rows(tile, 0), buf).start()

        @pl.when(tile >= tiles_p)
        def _():
            h2_copy(h2s_hbm, tile_rows(tile, tiles_p), buf).start()

    def fetch_x1(tile):
        @pl.when(tile < tiles_p)
        def _():
            x1_copy(x1p_hbm, tile_rows(tile, 0)).start()

        @pl.when(tile >= tiles_p)
        def _():
            x1_copy(x1s_hbm, tile_rows(tile, tiles_p)).start()

    def await_store(buf):
        pltpu.make_async_copy(o_buf.at[buf], outp_hbm.at[pl.ds(0, rows)], out_sems.at[buf]).wait()

    def weight_copies(c, buf):
        aligned = (lambda v: v) if isinstance(c, int) else (lambda v: pl.multiple_of(v, half))
        cv = pl.ds(aligned(c * half), half)
        cg = pl.ds(aligned(D_FF + c * half), half)
        return (pltpu.make_async_copy(up_hbm.at[layer, :, cv], up_stage.at[buf, 0], w_sems.at[buf, 0]),
                pltpu.make_async_copy(up_hbm.at[layer, :, cg], up_stage.at[buf, 1], w_sems.at[buf, 1]),
                pltpu.make_async_copy(down_hbm.at[layer, cv, :], down_stage.at[buf], w_sems.at[buf, 2]))

    @pl.when(i == 0)
    def _():
        h2_copy(h2p_hbm, tile_rows(0, 0), 0).start()
        depth = up_stage.shape[0]
        for c0 in range(depth - 1):
            for cp in weight_copies(c0, c0):
                cp.start()

        def convert(c, carry):
            buf = c % depth
            ahead = c + depth - 1

            @pl.when(ahead < nsub)
            def _():
                for cp in weight_copies(ahead, ahead % depth):
                    cp.start()

            for cp in weight_copies(c, buf):
                cp.wait()
            up_bf[c, :, 0:half] = up_stage[buf, 0].astype(BF16)
            up_bf[c, :, half:FFN_SUB] = up_stage[buf, 1].astype(BF16)
            down_bf[pl.ds(pl.multiple_of(c * half, half), half), :] = down_stage[buf].astype(BF16)
            return carry

        lax.fori_loop(0, nsub, convert, 0)

    h2_copy(h2p_hbm, pl.ds(0, rows), slot).wait()

    @pl.when(i >= 2)
    def _():
        await_store(slot)

    fetch_x1(i)

    @pl.when(i + 1 < n_tiles)
    def _():
        fetch_h2(i + 1, 1 - slot)

    row = lax.broadcasted_iota(jnp.int32, (rows, FFN_SUB), 0)
    first = ((row & (seq_s - 1)) == 0) | (((row & (seq_p - 1)) == 0) & is_prompt)
    last = ((row & (seq_s - 1)) == seq_s - 1) | (((row & (seq_p - 1)) == seq_p - 1) & is_prompt)
    h2 = h2_buf[slot]
    for s in range(nsub):
        a = jnp.dot(h2, up_bf[s], preferred_element_type=F32)
        cw = jnp.concatenate([cw_ref[0, :, s * half:(s + 1) * half],
                              0.5 * cw_ref[0, :, D_FF + s * half:D_FF + (s + 1) * half]], axis=1)
        a_prev = jnp.where(first, 0.0, pltpu.roll(a, 1, axis=0))
        a_next = jnp.where(last, 0.0, pltpu.roll(a, rows - 1, axis=0))
        ac = cw[0:1] * a_prev + cw[1:2] * a + cw[2:3] * a_next
        hg = ac[:, half:]
        act = (hg * (1.0 + jnp.tanh(hg))) * ac[:, :half]
        act_ref[:, s * half:(s + 1) * half] = act.astype(BF16)
    y = jnp.dot(act_ref[...], down_bf[...], preferred_element_type=F32)
    cond = jnp.where(is_prompt, 0, i - tiles_p + 1)
    gt2 = _mod_rows(mod_ref, cond)[5]

    res = _rms(y, gt2 * ng_ref[0][3:4])
    x1_copy(x1p_hbm, pl.ds(0, rows)).wait()
    o_buf[slot] = x1_buf[...] + res

    @pl.when(is_prompt)
    def _():
        pltpu.make_async_copy(o_buf.at[slot], outp_hbm.at[tile_rows(i, 0)], out_sems.at[slot]).start()

    @pl.when(jnp.logical_not(is_prompt))
    def _():
        pltpu.make_async_copy(o_buf.at[slot], outs_hbm.at[tile_rows(i, tiles_p)], out_sems.at[slot]).start()

    @pl.when(i == n_tiles - 1)
    def _():
        await_store(1 - slot)
        await_store(slot)


def _ffn_call(x1p, h2p, x1s, h2s, layer, mod_all, wts, seq_p, seq_s):
    tiles_p = x1p.shape[0] // FFN_ROWS
    tiles_s = x1s.shape[0] // FFN_ROWS
    assert tiles_p >= 1 and tiles_s >= 1 and tiles_p + tiles_s >= 2
    half = FFN_SUB // 2

    def resident(shape):
        nd = len(shape)
        return pl.BlockSpec((1,) + tuple(shape[1:]), lambda i: (layer,) + (0,) * (nd - 1),
                            pipeline_mode=pl.Buffered(1))

    hbm = pl.BlockSpec(memory_space=pl.ANY)
    return pl.pallas_call(
        functools.partial(_ffn_kernel, layer=layer, tiles_p=tiles_p, tiles_s=tiles_s,
                          seq_p=seq_p, seq_s=seq_s),
        out_shape=[jax.ShapeDtypeStruct(x1p.shape, F32), jax.ShapeDtypeStruct(x1s.shape, F32)],
        grid=(tiles_p + tiles_s,),
        in_specs=[resident(mod_all.shape), resident(wts["norm_g"].shape), resident(wts["ffn_conv"].shape),
                  hbm, hbm, hbm, hbm, hbm, hbm],
        out_specs=[hbm, hbm],
        scratch_shapes=[pltpu.VMEM((D_FF // half, D_MODEL, FFN_SUB), BF16),
                        pltpu.VMEM((D_FF, D_MODEL), BF16),
                        pltpu.VMEM((FFN_ROWS, D_FF), BF16),
                        pltpu.VMEM((2, FFN_ROWS, D_MODEL), BF16),
                        pltpu.VMEM((FFN_ROWS, D_MODEL), F32),
                        pltpu.VMEM((2, FFN_ROWS, D_MODEL), F32),
                        pltpu.VMEM((FFN_WEIGHT_BUFS, 2, D_MODEL, half), F32),
                        pltpu.VMEM((FFN_WEIGHT_BUFS, half, D_MODEL), F32),
                        pltpu.SemaphoreType.DMA((3,)),
                        pltpu.SemaphoreType.DMA((2,)),
                        pltpu.SemaphoreType.DMA((FFN_WEIGHT_BUFS, 3))],
        compiler_params=pltpu.CompilerParams(
            dimension_semantics=("arbitrary",), vmem_limit_bytes=VMEM_LIMIT),
        name="ffn",
    )(mod_all, wts["norm_g"], wts["ffn_conv"], x1p, h2p, x1s, h2s, wts["ffn_up"], wts["ffn_down"])


def _pos_tables(t_len):
    quarter = D_MODEL // 4
    omega = (1.0 / (np.float32(POS_BASE) ** (np.arange(quarter, dtype=np.float32) / np.float32(quarter))))
    omega = omega.astype(np.float32)

    def emb(p):
        a = p.astype(np.float32)[:, None] * omega[None, :]
        return np.concatenate([np.sin(a), np.cos(a)], axis=-1).astype(np.float32)

    return jnp.asarray(emb(np.arange(t_len // GRID_W))), jnp.asarray(emb(np.arange(GRID_W)))


def _stabiliser_rows(state_m):
    bs = state_m.shape[0]
    m = jnp.pad(state_m.astype(F32), ((0, 0), (0, 0), (0, 0), (0, 16 - HEADS)))
    return jnp.pad(m.reshape(bs, DEPTH, 1, 32), ((0, 0), (0, 0), (0, 0), (L_MF, LANES - 32 - L_MF)))


def kernel(x_prompt, x_sample, state_C, state_n, state_m, c, c_ctx, norm_g, ada_w, ada_b, w_in,
           gate_b, gmlp_ws, gmlp_b, pool_w, pool_scale, conv_w, w_out, ffn_up, ffn_conv, ffn_down):
    bp, tp, _ = x_prompt.shape
    bs, ts, _ = x_sample.shape
    assert tp == CHUNK and ts % CHUNK == 0 and FFN_ROWS % tp == 0 and ts == FFN_ROWS

    mod_all = _modulation(c_ctx, c, ada_w, ada_b)
    win_p, wout_p = _prep_weights(w_in, w_out)

    gb = jnp.pad(gate_b, ((0, 0), (0, 0), (0, 8 - HEADS))).reshape(DEPTH, 1, 32)
    dg = D_GROUP // 4
    wts = {
        "norm_g": norm_g,
        "w_in": win_p,
        "gate_b": jnp.pad(gb, ((0, 0), (0, 0), (0, LANES - 32))),
        "gmlp_ws": gmlp_ws,
        "gmlp_bias": jnp.repeat(jnp.swapaxes(gmlp_b, 1, 2), dg, axis=2),
        "pool_w": pool_w,
        "pool_scale": pool_scale,
        "conv_w": conv_w,
        "w_out": wout_p,
        "ffn_up": ffn_up,
        "ffn_conv": ffn_conv,
        "ffn_down": ffn_down,
    }
    pos = _pos_tables(ts)
    state = (state_C.astype(F32), state_n.astype(F32), _stabiliser_rows(state_m))

    xp, xs = x_prompt, x_sample
    prev, ms = None, []
    for l in range(DEPTH):
        x1p, h2p, c_all, n_all, m_fin = _mixer_call(xp, l, mod_all, wts, 0, emit_state=True,
                                                    prev_states=prev, nseq=MIXER_ROWS // tp)
        prev = (c_all, n_all)
        ms.append(jnp.stack([m_fin[:, 0, L_MF:L_MF + HEADS], m_fin[:, 0, L_MB:L_MB + HEADS]], axis=1))

        x1s, h2s = _mixer_call(xs, l, mod_all, wts, 1, pos=pos if l == 0 else None, state=state)

        xp, xs = _ffn_call(x1p.reshape(bp * tp, D_MODEL), h2p.reshape(bp * tp, D_MODEL),
                           x1s.reshape(bs * ts, D_MODEL), h2s.reshape(bs * ts, D_MODEL),
                           l, mod_all, wts, tp, ts)
        xp = xp.reshape(bp, tp, D_MODEL)
        xs = xs.reshape(bs, ts, D_MODEL)

    return (xp, xs, c_all, n_all, jnp.stack(ms, axis=1))
```

```python
import functools

import numpy as np
import jax
import jax.numpy as jnp
from jax import lax
from jax.experimental import pallas as pl
from jax.experimental.pallas import tpu as pltpu

F32 = jnp.float32
BF16 = jnp.bfloat16

D_MODEL = 1024
DEPTH = 2
HEADS = 4
PAIRS = HEADS // 2
DH = 64
D_GROUP = 256
D_FF = 2816
N_MOD = 6
EPS = 1e-6
GRID_W = 64
POS_BASE = 10000.0
D_IN = 2576
N_GATE = 4 * HEADS

CHUNK = 256
GMLP_CHUNK = 128
HALO = 8
LANES = 128

C_Q, C_K, C_V, C_O = 0, 256, 512, 768
C_GU, C_GV, C_PZ, C_CB, C_CC, C_CX = 1024, 1280, 1536, 1792, 2048, 2304
Z_COLS = 2560
C_GATE = Z_COLS
W_COLS = Z_COLS + LANES
Z_GROUP = 512
L_BF, L_MF, L_BB, L_MB = 0, 8, 16, 24

MIXER_ROWS = 1024
FFN_ROWS = 1024
FFN_SUB = 256
FFN_WEIGHT_BUFS = 8
VMEM_LIMIT = 60 * 1024 * 1024


def _rms(x, g):
    ms = jnp.mean(x * x, axis=-1, keepdims=True)
    return x * lax.rsqrt(ms + EPS) * g


def _sigmoid(x):
    return 1.0 / (1.0 + jnp.exp(-x))


def _log_sigmoid(x):
    return jnp.minimum(x, 0.0) - jnp.log1p(jnp.exp(-jnp.abs(x)))


def _scan_lanes(x, op, reverse, fill):
    n = x.shape[1]
    lane = lax.broadcasted_iota(jnp.int32, x.shape, 1)
    k = 1
    while k < n:
        if reverse:
            sh = jnp.where(lane < n - k, pltpu.roll(x, n - k, axis=1), fill)
        else:
            sh = jnp.where(lane >= k, pltpu.roll(x, k, axis=1), fill)
        x = op(x, sh)
        k *= 2
    return x


def _shift_down(x, k):
    return pltpu.roll(x, k, axis=0)


def _shift_up(x, k):
    return pltpu.roll(x, x.shape[0] - k, axis=0)


def _prep_kernel(wt_ref, sel_ref, wout_ref, win_o_ref, wout_o_ref):
    for n in range(Z_COLS // LANES):
        r0 = LANES * n if n < 4 * D_GROUP // LANES else LANES * n + N_GATE
        win_o_ref[0, :, LANES * n:LANES * (n + 1)] = wt_ref[0, r0:r0 + LANES, :].T.astype(BF16)
    g = wt_ref[0, 4 * D_GROUP:4 * D_GROUP + LANES, :].T.astype(BF16)
    win_o_ref[0, :, C_GATE:W_COLS] = jnp.dot(g, sel_ref[...], preferred_element_type=F32).astype(BF16)
    wout_o_ref[...] = wout_ref[...].astype(BF16)


def _prep_weights(w_in, w_out):
    sel = np.zeros((LANES, LANES), np.float32)
    for j in range(N_GATE):
        sel[j, 8 * (j // HEADS) + j % HEADS] = 1.0
    rows = D_MODEL // 2
    return pl.pallas_call(
        _prep_kernel,
        out_shape=[jax.ShapeDtypeStruct((DEPTH, D_MODEL, W_COLS), BF16),
                   jax.ShapeDtypeStruct(w_out.shape, BF16)],
        grid=(DEPTH, D_MODEL // rows),
        in_specs=[pl.BlockSpec((1, D_IN, rows), lambda l, i: (l, 0, i)),
                  pl.BlockSpec((LANES, LANES), lambda l, i: (0, 0)),
                  pl.BlockSpec((1, rows, D_MODEL), lambda l, i: (l, i, 0))],
        out_specs=[pl.BlockSpec((1, rows, W_COLS), lambda l, i: (l, i, 0)),
                   pl.BlockSpec((1, rows, D_MODEL), lambda l, i: (l, i, 0))],
        compiler_params=pltpu.CompilerParams(dimension_semantics=("arbitrary", "arbitrary"),
                                             vmem_limit_bytes=VMEM_LIMIT),
        name="prep_mixer_w")(jnp.swapaxes(w_in, 1, 2), jnp.asarray(sel, BF16), w_out)


def _mod_kernel(ctx_ref, c_ref, w_ref, b_ref, o_ref):
    n_c = c_ref.shape[0]
    c = jnp.concatenate([ctx_ref[...], c_ref[...], jnp.zeros((7 - n_c, D_MODEL), F32)], axis=0)
    s = (c * _sigmoid(c)).astype(BF16)
    bias = b_ref[pl.ds(pl.program_id(0), 1), :]
    o_ref[0] = jnp.dot(s, w_ref[0].astype(BF16), preferred_element_type=F32) + bias


def _modulation(c_ctx, c, ada_w, ada_b):
    n_out = N_MOD * D_MODEL
    bn = 1536
    return pl.pallas_call(
        _mod_kernel,
        out_shape=jax.ShapeDtypeStruct((DEPTH, 8, n_out), F32),
        grid=(DEPTH, n_out // bn),
        in_specs=[
            pl.BlockSpec((1, D_MODEL), lambda l, j: (0, 0)),
            pl.BlockSpec(c.shape, lambda l, j: (0, 0)),
            pl.BlockSpec((1, D_MODEL, bn), lambda l, j: (l, 0, j)),
            pl.BlockSpec((DEPTH, bn), lambda l, j: (0, j)),
        ],
        out_specs=pl.BlockSpec((1, 8, bn), lambda l, j: (l, 0, j)),
        compiler_params=pltpu.CompilerParams(
            dimension_semantics=("arbitrary", "arbitrary"), vmem_limit_bytes=VMEM_LIMIT),
        name="adaln_mod",
    )(c_ctx.reshape(1, D_MODEL), c, ada_w, ada_b)


def _mod_rows(mod_ref, cond):
    row = mod_ref[0, pl.ds(cond, 1), :]
    return [row[:, i * D_MODEL:(i + 1) * D_MODEL] for i in range(N_MOD)]


def _gate_stats(gates):
    gt = gates.T
    b_f = _scan_lanes(_log_sigmoid(gt[8:16]), jnp.add, False, 0.0)
    g_f = gt[0:8] - b_f
    cm_f = _scan_lanes(g_f, jnp.maximum, False, -jnp.inf)
    b_b = _scan_lanes(_log_sigmoid(gt[24:32]), jnp.add, True, 0.0)
    g_b = gt[16:24] - b_b
    cm_b = _scan_lanes(g_b, jnp.maximum, True, -jnp.inf)
    stack = jnp.concatenate([b_f, cm_f, b_b, cm_b, jnp.zeros((LANES - 32, CHUNK), F32)], axis=0)
    return jnp.concatenate([g_f, g_b], axis=0), stack.T


def _pair_cols(mat, lane0, lane_lt_dh):
    return jnp.where(lane_lt_dh, mat[:, lane0:lane0 + 1], mat[:, lane0 + 1:lane0 + 2])


def _mlstm(z_ref, gcol_ref, grow_ref, cols_ref, bd_ref, ycat_ref, nc, state0, m0, need_final):
    lane_row = lax.broadcasted_iota(jnp.int32, (1, LANES), 1)
    lane = lax.broadcasted_iota(jnp.int32, (CHUNK, LANES), 1)
    lt_dh = lane < DH
    row_ll = lax.broadcasted_iota(jnp.int32, (CHUNK, CHUNK), 0)
    col_ll = lax.broadcasted_iota(jnp.int32, (CHUNK, CHUNK), 1)
    tri = (col_ll <= row_ll, col_ll >= row_ll)
    bd_row = lax.broadcasted_iota(jnp.int32, (LANES, 2 * LANES), 0)
    bd_col = lax.broadcasted_iota(jnp.int32, (LANES, 2 * LANES), 1)
    bd_mask = (bd_row < DH) == ((bd_col & (LANES - 1)) < DH)
    dirs = ((0, L_BF, L_MF, CHUNK - 1), (1, L_BB, L_MB, 0))

    m_start = [[None] * nc, [None] * nc]
    m_fin = [None, None]
    for d, _, _, last in dirs:
        m = m0
        order = range(nc) if d == 0 else range(nc - 1, -1, -1)
        for step, c in enumerate(order):
            m_start[d][c] = m
            if need_final or step < nc - 1:
                cl = cols_ref[c, last:last + 1, :]
                m = pltpu.roll(cl, 8, axis=1) + jnp.maximum(m, cl)
        m_fin[d] = m

    have_state = [[False] * nc, [False] * nc]
    fin = [None, None]
    for d, l_b, l_m, last in dirs:
        bd = list(state0[d])
        order = range(nc) if d == 0 else range(nc - 1, -1, -1)
        for step, c in enumerate(order):
            if bd[0] is not None:
                have_state[d][c] = True
                for p in range(PAIRS):
                    bd_ref[d, c, p] = bd[p].astype(BF16)
            if not (need_final or step < nc - 1):
                continue
            rows = slice(HALO + c * CHUNK, HALO + (c + 1) * CHUNK)
            cols = cols_ref[c]
            m_last = jnp.maximum(cols[last:last + 1, :], m_start[d][c])
            a_col = jnp.exp(gcol_ref[c] - cols - pltpu.roll(m_last, LANES - 8, axis=1))
            a_prev = jnp.exp(m_start[d][c] - m_last)
            for p in range(PAIRS):
                a_pair = _pair_cols(a_col, l_b + 2 * p, lt_dh)
                v_pair = z_ref[rows, C_V + LANES * p:C_V + LANES * (p + 1)]
                rhs = jnp.concatenate([v_pair * a_pair, a_pair], axis=1).astype(BF16)
                k_pair = (z_ref[rows, C_K + LANES * p:C_K + LANES * (p + 1)] * (DH ** -0.5)).astype(BF16)
                upd = lax.dot_general(k_pair, rhs, (((0,), (0,)), ((), ())),
                                      preferred_element_type=F32)
                upd = jnp.where(bd_mask, upd, 0.0)
                if bd[p] is None:
                    bd[p] = upd
                else:
                    scale = jnp.where(bd_row < DH, a_prev[:, l_m + 2 * p:l_m + 2 * p + 1],
                                      a_prev[:, l_m + 2 * p + 1:l_m + 2 * p + 2])
                    bd[p] = scale * bd[p] + upd
        fin[d] = bd

    for c in range(nc):
        rows = slice(HALO + c * CHUNK, HALO + (c + 1) * CHUNK)
        cols = cols_ref[c]
        grow = grow_ref[c]
        m_prev = jnp.where(lane_row < L_BB, m_start[0][c], m_start[1][c])
        m_all = jnp.maximum(cols, m_prev)
        w_inter = jnp.exp(m_prev - m_all)
        e_neg = jnp.exp(-(pltpu.roll(cols, 8, axis=1) + m_all))
        for p in range(PAIRS):
            q_pair = z_ref[rows, C_Q + LANES * p:C_Q + LANES * (p + 1)].astype(BF16)
            k_f32 = z_ref[rows, C_K + LANES * p:C_K + LANES * (p + 1)] * (DH ** -0.5)
            v_pair = z_ref[rows, C_V + LANES * p:C_V + LANES * (p + 1)]
            p_cat = ([], [])
            va = []
            for j in range(2):
                h = 2 * p + j
                own = lt_dh if j == 0 else jnp.logical_not(lt_dh)
                k_own = jnp.where(own, k_f32, 0.0).astype(BF16)
                s = lax.dot_general(q_pair, k_own, (((1,), (1,)), ((), ())),
                                    preferred_element_type=F32)
                va.append(jnp.concatenate([jnp.where(own, v_pair, 0.0), jnp.where(own, 1.0, 0.0)],
                                          axis=1).astype(BF16))
                for d, _, l_m, _ in dirs:
                    w = jnp.exp(jnp.where(tri[d], grow[8 * d + h:8 * d + h + 1] - m_all[:, l_m + h:l_m + h + 1],
                                          -jnp.inf))
                    p_cat[d].append((s * w).astype(BF16))
            va = jnp.concatenate(va, axis=0)
            h_sum = None
            for d, _, l_m, _ in dirs:
                tot = jnp.dot(jnp.concatenate(p_cat[d], axis=1), va, preferred_element_type=F32)
                if have_state[d][c]:
                    e_pair = _pair_cols(w_inter, l_m + 2 * p, lt_dh)
                    qc = jnp.dot(q_pair, bd_ref[d, c, p], preferred_element_type=F32)
                    tot = tot + jnp.concatenate([e_pair, e_pair], axis=1) * qc
                floor = _pair_cols(e_neg, l_m + 2 * p, lt_dh)
                h_dir = tot[:, :LANES] / jnp.maximum(jnp.abs(tot[:, LANES:]), floor)
                h_sum = h_dir if h_sum is None else h_sum + h_dir
            o_pair = z_ref[rows, C_O + LANES * p:C_O + LANES * (p + 1)]
            ycat_ref[c * CHUNK:(c + 1) * CHUNK, LANES * p:LANES * (p + 1)] = (
                _sigmoid(o_pair) * h_sum).astype(BF16)

    m_row = jnp.where(lane_row < L_BB, m_fin[0], m_fin[1])
    return fin, m_row


def _pair_state(c_ref, n_ref, q, d, p):
    eye = (lax.broadcasted_iota(jnp.int32, (DH, DH), 0) == lax.broadcasted_iota(jnp.int32, (DH, DH), 1))
    zero = jnp.zeros((DH, DH), F32)
    blocks = []
    for j in range(2):
        h = 2 * p + j
        n_row = n_ref[q, 0, d, h:h + 1, :]
        n_col = jnp.sum(jnp.where(eye, n_row, 0.0), axis=1, keepdims=True)
        row = [zero] * 4
        row[j] = c_ref[q, 0, d, h]
        row[2 + j] = jnp.broadcast_to(n_col, (DH, DH))
        blocks.append(jnp.concatenate(row, axis=1))
    return jnp.concatenate(blocks, axis=0)


def _mixer_kernel(*refs, layer, seq_len, nseq, add_pos, has_state, emit_state, cond_base, n_prev):
    it = iter(refs)
    x_ref = next(it)
    if add_pos:
        er_ref = next(it)
        ec_ref = next(it)
    mod_ref = next(it)
    ng_ref = next(it)
    win_ref = next(it)
    gb_ref = next(it)
    ws_ref = next(it)
    gbias_ref = next(it)
    pw_ref = next(it)
    psc_ref = next(it)
    cw_ref = next(it)
    wout_ref = next(it)
    if has_state:
        c0_ref = next(it)
        n0_ref = next(it)
        m0_ref = next(it)
    if n_prev:
        cprev_ref = next(it)
        nprev_ref = next(it)
    x1_ref = next(it)
    h2_ref = next(it)
    if emit_state:
        cout_ref = next(it)
        nout_ref = next(it)
        mout_ref = next(it)
    z_all = next(it)
    gcol_all = next(it)
    grow_all = next(it)
    cols_all = next(it)
    bd_all = next(it)
    ycat_all = next(it)

    nblk = seq_len // CHUNK
    cond = cond_base if cond_base == 0 else cond_base + pl.program_id(0)
    sh1, sc1, gt1, sh2, sc2, _ = _mod_rows(mod_ref, cond)
    ng = ng_ref[0]
    seqs = range(nseq)
    units = [(q, r) for q in seqs for r in range(nblk)]

    zeros_halo = jnp.zeros((HALO, Z_COLS - C_PZ), F32)
    for q in seqs:
        z_all[q, 0:HALO, C_PZ:Z_COLS] = zeros_halo
        z_all[q, HALO + seq_len:2 * HALO + seq_len, C_PZ:Z_COLS] = zeros_halo

    g_in = ng[0:1] * (1.0 + sc1)
    hbs = []
    for q, r in units:
        rows = slice(r * CHUNK, (r + 1) * CHUNK)
        xb = x_ref[q, rows, :]
        if add_pos:
            pieces = []
            for g in range(CHUNK // GRID_W):
                gi = r * (CHUNK // GRID_W) + g
                er = jnp.broadcast_to(er_ref[gi:gi + 1, :], (GRID_W, D_MODEL // 2))
                pieces.append(jnp.concatenate([er, ec_ref[...]], axis=1))
            xb = xb + jnp.concatenate(pieces, axis=0)
            x1_ref[q, rows, :] = xb
        hbs.append((_rms(xb, g_in) + sh1).astype(BF16))
    hb = jnp.concatenate(hbs, axis=0)
    gates_all = jnp.dot(hb, win_ref[0, :, C_GATE:W_COLS], preferred_element_type=F32) + gb_ref[0]
    for cg in range(Z_COLS // Z_GROUP):
        zc = jnp.dot(hb, win_ref[0, :, cg * Z_GROUP:(cg + 1) * Z_GROUP], preferred_element_type=F32)
        for u, (q, r) in enumerate(units):
            z_all[q, HALO + r * CHUNK:HALO + (r + 1) * CHUNK, cg * Z_GROUP:(cg + 1) * Z_GROUP] = (
                zc[u * CHUNK:(u + 1) * CHUNK])
    for u, (q, r) in enumerate(units):
        gates = gates_all[u * CHUNK:(u + 1) * CHUNK]
        gcol_all[q, r] = gates
        grow, cols = _gate_stats(gates)
        grow_all[q, r] = grow
        cols_all[q, r] = cols

    for q in seqs:
        if has_state:
            state0 = [[_pair_state(c0_ref, n0_ref, q, d, p) for p in range(PAIRS)] for d in range(2)]
            m0 = m0_ref[q, 0]
        else:
            state0 = [[None] * PAIRS, [None] * PAIRS]
            m0 = jnp.zeros((1, LANES), F32)
        fin, m_row = _mlstm(z_all.at[q], gcol_all.at[q], grow_all.at[q], cols_all.at[q], bd_all.at[q],
                            ycat_all.at[q], nblk, state0, m0, need_final=emit_state)
        if emit_state:
            for d in range(2):
                for p in range(PAIRS):
                    for j in range(2):
                        cout_ref[q, n_prev, d, 2 * p + j] = fin[d][p][DH * j:DH * (j + 1), DH * j:DH * (j + 1)]
                    n_t = fin[d][p][:, LANES:].T
                    nout_ref[q, n_prev, d, 2 * p:2 * p + 1, :] = n_t[0:1, 0:DH]
                    nout_ref[q, n_prev, d, 2 * p + 1:2 * p + 2, :] = n_t[DH:DH + 1, DH:LANES]
            mout_ref[q] = m_row
    if emit_state and n_prev:
        cout_ref[:, 0:n_prev] = cprev_ref[...]
        nout_ref[:, 0:n_prev] = nprev_ref[...]

    lane_g = lax.broadcasted_iota(jnp.int32, (CHUNK, D_GROUP), 1) // (D_GROUP // 4)
    lg = lax.broadcasted_iota(jnp.int32, (GMLP_CHUNK, D_GROUP), 1) // (D_GROUP // 4)
    row_c = lax.broadcasted_iota(jnp.int32, (CHUNK, D_GROUP), 0)
    half = jnp.where(lane_g == 0, 1, jnp.where(lane_g == 1, 2, jnp.where(lane_g == 2, 4, 8)))
    cw = cw_ref[0]
    ws_cat = jnp.concatenate([ws_ref[0, g] for g in range(4)], axis=1).astype(BF16)
    dg = D_GROUP // 4
    pool_zero = jnp.zeros((dg, dg), F32)
    pool_bd = jnp.concatenate(
        [jnp.concatenate([pw_ref[0, g] if h == g else pool_zero for h in range(4)], axis=1)
         for g in range(4)], axis=0).astype(BF16)
    for q, r in units:
        z_ref = z_all.at[q]
        ycat_ref = ycat_all.at[q]
        rows = slice(r * CHUNK, (r + 1) * CHUNK)
        zrows = slice(HALO + r * CHUNK, HALO + (r + 1) * CHUNK)
        hrows = slice(r * CHUNK, (r + 1) * CHUNK + 2 * HALO)
        inner = slice(HALO, HALO + CHUNK)

        for s in range(CHUNK // GMLP_CHUNK):
            zs = slice(HALO + r * CHUNK + s * GMLP_CHUNK, HALO + r * CHUNK + (s + 1) * GMLP_CHUNK)
            vch = z_ref[zs, C_GV:C_GV + D_GROUP]
            v_stack = jnp.concatenate([jnp.where(lg == g, vch, 0.0) for g in range(4)], axis=0)
            mixed = gbias_ref[0] + jnp.dot(ws_cat, v_stack.astype(BF16), preferred_element_type=F32)
            yb = z_ref[zs, C_GU:C_GU + D_GROUP] * mixed
            ys = slice(r * CHUNK + s * GMLP_CHUNK, r * CHUNK + (s + 1) * GMLP_CHUNK)
            ycat_ref[ys, D_GROUP:2 * D_GROUP] = yb.astype(BF16)

        pz = z_ref[hrows, C_PZ:C_PZ + D_GROUP]
        a2 = pz + _shift_down(pz, 1)
        a4 = a2 + _shift_down(a2, 2)
        a8 = a4 + _shift_down(a4, 4)
        a16 = a8 + _shift_down(a8, 8)
        win = [a2[inner], _shift_up(a4, 1)[inner], _shift_up(a8, 3)[inner], _shift_up(a16, 7)[inner]]
        wsum = jnp.where(lane_g == 0, win[0], jnp.where(lane_g == 1, win[1],
                         jnp.where(lane_g == 2, win[2], win[3])))
        t_abs = row_c + r * CHUNK
        cnt = (jnp.minimum(t_abs + half, seq_len) - jnp.maximum(t_abs - half, 0)).astype(F32)
        pooled = wsum / cnt - pz[inner]
        yc = (jnp.dot(pooled.astype(BF16), pool_bd, preferred_element_type=F32)
              * psc_ref[layer:layer + 1, :])
        ycat_ref[rows, 2 * D_GROUP:3 * D_GROUP] = yc.astype(BF16)

        u = z_ref[hrows, C_CC:C_CC + D_GROUP] * z_ref[hrows, C_CX:C_CX + D_GROUP]
        conv = (cw[0:1] * _shift_down(u, 1)[inner] + cw[1:2] * u[inner]
                + cw[2:3] * _shift_up(u, 1)[inner])
        yd = z_ref[zrows, C_CB:C_CB + D_GROUP] * conv
        ycat_ref[rows, 3 * D_GROUP:4 * D_GROUP] = yd.astype(BF16)

    y_all = jnp.dot(jnp.concatenate([ycat_all[q, r * CHUNK:(r + 1) * CHUNK, :] for q, r in units], axis=0),
                    wout_ref[0], preferred_element_type=F32)
    g_y = gt1 * ng[1:2]
    g_h2 = ng[2:3] * (1.0 + sc2)
    for u, (q, r) in enumerate(units):
        rows = slice(r * CHUNK, (r + 1) * CHUNK)
        xb = x1_ref[q, rows, :] if add_pos else x_ref[q, rows, :]
        x1 = xb + _rms(y_all[u * CHUNK:(u + 1) * CHUNK], g_y)
        x1_ref[q, rows, :] = x1
        h2_ref[q, rows, :] = (_rms(x1, g_h2) + sh2).astype(BF16)


def _layer_spec(shape, layer):
    nd = len(shape)
    return pl.BlockSpec((1,) + tuple(shape[1:]), lambda b: (layer,) + (0,) * (nd - 1),
                        pipeline_mode=pl.Buffered(1))


def _mixer_call(x, layer, mod_all, wts, cond_base, pos=None, state=None, emit_state=False,
                prev_states=None, nseq=1):
    bsz, seq_len, _ = x.shape
    nblk = seq_len // CHUNK
    add_pos = pos is not None
    has_state = state is not None
    n_prev = 0 if prev_states is None else prev_states[0].shape[1]
    args = [x]
    in_specs = [pl.BlockSpec((nseq, seq_len, D_MODEL), lambda b: (b, 0, 0))]
    if add_pos:
        args += list(pos)
        in_specs += [pl.BlockSpec(p.shape, lambda b: (0, 0), pipeline_mode=pl.Buffered(1)) for p in pos]
    args.append(mod_all)
    in_specs.append(_layer_spec(mod_all.shape, layer))
    for name in ("norm_g", "w_in", "gate_b", "gmlp_ws", "gmlp_bias", "pool_w", "pool_scale",
                 "conv_w", "w_out"):
        args.append(wts[name])
        if wts[name].ndim == 2:
            in_specs.append(pl.BlockSpec(wts[name].shape, lambda b: (0, 0), pipeline_mode=pl.Buffered(1)))
        else:
            in_specs.append(_layer_spec(wts[name].shape, layer))
    if has_state:
        args += list(state)
        in_specs += [pl.BlockSpec((nseq, 1, 2, HEADS, DH, DH), lambda b: (b, layer, 0, 0, 0, 0)),
                     pl.BlockSpec((nseq, 1, 2, HEADS, DH), lambda b: (b, layer, 0, 0, 0)),
                     pl.BlockSpec((nseq, 1, 1, LANES), lambda b: (b, layer, 0, 0))]
    out_shape = [jax.ShapeDtypeStruct((bsz, seq_len, D_MODEL), F32),
                 jax.ShapeDtypeStruct((bsz, seq_len, D_MODEL), BF16)]
    out_specs = [pl.BlockSpec((nseq, seq_len, D_MODEL), lambda b: (b, 0, 0)),
                 pl.BlockSpec((nseq, seq_len, D_MODEL), lambda b: (b, 0, 0))]
    if emit_state:
        if n_prev:
            args += list(prev_states)
            in_specs += [pl.BlockSpec((nseq, n_prev, 2, HEADS, DH, DH), lambda b: (b, 0, 0, 0, 0, 0)),
                         pl.BlockSpec((nseq, n_prev, 2, HEADS, DH), lambda b: (b, 0, 0, 0, 0))]
        out_shape += [jax.ShapeDtypeStruct((bsz, n_prev + 1, 2, HEADS, DH, DH), F32),
                      jax.ShapeDtypeStruct((bsz, n_prev + 1, 2, HEADS, DH), F32),
                      jax.ShapeDtypeStruct((bsz, 1, LANES), F32)]
        out_specs += [pl.BlockSpec((nseq, n_prev + 1, 2, HEADS, DH, DH), lambda b: (b, 0, 0, 0, 0, 0)),
                      pl.BlockSpec((nseq, n_prev + 1, 2, HEADS, DH), lambda b: (b, 0, 0, 0, 0)),
                      pl.BlockSpec((nseq, 1, LANES), lambda b: (b, 0, 0))]
    kern = functools.partial(_mixer_kernel, layer=layer, seq_len=seq_len, nseq=nseq, add_pos=add_pos,
                             has_state=has_state, emit_state=emit_state, cond_base=cond_base,
                             n_prev=n_prev)
    return pl.pallas_call(
        kern,
        out_shape=out_shape,
        grid=(bsz // nseq,),
        in_specs=in_specs,
        out_specs=out_specs,
        scratch_shapes=[pltpu.VMEM((nseq, seq_len + 2 * HALO, Z_COLS), F32),
                        pltpu.VMEM((nseq, nblk, CHUNK, LANES), F32),
                        pltpu.VMEM((nseq, nblk, 16, CHUNK), F32),
                        pltpu.VMEM((nseq, nblk, CHUNK, LANES), F32),
                        pltpu.VMEM((nseq, 2, nblk, PAIRS, LANES, 2 * LANES), BF16),
                        pltpu.VMEM((nseq, seq_len, D_MODEL), BF16)],
        compiler_params=pltpu.CompilerParams(
            dimension_semantics=("arbitrary",), vmem_limit_bytes=VMEM_LIMIT),
        name="mixer_t%d" % seq_len,
    )(*args)


def _ffn_kernel(mod_ref, ng_ref, cw_ref, x1p_hbm, h2p_hbm, x1s_hbm, h2s_hbm, up_hbm, down_hbm,
                outp_hbm, outs_hbm,
                up_bf, down_bf, act_ref, h2_buf, x1_buf, o_buf, up_stage, down_stage,
                in_sems, out_sems, w_sems, *, layer, tiles_p, tiles_s, seq_p, seq_s):
    i = pl.program_id(0)
    n_tiles = tiles_p + tiles_s
    rows = FFN_ROWS
    half = FFN_SUB // 2
    nsub = D_FF // half
    slot = i % 2
    is_prompt = i < tiles_p

    def tile_rows(tile, first_tile):
        if isinstance(tile, int):
            return pl.ds((tile - first_tile) * rows, rows)
        return pl.ds(pl.multiple_of((tile - first_tile) * rows, rows), rows)

    def h2_copy(h2_hbm, r, buf):
        return pltpu.make_async_copy(h2_hbm.at[r], h2_buf.at[buf], in_sems.at[buf])

    def x1_copy(x1_hbm, r):
        return pltpu.make_async_copy(x1_hbm.at[r], x1_buf, in_sems.at[2])

    def fetch_h2(tile, buf):
        @pl.when(tile < tiles_p)
        def _():
            h2_copy(h2p_hbm, tile_rows(tile, 0), buf).start()

        @pl.when(tile >= tiles_p)
        def _():
            h2_copy(h2s_hbm, tile_rows(tile, tiles_p), buf).start()

    def fetch_x1(tile):
        @pl.when(tile < tiles_p)
        def _():
            x1_copy(x1p_hbm, tile_rows(tile, 0)).start()

        @pl.when(tile >= tiles_p)
        def _():
            x1_copy(x1s_hbm, tile_rows(tile, tiles_p)).start()

    def await_store(buf):
        pltpu.make_async_copy(o_buf.at[buf], outp_hbm.at[pl.ds(0, rows)], out_sems.at[buf]).wait()

    def weight_copies(c, buf):
        aligned = (lambda v: v) if isinstance(c, int) else (lambda v: pl.multiple_of(v, half))
        cv = pl.ds(aligned(c * half), half)
        cg = pl.ds(aligned(D_FF + c * half), half)
        return (pltpu.make_async_copy(up_hbm.at[layer, :, cv], up_stage.at[buf, 0], w_sems.at[buf, 0]),
                pltpu.make_async_copy(up_hbm.at[layer, :, cg], up_stage.at[buf, 1], w_sems.at[buf, 1]),
                pltpu.make_async_copy(down_hbm.at[layer, cv, :], down_stage.at[buf], w_sems.at[buf, 2]))

    @pl.when(i == 0)
    def _():
        h2_copy(h2p_hbm, tile_rows(0, 0), 0).start()
        depth = up_stage.shape[0]
        for c0 in range(depth - 1):
            for cp in weight_copies(c0, c0):
                cp.start()

        def convert(c, carry):
            buf = c % depth
            ahead = c + depth - 1

            @pl.when(ahead < nsub)
            def _():
                for cp in weight_copies(ahead, ahead % depth):
                    cp.start()

            for cp in weight_copies(c, buf):
                cp.wait()
            up_bf[c, :, 0:half] = up_stage[buf, 0].astype(BF16)
            up_bf[c, :, half:FFN_SUB] = up_stage[buf, 1].astype(BF16)
            down_bf[pl.ds(pl.multiple_of(c * half, half), half), :] = down_stage[buf].astype(BF16)
            return carry

        lax.fori_loop(0, nsub, convert, 0)

    h2_copy(h2p_hbm, pl.ds(0, rows), slot).wait()

    @pl.when(i >= 2)
    def _():
        await_store(slot)

    fetch_x1(i)

    @pl.when(i + 1 < n_tiles)
    def _():
        fetch_h2(i + 1, 1 - slot)

    row = lax.broadcasted_iota(jnp.int32, (rows, FFN_SUB), 0)
    first = ((row & (seq_s - 1)) == 0) | (((row & (seq_p - 1)) == 0) & is_prompt)
    last = ((row & (seq_s - 1)) == seq_s - 1) | (((row & (seq_p - 1)) == seq_p - 1) & is_prompt)
    h2 = h2_buf[slot]
    for s in range(nsub):
        a = jnp.dot(h2, up_bf[s], preferred_element_type=F32)
        cw = jnp.concatenate([cw_ref[0, :, s * half:(s + 1) * half],
                              0.5 * cw_ref[0, :, D_FF + s * half:D_FF + (s + 1) * half]], axis=1)
        a_prev = jnp.where(first, 0.0, pltpu.roll(a, 1, axis=0))
        a_next = jnp.where(last, 0.0, pltpu.roll(a, rows - 1, axis=0))
        ac = cw[0:1] * a_prev + cw[1:2] * a + cw[2:3] * a_next
        hg = ac[:, half:]
        act = (hg * (1.0 + jnp.tanh(hg))) * ac[:, :half]
        act_ref[:, s * half:(s + 1) * half] = act.astype(BF16)
    y = jnp.dot(act_ref[...], down_bf[...], preferred_element_type=F32)
    cond = jnp.where(is_prompt, 0, i - tiles_p + 1)
    gt2 = _mod_rows(mod_ref, cond)[5]

    res = gt2 * _rms(y, ng_ref[0][3:4])
    x1_copy(x1p_hbm, pl.ds(0, rows)).wait()
    o_buf[slot] = x1_buf[...] + res

    @pl.when(is_prompt)
    def _():
        pltpu.make_async_copy(o_buf.at[slot], outp_hbm.at[tile_rows(i, 0)], out_sems.at[slot]).start()

    @pl.when(jnp.logical_not(is_prompt))
    def _():
        pltpu.make_async_copy(o_buf.at[slot], outs_hbm.at[tile_rows(i, tiles_p)], out_sems.at[slot]).start()

    @pl.when(i == n_tiles - 1)
    def _():
        await_store(1 - slot)
        await_store(slot)


def _ffn_call(x1p, h2p, x1s, h2s, layer, mod_all, wts, seq_p, seq_s):
    tiles_p = x1p.shape[0] // FFN_ROWS
    tiles_s = x1s.shape[0] // FFN_ROWS
    assert tiles_p >= 1 and tiles_s >= 1 and tiles_p + tiles_s >= 2
    half = FFN_SUB // 2

    def resident(shape):
        nd = len(shape)
        return pl.BlockSpec((1,) + tuple(shape[1:]), lambda i: (layer,) + (0,) * (nd - 1),
                            pipeline_mode=pl.Buffered(1))

    hbm = pl.BlockSpec(memory_space=pl.ANY)
    return pl.pallas_call(
        functools.partial(_ffn_kernel, layer=layer, tiles_p=tiles_p, tiles_s=tiles_s,
                          seq_p=seq_p, seq_s=seq_s),
        out_shape=[jax.ShapeDtypeStruct(x1p.shape, F32), jax.ShapeDtypeStruct(x1s.shape, F32)],
        grid=(tiles_p + tiles_s,),
        in_specs=[resident(mod_all.shape), resident(wts["norm_g"].shape), resident(wts["ffn_conv"].shape),
                  hbm, hbm, hbm, hbm, hbm, hbm],
        out_specs=[hbm, hbm],
        scratch_shapes=[pltpu.VMEM((D_FF // half, D_MODEL, FFN_SUB), BF16),
                        pltpu.VMEM((D_FF, D_MODEL), BF16),
                        pltpu.VMEM((FFN_ROWS, D_FF), BF16),
                        pltpu.VMEM((2, FFN_ROWS, D_MODEL), BF16),
                        pltpu.VMEM((FFN_ROWS, D_MODEL), F32),
                        pltpu.VMEM((2, FFN_ROWS, D_MODEL), F32),
                        pltpu.VMEM((FFN_WEIGHT_BUFS, 2, D_MODEL, half), F32),
                        pltpu.VMEM((FFN_WEIGHT_BUFS, half, D_MODEL), F32),
                        pltpu.SemaphoreType.DMA((3,)),
                        pltpu.SemaphoreType.DMA((2,)),
                        pltpu.SemaphoreType.DMA((FFN_WEIGHT_BUFS, 3))],
        compiler_params=pltpu.CompilerParams(
            dimension_semantics=("arbitrary",), vmem_limit_bytes=VMEM_LIMIT),
        name="ffn",
    )(mod_all, wts["norm_g"], wts["ffn_conv"], x1p, h2p, x1s, h2s, wts["ffn_up"], wts["ffn_down"])


def _pos_tables(t_len):
    quarter = D_MODEL // 4
    omega = (1.0 / (np.float32(POS_BASE) ** (np.arange(quarter, dtype=np.float32) / np.float32(quarter))))
    omega = omega.astype(np.float32)

    def emb(p):
        a = p.astype(np.float32)[:, None] * omega[None, :]
        return np.concatenate([np.sin(a), np.cos(a)], axis=-1).astype(np.float32)

    return jnp.asarray(emb(np.arange(t_len // GRID_W))), jnp.asarray(emb(np.arange(GRID_W)))


def _stabiliser_rows(state_m):
    bs = state_m.shape[0]
    m = jnp.pad(state_m.astype(F32), ((0, 0), (0, 0), (0, 0), (0, 16 - HEADS)))
    return jnp.pad(m.reshape(bs, DEPTH, 1, 32), ((0, 0), (0, 0), (0, 0), (L_MF, LANES - 32 - L_MF)))


def kernel(x_prompt, x_sample, state_C, state_n, state_m, c, c_ctx, norm_g, ada_w, ada_b, w_in,
           gate_b, gmlp_ws, gmlp_b, pool_w, pool_scale, conv_w, w_out, ffn_up, ffn_conv, ffn_down):
    bp, tp, _ = x_prompt.shape
    bs, ts, _ = x_sample.shape
    assert tp == CHUNK and ts % CHUNK == 0 and FFN_ROWS % tp == 0 and ts == FFN_ROWS

    mod_all = _modulation(c_ctx, c, ada_w, ada_b)
    win_p, wout_p = _prep_weights(w_in, w_out)

    gb = jnp.pad(gate_b, ((0, 0), (0, 0), (0, 8 - HEADS))).reshape(DEPTH, 1, 32)
    dg = D_GROUP // 4
    wts = {
        "norm_g": norm_g,
        "w_in": win_p,
        "gate_b": jnp.pad(gb, ((0, 0), (0, 0), (0, LANES - 32))),
        "gmlp_ws": gmlp_ws,
        "gmlp_bias": jnp.repeat(jnp.swapaxes(gmlp_b, 1, 2), dg, axis=2),
        "pool_w": pool_w,
        "pool_scale": pool_scale,
        "conv_w": conv_w,
        "w_out": wout_p,
        "ffn_up": ffn_up,
        "ffn_conv": ffn_conv,
        "ffn_down": ffn_down,
    }
    pos = _pos_tables(ts)
    state = (state_C.astype(F32), state_n.astype(F32), _stabiliser_rows(state_m))

    xp, xs = x_prompt, x_sample
    prev, ms = None, []
    for l in range(DEPTH):
        x1p, h2p, c_all, n_all, m_fin = _mixer_call(xp, l, mod_all, wts, 0, emit_state=True,
                                                    prev_states=prev, nseq=MIXER_ROWS // tp)
        prev = (c_all, n_all)
        ms.append(jnp.stack([m_fin[:, 0, L_MF:L_MF + HEADS], m_fin[:, 0, L_MB:L_MB + HEADS]], axis=1))

        x1s, h2s = _mixer_call(xs, l, mod_all, wts, 1, pos=pos if l == 0 else None, state=state)

        xp, xs = _ffn_call(x1p.reshape(bp * tp, D_MODEL), h2p.reshape(bp * tp, D_MODEL),
                           x1s.reshape(bs * ts, D_MODEL), h2s.reshape(bs * ts, D_MODEL),
                           l, mod_all, wts, tp, ts)
        xp = xp.reshape(bp, tp, D_MODEL)
        xs = xs.reshape(bs, ts, D_MODEL)

    return (xp, xs, c_all, n_all, jnp.stack(ms, axis=1))
```

```python
import functools

import numpy as np
import jax
import jax.numpy as jnp
from jax import lax
from jax.experimental import pallas as pl
from jax.experimental.pallas import tpu as pltpu

F32 = jnp.float32
BF16 = jnp.bfloat16

D_MODEL = 1024
DEPTH = 2
HEADS = 4
PAIRS = HEADS // 2
DH = 64
D_GROUP = 256
D_FF = 2816
N_MOD = 6
EPS = 1e-6
GRID_W = 64
POS_BASE = 10000.0
D_IN = 2576
N_GATE = 4 * HEADS

CHUNK = 256
GMLP_CHUNK = 128
HALO = 8
LANES = 128

C_Q, C_K, C_V, C_O = 0, 256, 512, 768
C_GU, C_GV, C_PZ, C_CB, C_CC, C_CX = 1024, 1280, 1536, 1792, 2048, 2304
Z_COLS = 2560
C_GATE = Z_COLS
W_COLS = Z_COLS + LANES
Z_GROUP = 512
L_BF, L_MF, L_BB, L_MB = 0, 8, 16, 24
L_STEP = L_MF - L_BF

MIXER_ROWS = 1024
FFN_ROWS = 1024
FFN_SUB = 256
FFN_WEIGHT_CHUNKS = 16
FFN_WEIGHT_BUFS = 4
VMEM_LIMIT = 60 * 1024 * 1024


def _rms(x, g):
    ms = jnp.mean(x * x, axis=-1, keepdims=True)
    return x * lax.rsqrt(ms + EPS) * g


def _sigmoid(x):
    return 1.0 / (1.0 + jnp.exp(-x))


def _log_sigmoid(x):
    return jnp.minimum(x, 0.0) - jnp.log1p(jnp.exp(-jnp.abs(x)))


def _scan_lanes(x, op, reverse, fill):
    n = x.shape[1]
    lane = lax.broadcasted_iota(jnp.int32, x.shape, 1)
    k = 1
    while k < n:
        if reverse:
            sh = jnp.where(lane < n - k, pltpu.roll(x, n - k, axis=1), fill)
        else:
            sh = jnp.where(lane >= k, pltpu.roll(x, k, axis=1), fill)
        x = op(x, sh)
        k *= 2
    return x


def _shift_down(x, k):
    return pltpu.roll(x, k, axis=0)


def _shift_up(x, k):
    return pltpu.roll(x, x.shape[0] - k, axis=0)


def _prep_kernel(wt_ref, sel_ref, wout_ref, win_o_ref, wout_o_ref):
    for n in range(Z_COLS // LANES):
        r0 = LANES * n if n < 4 * D_GROUP // LANES else LANES * n + N_GATE
        win_o_ref[0, :, LANES * n:LANES * (n + 1)] = wt_ref[0, r0:r0 + LANES, :].T.astype(BF16)
    g = wt_ref[0, 4 * D_GROUP:4 * D_GROUP + LANES, :].T.astype(BF16)
    win_o_ref[0, :, C_GATE:W_COLS] = jnp.dot(g, sel_ref[...], preferred_element_type=F32).astype(BF16)
    wout_o_ref[...] = wout_ref[...].astype(BF16)


def _prep_weights(w_in, w_out):
    sel = np.zeros((LANES, LANES), np.float32)
    for j in range(N_GATE):
        sel[j, 8 * (j // HEADS) + j % HEADS] = 1.0
    rows = D_MODEL // 2
    return pl.pallas_call(
        _prep_kernel,
        out_shape=[jax.ShapeDtypeStruct((DEPTH, D_MODEL, W_COLS), BF16),
                   jax.ShapeDtypeStruct(w_out.shape, BF16)],
        grid=(DEPTH, D_MODEL // rows),
        in_specs=[pl.BlockSpec((1, D_IN, rows), lambda l, i: (l, 0, i)),
                  pl.BlockSpec((LANES, LANES), lambda l, i: (0, 0)),
                  pl.BlockSpec((1, rows, D_MODEL), lambda l, i: (l, i, 0))],
        out_specs=[pl.BlockSpec((1, rows, W_COLS), lambda l, i: (l, i, 0)),
                   pl.BlockSpec((1, rows, D_MODEL), lambda l, i: (l, i, 0))],
        compiler_params=pltpu.CompilerParams(dimension_semantics=("arbitrary", "arbitrary"),
                                             vmem_limit_bytes=VMEM_LIMIT),
        name="prep_mixer_w")(jnp.swapaxes(w_in, 1, 2), jnp.asarray(sel, BF16), w_out)


def _mod_kernel(c_ref, w_ref, b_ref, o_ref):
    c = c_ref[...]
    s = (c * _sigmoid(c)).astype(BF16)
    bias = b_ref[pl.ds(pl.program_id(0), 1), :]
    o_ref[0] = jnp.dot(s, w_ref[0].astype(BF16), preferred_element_type=F32) + bias


def _modulation(conds, ada_w, ada_b):
    n_out = N_MOD * D_MODEL
    bn = 1536
    return pl.pallas_call(
        _mod_kernel,
        out_shape=jax.ShapeDtypeStruct((DEPTH, 8, n_out), F32),
        grid=(DEPTH, n_out // bn),
        in_specs=[
            pl.BlockSpec((8, D_MODEL), lambda l, j: (0, 0)),
            pl.BlockSpec((1, D_MODEL, bn), lambda l, j: (l, 0, j)),
            pl.BlockSpec((DEPTH, bn), lambda l, j: (0, j)),
        ],
        out_specs=pl.BlockSpec((1, 8, bn), lambda l, j: (l, 0, j)),
        compiler_params=pltpu.CompilerParams(
            dimension_semantics=("arbitrary", "arbitrary"), vmem_limit_bytes=VMEM_LIMIT),
        name="adaln_mod",
    )(conds, ada_w, ada_b)


def _mod_rows(mod_ref, cond):
    row = mod_ref[0, pl.ds(cond, 1), :]
    return [row[:, i * D_MODEL:(i + 1) * D_MODEL] for i in range(N_MOD)]


def _gate_stats(gates):
    gt = gates.T
    b_f = _scan_lanes(_log_sigmoid(gt[8:16]), jnp.add, False, 0.0)
    g_f = gt[0:8] - b_f
    cm_f = _scan_lanes(g_f, jnp.maximum, False, -jnp.inf)
    b_b = _scan_lanes(_log_sigmoid(gt[24:32]), jnp.add, True, 0.0)
    g_b = gt[16:24] - b_b
    cm_b = _scan_lanes(g_b, jnp.maximum, True, -jnp.inf)
    stack = jnp.concatenate([b_f, cm_f, b_b, cm_b, jnp.zeros((LANES - 32, CHUNK), F32)], axis=0)
    return jnp.concatenate([g_f, g_b], axis=0), stack.T


def _pair_cols(mat, lane0, lane_lt_dh):
    return jnp.where(lane_lt_dh, mat[:, lane0:lane0 + 1], mat[:, lane0 + 1:lane0 + 2])


def _mlstm(z_ref, gcol_ref, grow_ref, cols_ref, bd_ref, ycat_ref, nc, state0, m0, need_final):
    lane_row = lax.broadcasted_iota(jnp.int32, (1, LANES), 1)
    lane = lax.broadcasted_iota(jnp.int32, (CHUNK, LANES), 1)
    lt_dh = lane < DH
    row_ll = lax.broadcasted_iota(jnp.int32, (CHUNK, CHUNK), 0)
    col_ll = lax.broadcasted_iota(jnp.int32, (CHUNK, CHUNK), 1)
    tri = (col_ll <= row_ll, col_ll >= row_ll)
    bd_row = lax.broadcasted_iota(jnp.int32, (LANES, 2 * LANES), 0)
    bd_col = lax.broadcasted_iota(jnp.int32, (LANES, 2 * LANES), 1)
    bd_mask = (bd_row < DH) == ((bd_col & (LANES - 1)) < DH)
    dirs = ((0, L_BF, L_MF, CHUNK - 1), (1, L_BB, L_MB, 0))

    m_start = [[None] * nc, [None] * nc]
    m_fin = [None, None]
    for d, _, _, last in dirs:
        m = m0
        order = range(nc) if d == 0 else range(nc - 1, -1, -1)
        for step, c in enumerate(order):
            m_start[d][c] = m
            if need_final or step < nc - 1:
                cl = cols_ref[c, last:last + 1, :]
                m = pltpu.roll(cl, L_STEP, axis=1) + jnp.maximum(m, cl)
        m_fin[d] = m

    have_state = [[False] * nc, [False] * nc]
    fin = [None, None]
    for d, l_b, l_m, last in dirs:
        bd = list(state0[d])
        order = range(nc) if d == 0 else range(nc - 1, -1, -1)
        for step, c in enumerate(order):
            if bd[0] is not None:
                have_state[d][c] = True
                for p in range(PAIRS):
                    bd_ref[d, c, p] = bd[p].astype(BF16)
            if not (need_final or step < nc - 1):
                continue
            rows = slice(HALO + c * CHUNK, HALO + (c + 1) * CHUNK)
            cols = cols_ref[c]
            m_last = jnp.maximum(cols[last:last + 1, :], m_start[d][c])
            a_col = jnp.exp(gcol_ref[c] - cols - pltpu.roll(m_last, LANES - L_STEP, axis=1))
            a_prev = jnp.exp(m_start[d][c] - m_last)
            for p in range(PAIRS):
                a_pair = _pair_cols(a_col, l_b + 2 * p, lt_dh)
                v_pair = z_ref[rows, C_V + LANES * p:C_V + LANES * (p + 1)]
                rhs = jnp.concatenate([v_pair * a_pair, a_pair], axis=1).astype(BF16)
                k_pair = (z_ref[rows, C_K + LANES * p:C_K + LANES * (p + 1)] * (DH ** -0.5)).astype(BF16)
                upd = lax.dot_general(k_pair, rhs, (((0,), (0,)), ((), ())),
                                      preferred_element_type=F32)
                upd = jnp.where(bd_mask, upd, 0.0)
                if bd[p] is None:
                    bd[p] = upd
                else:
                    scale = jnp.where(bd_row < DH, a_prev[:, l_m + 2 * p:l_m + 2 * p + 1],
                                      a_prev[:, l_m + 2 * p + 1:l_m + 2 * p + 2])
                    bd[p] = scale * bd[p] + upd
        fin[d] = bd

    for c in range(nc):
        rows = slice(HALO + c * CHUNK, HALO + (c + 1) * CHUNK)
        cols = cols_ref[c]
        grow = grow_ref[c]
        m_prev = jnp.where(lane_row < L_BB, m_start[0][c], m_start[1][c])
        m_all = jnp.maximum(cols, m_prev)
        w_inter = jnp.exp(m_prev - m_all)
        e_neg = jnp.exp(-(pltpu.roll(cols, L_STEP, axis=1) + m_all))
        for p in range(PAIRS):
            q_pair = z_ref[rows, C_Q + LANES * p:C_Q + LANES * (p + 1)].astype(BF16)
            k_f32 = z_ref[rows, C_K + LANES * p:C_K + LANES * (p + 1)] * (DH ** -0.5)
            v_pair = z_ref[rows, C_V + LANES * p:C_V + LANES * (p + 1)]
            p_cat = ([], [])
            va = []
            for j in range(2):
                h = 2 * p + j
                own = lt_dh if j == 0 else jnp.logical_not(lt_dh)
                k_own = jnp.where(own, k_f32, 0.0).astype(BF16)
                s = lax.dot_general(q_pair, k_own, (((1,), (1,)), ((), ())),
                                    preferred_element_type=F32)
                va.append(jnp.concatenate([jnp.where(own, v_pair, 0.0), jnp.where(own, 1.0, 0.0)],
                                          axis=1).astype(BF16))
                for d, _, l_m, _ in dirs:
                    w = jnp.exp(jnp.where(tri[d], grow[8 * d + h:8 * d + h + 1] - m_all[:, l_m + h:l_m + h + 1],
                                          -jnp.inf))
                    p_cat[d].append((s * w).astype(BF16))
            va = jnp.concatenate(va, axis=0)
            h_sum = None
            for d, _, l_m, _ in dirs:
                tot = jnp.dot(jnp.concatenate(p_cat[d], axis=1), va, preferred_element_type=F32)
                if have_state[d][c]:
                    e_pair = _pair_cols(w_inter, l_m + 2 * p, lt_dh)
                    qc = jnp.dot(q_pair, bd_ref[d, c, p], preferred_element_type=F32)
                    tot = tot + jnp.concatenate([e_pair, e_pair], axis=1) * qc
                floor = _pair_cols(e_neg, l_m + 2 * p, lt_dh)
                h_dir = tot[:, :LANES] / jnp.maximum(jnp.abs(tot[:, LANES:]), floor)
                h_sum = h_dir if h_sum is None else h_sum + h_dir
            o_pair = z_ref[rows, C_O + LANES * p:C_O + LANES * (p + 1)]
            ycat_ref[c * CHUNK:(c + 1) * CHUNK, LANES * p:LANES * (p + 1)] = (
                _sigmoid(o_pair) * h_sum).astype(BF16)

    m_row = jnp.where(lane_row < L_BB, m_fin[0], m_fin[1])
    return fin, m_row


def _pair_state(c_ref, n_ref, q, d, p):
    eye = (lax.broadcasted_iota(jnp.int32, (DH, DH), 0) == lax.broadcasted_iota(jnp.int32, (DH, DH), 1))
    zero = jnp.zeros((DH, DH), F32)
    blocks = []
    for j in range(2):
        h = 2 * p + j
        n_row = n_ref[q, 0, d, h:h + 1, :]
        n_col = jnp.sum(jnp.where(eye, n_row, 0.0), axis=1, keepdims=True)
        row = [zero] * 4
        row[j] = c_ref[q, 0, d, h]
        row[2 + j] = jnp.broadcast_to(n_col, (DH, DH))
        blocks.append(jnp.concatenate(row, axis=1))
    return jnp.concatenate(blocks, axis=0)


def _mixer_kernel(*refs, layer, seq_len, nseq, add_pos, has_state, emit_state, cond_base, n_prev):
    it = iter(refs)
    x_ref = next(it)
    if add_pos:
        er_ref = next(it)
        ec_ref = next(it)
    mod_ref = next(it)
    ng_ref = next(it)
    win_ref = next(it)
    gb_ref = next(it)
    ws_ref = next(it)
    gbias_ref = next(it)
    pbd_ref = next(it)
    psc_ref = next(it)
    cw_ref = next(it)
    wout_ref = next(it)
    if has_state:
        c0_ref = next(it)
        n0_ref = next(it)
        m0_ref = next(it)
    if n_prev:
        cprev_ref = next(it)
        nprev_ref = next(it)
    x1_ref = next(it)
    h2_ref = next(it)
    if emit_state:
        cout_ref = next(it)
        nout_ref = next(it)
        mout_ref = next(it)
    z_all = next(it)
    gcol_all = next(it)
    grow_all = next(it)
    cols_all = next(it)
    bd_all = next(it)
    ycat_all = next(it)

    nblk = seq_len // CHUNK
    cond = cond_base if cond_base == 0 else cond_base + pl.program_id(0)
    sh1, sc1, gt1, sh2, sc2, _ = _mod_rows(mod_ref, cond)
    ng = ng_ref[0]
    seqs = range(nseq)
    units = [(q, r) for q in seqs for r in range(nblk)]

    zeros_halo = jnp.zeros((HALO, Z_COLS - C_PZ), F32)
    for q in seqs:
        z_all[q, 0:HALO, C_PZ:Z_COLS] = zeros_halo
        z_all[q, HALO + seq_len:2 * HALO + seq_len, C_PZ:Z_COLS] = zeros_halo

    g_in = ng[0:1] * (1.0 + sc1)
    hbs = []
    for q, r in units:
        rows = slice(r * CHUNK, (r + 1) * CHUNK)
        xb = x_ref[q, rows, :]
        if add_pos:
            pieces = []
            for g in range(CHUNK // GRID_W):
                gi = r * (CHUNK // GRID_W) + g
                er = jnp.broadcast_to(er_ref[gi:gi + 1, :], (GRID_W, D_MODEL // 2))
                pieces.append(jnp.concatenate([er, ec_ref[...]], axis=1))
            xb = xb + jnp.concatenate(pieces, axis=0)
            x1_ref[q, rows, :] = xb
        hbs.append((_rms(xb, g_in) + sh1).astype(BF16))
    hb = jnp.concatenate(hbs, axis=0)
    gates_all = jnp.dot(hb, win_ref[0, :, C_GATE:W_COLS], preferred_element_type=F32) + gb_ref[0]
    for cg in range(Z_COLS // Z_GROUP):
        zc = jnp.dot(hb, win_ref[0, :, cg * Z_GROUP:(cg + 1) * Z_GROUP], preferred_element_type=F32)
        for u, (q, r) in enumerate(units):
            z_all[q, HALO + r * CHUNK:HALO + (r + 1) * CHUNK, cg * Z_GROUP:(cg + 1) * Z_GROUP] = (
                zc[u * CHUNK:(u + 1) * CHUNK])
    for u, (q, r) in enumerate(units):
        gates = gates_all[u * CHUNK:(u + 1) * CHUNK]
        gcol_all[q, r] = gates
        grow, cols = _gate_stats(gates)
        grow_all[q, r] = grow
        cols_all[q, r] = cols

    for q in seqs:
        if has_state:
            state0 = [[_pair_state(c0_ref, n0_ref, q, d, p) for p in range(PAIRS)] for d in range(2)]
            m0 = m0_ref[q, 0]
        else:
            state0 = [[None] * PAIRS, [None] * PAIRS]
            m0 = jnp.zeros((1, LANES), F32)
        fin, m_row = _mlstm(z_all.at[q], gcol_all.at[q], grow_all.at[q], cols_all.at[q], bd_all.at[q],
                            ycat_all.at[q], nblk, state0, m0, need_final=emit_state)
        if emit_state:
            for d in range(2):
                for p in range(PAIRS):
                    for j in range(2):
                        cout_ref[q, n_prev, d, 2 * p + j] = fin[d][p][DH * j:DH * (j + 1), DH * j:DH * (j + 1)]
                    n_t = fin[d][p][:, LANES:].T
                    nout_ref[q, n_prev, d, 2 * p:2 * p + 1, :] = n_t[0:1, 0:DH]
                    nout_ref[q, n_prev, d, 2 * p + 1:2 * p + 2, :] = n_t[DH:DH + 1, DH:LANES]
            mout_ref[q] = m_row
    if emit_state and n_prev:
        cout_ref[:, 0:n_prev] = cprev_ref[...]
        nout_ref[:, 0:n_prev] = nprev_ref[...]

    lane_g = lax.broadcasted_iota(jnp.int32, (CHUNK, D_GROUP), 1) // (D_GROUP // 4)
    lg = lax.broadcasted_iota(jnp.int32, (GMLP_CHUNK, D_GROUP), 1) // (D_GROUP // 4)
    row_c = lax.broadcasted_iota(jnp.int32, (CHUNK, D_GROUP), 0)
    half = jnp.where(lane_g == 0, 1, jnp.where(lane_g == 1, 2, jnp.where(lane_g == 2, 4, 8)))
    cw = cw_ref[0]
    ws_cat = jnp.concatenate([ws_ref[0, g] for g in range(4)], axis=1).astype(BF16)
    for q, r in units:
        z_ref = z_all.at[q]
        ycat_ref = ycat_all.at[q]
        rows = slice(r * CHUNK, (r + 1) * CHUNK)
        zrows = slice(HALO + r * CHUNK, HALO + (r + 1) * CHUNK)
        hrows = slice(r * CHUNK, (r + 1) * CHUNK + 2 * HALO)
        inner = slice(HALO, HALO + CHUNK)

        for s in range(CHUNK // GMLP_CHUNK):
            zs = slice(HALO + r * CHUNK + s * GMLP_CHUNK, HALO + r * CHUNK + (s + 1) * GMLP_CHUNK)
            vch = z_ref[zs, C_GV:C_GV + D_GROUP]
            v_stack = jnp.concatenate([jnp.where(lg == g, vch, 0.0) for g in range(4)], axis=0)
            mixed = gbias_ref[0] + jnp.dot(ws_cat, v_stack.astype(BF16), preferred_element_type=F32)
            yb = z_ref[zs, C_GU:C_GU + D_GROUP] * mixed
            ys = slice(r * CHUNK + s * GMLP_CHUNK, r * CHUNK + (s + 1) * GMLP_CHUNK)
            ycat_ref[ys, D_GROUP:2 * D_GROUP] = yb.astype(BF16)

        pz = z_ref[hrows, C_PZ:C_PZ + D_GROUP]
        a2 = pz + _shift_down(pz, 1)
        a4 = a2 + _shift_down(a2, 2)
        a8 = a4 + _shift_down(a4, 4)
        a16 = a8 + _shift_down(a8, 8)
        win = [a2[inner], _shift_up(a4, 1)[inner], _shift_up(a8, 3)[inner], _shift_up(a16, 7)[inner]]
        wsum = jnp.where(lane_g == 0, win[0], jnp.where(lane_g == 1, win[1],
                         jnp.where(lane_g == 2, win[2], win[3])))
        t_abs = row_c + r * CHUNK
        cnt = (jnp.minimum(t_abs + half, seq_len) - jnp.maximum(t_abs - half, 0)).astype(F32)
        pooled = wsum / cnt - pz[inner]
        yc = (jnp.dot(pooled.astype(BF16), pbd_ref[0], preferred_element_type=F32)
              * psc_ref[layer:layer + 1, :])
        ycat_ref[rows, 2 * D_GROUP:3 * D_GROUP] = yc.astype(BF16)

        u = z_ref[hrows, C_CC:C_CC + D_GROUP] * z_ref[hrows, C_CX:C_CX + D_GROUP]
        conv = (cw[0:1] * _shift_down(u, 1)[inner] + cw[1:2] * u[inner]
                + cw[2:3] * _shift_up(u, 1)[inner])
        yd = z_ref[zrows, C_CB:C_CB + D_GROUP] * conv
        ycat_ref[rows, 3 * D_GROUP:4 * D_GROUP] = yd.astype(BF16)

    y_all = jnp.dot(jnp.concatenate([ycat_all[q, r * CHUNK:(r + 1) * CHUNK, :] for q, r in units], axis=0),
                    wout_ref[0], preferred_element_type=F32)
    g_y = gt1 * ng[1:2]
    g_h2 = ng[2:3] * (1.0 + sc2)
    for u, (q, r) in enumerate(units):
        rows = slice(r * CHUNK, (r + 1) * CHUNK)
        xb = x1_ref[q, rows, :] if add_pos else x_ref[q, rows, :]
        x1 = xb + _rms(y_all[u * CHUNK:(u + 1) * CHUNK], g_y)
        x1_ref[q, rows, :] = x1
        h2_ref[q, rows, :] = (_rms(x1, g_h2) + sh2).astype(BF16)


def _layer_spec(shape, layer):
    nd = len(shape)
    return pl.BlockSpec((1,) + tuple(shape[1:]), lambda b: (layer,) + (0,) * (nd - 1),
                        pipeline_mode=pl.Buffered(1))


def _mixer_call(x, layer, mod_all, wts, cond_base, pos=None, state=None, emit_state=False,
                prev_states=None, nseq=1):
    bsz, seq_len, _ = x.shape
    nblk = seq_len // CHUNK
    add_pos = pos is not None
    has_state = state is not None
    n_prev = 0 if prev_states is None else prev_states[0].shape[1]
    args = [x]
    in_specs = [pl.BlockSpec((nseq, seq_len, D_MODEL), lambda b: (b, 0, 0))]
    if add_pos:
        args += list(pos)
        in_specs += [pl.BlockSpec(p.shape, lambda b: (0, 0), pipeline_mode=pl.Buffered(1)) for p in pos]
    args.append(mod_all)
    in_specs.append(_layer_spec(mod_all.shape, layer))
    for name in ("norm_g", "w_in", "gate_b", "gmlp_ws", "gmlp_bias", "pool_bd", "pool_scale",
                 "conv_w", "w_out"):
        args.append(wts[name])
        if wts[name].ndim == 2:
            in_specs.append(pl.BlockSpec(wts[name].shape, lambda b: (0, 0), pipeline_mode=pl.Buffered(1)))
        else:
            in_specs.append(_layer_spec(wts[name].shape, layer))
    if has_state:
        args += list(state)
        in_specs += [pl.BlockSpec((nseq, 1, 2, HEADS, DH, DH), lambda b: (b, layer, 0, 0, 0, 0)),
                     pl.BlockSpec((nseq, 1, 2, HEADS, DH), lambda b: (b, layer, 0, 0, 0)),
                     pl.BlockSpec((nseq, 1, 1, LANES), lambda b: (b, layer, 0, 0))]
    out_shape = [jax.ShapeDtypeStruct((bsz, seq_len, D_MODEL), F32),
                 jax.ShapeDtypeStruct((bsz, seq_len, D_MODEL), BF16)]
    out_specs = [pl.BlockSpec((nseq, seq_len, D_MODEL), lambda b: (b, 0, 0)),
                 pl.BlockSpec((nseq, seq_len, D_MODEL), lambda b: (b, 0, 0))]
    if emit_state:
        if n_prev:
            args += list(prev_states)
            in_specs += [pl.BlockSpec((nseq, n_prev, 2, HEADS, DH, DH), lambda b: (b, 0, 0, 0, 0, 0)),
                         pl.BlockSpec((nseq, n_prev, 2, HEADS, DH), lambda b: (b, 0, 0, 0, 0))]
        out_shape += [jax.ShapeDtypeStruct((bsz, n_prev + 1, 2, HEADS, DH, DH), F32),
                      jax.ShapeDtypeStruct((bsz, n_prev + 1, 2, HEADS, DH), F32),
                      jax.ShapeDtypeStruct((bsz, 1, LANES), F32)]
        out_specs += [pl.BlockSpec((nseq, n_prev + 1, 2, HEADS, DH, DH), lambda b: (b, 0, 0, 0, 0, 0)),
                      pl.BlockSpec((nseq, n_prev + 1, 2, HEADS, DH), lambda b: (b, 0, 0, 0, 0)),
                      pl.BlockSpec((nseq, 1, LANES), lambda b: (b, 0, 0))]
    kern = functools.partial(_mixer_kernel, layer=layer, seq_len=seq_len, nseq=nseq, add_pos=add_pos,
                             has_state=has_state, emit_state=emit_state, cond_base=cond_base,
                             n_prev=n_prev)
    return pl.pallas_call(
        kern,
        out_shape=out_shape,
        grid=(bsz // nseq,),
        in_specs=in_specs,
        out_specs=out_specs,
        scratch_shapes=[pltpu.VMEM((nseq, seq_len + 2 * HALO, Z_COLS), F32),
                        pltpu.VMEM((nseq, nblk, CHUNK, LANES), F32),
                        pltpu.VMEM((nseq, nblk, 16, CHUNK), F32),
                        pltpu.VMEM((nseq, nblk, CHUNK, LANES), F32),
                        pltpu.VMEM((nseq, 2, nblk, PAIRS, LANES, 2 * LANES), BF16),
                        pltpu.VMEM((nseq, seq_len, D_MODEL), BF16)],
        compiler_params=pltpu.CompilerParams(
            dimension_semantics=("arbitrary",), vmem_limit_bytes=VMEM_LIMIT),
        name="mixer_t%d" % seq_len,
    )(*args)


def _ffn_kernel(mod_ref, ng_ref, cw_ref, x1p_hbm, h2p_hbm, x1s_hbm, h2s_hbm, up_hbm, down_hbm,
                outp_hbm, outs_hbm,
                up_bf, down_bf, act_ref, h2_buf, x1_buf, o_buf, up_stage, down_stage,
                in_sems, out_sems, w_sems, *, layer, tiles_p, tiles_s, seq_p, seq_s):
    i = pl.program_id(0)
    n_tiles = tiles_p + tiles_s
    rows = FFN_ROWS
    half = FFN_SUB // 2
    nsub = D_FF // half
    slot = i % 2
    is_prompt = i < tiles_p

    def tile_rows(tile, first_tile):
        if isinstance(tile, int):
            return pl.ds((tile - first_tile) * rows, rows)
        return pl.ds(pl.multiple_of((tile - first_tile) * rows, rows), rows)

    def h2_copy(h2_hbm, r, buf):
        return pltpu.make_async_copy(h2_hbm.at[r], h2_buf.at[buf], in_sems.at[buf])

    def x1_copy(x1_hbm, r):
        return pltpu.make_async_copy(x1_hbm.at[r], x1_buf, in_sems.at[2])

    def fetch_h2(tile, buf):
        @pl.when(tile < tiles_p)
        def _():
            h2_copy(h2p_hbm, tile_rows(tile, 0), buf).start()

        @pl.when(tile >= tiles_p)
        def _():
            h2_copy(h2s_hbm, tile_rows(tile, tiles_p), buf).start()

    def fetch_x1(tile):
        @pl.when(tile < tiles_p)
        def _():
            x1_copy(x1p_hbm, tile_rows(tile, 0)).start()

        @pl.when(tile >= tiles_p)
        def _():
            x1_copy(x1s_hbm, tile_rows(tile, tiles_p)).start()

    def await_store(buf):
        pltpu.make_async_copy(o_buf.at[buf], outp_hbm.at[pl.ds(0, rows)], out_sems.at[buf]).wait()

    up_rows = up_stage.shape[1]
    down_rows = down_stage.shape[1]

    def slab(c, n):
        return pl.ds(c * n, n) if isinstance(c, int) else pl.ds(pl.multiple_of(c * n, n), n)

    def weight_copies(c, buf):
        return (pltpu.make_async_copy(up_hbm.at[layer, slab(c, up_rows), :], up_stage.at[buf],
                                      w_sems.at[buf, 0]),
                pltpu.make_async_copy(down_hbm.at[layer, slab(c, down_rows), :], down_stage.at[buf],
                                      w_sems.at[buf, 1]))

    @pl.when(i == 0)
    def _():
        h2_copy(h2p_hbm, tile_rows(0, 0), 0).start()
        depth = up_stage.shape[0]
        for c0 in range(depth - 1):
            for cp in weight_copies(c0, c0):
                cp.start()

        def convert(c, carry):
            buf = c % depth
            ahead = c + depth - 1

            @pl.when(ahead < FFN_WEIGHT_CHUNKS)
            def _():
                for cp in weight_copies(ahead, ahead % depth):
                    cp.start()

            for cp in weight_copies(c, buf):
                cp.wait()
            r = slab(c, up_rows)
            for s in range(nsub):
                up_bf[s, r, 0:half] = up_stage[buf, :, s * half:(s + 1) * half].astype(BF16)
                up_bf[s, r, half:FFN_SUB] = (
                    up_stage[buf, :, D_FF + s * half:D_FF + (s + 1) * half].astype(BF16))
            down_bf[slab(c, down_rows), :] = down_stage[buf].astype(BF16)
            return carry

        lax.fori_loop(0, FFN_WEIGHT_CHUNKS, convert, 0)

    h2_copy(h2p_hbm, pl.ds(0, rows), slot).wait()

    @pl.when(i >= 2)
    def _():
        await_store(slot)

    fetch_x1(i)

    @pl.when(i + 1 < n_tiles)
    def _():
        fetch_h2(i + 1, 1 - slot)

    row = lax.broadcasted_iota(jnp.int32, (rows, FFN_SUB), 0)
    first = ((row & (seq_s - 1)) == 0) | (((row & (seq_p - 1)) == 0) & is_prompt)
    last = ((row & (seq_s - 1)) == seq_s - 1) | (((row & (seq_p - 1)) == seq_p - 1) & is_prompt)
    h2 = h2_buf[slot]
    for s in range(nsub):
        a = jnp.dot(h2, up_bf[s], preferred_element_type=F32)
        cw = jnp.concatenate([cw_ref[0, :, s * half:(s + 1) * half],
                              0.5 * cw_ref[0, :, D_FF + s * half:D_FF + (s + 1) * half]], axis=1)
        a_prev = jnp.where(first, 0.0, pltpu.roll(a, 1, axis=0))
        a_next = jnp.where(last, 0.0, pltpu.roll(a, rows - 1, axis=0))
        ac = cw[0:1] * a_prev + cw[1:2] * a + cw[2:3] * a_next
        hg = ac[:, half:]
        act = (hg * (1.0 + jnp.tanh(hg))) * ac[:, :half]
        act_ref[:, s * half:(s + 1) * half] = act.astype(BF16)
    y = jnp.dot(act_ref[...], down_bf[...], preferred_element_type=F32)
    cond = jnp.where(is_prompt, 0, i - tiles_p + 1)
    gt2 = _mod_rows(mod_ref, cond)[5]

    res = gt2 * _rms(y, ng_ref[0][3:4])
    x1_copy(x1p_hbm, pl.ds(0, rows)).wait()
    o_buf[slot] = x1_buf[...] + res

    @pl.when(is_prompt)
    def _():
        pltpu.make_async_copy(o_buf.at[slot], outp_hbm.at[tile_rows(i, 0)], out_sems.at[slot]).start()

    @pl.when(jnp.logical_not(is_prompt))
    def _():
        pltpu.make_async_copy(o_buf.at[slot], outs_hbm.at[tile_rows(i, tiles_p)], out_sems.at[slot]).start()

    @pl.when(i == n_tiles - 1)
    def _():
        await_store(1 - slot)
        await_store(slot)


def _ffn_call(x1p, h2p, x1s, h2s, layer, mod_all, wts, seq_p, seq_s):
    tiles_p = x1p.shape[0] // FFN_ROWS
    tiles_s = x1s.shape[0] // FFN_ROWS
    assert tiles_p >= 1 and tiles_s >= 1 and tiles_p + tiles_s >= 2
    half = FFN_SUB // 2

    def resident(shape):
        nd = len(shape)
        return pl.BlockSpec((1,) + tuple(shape[1:]), lambda i: (layer,) + (0,) * (nd - 1),
                            pipeline_mode=pl.Buffered(1))

    hbm = pl.BlockSpec(memory_space=pl.ANY)
    return pl.pallas_call(
        functools.partial(_ffn_kernel, layer=layer, tiles_p=tiles_p, tiles_s=tiles_s,
                          seq_p=seq_p, seq_s=seq_s),
        out_shape=[jax.ShapeDtypeStruct(x1p.shape, F32), jax.ShapeDtypeStruct(x1s.shape, F32)],
        grid=(tiles_p + tiles_s,),
        in_specs=[resident(mod_all.shape), resident(wts["norm_g"].shape), resident(wts["ffn_conv"].shape),
                  hbm, hbm, hbm, hbm, hbm, hbm],
        out_specs=[hbm, hbm],
        scratch_shapes=[pltpu.VMEM((D_FF // half, D_MODEL, FFN_SUB), BF16),
                        pltpu.VMEM((D_FF, D_MODEL), BF16),
                        pltpu.VMEM((FFN_ROWS, D_FF), BF16),
                        pltpu.VMEM((2, FFN_ROWS, D_MODEL), BF16),
                        pltpu.VMEM((FFN_ROWS, D_MODEL), F32),
                        pltpu.VMEM((2, FFN_ROWS, D_MODEL), F32),
                        pltpu.VMEM((FFN_WEIGHT_BUFS, D_MODEL // FFN_WEIGHT_CHUNKS, 2 * D_FF), F32),
                        pltpu.VMEM((FFN_WEIGHT_BUFS, D_FF // FFN_WEIGHT_CHUNKS, D_MODEL), F32),
                        pltpu.SemaphoreType.DMA((3,)),
                        pltpu.SemaphoreType.DMA((2,)),
                        pltpu.SemaphoreType.DMA((FFN_WEIGHT_BUFS, 2))],
        compiler_params=pltpu.CompilerParams(
            dimension_semantics=("arbitrary",), vmem_limit_bytes=VMEM_LIMIT),
        name="ffn",
    )(mod_all, wts["norm_g"], wts["ffn_conv"], x1p, h2p, x1s, h2s, wts["ffn_up"], wts["ffn_down"])


def _pos_tables(t_len):
    quarter = D_MODEL // 4
    omega = (1.0 / (np.float32(POS_BASE) ** (np.arange(quarter, dtype=np.float32) / np.float32(quarter))))
    omega = omega.astype(np.float32)

    def emb(p):
        a = p.astype(np.float32)[:, None] * omega[None, :]
        return np.concatenate([np.sin(a), np.cos(a)], axis=-1).astype(np.float32)

    return jnp.asarray(emb(np.arange(t_len // GRID_W))), jnp.asarray(emb(np.arange(GRID_W)))


def _stabiliser_rows(state_m):
    bs = state_m.shape[0]
    m = jnp.pad(state_m.astype(F32), ((0, 0), (0, 0), (0, 0), (0, 16 - HEADS)))
    return jnp.pad(m.reshape(bs, DEPTH, 1, 32), ((0, 0), (0, 0), (0, 0), (L_MF, LANES - 32 - L_MF)))


def kernel(x_prompt, x_sample, state_C, state_n, state_m, c, c_ctx, norm_g, ada_w, ada_b, w_in,
           gate_b, gmlp_ws, gmlp_b, pool_w, pool_scale, conv_w, w_out, ffn_up, ffn_conv, ffn_down):
    bp, tp, _ = x_prompt.shape
    bs, ts, _ = x_sample.shape
    assert tp == CHUNK and ts % CHUNK == 0 and FFN_ROWS % tp == 0 and ts == FFN_ROWS

    conds = jnp.concatenate([c_ctx[None, :], c, jnp.zeros((7 - bs, D_MODEL), F32)], axis=0)
    mod_all = _modulation(conds, ada_w, ada_b)
    win_p, wout_p = _prep_weights(w_in, w_out)

    gb = jnp.pad(gate_b, ((0, 0), (0, 0), (0, 8 - HEADS))).reshape(DEPTH, 1, 32)
    dg = D_GROUP // 4
    wts = {
        "norm_g": norm_g,
        "w_in": win_p,
        "gate_b": jnp.pad(gb, ((0, 0), (0, 0), (0, LANES - 32))),
        "gmlp_ws": gmlp_ws,
        "gmlp_bias": jnp.repeat(jnp.swapaxes(gmlp_b, 1, 2), dg, axis=2),
        "pool_bd": jnp.einsum("lgcd,gh->lgchd", pool_w, jnp.eye(4, dtype=F32)).reshape(
            DEPTH, D_GROUP, D_GROUP).astype(BF16),
        "pool_scale": pool_scale,
        "conv_w": conv_w,
        "w_out": wout_p,
        "ffn_up": ffn_up,
        "ffn_conv": ffn_conv,
        "ffn_down": ffn_down,
    }
    pos = _pos_tables(ts)
    state = (state_C.astype(F32), state_n.astype(F32), _stabiliser_rows(state_m))

    xp, xs = x_prompt, x_sample
    prev, ms = None, []
    for l in range(DEPTH):
        x1p, h2p, c_all, n_all, m_fin = _mixer_call(xp, l, mod_all, wts, 0, emit_state=True,
                                                    prev_states=prev, nseq=MIXER_ROWS // tp)
        prev = (c_all, n_all)
        ms.append(jnp.stack([m_fin[:, 0, L_MF:L_MF + HEADS], m_fin[:, 0, L_MB:L_MB + HEADS]], axis=1))

        x1s, h2s = _mixer_call(xs, l, mod_all, wts, 1, pos=pos if l == 0 else None, state=state)

        xp, xs = _ffn_call(x1p.reshape(bp * tp, D_MODEL), h2p.reshape(bp * tp, D_MODEL),
                           x1s.reshape(bs * ts, D_MODEL), h2s.reshape(bs * ts, D_MODEL),
                           l, mod_all, wts, tp, ts)
        xp = xp.reshape(bp, tp, D_MODEL)
        xs = xs.reshape(bs, ts, D_MODEL)

    return (xp, xs, c_all, n_all, jnp.stack(ms, axis=1))
```

```python
import functools

import numpy as np
import jax
import jax.numpy as jnp
from jax import lax
from jax.experimental import pallas as pl
from jax.experimental.pallas import tpu as pltpu

F32 = jnp.float32
BF16 = jnp.bfloat16

D_MODEL = 1024
DEPTH = 2
HEADS = 4
PAIRS = HEADS // 2
DH = 64
D_GROUP = 256
D_FF = 2816
N_MOD = 6
EPS = 1e-6
GRID_W = 64
POS_BASE = 10000.0
D_IN = 2576
N_GATE = 4 * HEADS

CHUNK = 256
GMLP_CHUNK = 128
HALO = 8
LANES = 128

C_Q, C_K, C_V, C_O = 0, 256, 512, 768
C_GU, C_GV, C_PZ, C_CB, C_CC, C_CX = 1024, 1280, 1536, 1792, 2048, 2304
Z_COLS = 2560
C_GATE = Z_COLS
W_COLS = Z_COLS + LANES
Z_GROUP = 512
L_BF, L_MF, L_BB, L_MB = 0, 8, 16, 24
L_STEP = L_MF - L_BF

MIXER_ROWS = 1024
FFN_ROWS = 1024
FFN_GROUP = FFN_ROWS // 8
FFN_SUB = 256
FFN_WEIGHT_CHUNKS = 16
FFN_WEIGHT_BUFS = 4
VMEM_LIMIT = 60 * 1024 * 1024


def _rms(x, g):
    ms = jnp.mean(x * x, axis=-1, keepdims=True)
    return x * lax.rsqrt(ms + EPS) * g


def _sigmoid(x):
    return 1.0 / (1.0 + jnp.exp(-x))


def _log_sigmoid(x):
    return jnp.minimum(x, 0.0) - jnp.log1p(jnp.exp(-jnp.abs(x)))


def _scan_lanes(x, op, reverse, fill):
    n = x.shape[1]
    lane = lax.broadcasted_iota(jnp.int32, x.shape, 1)
    k = 1
    while k < n:
        if reverse:
            sh = jnp.where(lane < n - k, pltpu.roll(x, n - k, axis=1), fill)
        else:
            sh = jnp.where(lane >= k, pltpu.roll(x, k, axis=1), fill)
        x = op(x, sh)
        k *= 2
    return x


def _shift_down(x, k):
    return pltpu.roll(x, k, axis=0)


def _shift_up(x, k):
    return pltpu.roll(x, x.shape[0] - k, axis=0)


def _prep_kernel(wt_ref, sel_ref, wout_ref, win_o_ref, wout_o_ref):
    for n in range(Z_COLS // LANES):
        r0 = LANES * n if n < 4 * D_GROUP // LANES else LANES * n + N_GATE
        win_o_ref[0, :, LANES * n:LANES * (n + 1)] = wt_ref[0, r0:r0 + LANES, :].T.astype(BF16)
    g = wt_ref[0, 4 * D_GROUP:4 * D_GROUP + LANES, :].T.astype(BF16)
    win_o_ref[0, :, C_GATE:W_COLS] = jnp.dot(g, sel_ref[...], preferred_element_type=F32).astype(BF16)
    wout_o_ref[...] = wout_ref[...].astype(BF16)


def _prep_weights(w_in, w_out):
    sel = np.zeros((LANES, LANES), np.float32)
    for j in range(N_GATE):
        sel[j, 8 * (j // HEADS) + j % HEADS] = 1.0
    rows = D_MODEL // 2
    return pl.pallas_call(
        _prep_kernel,
        out_shape=[jax.ShapeDtypeStruct((DEPTH, D_MODEL, W_COLS), BF16),
                   jax.ShapeDtypeStruct(w_out.shape, BF16)],
        grid=(DEPTH, D_MODEL // rows),
        in_specs=[pl.BlockSpec((1, D_IN, rows), lambda l, i: (l, 0, i)),
                  pl.BlockSpec((LANES, LANES), lambda l, i: (0, 0)),
                  pl.BlockSpec((1, rows, D_MODEL), lambda l, i: (l, i, 0))],
        out_specs=[pl.BlockSpec((1, rows, W_COLS), lambda l, i: (l, i, 0)),
                   pl.BlockSpec((1, rows, D_MODEL), lambda l, i: (l, i, 0))],
        compiler_params=pltpu.CompilerParams(dimension_semantics=("arbitrary", "arbitrary"),
                                             vmem_limit_bytes=VMEM_LIMIT),
        name="prep_mixer_w")(jnp.swapaxes(w_in, 1, 2), jnp.asarray(sel, BF16), w_out)


def _mod_kernel(c_ref, w_ref, b_ref, o_ref):
    c = c_ref[...]
    s = (c * _sigmoid(c)).astype(BF16)
    bias = b_ref[pl.ds(pl.program_id(0), 1), :]
    o_ref[0] = jnp.dot(s, w_ref[0].astype(BF16), preferred_element_type=F32) + bias


def _modulation(conds, ada_w, ada_b):
    n_out = N_MOD * D_MODEL
    bn = 1536
    return pl.pallas_call(
        _mod_kernel,
        out_shape=jax.ShapeDtypeStruct((DEPTH, 8, n_out), F32),
        grid=(DEPTH, n_out // bn),
        in_specs=[
            pl.BlockSpec((8, D_MODEL), lambda l, j: (0, 0)),
            pl.BlockSpec((1, D_MODEL, bn), lambda l, j: (l, 0, j)),
            pl.BlockSpec((DEPTH, bn), lambda l, j: (0, j)),
        ],
        out_specs=pl.BlockSpec((1, 8, bn), lambda l, j: (l, 0, j)),
        compiler_params=pltpu.CompilerParams(
            dimension_semantics=("arbitrary", "arbitrary"), vmem_limit_bytes=VMEM_LIMIT),
        name="adaln_mod",
    )(conds, ada_w, ada_b)


def _mod_rows(mod_ref, cond):
    row = mod_ref[0, pl.ds(cond, 1), :]
    return [row[:, i * D_MODEL:(i + 1) * D_MODEL] for i in range(N_MOD)]


def _gate_stats(gates):
    gt = gates.T
    b_f = _scan_lanes(_log_sigmoid(gt[8:16]), jnp.add, False, 0.0)
    g_f = gt[0:8] - b_f
    cm_f = _scan_lanes(g_f, jnp.maximum, False, -jnp.inf)
    b_b = _scan_lanes(_log_sigmoid(gt[24:32]), jnp.add, True, 0.0)
    g_b = gt[16:24] - b_b
    cm_b = _scan_lanes(g_b, jnp.maximum, True, -jnp.inf)
    stack = jnp.concatenate([b_f, cm_f, b_b, cm_b, jnp.zeros((LANES - 32, CHUNK), F32)], axis=0)
    return jnp.concatenate([g_f, g_b], axis=0), stack.T


def _pair_cols(mat, lane0, lane_lt_dh):
    return jnp.where(lane_lt_dh, mat[:, lane0:lane0 + 1], mat[:, lane0 + 1:lane0 + 2])


def _mlstm(z_ref, gcol_ref, grow_ref, cols_ref, bd_ref, ycat_ref, nc, state0, m0, need_final):
    lane_row = lax.broadcasted_iota(jnp.int32, (1, LANES), 1)
    lane = lax.broadcasted_iota(jnp.int32, (CHUNK, LANES), 1)
    lt_dh = lane < DH
    row_ll = lax.broadcasted_iota(jnp.int32, (CHUNK, CHUNK), 0)
    col_ll = lax.broadcasted_iota(jnp.int32, (CHUNK, CHUNK), 1)
    tri = (col_ll <= row_ll, col_ll >= row_ll)
    bd_row = lax.broadcasted_iota(jnp.int32, (LANES, 2 * LANES), 0)
    bd_col = lax.broadcasted_iota(jnp.int32, (LANES, 2 * LANES), 1)
    bd_mask = (bd_row < DH) == ((bd_col & (LANES - 1)) < DH)
    dirs = ((0, L_BF, L_MF, CHUNK - 1), (1, L_BB, L_MB, 0))

    m_start = [[None] * nc, [None] * nc]
    m_fin = [None, None]
    for d, _, _, last in dirs:
        m = m0
        order = range(nc) if d == 0 else range(nc - 1, -1, -1)
        for step, c in enumerate(order):
            m_start[d][c] = m
            if need_final or step < nc - 1:
                cl = cols_ref[c, last:last + 1, :]
                m = pltpu.roll(cl, L_STEP, axis=1) + jnp.maximum(m, cl)
        m_fin[d] = m

    have_state = [[False] * nc, [False] * nc]
    fin = [None, None]
    for d, l_b, l_m, last in dirs:
        bd = list(state0[d])
        order = range(nc) if d == 0 else range(nc - 1, -1, -1)
        for step, c in enumerate(order):
            if bd[0] is not None:
                have_state[d][c] = True
                for p in range(PAIRS):
                    bd_ref[d, c, p] = bd[p].astype(BF16)
            if not (need_final or step < nc - 1):
                continue
            rows = slice(HALO + c * CHUNK, HALO + (c + 1) * CHUNK)
            cols = cols_ref[c]
            m_last = jnp.maximum(cols[last:last + 1, :], m_start[d][c])
            a_col = jnp.exp(gcol_ref[c] - cols - pltpu.roll(m_last, LANES - L_STEP, axis=1))
            a_prev = jnp.exp(m_start[d][c] - m_last)
            for p in range(PAIRS):
                a_pair = _pair_cols(a_col, l_b + 2 * p, lt_dh)
                v_pair = z_ref[rows, C_V + LANES * p:C_V + LANES * (p + 1)]
                rhs = jnp.concatenate([v_pair * a_pair, a_pair], axis=1).astype(BF16)
                k_pair = (z_ref[rows, C_K + LANES * p:C_K + LANES * (p + 1)] * (DH ** -0.5)).astype(BF16)
                upd = lax.dot_general(k_pair, rhs, (((0,), (0,)), ((), ())),
                                      preferred_element_type=F32)
                upd = jnp.where(bd_mask, upd, 0.0)
                if bd[p] is None:
                    bd[p] = upd
                else:
                    scale = jnp.where(bd_row < DH, a_prev[:, l_m + 2 * p:l_m + 2 * p + 1],
                                      a_prev[:, l_m + 2 * p + 1:l_m + 2 * p + 2])
                    bd[p] = scale * bd[p] + upd
        fin[d] = bd

    for c in range(nc):
        rows = slice(HALO + c * CHUNK, HALO + (c + 1) * CHUNK)
        cols = cols_ref[c]
        grow = grow_ref[c]
        m_prev = jnp.where(lane_row < L_BB, m_start[0][c], m_start[1][c])
        m_all = jnp.maximum(cols, m_prev)
        w_inter = jnp.exp(m_prev - m_all)
        e_neg = jnp.exp(-(pltpu.roll(cols, L_STEP, axis=1) + m_all))
        for p in range(PAIRS):
            q_pair = z_ref[rows, C_Q + LANES * p:C_Q + LANES * (p + 1)].astype(BF16)
            k_f32 = z_ref[rows, C_K + LANES * p:C_K + LANES * (p + 1)] * (DH ** -0.5)
            v_pair = z_ref[rows, C_V + LANES * p:C_V + LANES * (p + 1)]
            p_cat = ([], [])
            va = []
            for j in range(2):
                h = 2 * p + j
                own = lt_dh if j == 0 else jnp.logical_not(lt_dh)
                k_own = jnp.where(own, k_f32, 0.0).astype(BF16)
                s = lax.dot_general(q_pair, k_own, (((1,), (1,)), ((), ())),
                                    preferred_element_type=F32)
                va.append(jnp.concatenate([jnp.where(own, v_pair, 0.0), jnp.where(own, 1.0, 0.0)],
                                          axis=1).astype(BF16))
                for d, _, l_m, _ in dirs:
                    w = jnp.exp(jnp.where(tri[d], grow[8 * d + h:8 * d + h + 1] - m_all[:, l_m + h:l_m + h + 1],
                                          -jnp.inf))
                    p_cat[d].append((s * w).astype(BF16))
            va = jnp.concatenate(va, axis=0)
            h_sum = None
            for d, _, l_m, _ in dirs:
                tot = jnp.dot(jnp.concatenate(p_cat[d], axis=1), va, preferred_element_type=F32)
                if have_state[d][c]:
                    e_pair = _pair_cols(w_inter, l_m + 2 * p, lt_dh)
                    qc = jnp.dot(q_pair, bd_ref[d, c, p], preferred_element_type=F32)
                    tot = tot + jnp.concatenate([e_pair, e_pair], axis=1) * qc
                floor = _pair_cols(e_neg, l_m + 2 * p, lt_dh)
                h_dir = tot[:, :LANES] / jnp.maximum(jnp.abs(tot[:, LANES:]), floor)
                h_sum = h_dir if h_sum is None else h_sum + h_dir
            o_pair = z_ref[rows, C_O + LANES * p:C_O + LANES * (p + 1)]
            ycat_ref[c * CHUNK:(c + 1) * CHUNK, LANES * p:LANES * (p + 1)] = (
                _sigmoid(o_pair) * h_sum).astype(BF16)

    m_row = jnp.where(lane_row < L_BB, m_fin[0], m_fin[1])
    return fin, m_row


def _pair_state(c_ref, n_ref, q, d, p):
    eye = (lax.broadcasted_iota(jnp.int32, (DH, DH), 0) == lax.broadcasted_iota(jnp.int32, (DH, DH), 1))
    zero = jnp.zeros((DH, DH), F32)
    blocks = []
    for j in range(2):
        h = 2 * p + j
        n_row = n_ref[q, 0, d, h:h + 1, :]
        n_col = jnp.sum(jnp.where(eye, n_row, 0.0), axis=1, keepdims=True)
        row = [zero] * 4
        row[j] = c_ref[q, 0, d, h]
        row[2 + j] = jnp.broadcast_to(n_col, (DH, DH))
        blocks.append(jnp.concatenate(row, axis=1))
    return jnp.concatenate(blocks, axis=0)


def _mixer_kernel(*refs, layer, seq_len, nseq, add_pos, has_state, emit_state, cond_base, n_prev):
    it = iter(refs)
    x_ref = next(it)
    if add_pos:
        er_ref = next(it)
        ec_ref = next(it)
    mod_ref = next(it)
    ng_ref = next(it)
    win_ref = next(it)
    gb_ref = next(it)
    ws_ref = next(it)
    gbias_ref = next(it)
    pbd_ref = next(it)
    psc_ref = next(it)
    cw_ref = next(it)
    wout_ref = next(it)
    if has_state:
        c0_ref = next(it)
        n0_ref = next(it)
        m0_ref = next(it)
    if n_prev:
        cprev_ref = next(it)
        nprev_ref = next(it)
    x1_ref = next(it)
    if emit_state:
        cout_ref = next(it)
        nout_ref = next(it)
        mout_ref = next(it)
    z_all = next(it)
    gcol_all = next(it)
    grow_all = next(it)
    cols_all = next(it)
    bd_all = next(it)
    ycat_all = next(it)

    nblk = seq_len // CHUNK
    cond = cond_base if cond_base == 0 else cond_base + pl.program_id(0)
    sh1, sc1, gt1 = _mod_rows(mod_ref, cond)[:3]
    ng = ng_ref[0]
    seqs = range(nseq)
    units = [(q, r) for q in seqs for r in range(nblk)]

    zeros_halo = jnp.zeros((HALO, Z_COLS - C_PZ), F32)
    for q in seqs:
        z_all[q, 0:HALO, C_PZ:Z_COLS] = zeros_halo
        z_all[q, HALO + seq_len:2 * HALO + seq_len, C_PZ:Z_COLS] = zeros_halo

    g_in = ng[0:1] * (1.0 + sc1)
    hbs = []
    for q, r in units:
        rows = slice(r * CHUNK, (r + 1) * CHUNK)
        xb = x_ref[q, rows, :]
        if add_pos:
            pieces = []
            for g in range(CHUNK // GRID_W):
                gi = r * (CHUNK // GRID_W) + g
                er = jnp.broadcast_to(er_ref[gi:gi + 1, :], (GRID_W, D_MODEL // 2))
                pieces.append(jnp.concatenate([er, ec_ref[...]], axis=1))
            xb = xb + jnp.concatenate(pieces, axis=0)
            x1_ref[q, rows, :] = xb
        hbs.append((_rms(xb, g_in) + sh1).astype(BF16))
    hb = jnp.concatenate(hbs, axis=0)
    gates_all = jnp.dot(hb, win_ref[0, :, C_GATE:W_COLS], preferred_element_type=F32) + gb_ref[0]
    for cg in range(Z_COLS // Z_GROUP):
        zc = jnp.dot(hb, win_ref[0, :, cg * Z_GROUP:(cg + 1) * Z_GROUP], preferred_element_type=F32)
        for u, (q, r) in enumerate(units):
            z_all[q, HALO + r * CHUNK:HALO + (r + 1) * CHUNK, cg * Z_GROUP:(cg + 1) * Z_GROUP] = (
                zc[u * CHUNK:(u + 1) * CHUNK])
    for u, (q, r) in enumerate(units):
        gates = gates_all[u * CHUNK:(u + 1) * CHUNK]
        gcol_all[q, r] = gates
        grow, cols = _gate_stats(gates)
        grow_all[q, r] = grow
        cols_all[q, r] = cols

    for q in seqs:
        if has_state:
            state0 = [[_pair_state(c0_ref, n0_ref, q, d, p) for p in range(PAIRS)] for d in range(2)]
            m0 = m0_ref[q, 0]
        else:
            state0 = [[None] * PAIRS, [None] * PAIRS]
            m0 = jnp.zeros((1, LANES), F32)
        fin, m_row = _mlstm(z_all.at[q], gcol_all.at[q], grow_all.at[q], cols_all.at[q], bd_all.at[q],
                            ycat_all.at[q], nblk, state0, m0, need_final=emit_state)
        if emit_state:
            for d in range(2):
                for p in range(PAIRS):
                    for j in range(2):
                        cout_ref[q, n_prev, d, 2 * p + j] = fin[d][p][DH * j:DH * (j + 1), DH * j:DH * (j + 1)]
                    n_t = fin[d][p][:, LANES:].T
                    nout_ref[q, n_prev, d, 2 * p:2 * p + 1, :] = n_t[0:1, 0:DH]
                    nout_ref[q, n_prev, d, 2 * p + 1:2 * p + 2, :] = n_t[DH:DH + 1, DH:LANES]
            mout_ref[q] = m_row
    if emit_state and n_prev:
        cout_ref[:, 0:n_prev] = cprev_ref[...]
        nout_ref[:, 0:n_prev] = nprev_ref[...]

    lane_g = lax.broadcasted_iota(jnp.int32, (CHUNK, D_GROUP), 1) // (D_GROUP // 4)
    lg = lax.broadcasted_iota(jnp.int32, (GMLP_CHUNK, D_GROUP), 1) // (D_GROUP // 4)
    row_c = lax.broadcasted_iota(jnp.int32, (CHUNK, D_GROUP), 0)
    half = jnp.where(lane_g == 0, 1, jnp.where(lane_g == 1, 2, jnp.where(lane_g == 2, 4, 8)))
    cw = cw_ref[0]
    ws_cat = jnp.concatenate([ws_ref[0, g] for g in range(4)], axis=1).astype(BF16)
    for q, r in units:
        z_ref = z_all.at[q]
        ycat_ref = ycat_all.at[q]
        rows = slice(r * CHUNK, (r + 1) * CHUNK)
        zrows = slice(HALO + r * CHUNK, HALO + (r + 1) * CHUNK)
        hrows = slice(r * CHUNK, (r + 1) * CHUNK + 2 * HALO)
        inner = slice(HALO, HALO + CHUNK)

        for s in range(CHUNK // GMLP_CHUNK):
            zs = slice(HALO + r * CHUNK + s * GMLP_CHUNK, HALO + r * CHUNK + (s + 1) * GMLP_CHUNK)
            vch = z_ref[zs, C_GV:C_GV + D_GROUP]
            v_stack = jnp.concatenate([jnp.where(lg == g, vch, 0.0) for g in range(4)], axis=0)
            mixed = gbias_ref[0] + jnp.dot(ws_cat, v_stack.astype(BF16), preferred_element_type=F32)
            yb = z_ref[zs, C_GU:C_GU + D_GROUP] * mixed
            ys = slice(r * CHUNK + s * GMLP_CHUNK, r * CHUNK + (s + 1) * GMLP_CHUNK)
            ycat_ref[ys, D_GROUP:2 * D_GROUP] = yb.astype(BF16)

        pz = z_ref[hrows, C_PZ:C_PZ + D_GROUP]
        a2 = pz + _shift_down(pz, 1)
        a4 = a2 + _shift_down(a2, 2)
        a8 = a4 + _shift_down(a4, 4)
        a16 = a8 + _shift_down(a8, 8)
        win = [a2[inner], _shift_up(a4, 1)[inner], _shift_up(a8, 3)[inner], _shift_up(a16, 7)[inner]]
        wsum = jnp.where(lane_g == 0, win[0], jnp.where(lane_g == 1, win[1],
                         jnp.where(lane_g == 2, win[2], win[3])))
        t_abs = row_c + r * CHUNK
        cnt = (jnp.minimum(t_abs + half, seq_len) - jnp.maximum(t_abs - half, 0)).astype(F32)
        pooled = wsum / cnt - pz[inner]
        yc = (jnp.dot(pooled.astype(BF16), pbd_ref[0], preferred_element_type=F32)
              * psc_ref[layer:layer + 1, :])
        ycat_ref[rows, 2 * D_GROUP:3 * D_GROUP] = yc.astype(BF16)

        u = z_ref[hrows, C_CC:C_CC + D_GROUP] * z_ref[hrows, C_CX:C_CX + D_GROUP]
        conv = (cw[0:1] * _shift_down(u, 1)[inner] + cw[1:2] * u[inner]
                + cw[2:3] * _shift_up(u, 1)[inner])
        yd = z_ref[zrows, C_CB:C_CB + D_GROUP] * conv
        ycat_ref[rows, 3 * D_GROUP:4 * D_GROUP] = yd.astype(BF16)

    y_all = jnp.dot(jnp.concatenate([ycat_all[q, r * CHUNK:(r + 1) * CHUNK, :] for q, r in units], axis=0),
                    wout_ref[0], preferred_element_type=F32)
    g_y = gt1 * ng[1:2]
    for u, (q, r) in enumerate(units):
        rows = slice(r * CHUNK, (r + 1) * CHUNK)
        xb = x1_ref[q, rows, :] if add_pos else x_ref[q, rows, :]
        x1_ref[q, rows, :] = xb + _rms(y_all[u * CHUNK:(u + 1) * CHUNK], g_y)


def _layer_spec(shape, layer):
    nd = len(shape)
    return pl.BlockSpec((1,) + tuple(shape[1:]), lambda b: (layer,) + (0,) * (nd - 1),
                        pipeline_mode=pl.Buffered(1))


def _mixer_call(x, layer, mod_all, wts, cond_base, pos=None, state=None, emit_state=False,
                prev_states=None, nseq=1):
    bsz, seq_len, _ = x.shape
    nblk = seq_len // CHUNK
    add_pos = pos is not None
    has_state = state is not None
    n_prev = 0 if prev_states is None else prev_states[0].shape[1]
    args = [x]
    in_specs = [pl.BlockSpec((nseq, seq_len, D_MODEL), lambda b: (b, 0, 0))]
    if add_pos:
        args += list(pos)
        in_specs += [pl.BlockSpec(p.shape, lambda b: (0, 0), pipeline_mode=pl.Buffered(1)) for p in pos]
    args.append(mod_all)
    in_specs.append(_layer_spec(mod_all.shape, layer))
    for name in ("norm_g", "w_in", "gate_b", "gmlp_ws", "gmlp_bias", "pool_bd", "pool_scale",
                 "conv_w", "w_out"):
        args.append(wts[name])
        if wts[name].ndim == 2:
            in_specs.append(pl.BlockSpec(wts[name].shape, lambda b: (0, 0), pipeline_mode=pl.Buffered(1)))
        else:
            in_specs.append(_layer_spec(wts[name].shape, layer))
    if has_state:
        args += list(state)
        in_specs += [pl.BlockSpec((nseq, 1, 2, HEADS, DH, DH), lambda b: (b, layer, 0, 0, 0, 0)),
                     pl.BlockSpec((nseq, 1, 2, HEADS, DH), lambda b: (b, layer, 0, 0, 0)),
                     pl.BlockSpec((nseq, 1, 1, LANES), lambda b: (b, layer, 0, 0))]
    out_shape = [jax.ShapeDtypeStruct((bsz, seq_len, D_MODEL), F32)]
    out_specs = [pl.BlockSpec((nseq, seq_len, D_MODEL), lambda b: (b, 0, 0))]
    if emit_state:
        if n_prev:
            args += list(prev_states)
            in_specs += [pl.BlockSpec((nseq, n_prev, 2, HEADS, DH, DH), lambda b: (b, 0, 0, 0, 0, 0)),
                         pl.BlockSpec((nseq, n_prev, 2, HEADS, DH), lambda b: (b, 0, 0, 0, 0))]
        out_shape += [jax.ShapeDtypeStruct((bsz, n_prev + 1, 2, HEADS, DH, DH), F32),
                      jax.ShapeDtypeStruct((bsz, n_prev + 1, 2, HEADS, DH), F32),
                      jax.ShapeDtypeStruct((bsz, 1, LANES), F32)]
        out_specs += [pl.BlockSpec((nseq, n_prev + 1, 2, HEADS, DH, DH), lambda b: (b, 0, 0, 0, 0, 0)),
                      pl.BlockSpec((nseq, n_prev + 1, 2, HEADS, DH), lambda b: (b, 0, 0, 0, 0)),
                      pl.BlockSpec((nseq, 1, LANES), lambda b: (b, 0, 0))]
    kern = functools.partial(_mixer_kernel, layer=layer, seq_len=seq_len, nseq=nseq, add_pos=add_pos,
                             has_state=has_state, emit_state=emit_state, cond_base=cond_base,
                             n_prev=n_prev)
    return pl.pallas_call(
        kern,
        out_shape=out_shape,
        grid=(bsz // nseq,),
        in_specs=in_specs,
        out_specs=out_specs,
        scratch_shapes=[pltpu.VMEM((nseq, seq_len + 2 * HALO, Z_COLS), F32),
                        pltpu.VMEM((nseq, nblk, CHUNK, LANES), F32),
                        pltpu.VMEM((nseq, nblk, 16, CHUNK), F32),
                        pltpu.VMEM((nseq, nblk, CHUNK, LANES), F32),
                        pltpu.VMEM((nseq, 2, nblk, PAIRS, LANES, 2 * LANES), BF16),
                        pltpu.VMEM((nseq, seq_len, D_MODEL), BF16)],
        compiler_params=pltpu.CompilerParams(
            dimension_semantics=("arbitrary",), vmem_limit_bytes=VMEM_LIMIT),
        name="mixer_t%d" % seq_len,
    )(*args)


def _ffn_kernel(mod_ref, ng_ref, cw_ref, x1p_hbm, x1s_hbm, up_hbm, down_hbm, outp_hbm, outs_hbm,
                up_bf, down_bf, act_ref, h2_ref, x1_buf, o_buf, up_stage, down_stage,
                in_sems, out_sems, w_sems, *, layer, tiles_p, tiles_s, seq_p, seq_s):
    i = pl.program_id(0)
    n_tiles = tiles_p + tiles_s
    rows = FFN_ROWS
    half = FFN_SUB // 2
    nsub = D_FF // half
    slot = i % 2
    is_prompt = i < tiles_p

    def piece_rows(tile, first_tile, j):
        base = (tile - first_tile) * rows + j * FFN_GROUP
        return pl.ds(base if isinstance(base, int) else pl.multiple_of(base, FFN_GROUP), FFN_GROUP)

    def x1_copies(x1_hbm, tile, first_tile, buf):
        return [pltpu.make_async_copy(x1_hbm.at[piece_rows(tile, first_tile, j)],
                                      x1_buf.at[buf, :, j, :], in_sems.at[buf, j]) for j in range(8)]

    def out_copies(out_hbm, tile, first_tile, buf):
        return [pltpu.make_async_copy(o_buf.at[buf, :, j, :],
                                      out_hbm.at[piece_rows(tile, first_tile, j)], out_sems.at[buf, j])
                for j in range(8)]

    def fetch(tile, buf):
        @pl.when(tile < tiles_p)
        def _():
            for cp in x1_copies(x1p_hbm, tile, 0, buf):
                cp.start()

        @pl.when(tile >= tiles_p)
        def _():
            for cp in x1_copies(x1s_hbm, tile, tiles_p, buf):
                cp.start()

    def await_fetch(buf):
        for cp in x1_copies(x1p_hbm, 0, 0, buf):
            cp.wait()

    def await_store(buf):
        for cp in out_copies(outp_hbm, 0, 0, buf):
            cp.wait()

    up_rows = up_stage.shape[1]
    down_rows = down_stage.shape[1]

    def slab(c, n):
        return pl.ds(c * n, n) if isinstance(c, int) else pl.ds(pl.multiple_of(c * n, n), n)

    def weight_copies(c, buf):
        return (pltpu.make_async_copy(up_hbm.at[layer, slab(c, up_rows), :], up_stage.at[buf],
                                      w_sems.at[buf, 0]),
                pltpu.make_async_copy(down_hbm.at[layer, slab(c, down_rows), :], down_stage.at[buf],
                                      w_sems.at[buf, 1]))

    @pl.when(i == 0)
    def _():
        for cp in x1_copies(x1p_hbm, 0, 0, 0):
            cp.start()
        depth = up_stage.shape[0]
        for c0 in range(depth - 1):
            for cp in weight_copies(c0, c0):
                cp.start()

        def convert(c, carry):
            buf = c % depth
            ahead = c + depth - 1

            @pl.when(ahead < FFN_WEIGHT_CHUNKS)
            def _():
                for cp in weight_copies(ahead, ahead % depth):
                    cp.start()

            for cp in weight_copies(c, buf):
                cp.wait()
            r = slab(c, up_rows)
            for s in range(nsub):
                up_bf[s, r, 0:half] = up_stage[buf, :, s * half:(s + 1) * half].astype(BF16)
                up_bf[s, r, half:FFN_SUB] = (
                    up_stage[buf, :, D_FF + s * half:D_FF + (s + 1) * half].astype(BF16))
            down_bf[slab(c, down_rows), :] = down_stage[buf].astype(BF16)
            return carry

        lax.fori_loop(0, FFN_WEIGHT_CHUNKS, convert, 0)

    await_fetch(slot)

    @pl.when(i >= 2)
    def _():
        await_store(slot)

    @pl.when(i + 1 < n_tiles)
    def _():
        fetch(i + 1, 1 - slot)

    cond = jnp.where(is_prompt, 0, i - tiles_p + 1)
    _, _, _, sh2, sc2, gt2 = _mod_rows(mod_ref, cond)
    ng = ng_ref[0]
    x1 = x1_buf[slot].reshape(rows, D_MODEL)
    h2_ref[...] = (_rms(x1, ng[2:3] * (1.0 + sc2)) + sh2).astype(BF16)
    h2 = h2_ref[...]

    sub = lax.broadcasted_iota(jnp.int32, (8, FFN_SUB), 0)
    gp, gs = seq_p // FFN_GROUP, seq_s // FFN_GROUP
    starts = ((sub % gs) == 0) | (((sub % gp) == 0) & is_prompt)
    ends = ((sub % gs) == gs - 1) | (((sub % gp) == gp - 1) & is_prompt)
    for s in range(nsub):
        a = jnp.dot(h2, up_bf[s], preferred_element_type=F32)
        cw = jnp.concatenate([cw_ref[0, :, s * half:(s + 1) * half],
                              0.5 * cw_ref[0, :, D_FF + s * half:D_FF + (s + 1) * half]], axis=1)
        a_prev = jnp.concatenate(
            [jnp.where(starts, 0.0, pltpu.roll(a[rows - 8:rows], 1, axis=0)), a[0:rows - 8]], axis=0)
        a_next = jnp.concatenate(
            [a[8:rows], jnp.where(ends, 0.0, pltpu.roll(a[0:8], 7, axis=0))], axis=0)
        ac = cw[0:1] * a_prev + cw[1:2] * a + cw[2:3] * a_next
        hg = ac[:, half:]
        act = (hg * (1.0 + jnp.tanh(hg))) * ac[:, :half]
        act_ref[:, s * half:(s + 1) * half] = act.astype(BF16)
    y = jnp.dot(act_ref[...], down_bf[...], preferred_element_type=F32)
    out = x1_buf[slot].reshape(rows, D_MODEL) + gt2 * _rms(y, ng[3:4])
    o_buf[slot] = out.reshape(FFN_GROUP, 8, D_MODEL)

    @pl.when(is_prompt)
    def _():
        for cp in out_copies(outp_hbm, i, 0, slot):
            cp.start()

    @pl.when(jnp.logical_not(is_prompt))
    def _():
        for cp in out_copies(outs_hbm, i, tiles_p, slot):
            cp.start()

    @pl.when(i == n_tiles - 1)
    def _():
        await_store(1 - slot)
        await_store(slot)


def _ffn_call(x1p, x1s, layer, mod_all, wts, seq_p, seq_s):
    tiles_p = x1p.shape[0] // FFN_ROWS
    tiles_s = x1s.shape[0] // FFN_ROWS
    assert tiles_p >= 1 and tiles_s >= 1 and tiles_p + tiles_s >= 2
    assert FFN_ROWS == 8 * FFN_GROUP and seq_p % FFN_GROUP == 0 and seq_s % seq_p == 0
    half = FFN_SUB // 2

    def resident(shape):
        nd = len(shape)
        return pl.BlockSpec((1,) + tuple(shape[1:]), lambda i: (layer,) + (0,) * (nd - 1),
                            pipeline_mode=pl.Buffered(1))

    hbm = pl.BlockSpec(memory_space=pl.ANY)
    return pl.pallas_call(
        functools.partial(_ffn_kernel, layer=layer, tiles_p=tiles_p, tiles_s=tiles_s,
                          seq_p=seq_p, seq_s=seq_s),
        out_shape=[jax.ShapeDtypeStruct(x1p.shape, F32), jax.ShapeDtypeStruct(x1s.shape, F32)],
        grid=(tiles_p + tiles_s,),
        in_specs=[resident(mod_all.shape), resident(wts["norm_g"].shape), resident(wts["ffn_conv"].shape),
                  hbm, hbm, hbm, hbm],
        out_specs=[hbm, hbm],
        scratch_shapes=[pltpu.VMEM((D_FF // half, D_MODEL, FFN_SUB), BF16),
                        pltpu.VMEM((D_FF, D_MODEL), BF16),
                        pltpu.VMEM((FFN_ROWS, D_FF), BF16),
                        pltpu.VMEM((FFN_ROWS, D_MODEL), BF16),
                        pltpu.VMEM((2, FFN_GROUP, 8, D_MODEL), F32),
                        pltpu.VMEM((2, FFN_GROUP, 8, D_MODEL), F32),
                        pltpu.VMEM((FFN_WEIGHT_BUFS, D_MODEL // FFN_WEIGHT_CHUNKS, 2 * D_FF), F32),
                        pltpu.VMEM((FFN_WEIGHT_BUFS, D_FF // FFN_WEIGHT_CHUNKS, D_MODEL), F32),
                        pltpu.SemaphoreType.DMA((2, 8)),
                        pltpu.SemaphoreType.DMA((2, 8)),
                        pltpu.SemaphoreType.DMA((FFN_WEIGHT_BUFS, 2))],
        compiler_params=pltpu.CompilerParams(
            dimension_semantics=("arbitrary",), vmem_limit_bytes=VMEM_LIMIT),
        name="ffn",
    )(mod_all, wts["norm_g"], wts["ffn_conv"], x1p, x1s, wts["ffn_up"], wts["ffn_down"])


def _pos_tables(t_len):
    quarter = D_MODEL // 4
    omega = (1.0 / (np.float32(POS_BASE) ** (np.arange(quarter, dtype=np.float32) / np.float32(quarter))))
    omega = omega.astype(np.float32)

    def emb(p):
        a = p.astype(np.float32)[:, None] * omega[None, :]
        return np.concatenate([np.sin(a), np.cos(a)], axis=-1).astype(np.float32)

    return jnp.asarray(emb(np.arange(t_len // GRID_W))), jnp.asarray(emb(np.arange(GRID_W)))


def _stabiliser_rows(state_m):
    bs = state_m.shape[0]
    m = jnp.pad(state_m.astype(F32), ((0, 0), (0, 0), (0, 0), (0, 16 - HEADS)))
    return jnp.pad(m.reshape(bs, DEPTH, 1, 32), ((0, 0), (0, 0), (0, 0), (L_MF, LANES - 32 - L_MF)))


def kernel(x_prompt, x_sample, state_C, state_n, state_m, c, c_ctx, norm_g, ada_w, ada_b, w_in,
           gate_b, gmlp_ws, gmlp_b, pool_w, pool_scale, conv_w, w_out, ffn_up, ffn_conv, ffn_down):
    bp, tp, _ = x_prompt.shape
    bs, ts, _ = x_sample.shape
    assert tp == CHUNK and ts % CHUNK == 0 and FFN_ROWS % tp == 0 and ts == FFN_ROWS

    conds = jnp.concatenate([c_ctx[None, :], c, jnp.zeros((7 - bs, D_MODEL), F32)], axis=0)
    mod_all = _modulation(conds, ada_w, ada_b)
    win_p, wout_p = _prep_weights(w_in, w_out)

    gb = jnp.pad(gate_b, ((0, 0), (0, 0), (0, 8 - HEADS))).reshape(DEPTH, 1, 32)
    dg = D_GROUP // 4
    wts = {
        "norm_g": norm_g,
        "w_in": win_p,
        "gate_b": jnp.pad(gb, ((0, 0), (0, 0), (0, LANES - 32))),
        "gmlp_ws": gmlp_ws,
        "gmlp_bias": jnp.repeat(jnp.swapaxes(gmlp_b, 1, 2), dg, axis=2),
        "pool_bd": jnp.einsum("lgcd,gh->lgchd", pool_w, jnp.eye(4, dtype=F32)).reshape(
            DEPTH, D_GROUP, D_GROUP).astype(BF16),
        "pool_scale": pool_scale,
        "conv_w": conv_w,
        "w_out": wout_p,
        "ffn_up": ffn_up,
        "ffn_conv": ffn_conv,
        "ffn_down": ffn_down,
    }
    pos = _pos_tables(ts)
    state = (state_C.astype(F32), state_n.astype(F32), _stabiliser_rows(state_m))

    xp, xs = x_prompt, x_sample
    prev, ms = None, []
    for l in range(DEPTH):
        x1p, c_all, n_all, m_fin = _mixer_call(xp, l, mod_all, wts, 0, emit_state=True,
                                               prev_states=prev, nseq=MIXER_ROWS // tp)
        prev = (c_all, n_all)
        ms.append(jnp.stack([m_fin[:, 0, L_MF:L_MF + HEADS], m_fin[:, 0, L_MB:L_MB + HEADS]], axis=1))

        (x1s,) = _mixer_call(xs, l, mod_all, wts, 1, pos=pos if l == 0 else None, state=state)

        xp, xs = _ffn_call(x1p.reshape(bp * tp, D_MODEL), x1s.reshape(bs * ts, D_MODEL),
                           l, mod_all, wts, tp, ts)
        xp = xp.reshape(bp, tp, D_MODEL)
        xs = xs.reshape(bs, ts, D_MODEL)

    return (xp, xs, c_all, n_all, jnp.stack(ms, axis=1))
```

```python
import functools

import numpy as np
import jax
import jax.numpy as jnp
from jax import lax
from jax.experimental import pallas as pl
from jax.experimental.pallas import tpu as pltpu

F32 = jnp.float32
BF16 = jnp.bfloat16

D_MODEL = 1024
DEPTH = 2
HEADS = 4
PAIRS = HEADS // 2
DH = 64
D_GROUP = 256
D_FF = 2816
N_MOD = 6
EPS = 1e-6
GRID_W = 64
POS_BASE = 10000.0
D_IN = 2576
N_GATE = 4 * HEADS

CHUNK = 256
GMLP_CHUNK = 128
HALO = 8
LANES = 128

C_Q, C_K, C_V, C_O = 0, 256, 512, 768
C_GU, C_GV, C_PZ, C_CB, C_CC, C_CX = 1024, 1280, 1536, 1792, 2048, 2304
Z_COLS = 2560
C_GATE = Z_COLS
W_COLS = Z_COLS + LANES
Z_GROUP = 512
L_BF, L_MF, L_BB, L_MB = 0, 8, 16, 24
L_STEP = L_MF - L_BF

MIXER_ROWS = 1024
FFN_ROWS = 1024
FFN_GROUP = FFN_ROWS // 8
FFN_SUB = 512
FFN_WEIGHT_CHUNKS = 16
FFN_WEIGHT_BUFS = 4
VMEM_LIMIT = 60 * 1024 * 1024


def _rms(x, g):
    ms = jnp.mean(x * x, axis=-1, keepdims=True)
    return x * lax.rsqrt(ms + EPS) * g


def _sigmoid(x):
    return 1.0 / (1.0 + jnp.exp(-x))


def _log_sigmoid(x):
    return jnp.minimum(x, 0.0) - jnp.log1p(jnp.exp(-jnp.abs(x)))


def _scan_lanes(x, op, reverse, fill):
    n = x.shape[1]
    lane = lax.broadcasted_iota(jnp.int32, x.shape, 1)
    k = 1
    while k < n:
        if reverse:
            sh = jnp.where(lane < n - k, pltpu.roll(x, n - k, axis=1), fill)
        else:
            sh = jnp.where(lane >= k, pltpu.roll(x, k, axis=1), fill)
        x = op(x, sh)
        k *= 2
    return x


def _shift_down(x, k):
    return pltpu.roll(x, k, axis=0)


def _shift_up(x, k):
    return pltpu.roll(x, x.shape[0] - k, axis=0)


def _prep_kernel(wt_ref, sel_ref, wout_ref, win_o_ref, wout_o_ref):
    for n in range(Z_COLS // LANES):
        r0 = LANES * n if n < 4 * D_GROUP // LANES else LANES * n + N_GATE
        win_o_ref[0, :, LANES * n:LANES * (n + 1)] = wt_ref[0, r0:r0 + LANES, :].T.astype(BF16)
    g = wt_ref[0, 4 * D_GROUP:4 * D_GROUP + LANES, :].T.astype(BF16)
    win_o_ref[0, :, C_GATE:W_COLS] = jnp.dot(g, sel_ref[...], preferred_element_type=F32).astype(BF16)
    wout_o_ref[...] = wout_ref[...].astype(BF16)


def _prep_weights(w_in, w_out):
    sel = np.zeros((LANES, LANES), np.float32)
    for j in range(N_GATE):
        sel[j, 8 * (j // HEADS) + j % HEADS] = 1.0
    rows = D_MODEL // 2
    return pl.pallas_call(
        _prep_kernel,
        out_shape=[jax.ShapeDtypeStruct((DEPTH, D_MODEL, W_COLS), BF16),
                   jax.ShapeDtypeStruct(w_out.shape, BF16)],
        grid=(DEPTH, D_MODEL // rows),
        in_specs=[pl.BlockSpec((1, D_IN, rows), lambda l, i: (l, 0, i)),
                  pl.BlockSpec((LANES, LANES), lambda l, i: (0, 0)),
                  pl.BlockSpec((1, rows, D_MODEL), lambda l, i: (l, i, 0))],
        out_specs=[pl.BlockSpec((1, rows, W_COLS), lambda l, i: (l, i, 0)),
                   pl.BlockSpec((1, rows, D_MODEL), lambda l, i: (l, i, 0))],
        compiler_params=pltpu.CompilerParams(dimension_semantics=("arbitrary", "arbitrary"),
                                             vmem_limit_bytes=VMEM_LIMIT),
        name="prep_mixer_w")(jnp.swapaxes(w_in, 1, 2), jnp.asarray(sel, BF16), w_out)


def _mod_kernel(c_ref, w_ref, b_ref, o_ref):
    c = c_ref[...]
    s = (c * _sigmoid(c)).astype(BF16)
    bias = b_ref[pl.ds(pl.program_id(0), 1), :]
    o_ref[0] = jnp.dot(s, w_ref[0].astype(BF16), preferred_element_type=F32) + bias


def _modulation(conds, ada_w, ada_b):
    n_out = N_MOD * D_MODEL
    bn = 1536
    return pl.pallas_call(
        _mod_kernel,
        out_shape=jax.ShapeDtypeStruct((DEPTH, 8, n_out), F32),
        grid=(DEPTH, n_out // bn),
        in_specs=[
            pl.BlockSpec((8, D_MODEL), lambda l, j: (0, 0)),
            pl.BlockSpec((1, D_MODEL, bn), lambda l, j: (l, 0, j)),
            pl.BlockSpec((DEPTH, bn), lambda l, j: (0, j)),
        ],
        out_specs=pl.BlockSpec((1, 8, bn), lambda l, j: (l, 0, j)),
        compiler_params=pltpu.CompilerParams(
            dimension_semantics=("arbitrary", "arbitrary"), vmem_limit_bytes=VMEM_LIMIT),
        name="adaln_mod",
    )(conds, ada_w, ada_b)


def _mod_rows(mod_ref, cond):
    row = mod_ref[0, pl.ds(cond, 1), :]
    return [row[:, i * D_MODEL:(i + 1) * D_MODEL] for i in range(N_MOD)]


def _gate_stats(gates):
    gt = gates.T
    b_f = _scan_lanes(_log_sigmoid(gt[8:16]), jnp.add, False, 0.0)
    g_f = gt[0:8] - b_f
    cm_f = _scan_lanes(g_f, jnp.maximum, False, -jnp.inf)
    b_b = _scan_lanes(_log_sigmoid(gt[24:32]), jnp.add, True, 0.0)
    g_b = gt[16:24] - b_b
    cm_b = _scan_lanes(g_b, jnp.maximum, True, -jnp.inf)
    stack = jnp.concatenate([b_f, cm_f, b_b, cm_b, jnp.zeros((LANES - 32, CHUNK), F32)], axis=0)
    return jnp.concatenate([g_f, g_b], axis=0), stack.T


def _pair_cols(mat, lane0, lane_lt_dh):
    return jnp.where(lane_lt_dh, mat[:, lane0:lane0 + 1], mat[:, lane0 + 1:lane0 + 2])


def _mlstm(z_ref, gcol_ref, grow_ref, cols_ref, bd_ref, ycat_ref, nc, state0, m0, need_final):
    lane_row = lax.broadcasted_iota(jnp.int32, (1, LANES), 1)
    lane = lax.broadcasted_iota(jnp.int32, (CHUNK, LANES), 1)
    lt_dh = lane < DH
    row_ll = lax.broadcasted_iota(jnp.int32, (CHUNK, CHUNK), 0)
    col_ll = lax.broadcasted_iota(jnp.int32, (CHUNK, CHUNK), 1)
    tri = (col_ll <= row_ll, col_ll >= row_ll)
    bd_row = lax.broadcasted_iota(jnp.int32, (LANES, 2 * LANES), 0)
    bd_col = lax.broadcasted_iota(jnp.int32, (LANES, 2 * LANES), 1)
    bd_mask = (bd_row < DH) == ((bd_col & (LANES - 1)) < DH)
    dirs = ((0, L_BF, L_MF, CHUNK - 1), (1, L_BB, L_MB, 0))

    m_start = [[None] * nc, [None] * nc]
    m_fin = [None, None]
    for d, _, _, last in dirs:
        m = m0
        order = range(nc) if d == 0 else range(nc - 1, -1, -1)
        for step, c in enumerate(order):
            m_start[d][c] = m
            if need_final or step < nc - 1:
                cl = cols_ref[c, last:last + 1, :]
                m = pltpu.roll(cl, L_STEP, axis=1) + jnp.maximum(m, cl)
        m_fin[d] = m

    have_state = [[False] * nc, [False] * nc]
    fin = [None, None]
    for d, l_b, l_m, last in dirs:
        bd = list(state0[d])
        order = range(nc) if d == 0 else range(nc - 1, -1, -1)
        for step, c in enumerate(order):
            if bd[0] is not None:
                have_state[d][c] = True
                for p in range(PAIRS):
                    bd_ref[d, c, p] = bd[p].astype(BF16)
            if not (need_final or step < nc - 1):
                continue
            rows = slice(HALO + c * CHUNK, HALO + (c + 1) * CHUNK)
            cols = cols_ref[c]
            m_last = jnp.maximum(cols[last:last + 1, :], m_start[d][c])
            a_col = jnp.exp(gcol_ref[c] - cols - pltpu.roll(m_last, LANES - L_STEP, axis=1))
            a_prev = jnp.exp(m_start[d][c] - m_last)
            for p in range(PAIRS):
                a_pair = _pair_cols(a_col, l_b + 2 * p, lt_dh)
                v_pair = z_ref[rows, C_V + LANES * p:C_V + LANES * (p + 1)]
                rhs = jnp.concatenate([v_pair * a_pair, a_pair], axis=1).astype(BF16)
                k_pair = (z_ref[rows, C_K + LANES * p:C_K + LANES * (p + 1)] * (DH ** -0.5)).astype(BF16)
                upd = lax.dot_general(k_pair, rhs, (((0,), (0,)), ((), ())),
                                      preferred_element_type=F32)
                upd = jnp.where(bd_mask, upd, 0.0)
                if bd[p] is None:
                    bd[p] = upd
                else:
                    scale = jnp.where(bd_row < DH, a_prev[:, l_m + 2 * p:l_m + 2 * p + 1],
                                      a_prev[:, l_m + 2 * p + 1:l_m + 2 * p + 2])
                    bd[p] = scale * bd[p] + upd
        fin[d] = bd

    for c in range(nc):
        rows = slice(HALO + c * CHUNK, HALO + (c + 1) * CHUNK)
        cols = cols_ref[c]
        grow = grow_ref[c]
        m_prev = jnp.where(lane_row < L_BB, m_start[0][c], m_start[1][c])
        m_all = jnp.maximum(cols, m_prev)
        w_inter = jnp.exp(m_prev - m_all)
        e_neg = jnp.exp(-(pltpu.roll(cols, L_STEP, axis=1) + m_all))
        for p in range(PAIRS):
            q_pair = z_ref[rows, C_Q + LANES * p:C_Q + LANES * (p + 1)].astype(BF16)
            k_f32 = z_ref[rows, C_K + LANES * p:C_K + LANES * (p + 1)] * (DH ** -0.5)
            v_pair = z_ref[rows, C_V + LANES * p:C_V + LANES * (p + 1)]
            p_cat = ([], [])
            va = []
            for j in range(2):
                h = 2 * p + j
                own = lt_dh if j == 0 else jnp.logical_not(lt_dh)
                k_own = jnp.where(own, k_f32, 0.0).astype(BF16)
                s = lax.dot_general(q_pair, k_own, (((1,), (1,)), ((), ())),
                                    preferred_element_type=F32)
                va.append(jnp.concatenate([jnp.where(own, v_pair, 0.0), jnp.where(own, 1.0, 0.0)],
                                          axis=1).astype(BF16))
                for d, _, l_m, _ in dirs:
                    w = jnp.exp(jnp.where(tri[d], grow[8 * d + h:8 * d + h + 1] - m_all[:, l_m + h:l_m + h + 1],
                                          -jnp.inf))
                    p_cat[d].append((s * w).astype(BF16))
            va = jnp.concatenate(va, axis=0)
            h_sum = None
            for d, _, l_m, _ in dirs:
                tot = jnp.dot(jnp.concatenate(p_cat[d], axis=1), va, preferred_element_type=F32)
                if have_state[d][c]:
                    e_pair = _pair_cols(w_inter, l_m + 2 * p, lt_dh)
                    qc = jnp.dot(q_pair, bd_ref[d, c, p], preferred_element_type=F32)
                    tot = tot + jnp.concatenate([e_pair, e_pair], axis=1) * qc
                floor = _pair_cols(e_neg, l_m + 2 * p, lt_dh)
                h_dir = tot[:, :LANES] / jnp.maximum(jnp.abs(tot[:, LANES:]), floor)
                h_sum = h_dir if h_sum is None else h_sum + h_dir
            o_pair = z_ref[rows, C_O + LANES * p:C_O + LANES * (p + 1)]
            ycat_ref[c * CHUNK:(c + 1) * CHUNK, LANES * p:LANES * (p + 1)] = (
                _sigmoid(o_pair) * h_sum).astype(BF16)

    m_row = jnp.where(lane_row < L_BB, m_fin[0], m_fin[1])
    return fin, m_row


def _pair_state(c_ref, n_ref, q, d, p):
    eye = (lax.broadcasted_iota(jnp.int32, (DH, DH), 0) == lax.broadcasted_iota(jnp.int32, (DH, DH), 1))
    zero = jnp.zeros((DH, DH), F32)
    blocks = []
    for j in range(2):
        h = 2 * p + j
        n_row = n_ref[q, 0, d, h:h + 1, :]
        n_col = jnp.sum(jnp.where(eye, n_row, 0.0), axis=1, keepdims=True)
        row = [zero] * 4
        row[j] = c_ref[q, 0, d, h]
        row[2 + j] = jnp.broadcast_to(n_col, (DH, DH))
        blocks.append(jnp.concatenate(row, axis=1))
    return jnp.concatenate(blocks, axis=0)


def _mixer_kernel(*refs, layer, seq_len, nseq, add_pos, has_state, emit_state, cond_base, n_prev):
    it = iter(refs)
    x_ref = next(it)
    if add_pos:
        er_ref = next(it)
        ec_ref = next(it)
    mod_ref = next(it)
    ng_ref = next(it)
    win_ref = next(it)
    gb_ref = next(it)
    ws_ref = next(it)
    gbias_ref = next(it)
    pbd_ref = next(it)
    psc_ref = next(it)
    cw_ref = next(it)
    wout_ref = next(it)
    if has_state:
        c0_ref = next(it)
        n0_ref = next(it)
        m0_ref = next(it)
    if n_prev:
        cprev_ref = next(it)
        nprev_ref = next(it)
    x1_ref = next(it)
    if emit_state:
        cout_ref = next(it)
        nout_ref = next(it)
        mout_ref = next(it)
    z_all = next(it)
    gcol_all = next(it)
    grow_all = next(it)
    cols_all = next(it)
    bd_all = next(it)
    ycat_all = next(it)

    nblk = seq_len // CHUNK
    cond = cond_base if cond_base == 0 else cond_base + pl.program_id(0)
    sh1, sc1, gt1 = _mod_rows(mod_ref, cond)[:3]
    ng = ng_ref[0]
    seqs = range(nseq)
    units = [(q, r) for q in seqs for r in range(nblk)]

    zeros_halo = jnp.zeros((HALO, Z_COLS - C_PZ), F32)
    for q in seqs:
        z_all[q, 0:HALO, C_PZ:Z_COLS] = zeros_halo
        z_all[q, HALO + seq_len:2 * HALO + seq_len, C_PZ:Z_COLS] = zeros_halo

    g_in = ng[0:1] * (1.0 + sc1)
    hbs = []
    for q, r in units:
        rows = slice(r * CHUNK, (r + 1) * CHUNK)
        xb = x_ref[q, rows, :]
        if add_pos:
            pieces = []
            for g in range(CHUNK // GRID_W):
                gi = r * (CHUNK // GRID_W) + g
                er = jnp.broadcast_to(er_ref[gi:gi + 1, :], (GRID_W, D_MODEL // 2))
                pieces.append(jnp.concatenate([er, ec_ref[...]], axis=1))
            xb = xb + jnp.concatenate(pieces, axis=0)
            x1_ref[q, rows, :] = xb
        hbs.append((_rms(xb, g_in) + sh1).astype(BF16))
    hb = jnp.concatenate(hbs, axis=0)
    gates_all = jnp.dot(hb, win_ref[0, :, C_GATE:W_COLS], preferred_element_type=F32) + gb_ref[0]
    for cg in range(Z_COLS // Z_GROUP):
        zc = jnp.dot(hb, win_ref[0, :, cg * Z_GROUP:(cg + 1) * Z_GROUP], preferred_element_type=F32)
        for u, (q, r) in enumerate(units):
            z_all[q, HALO + r * CHUNK:HALO + (r + 1) * CHUNK, cg * Z_GROUP:(cg + 1) * Z_GROUP] = (
                zc[u * CHUNK:(u + 1) * CHUNK])
    for u, (q, r) in enumerate(units):
        gates = gates_all[u * CHUNK:(u + 1) * CHUNK]
        gcol_all[q, r] = gates
        grow, cols = _gate_stats(gates)
        grow_all[q, r] = grow
        cols_all[q, r] = cols

    for q in seqs:
        if has_state:
            state0 = [[_pair_state(c0_ref, n0_ref, q, d, p) for p in range(PAIRS)] for d in range(2)]
            m0 = m0_ref[q, 0]
        else:
            state0 = [[None] * PAIRS, [None] * PAIRS]
            m0 = jnp.zeros((1, LANES), F32)
        fin, m_row = _mlstm(z_all.at[q], gcol_all.at[q], grow_all.at[q], cols_all.at[q], bd_all.at[q],
                            ycat_all.at[q], nblk, state0, m0, need_final=emit_state)
        if emit_state:
            for d in range(2):
                for p in range(PAIRS):
                    for j in range(2):
                        cout_ref[q, n_prev, d, 2 * p + j] = fin[d][p][DH * j:DH * (j + 1), DH * j:DH * (j + 1)]
                    n_t = fin[d][p][:, LANES:].T
                    nout_ref[q, n_prev, d, 2 * p:2 * p + 1, :] = n_t[0:1, 0:DH]
                    nout_ref[q, n_prev, d, 2 * p + 1:2 * p + 2, :] = n_t[DH:DH + 1, DH:LANES]
            mout_ref[q] = m_row
    if emit_state and n_prev:
        cout_ref[:, 0:n_prev] = cprev_ref[...]
        nout_ref[:, 0:n_prev] = nprev_ref[...]

    lane_g = lax.broadcasted_iota(jnp.int32, (CHUNK, D_GROUP), 1) // (D_GROUP // 4)
    lg = lax.broadcasted_iota(jnp.int32, (GMLP_CHUNK, D_GROUP), 1) // (D_GROUP // 4)
    row_c = lax.broadcasted_iota(jnp.int32, (CHUNK, D_GROUP), 0)
    half = jnp.where(lane_g == 0, 1, jnp.where(lane_g == 1, 2, jnp.where(lane_g == 2, 4, 8)))
    cw = cw_ref[0]
    ws_cat = jnp.concatenate([ws_ref[0, g] for g in range(4)], axis=1).astype(BF16)
    for q, r in units:
        z_ref = z_all.at[q]
        ycat_ref = ycat_all.at[q]
        rows = slice(r * CHUNK, (r + 1) * CHUNK)
        zrows = slice(HALO + r * CHUNK, HALO + (r + 1) * CHUNK)
        hrows = slice(r * CHUNK, (r + 1) * CHUNK + 2 * HALO)
        inner = slice(HALO, HALO + CHUNK)

        for s in range(CHUNK // GMLP_CHUNK):
            zs = slice(HALO + r * CHUNK + s * GMLP_CHUNK, HALO + r * CHUNK + (s + 1) * GMLP_CHUNK)
            vch = z_ref[zs, C_GV:C_GV + D_GROUP]
            v_stack = jnp.concatenate([jnp.where(lg == g, vch, 0.0) for g in range(4)], axis=0)
            mixed = gbias_ref[0] + jnp.dot(ws_cat, v_stack.astype(BF16), preferred_element_type=F32)
            yb = z_ref[zs, C_GU:C_GU + D_GROUP] * mixed
            ys = slice(r * CHUNK + s * GMLP_CHUNK, r * CHUNK + (s + 1) * GMLP_CHUNK)
            ycat_ref[ys, D_GROUP:2 * D_GROUP] = yb.astype(BF16)

        pz = z_ref[hrows, C_PZ:C_PZ + D_GROUP]
        a2 = pz + _shift_down(pz, 1)
        a4 = a2 + _shift_down(a2, 2)
        a8 = a4 + _shift_down(a4, 4)
        a16 = a8 + _shift_down(a8, 8)
        win = [a2[inner], _shift_up(a4, 1)[inner], _shift_up(a8, 3)[inner], _shift_up(a16, 7)[inner]]
        wsum = jnp.where(lane_g == 0, win[0], jnp.where(lane_g == 1, win[1],
                         jnp.where(lane_g == 2, win[2], win[3])))
        t_abs = row_c + r * CHUNK
        cnt = (jnp.minimum(t_abs + half, seq_len) - jnp.maximum(t_abs - half, 0)).astype(F32)
        pooled = wsum / cnt - pz[inner]
        yc = (jnp.dot(pooled.astype(BF16), pbd_ref[0], preferred_element_type=F32)
              * psc_ref[layer:layer + 1, :])
        ycat_ref[rows, 2 * D_GROUP:3 * D_GROUP] = yc.astype(BF16)

        u = z_ref[hrows, C_CC:C_CC + D_GROUP] * z_ref[hrows, C_CX:C_CX + D_GROUP]
        conv = (cw[0:1] * _shift_down(u, 1)[inner] + cw[1:2] * u[inner]
                + cw[2:3] * _shift_up(u, 1)[inner])
        yd = z_ref[zrows, C_CB:C_CB + D_GROUP] * conv
        ycat_ref[rows, 3 * D_GROUP:4 * D_GROUP] = yd.astype(BF16)

    y_all = jnp.dot(jnp.concatenate([ycat_all[q, r * CHUNK:(r + 1) * CHUNK, :] for q, r in units], axis=0),
                    wout_ref[0], preferred_element_type=F32)
    g_y = gt1 * ng[1:2]
    for u, (q, r) in enumerate(units):
        rows = slice(r * CHUNK, (r + 1) * CHUNK)
        xb = x1_ref[q, rows, :] if add_pos else x_ref[q, rows, :]
        x1_ref[q, rows, :] = xb + _rms(y_all[u * CHUNK:(u + 1) * CHUNK], g_y)


def _layer_spec(shape, layer):
    nd = len(shape)
    return pl.BlockSpec((1,) + tuple(shape[1:]), lambda b: (layer,) + (0,) * (nd - 1),
                        pipeline_mode=pl.Buffered(1))


def _mixer_call(x, layer, mod_all, wts, cond_base, pos=None, state=None, emit_state=False,
                prev_states=None, nseq=1):
    bsz, seq_len, _ = x.shape
    nblk = seq_len // CHUNK
    add_pos = pos is not None
    has_state = state is not None
    n_prev = 0 if prev_states is None else prev_states[0].shape[1]
    args = [x]
    in_specs = [pl.BlockSpec((nseq, seq_len, D_MODEL), lambda b: (b, 0, 0))]
    if add_pos:
        args += list(pos)
        in_specs += [pl.BlockSpec(p.shape, lambda b: (0, 0), pipeline_mode=pl.Buffered(1)) for p in pos]
    args.append(mod_all)
    in_specs.append(_layer_spec(mod_all.shape, layer))
    for name in ("norm_g", "w_in", "gate_b", "gmlp_ws", "gmlp_bias", "pool_bd", "pool_scale",
                 "conv_w", "w_out"):
        args.append(wts[name])
        if wts[name].ndim == 2:
            in_specs.append(pl.BlockSpec(wts[name].shape, lambda b: (0, 0), pipeline_mode=pl.Buffered(1)))
        else:
            in_specs.append(_layer_spec(wts[name].shape, layer))
    if has_state:
        args += list(state)
        in_specs += [pl.BlockSpec((nseq, 1, 2, HEADS, DH, DH), lambda b: (b, layer, 0, 0, 0, 0)),
                     pl.BlockSpec((nseq, 1, 2, HEADS, DH), lambda b: (b, layer, 0, 0, 0)),
                     pl.BlockSpec((nseq, 1, 1, LANES), lambda b: (b, layer, 0, 0))]
    out_shape = [jax.ShapeDtypeStruct((bsz, seq_len, D_MODEL), F32)]
    out_specs = [pl.BlockSpec((nseq, seq_len, D_MODEL), lambda b: (b, 0, 0))]
    if emit_state:
        if n_prev:
            args += list(prev_states)
            in_specs += [pl.BlockSpec((nseq, n_prev, 2, HEADS, DH, DH), lambda b: (b, 0, 0, 0, 0, 0)),
                         pl.BlockSpec((nseq, n_prev, 2, HEADS, DH), lambda b: (b, 0, 0, 0, 0))]
        out_shape += [jax.ShapeDtypeStruct((bsz, n_prev + 1, 2, HEADS, DH, DH), F32),
                      jax.ShapeDtypeStruct((bsz, n_prev + 1, 2, HEADS, DH), F32),
                      jax.ShapeDtypeStruct((bsz, 1, LANES), F32)]
        out_specs += [pl.BlockSpec((nseq, n_prev + 1, 2, HEADS, DH, DH), lambda b: (b, 0, 0, 0, 0, 0)),
                      pl.BlockSpec((nseq, n_prev + 1, 2, HEADS, DH), lambda b: (b, 0, 0, 0, 0)),
                      pl.BlockSpec((nseq, 1, LANES), lambda b: (b, 0, 0))]
    kern = functools.partial(_mixer_kernel, layer=layer, seq_len=seq_len, nseq=nseq, add_pos=add_pos,
                             has_state=has_state, emit_state=emit_state, cond_base=cond_base,
                             n_prev=n_prev)
    return pl.pallas_call(
        kern,
        out_shape=out_shape,
        grid=(bsz // nseq,),
        in_specs=in_specs,
        out_specs=out_specs,
        scratch_shapes=[pltpu.VMEM((nseq, seq_len + 2 * HALO, Z_COLS), F32),
                        pltpu.VMEM((nseq, nblk, CHUNK, LANES), F32),
                        pltpu.VMEM((nseq, nblk, 16, CHUNK), F32),
                        pltpu.VMEM((nseq, nblk, CHUNK, LANES), F32),
                        pltpu.VMEM((nseq, 2, nblk, PAIRS, LANES, 2 * LANES), BF16),
                        pltpu.VMEM((nseq, seq_len, D_MODEL), BF16)],
        compiler_params=pltpu.CompilerParams(
            dimension_semantics=("arbitrary",), vmem_limit_bytes=VMEM_LIMIT),
        name="mixer_t%d" % seq_len,
    )(*args)


def _ffn_kernel(mod_ref, ng_ref, cw_ref, x1p_hbm, x1s_hbm, up_hbm, down_hbm, outp_hbm, outs_hbm,
                up_bf, down_bf, act_ref, h2_ref, x1_buf, o_buf, up_stage, down_stage,
                in_sems, out_sems, w_sems, *, layer, tiles_p, tiles_s, seq_p, seq_s):
    i = pl.program_id(0)
    n_tiles = tiles_p + tiles_s
    rows = FFN_ROWS
    half = FFN_SUB // 2
    nsub = D_FF // half
    slot = i % 2
    is_prompt = i < tiles_p

    def piece_rows(tile, first_tile, j):
        base = (tile - first_tile) * rows + j * FFN_GROUP
        return pl.ds(base if isinstance(base, int) else pl.multiple_of(base, FFN_GROUP), FFN_GROUP)

    def x1_copies(x1_hbm, tile, first_tile, buf):
        return [pltpu.make_async_copy(x1_hbm.at[piece_rows(tile, first_tile, j)],
                                      x1_buf.at[buf, :, j, :], in_sems.at[buf, j]) for j in range(8)]

    def out_copies(out_hbm, tile, first_tile, buf):
        return [pltpu.make_async_copy(o_buf.at[buf, :, j, :],
                                      out_hbm.at[piece_rows(tile, first_tile, j)], out_sems.at[buf, j])
                for j in range(8)]

    def fetch(tile, buf):
        @pl.when(tile < tiles_p)
        def _():
            for cp in x1_copies(x1p_hbm, tile, 0, buf):
                cp.start()

        @pl.when(tile >= tiles_p)
        def _():
            for cp in x1_copies(x1s_hbm, tile, tiles_p, buf):
                cp.start()

    def await_fetch(buf):
        for cp in x1_copies(x1p_hbm, 0, 0, buf):
            cp.wait()

    def await_store(buf):
        for cp in out_copies(outp_hbm, 0, 0, buf):
            cp.wait()

    up_rows = up_stage.shape[1]
    down_rows = down_stage.shape[1]

    def slab(c, n):
        return pl.ds(c * n, n) if isinstance(c, int) else pl.ds(pl.multiple_of(c * n, n), n)

    def weight_copies(c, buf):
        return (pltpu.make_async_copy(up_hbm.at[layer, slab(c, up_rows), :], up_stage.at[buf],
                                      w_sems.at[buf, 0]),
                pltpu.make_async_copy(down_hbm.at[layer, slab(c, down_rows), :], down_stage.at[buf],
                                      w_sems.at[buf, 1]))

    @pl.when(i == 0)
    def _():
        for cp in x1_copies(x1p_hbm, 0, 0, 0):
            cp.start()
        depth = up_stage.shape[0]
        for c0 in range(depth - 1):
            for cp in weight_copies(c0, c0):
                cp.start()

        def convert(c, carry):
            buf = c % depth
            ahead = c + depth - 1

            @pl.when(ahead < FFN_WEIGHT_CHUNKS)
            def _():
                for cp in weight_copies(ahead, ahead % depth):
                    cp.start()

            for cp in weight_copies(c, buf):
                cp.wait()
            r = slab(c, up_rows)
            for s in range(nsub):
                up_bf[s, r, 0:half] = up_stage[buf, :, s * half:(s + 1) * half].astype(BF16)
                up_bf[s, r, half:FFN_SUB] = (
                    up_stage[buf, :, D_FF + s * half:D_FF + (s + 1) * half].astype(BF16))
            down_bf[slab(c, down_rows), :] = down_stage[buf].astype(BF16)
            return carry

        lax.fori_loop(0, FFN_WEIGHT_CHUNKS, convert, 0)

    await_fetch(slot)

    @pl.when(i >= 2)
    def _():
        await_store(slot)

    @pl.when(i + 1 < n_tiles)
    def _():
        fetch(i + 1, 1 - slot)

    cond = jnp.where(is_prompt, 0, i - tiles_p + 1)
    _, _, _, sh2, sc2, gt2 = _mod_rows(mod_ref, cond)
    ng = ng_ref[0]
    x1 = x1_buf[slot].reshape(rows, D_MODEL)
    h2_ref[...] = (_rms(x1, ng[2:3] * (1.0 + sc2)) + sh2).astype(BF16)
    h2 = h2_ref[...]

    sub = lax.broadcasted_iota(jnp.int32, (8, FFN_SUB), 0)
    gp, gs = seq_p // FFN_GROUP, seq_s // FFN_GROUP
    starts = ((sub % gs) == 0) | (((sub % gp) == 0) & is_prompt)
    ends = ((sub % gs) == gs - 1) | (((sub % gp) == gp - 1) & is_prompt)
    for s in range(nsub):
        a = jnp.dot(h2, up_bf[s], preferred_element_type=F32)
        cw = jnp.concatenate([cw_ref[0, :, s * half:(s + 1) * half],
                              0.5 * cw_ref[0, :, D_FF + s * half:D_FF + (s + 1) * half]], axis=1)
        a_prev = jnp.concatenate(
            [jnp.where(starts, 0.0, pltpu.roll(a[rows - 8:rows], 1, axis=0)), a[0:rows - 8]], axis=0)
        a_next = jnp.concatenate(
            [a[8:rows], jnp.where(ends, 0.0, pltpu.roll(a[0:8], 7, axis=0))], axis=0)
        ac = cw[0:1] * a_prev + cw[1:2] * a + cw[2:3] * a_next
        hg = ac[:, half:]
        act = (hg * (1.0 + jnp.tanh(hg))) * ac[:, :half]
        act_ref[:, s * half:(s + 1) * half] = act.astype(BF16)
    y = jnp.dot(act_ref[...], down_bf[...], preferred_element_type=F32)
    out = x1_buf[slot].reshape(rows, D_MODEL) + gt2 * _rms(y, ng[3:4])
    o_buf[slot] = out.reshape(FFN_GROUP, 8, D_MODEL)

    @pl.when(is_prompt)
    def _():
        for cp in out_copies(outp_hbm, i, 0, slot):
            cp.start()

    @pl.when(jnp.logical_not(is_prompt))
    def _():
        for cp in out_copies(outs_hbm, i, tiles_p, slot):
            cp.start()

    @pl.when(i == n_tiles - 1)
    def _():
        await_store(1 - slot)
        await_store(slot)


def _ffn_call(x1p, x1s, layer, mod_all, wts, seq_p, seq_s):
    tiles_p = x1p.shape[0] // FFN_ROWS
    tiles_s = x1s.shape[0] // FFN_ROWS
    assert tiles_p >= 1 and tiles_s >= 1 and tiles_p + tiles_s >= 2
    assert FFN_ROWS == 8 * FFN_GROUP and seq_p % FFN_GROUP == 0 and seq_s % seq_p == 0
    half = FFN_SUB // 2

    def resident(shape):
        nd = len(shape)
        return pl.BlockSpec((1,) + tuple(shape[1:]), lambda i: (layer,) + (0,) * (nd - 1),
                            pipeline_mode=pl.Buffered(1))

    hbm = pl.BlockSpec(memory_space=pl.ANY)
    return pl.pallas_call(
        functools.partial(_ffn_kernel, layer=layer, tiles_p=tiles_p, tiles_s=tiles_s,
                          seq_p=seq_p, seq_s=seq_s),
        out_shape=[jax.ShapeDtypeStruct(x1p.shape, F32), jax.ShapeDtypeStruct(x1s.shape, F32)],
        grid=(tiles_p + tiles_s,),
        in_specs=[resident(mod_all.shape), resident(wts["norm_g"].shape), resident(wts["ffn_conv"].shape),
                  hbm, hbm, hbm, hbm],
        out_specs=[hbm, hbm],
        scratch_shapes=[pltpu.VMEM((D_FF // half, D_MODEL, FFN_SUB), BF16),
                        pltpu.VMEM((D_FF, D_MODEL), BF16),
                        pltpu.VMEM((FFN_ROWS, D_FF), BF16),
                        pltpu.VMEM((FFN_ROWS, D_MODEL), BF16),
                        pltpu.VMEM((2, FFN_GROUP, 8, D_MODEL), F32),
                        pltpu.VMEM((2, FFN_GROUP, 8, D_MODEL), F32),
                        pltpu.VMEM((FFN_WEIGHT_BUFS, D_MODEL // FFN_WEIGHT_CHUNKS, 2 * D_FF), F32),
                        pltpu.VMEM((FFN_WEIGHT_BUFS, D_FF // FFN_WEIGHT_CHUNKS, D_MODEL), F32),
                        pltpu.SemaphoreType.DMA((2, 8)),
                        pltpu.SemaphoreType.DMA((2, 8)),
                        pltpu.SemaphoreType.DMA((FFN_WEIGHT_BUFS, 2))],
        compiler_params=pltpu.CompilerParams(
            dimension_semantics=("arbitrary",), vmem_limit_bytes=VMEM_LIMIT),
        name="ffn",
    )(mod_all, wts["norm_g"], wts["ffn_conv"], x1p, x1s, wts["ffn_up"], wts["ffn_down"])


def _pos_tables(t_len):
    quarter = D_MODEL // 4
    omega = (1.0 / (np.float32(POS_BASE) ** (np.arange(quarter, dtype=np.float32) / np.float32(quarter))))
    omega = omega.astype(np.float32)

    def emb(p):
        a = p.astype(np.float32)[:, None] * omega[None, :]
        return np.concatenate([np.sin(a), np.cos(a)], axis=-1).astype(np.float32)

    return jnp.asarray(emb(np.arange(t_len // GRID_W))), jnp.asarray(emb(np.arange(GRID_W)))


def _stabiliser_rows(state_m):
    bs = state_m.shape[0]
    m = jnp.pad(state_m.astype(F32), ((0, 0), (0, 0), (0, 0), (0, 16 - HEADS)))
    return jnp.pad(m.reshape(bs, DEPTH, 1, 32), ((0, 0), (0, 0), (0, 0), (L_MF, LANES - 32 - L_MF)))


def kernel(x_prompt, x_sample, state_C, state_n, state_m, c, c_ctx, norm_g, ada_w, ada_b, w_in,
           gate_b, gmlp_ws, gmlp_b, pool_w, pool_scale, conv_w, w_out, ffn_up, ffn_conv, ffn_down):
    bp, tp, _ = x_prompt.shape
    bs, ts, _ = x_sample.shape
    assert tp == CHUNK and ts % CHUNK == 0 and FFN_ROWS % tp == 0 and ts == FFN_ROWS

    conds = jnp.concatenate([c_ctx[None, :], c, jnp.zeros((7 - bs, D_MODEL), F32)], axis=0)
    mod_all = _modulation(conds, ada_w, ada_b)
    win_p, wout_p = _prep_weights(w_in, w_out)

    gb = jnp.pad(gate_b, ((0, 0), (0, 0), (0, 8 - HEADS))).reshape(DEPTH, 1, 32)
    dg = D_GROUP // 4
    wts = {
        "norm_g": norm_g,
        "w_in": win_p,
        "gate_b": jnp.pad(gb, ((0, 0), (0, 0), (0, LANES - 32))),
        "gmlp_ws": gmlp_ws,
        "gmlp_bias": jnp.repeat(jnp.swapaxes(gmlp_b, 1, 2), dg, axis=2),
        "pool_bd": jnp.einsum("lgcd,gh->lgchd", pool_w, jnp.eye(4, dtype=F32)).reshape(
            DEPTH, D_GROUP, D_GROUP).astype(BF16),
        "pool_scale": pool_scale,
        "conv_w": conv_w,
        "w_out": wout_p,
        "ffn_up": ffn_up,
        "ffn_conv": ffn_conv,
        "ffn_down": ffn_down,
    }
    pos = _pos_tables(ts)
    state = (state_C.astype(F32), state_n.astype(F32), _stabiliser_rows(state_m))

    xp, xs = x_prompt, x_sample
    prev, ms = None, []
    for l in range(DEPTH):
        x1p, c_all, n_all, m_fin = _mixer_call(xp, l, mod_all, wts, 0, emit_state=True,
                                               prev_states=prev, nseq=MIXER_ROWS // tp)
        prev = (c_all, n_all)
        ms.append(jnp.stack([m_fin[:, 0, L_MF:L_MF + HEADS], m_fin[:, 0, L_MB:L_MB + HEADS]], axis=1))

        (x1s,) = _mixer_call(xs, l, mod_all, wts, 1, pos=pos if l == 0 else None, state=state)

        xp, xs = _ffn_call(x1p.reshape(bp * tp, D_MODEL), x1s.reshape(bs * ts, D_MODEL),
                           l, mod_all, wts, tp, ts)
        xp = xp.reshape(bp, tp, D_MODEL)
        xs = xs.reshape(bs, ts, D_MODEL)

    return (xp, xs, c_all, n_all, jnp.stack(ms, axis=1))
```

```python
import functools

import numpy as np
import jax
import jax.numpy as jnp
from jax import lax
from jax.experimental import pallas as pl
from jax.experimental.pallas import tpu as pltpu

F32 = jnp.float32
BF16 = jnp.bfloat16

D_MODEL = 1024
DEPTH = 2
HEADS = 4
PAIRS = HEADS // 2
DH = 64
D_GROUP = 256
D_FF = 2816
N_MOD = 6
EPS = 1e-6
GRID_W = 64
POS_BASE = 10000.0
D_IN = 2576
N_GATE = 4 * HEADS

CHUNK = 256
GMLP_CHUNK = 128
HALO = 8
LANES = 128

C_Q, C_K, C_V, C_O = 0, 256, 512, 768
C_GU, C_GV, C_PZ, C_CB, C_CC, C_CX = 1024, 1280, 1536, 1792, 2048, 2304
Z_COLS = 2560
C_GATE = Z_COLS
W_COLS = Z_COLS + LANES
Z_GROUP = 1280
L_BF, L_MF, L_BB, L_MB = 0, 8, 16, 24
L_STEP = L_MF - L_BF

MIXER_ROWS = 1024
FFN_ROWS = 1024
FFN_GROUP = FFN_ROWS // 8
FFN_SUB = 512
FFN_WEIGHT_CHUNKS = 16
FFN_WEIGHT_BUFS = 4
VMEM_LIMIT = 60 * 1024 * 1024


def _rms(x, g):
    ms = jnp.mean(x * x, axis=-1, keepdims=True)
    return x * lax.rsqrt(ms + EPS) * g


def _sigmoid(x):
    return 1.0 / (1.0 + jnp.exp(-x))


def _log_sigmoid(x):
    return jnp.minimum(x, 0.0) - jnp.log1p(jnp.exp(-jnp.abs(x)))


def _scan_lanes(x, op, reverse, fill):
    n = x.shape[1]
    lane = lax.broadcasted_iota(jnp.int32, x.shape, 1)
    k = 1
    while k < n:
        if reverse:
            sh = jnp.where(lane < n - k, pltpu.roll(x, n - k, axis=1), fill)
        else:
            sh = jnp.where(lane >= k, pltpu.roll(x, k, axis=1), fill)
        x = op(x, sh)
        k *= 2
    return x


def _shift_down(x, k):
    return pltpu.roll(x, k, axis=0)


def _shift_up(x, k):
    return pltpu.roll(x, x.shape[0] - k, axis=0)


def _prep_kernel(wt_ref, sel_ref, wout_ref, win_o_ref, wout_o_ref):
    for n in range(Z_COLS // LANES):
        r0 = LANES * n if n < 4 * D_GROUP // LANES else LANES * n + N_GATE
        win_o_ref[0, :, LANES * n:LANES * (n + 1)] = wt_ref[0, r0:r0 + LANES, :].T.astype(BF16)
    g = wt_ref[0, 4 * D_GROUP:4 * D_GROUP + LANES, :].T.astype(BF16)
    win_o_ref[0, :, C_GATE:W_COLS] = jnp.dot(g, sel_ref[...], preferred_element_type=F32).astype(BF16)
    wout_o_ref[...] = wout_ref[...].astype(BF16)


def _prep_weights(w_in, w_out):
    sel = np.zeros((LANES, LANES), np.float32)
    for j in range(N_GATE):
        sel[j, 8 * (j // HEADS) + j % HEADS] = 1.0
    rows = D_MODEL // 2
    return pl.pallas_call(
        _prep_kernel,
        out_shape=[jax.ShapeDtypeStruct((DEPTH, D_MODEL, W_COLS), BF16),
                   jax.ShapeDtypeStruct(w_out.shape, BF16)],
        grid=(DEPTH, D_MODEL // rows),
        in_specs=[pl.BlockSpec((1, D_IN, rows), lambda l, i: (l, 0, i)),
                  pl.BlockSpec((LANES, LANES), lambda l, i: (0, 0)),
                  pl.BlockSpec((1, rows, D_MODEL), lambda l, i: (l, i, 0))],
        out_specs=[pl.BlockSpec((1, rows, W_COLS), lambda l, i: (l, i, 0)),
                   pl.BlockSpec((1, rows, D_MODEL), lambda l, i: (l, i, 0))],
        compiler_params=pltpu.CompilerParams(dimension_semantics=("arbitrary", "arbitrary"),
                                             vmem_limit_bytes=VMEM_LIMIT),
        name="prep_mixer_w")(jnp.swapaxes(w_in, 1, 2), jnp.asarray(sel, BF16), w_out)


def _mod_kernel(c_ref, w_ref, b_ref, o_ref):
    c = c_ref[...]
    s = (c * _sigmoid(c)).astype(BF16)
    bias = b_ref[pl.ds(pl.program_id(0), 1), :]
    o_ref[0] = jnp.dot(s, w_ref[0].astype(BF16), preferred_element_type=F32) + bias


def _modulation(conds, ada_w, ada_b):
    n_out = N_MOD * D_MODEL
    bn = 1536
    return pl.pallas_call(
        _mod_kernel,
        out_shape=jax.ShapeDtypeStruct((DEPTH, 8, n_out), F32),
        grid=(DEPTH, n_out // bn),
        in_specs=[
            pl.BlockSpec((8, D_MODEL), lambda l, j: (0, 0)),
            pl.BlockSpec((1, D_MODEL, bn), lambda l, j: (l, 0, j)),
            pl.BlockSpec((DEPTH, bn), lambda l, j: (0, j)),
        ],
        out_specs=pl.BlockSpec((1, 8, bn), lambda l, j: (l, 0, j)),
        compiler_params=pltpu.CompilerParams(
            dimension_semantics=("arbitrary", "arbitrary"), vmem_limit_bytes=VMEM_LIMIT),
        name="adaln_mod",
    )(conds, ada_w, ada_b)


def _mod_rows(mod_ref, cond):
    row = mod_ref[0, pl.ds(cond, 1), :]
    return [row[:, i * D_MODEL:(i + 1) * D_MODEL] for i in range(N_MOD)]


def _gate_stats(gates):
    gt = gates.T
    b_f = _scan_lanes(_log_sigmoid(gt[8:16]), jnp.add, False, 0.0)
    g_f = gt[0:8] - b_f
    cm_f = _scan_lanes(g_f, jnp.maximum, False, -jnp.inf)
    b_b = _scan_lanes(_log_sigmoid(gt[24:32]), jnp.add, True, 0.0)
    g_b = gt[16:24] - b_b
    cm_b = _scan_lanes(g_b, jnp.maximum, True, -jnp.inf)
    stack = jnp.concatenate([b_f, cm_f, b_b, cm_b, jnp.zeros((LANES - 32, CHUNK), F32)], axis=0)
    return jnp.concatenate([g_f, g_b], axis=0), stack.T


def _pair_cols(mat, lane0, lane_lt_dh):
    return jnp.where(lane_lt_dh, mat[:, lane0:lane0 + 1], mat[:, lane0 + 1:lane0 + 2])


def _mlstm(z_ref, gcol_ref, grow_ref, cols_ref, bd_ref, ycat_ref, nc, state0, m0, need_final):
    lane_row = lax.broadcasted_iota(jnp.int32, (1, LANES), 1)
    lane = lax.broadcasted_iota(jnp.int32, (CHUNK, LANES), 1)
    lt_dh = lane < DH
    row_ll = lax.broadcasted_iota(jnp.int32, (CHUNK, CHUNK), 0)
    col_ll = lax.broadcasted_iota(jnp.int32, (CHUNK, CHUNK), 1)
    tri = (col_ll <= row_ll, col_ll >= row_ll)
    bd_row = lax.broadcasted_iota(jnp.int32, (LANES, 2 * LANES), 0)
    bd_col = lax.broadcasted_iota(jnp.int32, (LANES, 2 * LANES), 1)
    bd_mask = (bd_row < DH) == ((bd_col & (LANES - 1)) < DH)
    dirs = ((0, L_BF, L_MF, CHUNK - 1), (1, L_BB, L_MB, 0))

    m_start = [[None] * nc, [None] * nc]
    m_fin = [None, None]
    for d, _, _, last in dirs:
        m = m0
        order = range(nc) if d == 0 else range(nc - 1, -1, -1)
        for step, c in enumerate(order):
            m_start[d][c] = m
            if need_final or step < nc - 1:
                cl = cols_ref[c, last:last + 1, :]
                m = pltpu.roll(cl, L_STEP, axis=1) + jnp.maximum(m, cl)
        m_fin[d] = m

    have_state = [[False] * nc, [False] * nc]
    fin = [None, None]
    for d, l_b, l_m, last in dirs:
        bd = list(state0[d])
        order = range(nc) if d == 0 else range(nc - 1, -1, -1)
        for step, c in enumerate(order):
            if bd[0] is not None:
                have_state[d][c] = True
                for p in range(PAIRS):
                    bd_ref[d, c, p] = bd[p].astype(BF16)
            if not (need_final or step < nc - 1):
                continue
            rows = slice(HALO + c * CHUNK, HALO + (c + 1) * CHUNK)
            cols = cols_ref[c]
            m_last = jnp.maximum(cols[last:last + 1, :], m_start[d][c])
            a_col = jnp.exp(gcol_ref[c] - cols - pltpu.roll(m_last, LANES - L_STEP, axis=1))
            a_prev = jnp.exp(m_start[d][c] - m_last)
            for p in range(PAIRS):
                a_pair = _pair_cols(a_col, l_b + 2 * p, lt_dh)
                v_pair = z_ref[rows, C_V + LANES * p:C_V + LANES * (p + 1)]
                rhs = jnp.concatenate([v_pair * a_pair, a_pair], axis=1).astype(BF16)
                k_pair = (z_ref[rows, C_K + LANES * p:C_K + LANES * (p + 1)] * (DH ** -0.5)).astype(BF16)
                upd = lax.dot_general(k_pair, rhs, (((0,), (0,)), ((), ())),
                                      preferred_element_type=F32)
                upd = jnp.where(bd_mask, upd, 0.0)
                if bd[p] is None:
                    bd[p] = upd
                else:
                    scale = jnp.where(bd_row < DH, a_prev[:, l_m + 2 * p:l_m + 2 * p + 1],
                                      a_prev[:, l_m + 2 * p + 1:l_m + 2 * p + 2])
                    bd[p] = scale * bd[p] + upd
        fin[d] = bd

    for c in range(nc):
        rows = slice(HALO + c * CHUNK, HALO + (c + 1) * CHUNK)
        cols = cols_ref[c]
        grow = grow_ref[c]
        m_prev = jnp.where(lane_row < L_BB, m_start[0][c], m_start[1][c])
        m_all = jnp.maximum(cols, m_prev)
        w_inter = jnp.exp(m_prev - m_all)
        e_neg = jnp.exp(-(pltpu.roll(cols, L_STEP, axis=1) + m_all))
        for p in range(PAIRS):
            q_pair = z_ref[rows, C_Q + LANES * p:C_Q + LANES * (p + 1)].astype(BF16)
            k_f32 = z_ref[rows, C_K + LANES * p:C_K + LANES * (p + 1)] * (DH ** -0.5)
            v_pair = z_ref[rows, C_V + LANES * p:C_V + LANES * (p + 1)]
            p_cat = ([], [])
            va = []
            for j in range(2):
                h = 2 * p + j
                own = lt_dh if j == 0 else jnp.logical_not(lt_dh)
                k_own = jnp.where(own, k_f32, 0.0).astype(BF16)
                s = lax.dot_general(q_pair, k_own, (((1,), (1,)), ((), ())),
                                    preferred_element_type=F32)
                va.append(jnp.concatenate([jnp.where(own, v_pair, 0.0), jnp.where(own, 1.0, 0.0)],
                                          axis=1).astype(BF16))
                for d, _, l_m, _ in dirs:
                    w = jnp.exp(jnp.where(tri[d], grow[8 * d + h:8 * d + h + 1] - m_all[:, l_m + h:l_m + h + 1],
                                          -jnp.inf))
                    p_cat[d].append((s * w).astype(BF16))
            va = jnp.concatenate(va, axis=0)
            h_sum = None
            for d, _, l_m, _ in dirs:
                tot = jnp.dot(jnp.concatenate(p_cat[d], axis=1), va, preferred_element_type=F32)
                if have_state[d][c]:
                    e_pair = _pair_cols(w_inter, l_m + 2 * p, lt_dh)
                    qc = jnp.dot(q_pair, bd_ref[d, c, p], preferred_element_type=F32)
                    tot = tot + jnp.concatenate([e_pair, e_pair], axis=1) * qc
                floor = _pair_cols(e_neg, l_m + 2 * p, lt_dh)
                h_dir = tot[:, :LANES] / jnp.maximum(jnp.abs(tot[:, LANES:]), floor)
                h_sum = h_dir if h_sum is None else h_sum + h_dir
            o_pair = z_ref[rows, C_O + LANES * p:C_O + LANES * (p + 1)]
            ycat_ref[c * CHUNK:(c + 1) * CHUNK, LANES * p:LANES * (p + 1)] = (
                _sigmoid(o_pair) * h_sum).astype(BF16)

    m_row = jnp.where(lane_row < L_BB, m_fin[0], m_fin[1])
    return fin, m_row


def _pair_state(c_ref, n_ref, q, d, p):
    eye = (lax.broadcasted_iota(jnp.int32, (DH, DH), 0) == lax.broadcasted_iota(jnp.int32, (DH, DH), 1))
    zero = jnp.zeros((DH, DH), F32)
    blocks = []
    for j in range(2):
        h = 2 * p + j
        n_row = n_ref[q, 0, d, h:h + 1, :]
        n_col = jnp.sum(jnp.where(eye, n_row, 0.0), axis=1, keepdims=True)
        row = [zero] * 4
        row[j] = c_ref[q, 0, d, h]
        row[2 + j] = jnp.broadcast_to(n_col, (DH, DH))
        blocks.append(jnp.concatenate(row, axis=1))
    return jnp.concatenate(blocks, axis=0)


def _mixer_kernel(*refs, layer, seq_len, nseq, add_pos, has_state, emit_state, cond_base, n_prev):
    it = iter(refs)
    x_ref = next(it)
    if add_pos:
        er_ref = next(it)
        ec_ref = next(it)
    mod_ref = next(it)
    ng_ref = next(it)
    win_ref = next(it)
    gb_ref = next(it)
    ws_ref = next(it)
    gbias_ref = next(it)
    pbd_ref = next(it)
    psc_ref = next(it)
    cw_ref = next(it)
    wout_ref = next(it)
    if has_state:
        c0_ref = next(it)
        n0_ref = next(it)
        m0_ref = next(it)
    if n_prev:
        cprev_ref = next(it)
        nprev_ref = next(it)
    x1_ref = next(it)
    if emit_state:
        cout_ref = next(it)
        nout_ref = next(it)
        mout_ref = next(it)
    z_all = next(it)
    gcol_all = next(it)
    grow_all = next(it)
    cols_all = next(it)
    bd_all = next(it)
    ycat_all = next(it)

    nblk = seq_len // CHUNK
    cond = cond_base if cond_base == 0 else cond_base + pl.program_id(0)
    sh1, sc1, gt1 = _mod_rows(mod_ref, cond)[:3]
    ng = ng_ref[0]
    seqs = range(nseq)
    units = [(q, r) for q in seqs for r in range(nblk)]

    zeros_halo = jnp.zeros((HALO, Z_COLS - C_PZ), F32)
    for q in seqs:
        z_all[q, 0:HALO, C_PZ:Z_COLS] = zeros_halo
        z_all[q, HALO + seq_len:2 * HALO + seq_len, C_PZ:Z_COLS] = zeros_halo

    g_in = ng[0:1] * (1.0 + sc1)
    hbs = []
    for q, r in units:
        rows = slice(r * CHUNK, (r + 1) * CHUNK)
        xb = x_ref[q, rows, :]
        if add_pos:
            pieces = []
            for g in range(CHUNK // GRID_W):
                gi = r * (CHUNK // GRID_W) + g
                er = jnp.broadcast_to(er_ref[gi:gi + 1, :], (GRID_W, D_MODEL // 2))
                pieces.append(jnp.concatenate([er, ec_ref[...]], axis=1))
            xb = xb + jnp.concatenate(pieces, axis=0)
            x1_ref[q, rows, :] = xb
        hbs.append((_rms(xb, g_in) + sh1).astype(BF16))
    hb = jnp.concatenate(hbs, axis=0)
    gates_all = jnp.dot(hb, win_ref[0, :, C_GATE:W_COLS], preferred_element_type=F32) + gb_ref[0]
    for cg in range(Z_COLS // Z_GROUP):
        zc = jnp.dot(hb, win_ref[0, :, cg * Z_GROUP:(cg + 1) * Z_GROUP], preferred_element_type=F32)
        for u, (q, r) in enumerate(units):
            z_all[q, HALO + r * CHUNK:HALO + (r + 1) * CHUNK, cg * Z_GROUP:(cg + 1) * Z_GROUP] = (
                zc[u * CHUNK:(u + 1) * CHUNK])
    for u, (q, r) in enumerate(units):
        gates = gates_all[u * CHUNK:(u + 1) * CHUNK]
        gcol_all[q, r] = gates
        grow, cols = _gate_stats(gates)
        grow_all[q, r] = grow
        cols_all[q, r] = cols

    for q in seqs:
        if has_state:
            state0 = [[_pair_state(c0_ref, n0_ref, q, d, p) for p in range(PAIRS)] for d in range(2)]
            m0 = m0_ref[q, 0]
        else:
            state0 = [[None] * PAIRS, [None] * PAIRS]
            m0 = jnp.zeros((1, LANES), F32)
        fin, m_row = _mlstm(z_all.at[q], gcol_all.at[q], grow_all.at[q], cols_all.at[q], bd_all.at[q],
                            ycat_all.at[q], nblk, state0, m0, need_final=emit_state)
        if emit_state:
            for d in range(2):
                for p in range(PAIRS):
                    for j in range(2):
                        cout_ref[q, n_prev, d, 2 * p + j] = fin[d][p][DH * j:DH * (j + 1), DH * j:DH * (j + 1)]
                    n_t = fin[d][p][:, LANES:].T
                    nout_ref[q, n_prev, d, 2 * p:2 * p + 1, :] = n_t[0:1, 0:DH]
                    nout_ref[q, n_prev, d, 2 * p + 1:2 * p + 2, :] = n_t[DH:DH + 1, DH:LANES]
            mout_ref[q] = m_row
    if emit_state and n_prev:
        cout_ref[:, 0:n_prev] = cprev_ref[...]
        nout_ref[:, 0:n_prev] = nprev_ref[...]

    lane_g = lax.broadcasted_iota(jnp.int32, (CHUNK, D_GROUP), 1) // (D_GROUP // 4)
    lg = lax.broadcasted_iota(jnp.int32, (GMLP_CHUNK, D_GROUP), 1) // (D_GROUP // 4)
    row_c = lax.broadcasted_iota(jnp.int32, (CHUNK, D_GROUP), 0)
    half = jnp.where(lane_g == 0, 1, jnp.where(lane_g == 1, 2, jnp.where(lane_g == 2, 4, 8)))
    cw = cw_ref[0]
    ws_cat = jnp.concatenate([ws_ref[0, g] for g in range(4)], axis=1).astype(BF16)
    for q, r in units:
        z_ref = z_all.at[q]
        ycat_ref = ycat_all.at[q]
        rows = slice(r * CHUNK, (r + 1) * CHUNK)
        zrows = slice(HALO + r * CHUNK, HALO + (r + 1) * CHUNK)
        hrows = slice(r * CHUNK, (r + 1) * CHUNK + 2 * HALO)
        inner = slice(HALO, HALO + CHUNK)

        for s in range(CHUNK // GMLP_CHUNK):
            zs = slice(HALO + r * CHUNK + s * GMLP_CHUNK, HALO + r * CHUNK + (s + 1) * GMLP_CHUNK)
            vch = z_ref[zs, C_GV:C_GV + D_GROUP]
            v_stack = jnp.concatenate([jnp.where(lg == g, vch, 0.0) for g in range(4)], axis=0)
            mixed = gbias_ref[0] + jnp.dot(ws_cat, v_stack.astype(BF16), preferred_element_type=F32)
            yb = z_ref[zs, C_GU:C_GU + D_GROUP] * mixed
            ys = slice(r * CHUNK + s * GMLP_CHUNK, r * CHUNK + (s + 1) * GMLP_CHUNK)
            ycat_ref[ys, D_GROUP:2 * D_GROUP] = yb.astype(BF16)

        pz = z_ref[hrows, C_PZ:C_PZ + D_GROUP]
        a2 = pz + _shift_down(pz, 1)
        a4 = a2 + _shift_down(a2, 2)
        a8 = a4 + _shift_down(a4, 4)
        a16 = a8 + _shift_down(a8, 8)
        win = [a2[inner], _shift_up(a4, 1)[inner], _shift_up(a8, 3)[inner], _shift_up(a16, 7)[inner]]
        wsum = jnp.where(lane_g == 0, win[0], jnp.where(lane_g == 1, win[1],
                         jnp.where(lane_g == 2, win[2], win[3])))
        t_abs = row_c + r * CHUNK
        cnt = (jnp.minimum(t_abs + half, seq_len) - jnp.maximum(t_abs - half, 0)).astype(F32)
        pooled = wsum / cnt - pz[inner]
        yc = (jnp.dot(pooled.astype(BF16), pbd_ref[0], preferred_element_type=F32)
              * psc_ref[layer:layer + 1, :])
        ycat_ref[rows, 2 * D_GROUP:3 * D_GROUP] = yc.astype(BF16)

        u = z_ref[hrows, C_CC:C_CC + D_GROUP] * z_ref[hrows, C_CX:C_CX + D_GROUP]
        conv = (cw[0:1] * _shift_down(u, 1)[inner] + cw[1:2] * u[inner]
                + cw[2:3] * _shift_up(u, 1)[inner])
        yd = z_ref[zrows, C_CB:C_CB + D_GROUP] * conv
        ycat_ref[rows, 3 * D_GROUP:4 * D_GROUP] = yd.astype(BF16)

    y_all = jnp.dot(jnp.concatenate([ycat_all[q, r * CHUNK:(r + 1) * CHUNK, :] for q, r in units], axis=0),
                    wout_ref[0], preferred_element_type=F32)
    g_y = gt1 * ng[1:2]
    for u, (q, r) in enumerate(units):
        rows = slice(r * CHUNK, (r + 1) * CHUNK)
        xb = x1_ref[q, rows, :] if add_pos else x_ref[q, rows, :]
        x1_ref[q, rows, :] = xb + _rms(y_all[u * CHUNK:(u + 1) * CHUNK], g_y)


def _layer_spec(shape, layer):
    nd = len(shape)
    return pl.BlockSpec((1,) + tuple(shape[1:]), lambda b: (layer,) + (0,) * (nd - 1),
                        pipeline_mode=pl.Buffered(1))


def _mixer_call(x, layer, mod_all, wts, cond_base, pos=None, state=None, emit_state=False,
                prev_states=None, nseq=1):
    bsz, seq_len, _ = x.shape
    nblk = seq_len // CHUNK
    add_pos = pos is not None
    has_state = state is not None
    n_prev = 0 if prev_states is None else prev_states[0].shape[1]
    args = [x]
    in_specs = [pl.BlockSpec((nseq, seq_len, D_MODEL), lambda b: (b, 0, 0))]
    if add_pos:
        args += list(pos)
        in_specs += [pl.BlockSpec(p.shape, lambda b: (0, 0), pipeline_mode=pl.Buffered(1)) for p in pos]
    args.append(mod_all)
    in_specs.append(_layer_spec(mod_all.shape, layer))
    for name in ("norm_g", "w_in", "gate_b", "gmlp_ws", "gmlp_bias", "pool_bd", "pool_scale",
                 "conv_w", "w_out"):
        args.append(wts[name])
        if wts[name].ndim == 2:
            in_specs.append(pl.BlockSpec(wts[name].shape, lambda b: (0, 0), pipeline_mode=pl.Buffered(1)))
        else:
            in_specs.append(_layer_spec(wts[name].shape, layer))
    if has_state:
        args += list(state)
        in_specs += [pl.BlockSpec((nseq, 1, 2, HEADS, DH, DH), lambda b: (b, layer, 0, 0, 0, 0)),
                     pl.BlockSpec((nseq, 1, 2, HEADS, DH), lambda b: (b, layer, 0, 0, 0)),
                     pl.BlockSpec((nseq, 1, 1, LANES), lambda b: (b, layer, 0, 0))]
    out_shape = [jax.ShapeDtypeStruct((bsz, seq_len, D_MODEL), F32)]
    out_specs = [pl.BlockSpec((nseq, seq_len, D_MODEL), lambda b: (b, 0, 0))]
    if emit_state:
        if n_prev:
            args += list(prev_states)
            in_specs += [pl.BlockSpec((nseq, n_prev, 2, HEADS, DH, DH), lambda b: (b, 0, 0, 0, 0, 0)),
                         pl.BlockSpec((nseq, n_prev, 2, HEADS, DH), lambda b: (b, 0, 0, 0, 0))]
        out_shape += [jax.ShapeDtypeStruct((bsz, n_prev + 1, 2, HEADS, DH, DH), F32),
                      jax.ShapeDtypeStruct((bsz, n_prev + 1, 2, HEADS, DH), F32),
                      jax.ShapeDtypeStruct((bsz, 1, LANES), F32)]
        out_specs += [pl.BlockSpec((nseq, n_prev + 1, 2, HEADS, DH, DH), lambda b: (b, 0, 0, 0, 0, 0)),
                      pl.BlockSpec((nseq, n_prev + 1, 2, HEADS, DH), lambda b: (b, 0, 0, 0, 0)),
                      pl.BlockSpec((nseq, 1, LANES), lambda b: (b, 0, 0))]
    kern = functools.partial(_mixer_kernel, layer=layer, seq_len=seq_len, nseq=nseq, add_pos=add_pos,
                             has_state=has_state, emit_state=emit_state, cond_base=cond_base,
                             n_prev=n_prev)
    return pl.pallas_call(
        kern,
        out_shape=out_shape,
        grid=(bsz // nseq,),
        in_specs=in_specs,
        out_specs=out_specs,
        scratch_shapes=[pltpu.VMEM((nseq, seq_len + 2 * HALO, Z_COLS), F32),
                        pltpu.VMEM((nseq, nblk, CHUNK, LANES), F32),
                        pltpu.VMEM((nseq, nblk, 16, CHUNK), F32),
                        pltpu.VMEM((nseq, nblk, CHUNK, LANES), F32),
                        pltpu.VMEM((nseq, 2, nblk, PAIRS, LANES, 2 * LANES), BF16),
                        pltpu.VMEM((nseq, seq_len, D_MODEL), BF16)],
        compiler_params=pltpu.CompilerParams(
            dimension_semantics=("arbitrary",), vmem_limit_bytes=VMEM_LIMIT),
        name="mixer_t%d" % seq_len,
    )(*args)


def _ffn_kernel(mod_ref, ng_ref, cw_ref, x1p_hbm, x1s_hbm, up_hbm, down_hbm, outp_hbm, outs_hbm,
                up_bf, down_bf, act_ref, h2_ref, x1_buf, o_buf, up_stage, down_stage,
                in_sems, out_sems, w_sems, *, layer, tiles_p, tiles_s, seq_p, seq_s):
    i = pl.program_id(0)
    n_tiles = tiles_p + tiles_s
    rows = FFN_ROWS
    half = FFN_SUB // 2
    nsub = D_FF // half
    slot = i % 2
    is_prompt = i < tiles_p

    def piece_rows(tile, first_tile, j):
        base = (tile - first_tile) * rows + j * FFN_GROUP
        return pl.ds(base if isinstance(base, int) else pl.multiple_of(base, FFN_GROUP), FFN_GROUP)

    def x1_copies(x1_hbm, tile, first_tile, buf):
        return [pltpu.make_async_copy(x1_hbm.at[piece_rows(tile, first_tile, j)],
                                      x1_buf.at[buf, :, j, :], in_sems.at[buf, j]) for j in range(8)]

    def out_copies(out_hbm, tile, first_tile, buf):
        return [pltpu.make_async_copy(o_buf.at[buf, :, j, :],
                                      out_hbm.at[piece_rows(tile, first_tile, j)], out_sems.at[buf, j])
                for j in range(8)]

    def fetch(tile, buf):
        @pl.when(tile < tiles_p)
        def _():
            for cp in x1_copies(x1p_hbm, tile, 0, buf):
                cp.start()

        @pl.when(tile >= tiles_p)
        def _():
            for cp in x1_copies(x1s_hbm, tile, tiles_p, buf):
                cp.start()

    def await_fetch(buf):
        for cp in x1_copies(x1p_hbm, 0, 0, buf):
            cp.wait()

    def await_store(buf):
        for cp in out_copies(outp_hbm, 0, 0, buf):
            cp.wait()

    up_rows = up_stage.shape[1]
    down_rows = down_stage.shape[1]

    def slab(c, n):
        return pl.ds(c * n, n) if isinstance(c, int) else pl.ds(pl.multiple_of(c * n, n), n)

    def weight_copies(c, buf):
        return (pltpu.make_async_copy(up_hbm.at[layer, slab(c, up_rows), :], up_stage.at[buf],
                                      w_sems.at[buf, 0]),
                pltpu.make_async_copy(down_hbm.at[layer, slab(c, down_rows), :], down_stage.at[buf],
                                      w_sems.at[buf, 1]))

    @pl.when(i == 0)
    def _():
        for cp in x1_copies(x1p_hbm, 0, 0, 0):
            cp.start()
        depth = up_stage.shape[0]
        for c0 in range(depth - 1):
            for cp in weight_copies(c0, c0):
                cp.start()

        def convert(c, carry):
            buf = c % depth
            ahead = c + depth - 1

            @pl.when(ahead < FFN_WEIGHT_CHUNKS)
            def _():
                for cp in weight_copies(ahead, ahead % depth):
                    cp.start()

            for cp in weight_copies(c, buf):
                cp.wait()
            r = slab(c, up_rows)
            for s in range(nsub):
                up_bf[s, r, 0:half] = up_stage[buf, :, s * half:(s + 1) * half].astype(BF16)
                up_bf[s, r, half:FFN_SUB] = (
                    up_stage[buf, :, D_FF + s * half:D_FF + (s + 1) * half].astype(BF16))
            down_bf[slab(c, down_rows), :] = down_stage[buf].astype(BF16)
            return carry

        lax.fori_loop(0, FFN_WEIGHT_CHUNKS, convert, 0)

    await_fetch(slot)

    @pl.when(i >= 2)
    def _():
        await_store(slot)

    @pl.when(i + 1 < n_tiles)
    def _():
        fetch(i + 1, 1 - slot)

    cond = jnp.where(is_prompt, 0, i - tiles_p + 1)
    _, _, _, sh2, sc2, gt2 = _mod_rows(mod_ref, cond)
    ng = ng_ref[0]
    x1 = x1_buf[slot].reshape(rows, D_MODEL)
    h2_ref[...] = (_rms(x1, ng[2:3] * (1.0 + sc2)) + sh2).astype(BF16)
    h2 = h2_ref[...]

    sub = lax.broadcasted_iota(jnp.int32, (8, FFN_SUB), 0)
    gp, gs = seq_p // FFN_GROUP, seq_s // FFN_GROUP
    starts = ((sub % gs) == 0) | (((sub % gp) == 0) & is_prompt)
    ends = ((sub % gs) == gs - 1) | (((sub % gp) == gp - 1) & is_prompt)
    for s in range(nsub):
        a = jnp.dot(h2, up_bf[s], preferred_element_type=F32)
        cw = jnp.concatenate([cw_ref[0, :, s * half:(s + 1) * half],
                              0.5 * cw_ref[0, :, D_FF + s * half:D_FF + (s + 1) * half]], axis=1)
        a_prev = jnp.concatenate(
            [jnp.where(starts, 0.0, pltpu.roll(a[rows - 8:rows], 1, axis=0)), a[0:rows - 8]], axis=0)
        a_next = jnp.concatenate(
            [a[8:rows], jnp.where(ends, 0.0, pltpu.roll(a[0:8], 7, axis=0))], axis=0)
        ac = cw[0:1] * a_prev + cw[1:2] * a + cw[2:3] * a_next
        hg = ac[:, half:]
        act = (hg * (1.0 + jnp.tanh(hg))) * ac[:, :half]
        act_ref[:, s * half:(s + 1) * half] = act.astype(BF16)
    y = jnp.dot(act_ref[...], down_bf[...], preferred_element_type=F32)
    out = x1_buf[slot].reshape(rows, D_MODEL) + gt2 * _rms(y, ng[3:4])
    o_buf[slot] = out.reshape(FFN_GROUP, 8, D_MODEL)

    @pl.when(is_prompt)
    def _():
        for cp in out_copies(outp_hbm, i, 0, slot):
            cp.start()

    @pl.when(jnp.logical_not(is_prompt))
    def _():
        for cp in out_copies(outs_hbm, i, tiles_p, slot):
            cp.start()

    @pl.when(i == n_tiles - 1)
    def _():
        await_store(1 - slot)
        await_store(slot)


def _ffn_call(x1p, x1s, layer, mod_all, wts, seq_p, seq_s):
    tiles_p = x1p.shape[0] // FFN_ROWS
    tiles_s = x1s.shape[0] // FFN_ROWS
    assert tiles_p >= 1 and tiles_s >= 1 and tiles_p + tiles_s >= 2
    assert FFN_ROWS == 8 * FFN_GROUP and seq_p % FFN_GROUP == 0 and seq_s % seq_p == 0
    half = FFN_SUB // 2

    def resident(shape):
        nd = len(shape)
        return pl.BlockSpec((1,) + tuple(shape[1:]), lambda i: (layer,) + (0,) * (nd - 1),
                            pipeline_mode=pl.Buffered(1))

    hbm = pl.BlockSpec(memory_space=pl.ANY)
    return pl.pallas_call(
        functools.partial(_ffn_kernel, layer=layer, tiles_p=tiles_p, tiles_s=tiles_s,
                          seq_p=seq_p, seq_s=seq_s),
        out_shape=[jax.ShapeDtypeStruct(x1p.shape, F32), jax.ShapeDtypeStruct(x1s.shape, F32)],
        grid=(tiles_p + tiles_s,),
        in_specs=[resident(mod_all.shape), resident(wts["norm_g"].shape), resident(wts["ffn_conv"].shape),
                  hbm, hbm, hbm, hbm],
        out_specs=[hbm, hbm],
        scratch_shapes=[pltpu.VMEM((D_FF // half, D_MODEL, FFN_SUB), BF16),
                        pltpu.VMEM((D_FF, D_MODEL), BF16),
                        pltpu.VMEM((FFN_ROWS, D_FF), BF16),
                        pltpu.VMEM((FFN_ROWS, D_MODEL), BF16),
                        pltpu.VMEM((2, FFN_GROUP, 8, D_MODEL), F32),
                        pltpu.VMEM((2, FFN_GROUP, 8, D_MODEL), F32),
                        pltpu.VMEM((FFN_WEIGHT_BUFS, D_MODEL // FFN_WEIGHT_CHUNKS, 2 * D_FF), F32),
                        pltpu.VMEM((FFN_WEIGHT_BUFS, D_FF // FFN_WEIGHT_CHUNKS, D_MODEL), F32),
                        pltpu.SemaphoreType.DMA((2, 8)),
                        pltpu.SemaphoreType.DMA((2, 8)),
                        pltpu.SemaphoreType.DMA((FFN_WEIGHT_BUFS, 2))],
        compiler_params=pltpu.CompilerParams(
            dimension_semantics=("arbitrary",), vmem_limit_bytes=VMEM_LIMIT),
        name="ffn",
    )(mod_all, wts["norm_g"], wts["ffn_conv"], x1p, x1s, wts["ffn_up"], wts["ffn_down"])


def _pos_tables(t_len):
    quarter = D_MODEL // 4
    omega = (1.0 / (np.float32(POS_BASE) ** (np.arange(quarter, dtype=np.float32) / np.float32(quarter))))
    omega = omega.astype(np.float32)

    def emb(p):
        a = p.astype(np.float32)[:, None] * omega[None, :]
        return np.concatenate([np.sin(a), np.cos(a)], axis=-1).astype(np.float32)

    return jnp.asarray(emb(np.arange(t_len // GRID_W))), jnp.asarray(emb(np.arange(GRID_W)))


def _stabiliser_rows(state_m):
    bs = state_m.shape[0]
    m = jnp.pad(state_m.astype(F32), ((0, 0), (0, 0), (0, 0), (0, 16 - HEADS)))
    return jnp.pad(m.reshape(bs, DEPTH, 1, 32), ((0, 0), (0, 0), (0, 0), (L_MF, LANES - 32 - L_MF)))


def kernel(x_prompt, x_sample, state_C, state_n, state_m, c, c_ctx, norm_g, ada_w, ada_b, w_in,
           gate_b, gmlp_ws, gmlp_b, pool_w, pool_scale, conv_w, w_out, ffn_up, ffn_conv, ffn_down):
    bp, tp, _ = x_prompt.shape
    bs, ts, _ = x_sample.shape
    assert tp == CHUNK and ts % CHUNK == 0 and FFN_ROWS % tp == 0 and ts == FFN_ROWS

    conds = jnp.concatenate([c_ctx[None, :], c, jnp.zeros((7 - bs, D_MODEL), F32)], axis=0)
    mod_all = _modulation(conds, ada_w, ada_b)
    win_p, wout_p = _prep_weights(w_in, w_out)

    gb = jnp.pad(gate_b, ((0, 0), (0, 0), (0, 8 - HEADS))).reshape(DEPTH, 1, 32)
    dg = D_GROUP // 4
    wts = {
        "norm_g": norm_g,
        "w_in": win_p,
        "gate_b": jnp.pad(gb, ((0, 0), (0, 0), (0, LANES - 32))),
        "gmlp_ws": gmlp_ws,
        "gmlp_bias": jnp.repeat(jnp.swapaxes(gmlp_b, 1, 2), dg, axis=2),
        "pool_bd": jnp.einsum("lgcd,gh->lgchd", pool_w, jnp.eye(4, dtype=F32)).reshape(
            DEPTH, D_GROUP, D_GROUP).astype(BF16),
        "pool_scale": pool_scale,
        "conv_w": conv_w,
        "w_out": wout_p,
        "ffn_up": ffn_up,
        "ffn_conv": ffn_conv,
        "ffn_down": ffn_down,
    }
    pos = _pos_tables(ts)
    state = (state_C.astype(F32), state_n.astype(F32), _stabiliser_rows(state_m))

    xp, xs = x_prompt, x_sample
    prev, ms = None, []
    for l in range(DEPTH):
        x1p, c_all, n_all, m_fin = _mixer_call(xp, l, mod_all, wts, 0, emit_state=True,
                                               prev_states=prev, nseq=MIXER_ROWS // tp)
        prev = (c_all, n_all)
        ms.append(jnp.stack([m_fin[:, 0, L_MF:L_MF + HEADS], m_fin[:, 0, L_MB:L_MB + HEADS]], axis=1))

        (x1s,) = _mixer_call(xs, l, mod_all, wts, 1, pos=pos if l == 0 else None, state=state)

        xp, xs = _ffn_call(x1p.reshape(bp * tp, D_MODEL), x1s.reshape(bs * ts, D_MODEL),
                           l, mod_all, wts, tp, ts)
        xp = xp.reshape(bp, tp, D_MODEL)
        xs = xs.reshape(bs, ts, D_MODEL)

    return (xp, xs, c_all, n_all, jnp.stack(ms, axis=1))
```

```python
import functools

import numpy as np
import jax
import jax.numpy as jnp
from jax import lax
from jax.experimental import pallas as pl
from jax.experimental.pallas import tpu as pltpu

F32 = jnp.float32
BF16 = jnp.bfloat16

D_MODEL = 1024
DEPTH = 2
HEADS = 4
PAIRS = HEADS // 2
DH = 64
D_GROUP = 256
D_FF = 2816
N_MOD = 6
EPS = 1e-6
GRID_W = 64
POS_BASE = 10000.0
D_IN = 2576
N_GATE = 4 * HEADS

CHUNK = 256
GMLP_CHUNK = 128
HALO = 8
LANES = 128

C_Q, C_K, C_V, C_O = 0, 256, 512, 768
C_GU, C_GV, C_PZ, C_CB, C_CC, C_CX = 1024, 1280, 1536, 1792, 2048, 2304
Z_COLS = 2560
C_GATE = Z_COLS
W_COLS = Z_COLS + LANES
Z_GROUP = 512
L_BF, L_MF, L_BB, L_MB = 0, 8, 16, 24
L_STEP = L_MF - L_BF

MIXER_ROWS = 1024
FFN_ROWS = 1024
FFN_GROUP = FFN_ROWS // 8
FFN_SUB = 512
FFN_WEIGHT_CHUNKS = 16
FFN_WEIGHT_BUFS = 4
VMEM_LIMIT = 60 * 1024 * 1024


def _rms(x, g):
    ms = jnp.mean(x * x, axis=-1, keepdims=True)
    return x * lax.rsqrt(ms + EPS) * g


def _sigmoid(x):
    return 1.0 / (1.0 + jnp.exp(-x))


def _log_sigmoid(x):
    return jnp.minimum(x, 0.0) - jnp.log1p(jnp.exp(-jnp.abs(x)))


def _scan_lanes(x, op, reverse, fill):
    n = x.shape[1]
    lane = lax.broadcasted_iota(jnp.int32, x.shape, 1)
    k = 1
    while k < n:
        if reverse:
            sh = jnp.where(lane < n - k, pltpu.roll(x, n - k, axis=1), fill)
        else:
            sh = jnp.where(lane >= k, pltpu.roll(x, k, axis=1), fill)
        x = op(x, sh)
        k *= 2
    return x


def _shift_down(x, k):
    return pltpu.roll(x, k, axis=0)


def _shift_up(x, k):
    return pltpu.roll(x, x.shape[0] - k, axis=0)


def _prep_kernel(wt_ref, sel_ref, wout_ref, win_o_ref, wout_o_ref):
    for n in range(Z_COLS // LANES):
        r0 = LANES * n if n < 4 * D_GROUP // LANES else LANES * n + N_GATE
        win_o_ref[0, :, LANES * n:LANES * (n + 1)] = wt_ref[0, r0:r0 + LANES, :].T.astype(BF16)
    g = wt_ref[0, 4 * D_GROUP:4 * D_GROUP + LANES, :].T.astype(BF16)
    win_o_ref[0, :, C_GATE:W_COLS] = jnp.dot(g, sel_ref[...], preferred_element_type=F32).astype(BF16)
    wout_o_ref[...] = wout_ref[...].astype(BF16)


def _prep_weights(w_in, w_out):
    sel = np.zeros((LANES, LANES), np.float32)
    for j in range(N_GATE):
        sel[j, 8 * (j // HEADS) + j % HEADS] = 1.0
    rows = D_MODEL // 2
    return pl.pallas_call(
        _prep_kernel,
        out_shape=[jax.ShapeDtypeStruct((DEPTH, D_MODEL, W_COLS), BF16),
                   jax.ShapeDtypeStruct(w_out.shape, BF16)],
        grid=(DEPTH, D_MODEL // rows),
        in_specs=[pl.BlockSpec((1, D_IN, rows), lambda l, i: (l, 0, i)),
                  pl.BlockSpec((LANES, LANES), lambda l, i: (0, 0)),
                  pl.BlockSpec((1, rows, D_MODEL), lambda l, i: (l, i, 0))],
        out_specs=[pl.BlockSpec((1, rows, W_COLS), lambda l, i: (l, i, 0)),
                   pl.BlockSpec((1, rows, D_MODEL), lambda l, i: (l, i, 0))],
        compiler_params=pltpu.CompilerParams(dimension_semantics=("arbitrary", "arbitrary"),
                                             vmem_limit_bytes=VMEM_LIMIT),
        name="prep_mixer_w")(jnp.swapaxes(w_in, 1, 2), jnp.asarray(sel, BF16), w_out)


def _mod_kernel(c_ref, w_ref, b_ref, o_ref):
    c = c_ref[...]
    s = (c * _sigmoid(c)).astype(BF16)
    bias = b_ref[pl.ds(pl.program_id(0), 1), :]
    o_ref[0] = jnp.dot(s, w_ref[0].astype(BF16), preferred_element_type=F32) + bias


def _modulation(conds, ada_w, ada_b):
    n_out = N_MOD * D_MODEL
    bn = 1536
    return pl.pallas_call(
        _mod_kernel,
        out_shape=jax.ShapeDtypeStruct((DEPTH, 8, n_out), F32),
        grid=(DEPTH, n_out // bn),
        in_specs=[
            pl.BlockSpec((8, D_MODEL), lambda l, j: (0, 0)),
            pl.BlockSpec((1, D_MODEL, bn), lambda l, j: (l, 0, j)),
            pl.BlockSpec((DEPTH, bn), lambda l, j: (0, j)),
        ],
        out_specs=pl.BlockSpec((1, 8, bn), lambda l, j: (l, 0, j)),
        compiler_params=pltpu.CompilerParams(
            dimension_semantics=("arbitrary", "arbitrary"), vmem_limit_bytes=VMEM_LIMIT),
        name="adaln_mod",
    )(conds, ada_w, ada_b)


def _mod_rows(mod_ref, cond):
    row = mod_ref[0, pl.ds(cond, 1), :]
    return [row[:, i * D_MODEL:(i + 1) * D_MODEL] for i in range(N_MOD)]


def _gate_stats(gates):
    gt = gates.T
    b_f = _scan_lanes(_log_sigmoid(gt[8:16]), jnp.add, False, 0.0)
    g_f = gt[0:8] - b_f
    cm_f = _scan_lanes(g_f, jnp.maximum, False, -jnp.inf)
    b_b = _scan_lanes(_log_sigmoid(gt[24:32]), jnp.add, True, 0.0)
    g_b = gt[16:24] - b_b
    cm_b = _scan_lanes(g_b, jnp.maximum, True, -jnp.inf)
    stack = jnp.concatenate([b_f, cm_f, b_b, cm_b, jnp.zeros((LANES - 32, CHUNK), F32)], axis=0)
    return jnp.concatenate([g_f, g_b], axis=0), stack.T


def _pair_cols(mat, lane0, lane_lt_dh):
    return jnp.where(lane_lt_dh, mat[:, lane0:lane0 + 1], mat[:, lane0 + 1:lane0 + 2])


def _mlstm(z_ref, gcol_ref, grow_ref, cols_ref, bd_ref, ycat_ref, nc, state0, m0, need_final):
    lane_row = lax.broadcasted_iota(jnp.int32, (1, LANES), 1)
    lane = lax.broadcasted_iota(jnp.int32, (CHUNK, LANES), 1)
    lt_dh = lane < DH
    row_ll = lax.broadcasted_iota(jnp.int32, (CHUNK, CHUNK), 0)
    col_ll = lax.broadcasted_iota(jnp.int32, (CHUNK, CHUNK), 1)
    tri = (col_ll <= row_ll, col_ll >= row_ll)
    bd_row = lax.broadcasted_iota(jnp.int32, (LANES, 2 * LANES), 0)
    bd_col = lax.broadcasted_iota(jnp.int32, (LANES, 2 * LANES), 1)
    bd_mask = (bd_row < DH) == ((bd_col & (LANES - 1)) < DH)
    dirs = ((0, L_BF, L_MF, CHUNK - 1), (1, L_BB, L_MB, 0))

    m_start = [[None] * nc, [None] * nc]
    m_fin = [None, None]
    for d, _, _, last in dirs:
        m = m0
        order = range(nc) if d == 0 else range(nc - 1, -1, -1)
        for step, c in enumerate(order):
            m_start[d][c] = m
            if need_final or step < nc - 1:
                cl = cols_ref[c, last:last + 1, :]
                m = pltpu.roll(cl, L_STEP, axis=1) + jnp.maximum(m, cl)
        m_fin[d] = m

    have_state = [[False] * nc, [False] * nc]
    fin = [None, None]
    for d, l_b, l_m, last in dirs:
        bd = list(state0[d])
        order = range(nc) if d == 0 else range(nc - 1, -1, -1)
        for step, c in enumerate(order):
            if bd[0] is not None:
                have_state[d][c] = True
                for p in range(PAIRS):
                    bd_ref[d, c, p] = bd[p].astype(BF16)
            if not (need_final or step < nc - 1):
                continue
            rows = slice(HALO + c * CHUNK, HALO + (c + 1) * CHUNK)
            cols = cols_ref[c]
            m_last = jnp.maximum(cols[last:last + 1, :], m_start[d][c])
            a_col = jnp.exp(gcol_ref[c] - cols - pltpu.roll(m_last, LANES - L_STEP, axis=1))
            a_prev = jnp.exp(m_start[d][c] - m_last)
            for p in range(PAIRS):
                a_pair = _pair_cols(a_col, l_b + 2 * p, lt_dh)
                v_pair = z_ref[rows, C_V + LANES * p:C_V + LANES * (p + 1)]
                rhs = jnp.concatenate([v_pair * a_pair, a_pair], axis=1).astype(BF16)
                k_pair = (z_ref[rows, C_K + LANES * p:C_K + LANES * (p + 1)] * (DH ** -0.5)).astype(BF16)
                upd = lax.dot_general(k_pair, rhs, (((0,), (0,)), ((), ())),
                                      preferred_element_type=F32)
                upd = jnp.where(bd_mask, upd, 0.0)
                if bd[p] is None:
                    bd[p] = upd
                else:
                    scale = jnp.where(bd_row < DH, a_prev[:, l_m + 2 * p:l_m + 2 * p + 1],
                                      a_prev[:, l_m + 2 * p + 1:l_m + 2 * p + 2])
                    bd[p] = scale * bd[p] + upd
        fin[d] = bd

    for c in range(nc):
        rows = slice(HALO + c * CHUNK, HALO + (c + 1) * CHUNK)
        cols = cols_ref[c]
        grow = grow_ref[c]
        m_prev = jnp.where(lane_row < L_BB, m_start[0][c], m_start[1][c])
        m_all = jnp.maximum(cols, m_prev)
        w_inter = jnp.exp(m_prev - m_all)
        e_neg = jnp.exp(-(pltpu.roll(cols, L_STEP, axis=1) + m_all))
        for p in range(PAIRS):
            q_pair = z_ref[rows, C_Q + LANES * p:C_Q + LANES * (p + 1)].astype(BF16)
            k_f32 = z_ref[rows, C_K + LANES * p:C_K + LANES * (p + 1)] * (DH ** -0.5)
            v_pair = z_ref[rows, C_V + LANES * p:C_V + LANES * (p + 1)]
            p_cat = ([], [])
            va = []
            for j in range(2):
                h = 2 * p + j
                own = lt_dh if j == 0 else jnp.logical_not(lt_dh)
                k_own = jnp.where(own, k_f32, 0.0).astype(BF16)
                s = lax.dot_general(q_pair, k_own, (((1,), (1,)), ((), ())),
                                    preferred_element_type=F32)
                va.append(jnp.concatenate([jnp.where(own, v_pair, 0.0), jnp.where(own, 1.0, 0.0)],
                                          axis=1).astype(BF16))
                for d, _, l_m, _ in dirs:
                    w = jnp.exp(jnp.where(tri[d], grow[8 * d + h:8 * d + h + 1] - m_all[:, l_m + h:l_m + h + 1],
                                          -jnp.inf))
                    p_cat[d].append((s * w).astype(BF16))
            va = jnp.concatenate(va, axis=0)
            h_sum = None
            for d, _, l_m, _ in dirs:
                tot = jnp.dot(jnp.concatenate(p_cat[d], axis=1), va, preferred_element_type=F32)
                if have_state[d][c]:
                    e_pair = _pair_cols(w_inter, l_m + 2 * p, lt_dh)
                    qc = jnp.dot(q_pair, bd_ref[d, c, p], preferred_element_type=F32)
                    tot = tot + jnp.concatenate([e_pair, e_pair], axis=1) * qc
                floor = _pair_cols(e_neg, l_m + 2 * p, lt_dh)
                h_dir = tot[:, :LANES] / jnp.maximum(jnp.abs(tot[:, LANES:]), floor)
                h_sum = h_dir if h_sum is None else h_sum + h_dir
            o_pair = z_ref[rows, C_O + LANES * p:C_O + LANES * (p + 1)]
            ycat_ref[c * CHUNK:(c + 1) * CHUNK, LANES * p:LANES * (p + 1)] = (
                _sigmoid(o_pair) * h_sum).astype(BF16)

    m_row = jnp.where(lane_row < L_BB, m_fin[0], m_fin[1])
    return fin, m_row


def _pair_state(c_ref, n_ref, q, d, p):
    eye = (lax.broadcasted_iota(jnp.int32, (DH, DH), 0) == lax.broadcasted_iota(jnp.int32, (DH, DH), 1))
    zero = jnp.zeros((DH, DH), F32)
    blocks = []
    for j in range(2):
        h = 2 * p + j
        n_row = n_ref[q, 0, d, h:h + 1, :]
        n_col = jnp.sum(jnp.where(eye, n_row, 0.0), axis=1, keepdims=True)
        row = [zero] * 4
        row[j] = c_ref[q, 0, d, h]
        row[2 + j] = jnp.broadcast_to(n_col, (DH, DH))
        blocks.append(jnp.concatenate(row, axis=1))
    return jnp.concatenate(blocks, axis=0)


def _mixer_kernel(*refs, layer, seq_len, nseq, add_pos, has_state, emit_state, cond_base, n_prev):
    it = iter(refs)
    x_ref = next(it)
    if add_pos:
        er_ref = next(it)
        ec_ref = next(it)
    mod_ref = next(it)
    ng_ref = next(it)
    win_ref = next(it)
    gb_ref = next(it)
    ws_ref = next(it)
    gbias_ref = next(it)
    pbd_ref = next(it)
    psc_ref = next(it)
    cw_ref = next(it)
    wout_ref = next(it)
    if has_state:
        c0_ref = next(it)
        n0_ref = next(it)
        m0_ref = next(it)
    if n_prev:
        cprev_ref = next(it)
        nprev_ref = next(it)
    x1_ref = next(it)
    if emit_state:
        cout_ref = next(it)
        nout_ref = next(it)
        mout_ref = next(it)
    z_all = next(it)
    gcol_all = next(it)
    grow_all = next(it)
    cols_all = next(it)
    bd_all = next(it)
    ycat_all = next(it)

    nblk = seq_len // CHUNK
    cond = cond_base if cond_base == 0 else cond_base + pl.program_id(0)
    sh1, sc1, gt1 = _mod_rows(mod_ref, cond)[:3]
    ng = ng_ref[0]
    seqs = range(nseq)
    units = [(q, r) for q in seqs for r in range(nblk)]

    zeros_halo = jnp.zeros((HALO, Z_COLS - C_PZ), F32)
    for q in seqs:
        z_all[q, 0:HALO, C_PZ:Z_COLS] = zeros_halo
        z_all[q, HALO + seq_len:2 * HALO + seq_len, C_PZ:Z_COLS] = zeros_halo

    g_in = ng[0:1] * (1.0 + sc1)
    hbs = []
    for q, r in units:
        rows = slice(r * CHUNK, (r + 1) * CHUNK)
        xb = x_ref[q, rows, :]
        if add_pos:
            pieces = []
            for g in range(CHUNK // GRID_W):
                gi = r * (CHUNK // GRID_W) + g
                er = jnp.broadcast_to(er_ref[gi:gi + 1, :], (GRID_W, D_MODEL // 2))
                pieces.append(jnp.concatenate([er, ec_ref[...]], axis=1))
            xb = xb + jnp.concatenate(pieces, axis=0)
            x1_ref[q, rows, :] = xb
        hbs.append((_rms(xb, g_in) + sh1).astype(BF16))
    hb = jnp.concatenate(hbs, axis=0)
    gates_all = jnp.dot(hb, win_ref[0, :, C_GATE:W_COLS], preferred_element_type=F32) + gb_ref[0]
    for cg in range(Z_COLS // Z_GROUP):
        zc = jnp.dot(hb, win_ref[0, :, cg * Z_GROUP:(cg + 1) * Z_GROUP], preferred_element_type=F32)
        for u, (q, r) in enumerate(units):
            z_all[q, HALO + r * CHUNK:HALO + (r + 1) * CHUNK, cg * Z_GROUP:(cg + 1) * Z_GROUP] = (
                zc[u * CHUNK:(u + 1) * CHUNK])
    for u, (q, r) in enumerate(units):
        gates = gates_all[u * CHUNK:(u + 1) * CHUNK]
        gcol_all[q, r] = gates
        grow, cols = _gate_stats(gates)
        grow_all[q, r] = grow
        cols_all[q, r] = cols

    for q in seqs:
        if has_state:
            state0 = [[_pair_state(c0_ref, n0_ref, q, d, p) for p in range(PAIRS)] for d in range(2)]
            m0 = m0_ref[q, 0]
        else:
            state0 = [[None] * PAIRS, [None] * PAIRS]
            m0 = jnp.zeros((1, LANES), F32)
        fin, m_row = _mlstm(z_all.at[q], gcol_all.at[q], grow_all.at[q], cols_all.at[q], bd_all.at[q],
                            ycat_all.at[q], nblk, state0, m0, need_final=emit_state)
        if emit_state:
            for d in range(2):
                for p in range(PAIRS):
                    for j in range(2):
                        cout_ref[q, n_prev, d, 2 * p + j] = fin[d][p][DH * j:DH * (j + 1), DH * j:DH * (j + 1)]
                    n_t = fin[d][p][:, LANES:].T
                    nout_ref[q, n_prev, d, 2 * p:2 * p + 1, :] = n_t[0:1, 0:DH]
                    nout_ref[q, n_prev, d, 2 * p + 1:2 * p + 2, :] = n_t[DH:DH + 1, DH:LANES]
            mout_ref[q] = m_row
    if emit_state and n_prev:
        cout_ref[:, 0:n_prev] = cprev_ref[...]
        nout_ref[:, 0:n_prev] = nprev_ref[...]

    lane_g = lax.broadcasted_iota(jnp.int32, (CHUNK, D_GROUP), 1) // (D_GROUP // 4)
    lg = lax.broadcasted_iota(jnp.int32, (GMLP_CHUNK, D_GROUP), 1) // (D_GROUP // 4)
    row_c = lax.broadcasted_iota(jnp.int32, (CHUNK, D_GROUP), 0)
    half = jnp.where(lane_g == 0, 1, jnp.where(lane_g == 1, 2, jnp.where(lane_g == 2, 4, 8)))
    cw = cw_ref[0]
    ws_cat = jnp.concatenate([ws_ref[0, g] for g in range(4)], axis=1).astype(BF16)
    for q, r in units:
        z_ref = z_all.at[q]
        ycat_ref = ycat_all.at[q]
        rows = slice(r * CHUNK, (r + 1) * CHUNK)
        zrows = slice(HALO + r * CHUNK, HALO + (r + 1) * CHUNK)
        hrows = slice(r * CHUNK, (r + 1) * CHUNK + 2 * HALO)
        inner = slice(HALO, HALO + CHUNK)

        for s in range(CHUNK // GMLP_CHUNK):
            zs = slice(HALO + r * CHUNK + s * GMLP_CHUNK, HALO + r * CHUNK + (s + 1) * GMLP_CHUNK)
            vch = z_ref[zs, C_GV:C_GV + D_GROUP]
            v_stack = jnp.concatenate([jnp.where(lg == g, vch, 0.0) for g in range(4)], axis=0)
            mixed = gbias_ref[0] + jnp.dot(ws_cat, v_stack.astype(BF16), preferred_element_type=F32)
            yb = z_ref[zs, C_GU:C_GU + D_GROUP] * mixed
            ys = slice(r * CHUNK + s * GMLP_CHUNK, r * CHUNK + (s + 1) * GMLP_CHUNK)
            ycat_ref[ys, D_GROUP:2 * D_GROUP] = yb.astype(BF16)

        pz = z_ref[hrows, C_PZ:C_PZ + D_GROUP]
        a2 = pz + _shift_down(pz, 1)
        a4 = a2 + _shift_down(a2, 2)
        a8 = a4 + _shift_down(a4, 4)
        a16 = a8 + _shift_down(a8, 8)
        win = [a2[inner], _shift_up(a4, 1)[inner], _shift_up(a8, 3)[inner], _shift_up(a16, 7)[inner]]
        wsum = jnp.where(lane_g == 0, win[0], jnp.where(lane_g == 1, win[1],
                         jnp.where(lane_g == 2, win[2], win[3])))
        t_abs = row_c + r * CHUNK
        cnt = (jnp.minimum(t_abs + half, seq_len) - jnp.maximum(t_abs - half, 0)).astype(F32)
        pooled = wsum / cnt - pz[inner]
        yc = (jnp.dot(pooled.astype(BF16), pbd_ref[0], preferred_element_type=F32)
              * psc_ref[layer:layer + 1, :])
        ycat_ref[rows, 2 * D_GROUP:3 * D_GROUP] = yc.astype(BF16)

        u = z_ref[hrows, C_CC:C_CC + D_GROUP] * z_ref[hrows, C_CX:C_CX + D_GROUP]
        conv = (cw[0:1] * _shift_down(u, 1)[inner] + cw[1:2] * u[inner]
                + cw[2:3] * _shift_up(u, 1)[inner])
        yd = z_ref[zrows, C_CB:C_CB + D_GROUP] * conv
        ycat_ref[rows, 3 * D_GROUP:4 * D_GROUP] = yd.astype(BF16)

    y_all = jnp.dot(jnp.concatenate([ycat_all[q, r * CHUNK:(r + 1) * CHUNK, :] for q, r in units], axis=0),
                    wout_ref[0], preferred_element_type=F32)
    g_y = gt1 * ng[1:2]
    for u, (q, r) in enumerate(units):
        rows = slice(r * CHUNK, (r + 1) * CHUNK)
        xb = x1_ref[q, rows, :] if add_pos else x_ref[q, rows, :]
        x1_ref[q, rows, :] = xb + _rms(y_all[u * CHUNK:(u + 1) * CHUNK], g_y)


def _layer_spec(shape, layer):
    nd = len(shape)
    return pl.BlockSpec((1,) + tuple(shape[1:]), lambda b: (layer,) + (0,) * (nd - 1),
                        pipeline_mode=pl.Buffered(1))


def _mixer_call(x, layer, mod_all, wts, cond_base, pos=None, state=None, emit_state=False,
                prev_states=None, nseq=1):
    bsz, seq_len, _ = x.shape
    nblk = seq_len // CHUNK
    add_pos = pos is not None
    has_state = state is not None
    n_prev = 0 if prev_states is None else prev_states[0].shape[1]
    args = [x]
    in_specs = [pl.BlockSpec((nseq, seq_len, D_MODEL), lambda b: (b, 0, 0))]
    if add_pos:
        args += list(pos)
        in_specs += [pl.BlockSpec(p.shape, lambda b: (0, 0), pipeline_mode=pl.Buffered(1)) for p in pos]
    args.append(mod_all)
    in_specs.append(_layer_spec(mod_all.shape, layer))
    for name in ("norm_g", "w_in", "gate_b", "gmlp_ws", "gmlp_bias", "pool_bd", "pool_scale",
                 "conv_w", "w_out"):
        args.append(wts[name])
        if wts[name].ndim == 2:
            in_specs.append(pl.BlockSpec(wts[name].shape, lambda b: (0, 0), pipeline_mode=pl.Buffered(1)))
        else:
            in_specs.append(_layer_spec(wts[name].shape, layer))
    if has_state:
        args += list(state)
        in_specs += [pl.BlockSpec((nseq, 1, 2, HEADS, DH, DH), lambda b: (b, layer, 0, 0, 0, 0)),
                     pl.BlockSpec((nseq, 1, 2, HEADS, DH), lambda b: (b, layer, 0, 0, 0)),
                     pl.BlockSpec((nseq, 1, 1, LANES), lambda b: (b, layer, 0, 0))]
    out_shape = [jax.ShapeDtypeStruct((bsz, seq_len, D_MODEL), F32)]
    out_specs = [pl.BlockSpec((nseq, seq_len, D_MODEL), lambda b: (b, 0, 0))]
    if emit_state:
        if n_prev:
            args += list(prev_states)
            in_specs += [pl.BlockSpec((nseq, n_prev, 2, HEADS, DH, DH), lambda b: (b, 0, 0, 0, 0, 0)),
                         pl.BlockSpec((nseq, n_prev, 2, HEADS, DH), lambda b: (b, 0, 0, 0, 0))]
        out_shape += [jax.ShapeDtypeStruct((bsz, n_prev + 1, 2, HEADS, DH, DH), F32),
                      jax.ShapeDtypeStruct((bsz, n_prev + 1, 2, HEADS, DH), F32),
                      jax.ShapeDtypeStruct((bsz, 1, LANES), F32)]
        out_specs += [pl.BlockSpec((nseq, n_prev + 1, 2, HEADS, DH, DH), lambda b: (b, 0, 0, 0, 0, 0)),
                      pl.BlockSpec((nseq, n_prev + 1, 2, HEADS, DH), lambda b: (b, 0, 0, 0, 0)),
                      pl.BlockSpec((nseq, 1, LANES), lambda b: (b, 0, 0))]
    kern = functools.partial(_mixer_kernel, layer=layer, seq_len=seq_len, nseq=nseq, add_pos=add_pos,
                             has_state=has_state, emit_state=emit_state, cond_base=cond_base,
                             n_prev=n_prev)
    return pl.pallas_call(
        kern,
        out_shape=out_shape,
        grid=(bsz // nseq,),
        in_specs=in_specs,
        out_specs=out_specs,
        scratch_shapes=[pltpu.VMEM((nseq, seq_len + 2 * HALO, Z_COLS), F32),
                        pltpu.VMEM((nseq, nblk, CHUNK, LANES), F32),
                        pltpu.VMEM((nseq, nblk, 16, CHUNK), F32),
                        pltpu.VMEM((nseq, nblk, CHUNK, LANES), F32),
                        pltpu.VMEM((nseq, 2, nblk, PAIRS, LANES, 2 * LANES), BF16),
                        pltpu.VMEM((nseq, seq_len, D_MODEL), BF16)],
        compiler_params=pltpu.CompilerParams(
            dimension_semantics=("arbitrary",), vmem_limit_bytes=VMEM_LIMIT),
        name="mixer_t%d" % seq_len,
    )(*args)


def _ffn_kernel(mod_ref, ng_ref, cw_ref, x1p_hbm, x1s_hbm, up_hbm, down_hbm, outp_hbm, outs_hbm,
                up_bf, down_bf, act_ref, x1_buf, o_buf, up_stage, down_stage,
                in_sems, out_sems, w_sems, *, layer, tiles_p, tiles_s, seq_p, seq_s):
    i = pl.program_id(0)
    n_tiles = tiles_p + tiles_s
    rows = FFN_ROWS
    half = FFN_SUB // 2
    nsub = D_FF // half
    slot = i % 2
    is_prompt = i < tiles_p

    def piece_rows(tile, first_tile, j):
        base = (tile - first_tile) * rows + j * FFN_GROUP
        return pl.ds(base if isinstance(base, int) else pl.multiple_of(base, FFN_GROUP), FFN_GROUP)

    def x1_copies(x1_hbm, tile, first_tile, buf):
        return [pltpu.make_async_copy(x1_hbm.at[piece_rows(tile, first_tile, j)],
                                      x1_buf.at[buf, :, j, :], in_sems.at[buf, j]) for j in range(8)]

    def out_copies(out_hbm, tile, first_tile, buf):
        return [pltpu.make_async_copy(o_buf.at[buf, :, j, :],
                                      out_hbm.at[piece_rows(tile, first_tile, j)], out_sems.at[buf, j])
                for j in range(8)]

    def fetch(tile, buf):
        @pl.when(tile < tiles_p)
        def _():
            for cp in x1_copies(x1p_hbm, tile, 0, buf):
                cp.start()

        @pl.when(tile >= tiles_p)
        def _():
            for cp in x1_copies(x1s_hbm, tile, tiles_p, buf):
                cp.start()

    def await_fetch(buf):
        for cp in x1_copies(x1p_hbm, 0, 0, buf):
            cp.wait()

    def await_store(buf):
        for cp in out_copies(outp_hbm, 0, 0, buf):
            cp.wait()

    up_rows = up_stage.shape[1]
    down_rows = down_stage.shape[1]

    def slab(c, n):
        return pl.ds(c * n, n) if isinstance(c, int) else pl.ds(pl.multiple_of(c * n, n), n)

    def weight_copies(c, buf):
        return (pltpu.make_async_copy(up_hbm.at[layer, slab(c, up_rows), :], up_stage.at[buf],
                                      w_sems.at[buf, 0]),
                pltpu.make_async_copy(down_hbm.at[layer, slab(c, down_rows), :], down_stage.at[buf],
                                      w_sems.at[buf, 1]))

    @pl.when(i == 0)
    def _():
        for cp in x1_copies(x1p_hbm, 0, 0, 0):
            cp.start()
        depth = up_stage.shape[0]
        for c0 in range(depth - 1):
            for cp in weight_copies(c0, c0):
                cp.start()

        def convert(c, carry):
            buf = c % depth
            ahead = c + depth - 1

            @pl.when(ahead < FFN_WEIGHT_CHUNKS)
            def _():
                for cp in weight_copies(ahead, ahead % depth):
                    cp.start()

            for cp in weight_copies(c, buf):
                cp.wait()
            r = slab(c, up_rows)
            for s in range(nsub):
                up_bf[s, r, 0:half] = up_stage[buf, :, s * half:(s + 1) * half].astype(BF16)
                up_bf[s, r, half:FFN_SUB] = (
                    up_stage[buf, :, D_FF + s * half:D_FF + (s + 1) * half].astype(BF16))
            down_bf[slab(c, down_rows), :] = down_stage[buf].astype(BF16)
            return carry

        lax.fori_loop(0, FFN_WEIGHT_CHUNKS, convert, 0)

    await_fetch(slot)

    @pl.when(i >= 2)
    def _():
        await_store(slot)

    @pl.when(i + 1 < n_tiles)
    def _():
        fetch(i + 1, 1 - slot)

    cond = jnp.where(is_prompt, 0, i - tiles_p + 1)
    _, _, _, sh2, sc2, gt2 = _mod_rows(mod_ref, cond)
    ng = ng_ref[0]
    x1 = x1_buf[slot].reshape(rows, D_MODEL)
    h2 = (_rms(x1, ng[2:3] * (1.0 + sc2)) + sh2).astype(BF16)

    sub = lax.broadcasted_iota(jnp.int32, (8, FFN_SUB), 0)
    gp, gs = seq_p // FFN_GROUP, seq_s // FFN_GROUP
    starts = ((sub % gs) == 0) | (((sub % gp) == 0) & is_prompt)
    ends = ((sub % gs) == gs - 1) | (((sub % gp) == gp - 1) & is_prompt)
    for s in range(nsub):
        a = jnp.dot(h2, up_bf[s], preferred_element_type=F32)
        cw = jnp.concatenate([cw_ref[0, :, s * half:(s + 1) * half],
                              0.5 * cw_ref[0, :, D_FF + s * half:D_FF + (s + 1) * half]], axis=1)
        a_prev = jnp.concatenate(
            [jnp.where(starts, 0.0, pltpu.roll(a[rows - 8:rows], 1, axis=0)), a[0:rows - 8]], axis=0)
        a_next = jnp.concatenate(
            [a[8:rows], jnp.where(ends, 0.0, pltpu.roll(a[0:8], 7, axis=0))], axis=0)
        ac = cw[0:1] * a_prev + cw[1:2] * a + cw[2:3] * a_next
        hg = ac[:, half:]
        act = (hg * (1.0 + jnp.tanh(hg))) * ac[:, :half]
        act_ref[:, s * half:(s + 1) * half] = act.astype(BF16)
    y = jnp.dot(act_ref[...], down_bf[...], preferred_element_type=F32)
    out = x1_buf[slot].reshape(rows, D_MODEL) + gt2 * _rms(y, ng[3:4])
    o_buf[slot] = out.reshape(FFN_GROUP, 8, D_MODEL)

    @pl.when(is_prompt)
    def _():
        for cp in out_copies(outp_hbm, i, 0, slot):
            cp.start()

    @pl.when(jnp.logical_not(is_prompt))
    def _():
        for cp in out_copies(outs_hbm, i, tiles_p, slot):
            cp.start()

    @pl.when(i == n_tiles - 1)
    def _():
        await_store(1 - slot)
        await_store(slot)


def _ffn_call(x1p, x1s, layer, mod_all, wts, seq_p, seq_s):
    tiles_p = x1p.shape[0] // FFN_ROWS
    tiles_s = x1s.shape[0] // FFN_ROWS
    assert tiles_p >= 1 and tiles_s >= 1 and tiles_p + tiles_s >= 2
    assert FFN_ROWS == 8 * FFN_GROUP and seq_p % FFN_GROUP == 0 and seq_s % seq_p == 0
    half = FFN_SUB // 2

    def resident(shape):
        nd = len(shape)
        return pl.BlockSpec((1,) + tuple(shape[1:]), lambda i: (layer,) + (0,) * (nd - 1),
                            pipeline_mode=pl.Buffered(1))

    hbm = pl.BlockSpec(memory_space=pl.ANY)
    return pl.pallas_call(
        functools.partial(_ffn_kernel, layer=layer, tiles_p=tiles_p, tiles_s=tiles_s,
                          seq_p=seq_p, seq_s=seq_s),
        out_shape=[jax.ShapeDtypeStruct(x1p.shape, F32), jax.ShapeDtypeStruct(x1s.shape, F32)],
        grid=(tiles_p + tiles_s,),
        in_specs=[resident(mod_all.shape), resident(wts["norm_g"].shape), resident(wts["ffn_conv"].shape),
                  hbm, hbm, hbm, hbm],
        out_specs=[hbm, hbm],
        scratch_shapes=[pltpu.VMEM((D_FF // half, D_MODEL, FFN_SUB), BF16),
                        pltpu.VMEM((D_FF, D_MODEL), BF16),
                        pltpu.VMEM((FFN_ROWS, D_FF), BF16),
                        pltpu.VMEM((2, FFN_GROUP, 8, D_MODEL), F32),
                        pltpu.VMEM((2, FFN_GROUP, 8, D_MODEL), F32),
                        pltpu.VMEM((FFN_WEIGHT_BUFS, D_MODEL // FFN_WEIGHT_CHUNKS, 2 * D_FF), F32),
                        pltpu.VMEM((FFN_WEIGHT_BUFS, D_FF // FFN_WEIGHT_CHUNKS, D_MODEL), F32),
                        pltpu.SemaphoreType.DMA((2, 8)),
                        pltpu.SemaphoreType.DMA((2, 8)),
                        pltpu.SemaphoreType.DMA((FFN_WEIGHT_BUFS, 2))],
        compiler_params=pltpu.CompilerParams(
            dimension_semantics=("arbitrary",), vmem_limit_bytes=VMEM_LIMIT),
        name="ffn",
    )(mod_all, wts["norm_g"], wts["ffn_conv"], x1p, x1s, wts["ffn_up"], wts["ffn_down"])


def _pos_tables(t_len):
    quarter = D_MODEL // 4
    omega = (1.0 / (np.float32(POS_BASE) ** (np.arange(quarter, dtype=np.float32) / np.float32(quarter))))
    omega = omega.astype(np.float32)

    def emb(p):
        a = p.astype(np.float32)[:, None] * omega[None, :]
        return np.concatenate([np.sin(a), np.cos(a)], axis=-1).astype(np.float32)

    return jnp.asarray(emb(np.arange(t_len // GRID_W))), jnp.asarray(emb(np.arange(GRID_W)))


def _stabiliser_rows(state_m):
    bs = state_m.shape[0]
    m = jnp.pad(state_m.astype(F32), ((0, 0), (0, 0), (0, 0), (0, 16 - HEADS)))
    return jnp.pad(m.reshape(bs, DEPTH, 1, 32), ((0, 0), (0, 0), (0, 0), (L_MF, LANES - 32 - L_MF)))


def kernel(x_prompt, x_sample, state_C, state_n, state_m, c, c_ctx, norm_g, ada_w, ada_b, w_in,
           gate_b, gmlp_ws, gmlp_b, pool_w, pool_scale, conv_w, w_out, ffn_up, ffn_conv, ffn_down):
    bp, tp, _ = x_prompt.shape
    bs, ts, _ = x_sample.shape
    assert tp == CHUNK and ts % CHUNK == 0 and FFN_ROWS % tp == 0 and ts == FFN_ROWS

    conds = jnp.concatenate([c_ctx[None, :], c, jnp.zeros((7 - bs, D_MODEL), F32)], axis=0)
    mod_all = _modulation(conds, ada_w, ada_b)
    win_p, wout_p = _prep_weights(w_in, w_out)

    gb = jnp.pad(gate_b, ((0, 0), (0, 0), (0, 8 - HEADS))).reshape(DEPTH, 1, 32)
    dg = D_GROUP // 4
    wts = {
        "norm_g": norm_g,
        "w_in": win_p,
        "gate_b": jnp.pad(gb, ((0, 0), (0, 0), (0, LANES - 32))),
        "gmlp_ws": gmlp_ws,
        "gmlp_bias": jnp.repeat(jnp.swapaxes(gmlp_b, 1, 2), dg, axis=2),
        "pool_bd": jnp.einsum("lgcd,gh->lgchd", pool_w, jnp.eye(4, dtype=F32)).reshape(
            DEPTH, D_GROUP, D_GROUP).astype(BF16),
        "pool_scale": pool_scale,
        "conv_w": conv_w,
        "w_out": wout_p,
        "ffn_up": ffn_up,
        "ffn_conv": ffn_conv,
        "ffn_down": ffn_down,
    }
    pos = _pos_tables(ts)
    state = (state_C.astype(F32), state_n.astype(F32), _stabiliser_rows(state_m))

    xp, xs = x_prompt, x_sample
    prev, ms = None, []
    for l in range(DEPTH):
        x1p, c_all, n_all, m_fin = _mixer_call(xp, l, mod_all, wts, 0, emit_state=True,
                                               prev_states=prev, nseq=MIXER_ROWS // tp)
        prev = (c_all, n_all)
        ms.append(jnp.stack([m_fin[:, 0, L_MF:L_MF + HEADS], m_fin[:, 0, L_MB:L_MB + HEADS]], axis=1))

        (x1s,) = _mixer_call(xs, l, mod_all, wts, 1, pos=pos if l == 0 else None, state=state)

        xp, xs = _ffn_call(x1p.reshape(bp * tp, D_MODEL), x1s.reshape(bs * ts, D_MODEL),
                           l, mod_all, wts, tp, ts)
        xp = xp.reshape(bp, tp, D_MODEL)
        xs = xs.reshape(bs, ts, D_MODEL)

    return (xp, xs, c_all, n_all, jnp.stack(ms, axis=1))
```

```python
import functools

import numpy as np
import jax
import jax.numpy as jnp
from jax import lax
from jax.experimental import pallas as pl
from jax.experimental.pallas import tpu as pltpu

F32 = jnp.float32
BF16 = jnp.bfloat16

D_MODEL = 1024
DEPTH = 2
HEADS = 4
PAIRS = HEADS // 2
DH = 64
D_GROUP = 256
D_FF = 2816
N_MOD = 6
EPS = 1e-6
GRID_W = 64
POS_BASE = 10000.0
D_IN = 2576
N_GATE = 4 * HEADS

CHUNK = 256
GMLP_CHUNK = 128
HALO = 8
LANES = 128

C_Q, C_K, C_V, C_O = 0, 256, 512, 768
C_GU, C_GV, C_PZ, C_CB, C_CC, C_CX = 1024, 1280, 1536, 1792, 2048, 2304
Z_COLS = 2560
C_GATE = Z_COLS
W_COLS = Z_COLS + LANES
Z_GROUP = 512
L_BF, L_MF, L_BB, L_MB = 0, 8, 16, 24
L_STEP = L_MF - L_BF

MIXER_ROWS = 1024
FFN_ROWS = 1024
FFN_GROUP = FFN_ROWS // 8
FFN_SUB = 512
FFN_WEIGHT_CHUNKS = 16
FFN_WEIGHT_BUFS = 4
VMEM_LIMIT = 60 * 1024 * 1024


def _rms(x, g):
    ms = jnp.mean(x * x, axis=-1, keepdims=True)
    return x * lax.rsqrt(ms + EPS) * g


def _sigmoid(x):
    return 1.0 / (1.0 + jnp.exp(-x))


def _log_sigmoid(x):
    return jnp.minimum(x, 0.0) - jnp.log1p(jnp.exp(-jnp.abs(x)))


def _scan_lanes(x, op, reverse, fill):
    n = x.shape[1]
    lane = lax.broadcasted_iota(jnp.int32, x.shape, 1)
    k = 1
    while k < n:
        if reverse:
            sh = jnp.where(lane < n - k, pltpu.roll(x, n - k, axis=1), fill)
        else:
            sh = jnp.where(lane >= k, pltpu.roll(x, k, axis=1), fill)
        x = op(x, sh)
        k *= 2
    return x


def _shift_down(x, k):
    return pltpu.roll(x, k, axis=0)


def _shift_up(x, k):
    return pltpu.roll(x, x.shape[0] - k, axis=0)


def _prep_kernel(wt_ref, sel_ref, wout_ref, win_o_ref, wout_o_ref):
    for n in range(Z_COLS // LANES):
        r0 = LANES * n if n < 4 * D_GROUP // LANES else LANES * n + N_GATE
        win_o_ref[0, :, LANES * n:LANES * (n + 1)] = wt_ref[0, r0:r0 + LANES, :].T.astype(BF16)
    g = wt_ref[0, 4 * D_GROUP:4 * D_GROUP + LANES, :].T.astype(BF16)
    win_o_ref[0, :, C_GATE:W_COLS] = jnp.dot(g, sel_ref[...], preferred_element_type=F32).astype(BF16)
    wout_o_ref[...] = wout_ref[...].astype(BF16)


def _prep_weights(w_in, w_out):
    sel = np.zeros((LANES, LANES), np.float32)
    for j in range(N_GATE):
        sel[j, 8 * (j // HEADS) + j % HEADS] = 1.0
    rows = D_MODEL // 2
    return pl.pallas_call(
        _prep_kernel,
        out_shape=[jax.ShapeDtypeStruct((DEPTH, D_MODEL, W_COLS), BF16),
                   jax.ShapeDtypeStruct(w_out.shape, BF16)],
        grid=(DEPTH, D_MODEL // rows),
        in_specs=[pl.BlockSpec((1, D_IN, rows), lambda l, i: (l, 0, i)),
                  pl.BlockSpec((LANES, LANES), lambda l, i: (0, 0)),
                  pl.BlockSpec((1, rows, D_MODEL), lambda l, i: (l, i, 0))],
        out_specs=[pl.BlockSpec((1, rows, W_COLS), lambda l, i: (l, i, 0)),
                   pl.BlockSpec((1, rows, D_MODEL), lambda l, i: (l, i, 0))],
        compiler_params=pltpu.CompilerParams(dimension_semantics=("arbitrary", "arbitrary"),
                                             vmem_limit_bytes=VMEM_LIMIT),
        name="prep_mixer_w")(jnp.swapaxes(w_in, 1, 2), jnp.asarray(sel, BF16), w_out)


def _mod_kernel(c_ref, w_ref, b_ref, o_ref):
    c = c_ref[...]
    s = (c * _sigmoid(c)).astype(BF16)
    bias = b_ref[pl.ds(pl.program_id(0), 1), :]
    o_ref[0] = jnp.dot(s, w_ref[0].astype(BF16), preferred_element_type=F32) + bias


def _modulation(conds, ada_w, ada_b):
    n_out = N_MOD * D_MODEL
    bn = 1536
    return pl.pallas_call(
        _mod_kernel,
        out_shape=jax.ShapeDtypeStruct((DEPTH, 8, n_out), F32),
        grid=(DEPTH, n_out // bn),
        in_specs=[
            pl.BlockSpec((8, D_MODEL), lambda l, j: (0, 0)),
            pl.BlockSpec((1, D_MODEL, bn), lambda l, j: (l, 0, j)),
            pl.BlockSpec((DEPTH, bn), lambda l, j: (0, j)),
        ],
        out_specs=pl.BlockSpec((1, 8, bn), lambda l, j: (l, 0, j)),
        compiler_params=pltpu.CompilerParams(
            dimension_semantics=("arbitrary", "arbitrary"), vmem_limit_bytes=VMEM_LIMIT),
        name="adaln_mod",
    )(conds, ada_w, ada_b)


def _mod_rows(mod_ref, cond):
    row = mod_ref[0, pl.ds(cond, 1), :]
    return [row[:, i * D_MODEL:(i + 1) * D_MODEL] for i in range(N_MOD)]


def _gate_stats(gates):
    gt = gates.T
    b_f = _scan_lanes(_log_sigmoid(gt[8:16]), jnp.add, False, 0.0)
    g_f = gt[0:8] - b_f
    cm_f = _scan_lanes(g_f, jnp.maximum, False, -jnp.inf)
    b_b = _scan_lanes(_log_sigmoid(gt[24:32]), jnp.add, True, 0.0)
    g_b = gt[16:24] - b_b
    cm_b = _scan_lanes(g_b, jnp.maximum, True, -jnp.inf)
    stack = jnp.concatenate([b_f, cm_f, b_b, cm_b, jnp.zeros((LANES - 32, CHUNK), F32)], axis=0)
    return jnp.concatenate([g_f, g_b], axis=0), stack.T


def _pair_cols(mat, lane0, lane_lt_dh):
    return jnp.where(lane_lt_dh, mat[:, lane0:lane0 + 1], mat[:, lane0 + 1:lane0 + 2])


def _mlstm(z_ref, gcol_ref, grow_ref, cols_ref, bd_ref, ycat_ref, nc, state0, m0, need_final):
    lane_row = lax.broadcasted_iota(jnp.int32, (1, LANES), 1)
    lane = lax.broadcasted_iota(jnp.int32, (CHUNK, LANES), 1)
    lt_dh = lane < DH
    row_ll = lax.broadcasted_iota(jnp.int32, (CHUNK, CHUNK), 0)
    col_ll = lax.broadcasted_iota(jnp.int32, (CHUNK, CHUNK), 1)
    tri = (col_ll <= row_ll, col_ll >= row_ll)
    bd_row = lax.broadcasted_iota(jnp.int32, (LANES, 2 * LANES), 0)
    bd_col = lax.broadcasted_iota(jnp.int32, (LANES, 2 * LANES), 1)
    bd_mask = (bd_row < DH) == ((bd_col & (LANES - 1)) < DH)
    dirs = ((0, L_BF, L_MF, CHUNK - 1), (1, L_BB, L_MB, 0))

    m_start = [[None] * nc, [None] * nc]
    m_fin = [None, None]
    for d, _, _, last in dirs:
        m = m0
        order = range(nc) if d == 0 else range(nc - 1, -1, -1)
        for step, c in enumerate(order):
            m_start[d][c] = m
            if need_final or step < nc - 1:
                cl = cols_ref[c, last:last + 1, :]
                m = pltpu.roll(cl, L_STEP, axis=1) + jnp.maximum(m, cl)
        m_fin[d] = m

    have_state = [[False] * nc, [False] * nc]
    fin = [None, None]
    for d, l_b, l_m, last in dirs:
        bd = list(state0[d])
        order = range(nc) if d == 0 else range(nc - 1, -1, -1)
        for step, c in enumerate(order):
            if bd[0] is not None:
                have_state[d][c] = True
                for p in range(PAIRS):
                    bd_ref[d, c, p] = bd[p].astype(BF16)
            if not (need_final or step < nc - 1):
                continue
            rows = slice(HALO + c * CHUNK, HALO + (c + 1) * CHUNK)
            cols = cols_ref[c]
            m_last = jnp.maximum(cols[last:last + 1, :], m_start[d][c])
            a_col = jnp.exp(gcol_ref[c] - cols - pltpu.roll(m_last, LANES - L_STEP, axis=1))
            a_prev = jnp.exp(m_start[d][c] - m_last)
            for p in range(PAIRS):
                a_pair = _pair_cols(a_col, l_b + 2 * p, lt_dh)
                v_pair = z_ref[rows, C_V + LANES * p:C_V + LANES * (p + 1)]
                rhs = jnp.concatenate([v_pair * a_pair, a_pair], axis=1).astype(BF16)
                k_pair = (z_ref[rows, C_K + LANES * p:C_K + LANES * (p + 1)] * (DH ** -0.5)).astype(BF16)
                upd = lax.dot_general(k_pair, rhs, (((0,), (0,)), ((), ())),
                                      preferred_element_type=F32)
                upd = jnp.where(bd_mask, upd, 0.0)
                if bd[p] is None:
                    bd[p] = upd
                else:
                    scale = jnp.where(bd_row < DH, a_prev[:, l_m + 2 * p:l_m + 2 * p + 1],
                                      a_prev[:, l_m + 2 * p + 1:l_m + 2 * p + 2])
                    bd[p] = scale * bd[p] + upd
        fin[d] = bd

    for c in range(nc):
        rows = slice(HALO + c * CHUNK, HALO + (c + 1) * CHUNK)
        cols = cols_ref[c]
        grow = grow_ref[c]
        m_prev = jnp.where(lane_row < L_BB, m_start[0][c], m_start[1][c])
        m_all = jnp.maximum(cols, m_prev)
        w_inter = jnp.exp(m_prev - m_all)
        e_neg = jnp.exp(-(pltpu.roll(cols, L_STEP, axis=1) + m_all))
        for p in range(PAIRS):
            q_pair = z_ref[rows, C_Q + LANES * p:C_Q + LANES * (p + 1)].astype(BF16)
            k_f32 = z_ref[rows, C_K + LANES * p:C_K + LANES * (p + 1)] * (DH ** -0.5)
            v_pair = z_ref[rows, C_V + LANES * p:C_V + LANES * (p + 1)]
            p_cat = ([], [])
            va = []
            for j in range(2):
                h = 2 * p + j
                own = lt_dh if j == 0 else jnp.logical_not(lt_dh)
                k_own = jnp.where(own, k_f32, 0.0).astype(BF16)
                s = lax.dot_general(q_pair, k_own, (((1,), (1,)), ((), ())),
                                    preferred_element_type=F32)
                va.append(jnp.concatenate([jnp.where(own, v_pair, 0.0), jnp.where(own, 1.0, 0.0)],
                                          axis=1).astype(BF16))
                for d, _, l_m, _ in dirs:
                    w = jnp.exp(jnp.where(tri[d], grow[8 * d + h:8 * d + h + 1] - m_all[:, l_m + h:l_m + h + 1],
                                          -jnp.inf))
                    p_cat[d].append((s * w).astype(BF16))
            va = jnp.concatenate(va, axis=0)
            h_sum = None
            for d, _, l_m, _ in dirs:
                tot = jnp.dot(jnp.concatenate(p_cat[d], axis=1), va, preferred_element_type=F32)
                if have_state[d][c]:
                    e_pair = _pair_cols(w_inter, l_m + 2 * p, lt_dh)
                    qc = jnp.dot(q_pair, bd_ref[d, c, p], preferred_element_type=F32)
                    tot = tot + jnp.concatenate([e_pair, e_pair], axis=1) * qc
                floor = _pair_cols(e_neg, l_m + 2 * p, lt_dh)
                h_dir = tot[:, :LANES] / jnp.maximum(jnp.abs(tot[:, LANES:]), floor)
                h_sum = h_dir if h_sum is None else h_sum + h_dir
            o_pair = z_ref[rows, C_O + LANES * p:C_O + LANES * (p + 1)]
            ycat_ref[c * CHUNK:(c + 1) * CHUNK, LANES * p:LANES * (p + 1)] = (
                _sigmoid(o_pair) * h_sum).astype(BF16)

    m_row = jnp.where(lane_row < L_BB, m_fin[0], m_fin[1])
    return fin, m_row


def _pair_state(c_ref, n_ref, q, d, p):
    eye = (lax.broadcasted_iota(jnp.int32, (DH, DH), 0) == lax.broadcasted_iota(jnp.int32, (DH, DH), 1))
    zero = jnp.zeros((DH, DH), F32)
    blocks = []
    for j in range(2):
        h = 2 * p + j
        n_row = n_ref[q, 0, d, h:h + 1, :]
        n_col = jnp.sum(jnp.where(eye, n_row, 0.0), axis=1, keepdims=True)
        row = [zero] * 4
        row[j] = c_ref[q, 0, d, h]
        row[2 + j] = jnp.broadcast_to(n_col, (DH, DH))
        blocks.append(jnp.concatenate(row, axis=1))
    return jnp.concatenate(blocks, axis=0)


def _mixer_kernel(*refs, layer, seq_len, nseq, add_pos, has_state, emit_state, cond_base, n_prev):
    it = iter(refs)
    x_ref = next(it)
    if add_pos:
        er_ref = next(it)
        ec_ref = next(it)
    mod_ref = next(it)
    ng_ref = next(it)
    win_ref = next(it)
    gb_ref = next(it)
    ws_ref = next(it)
    gbias_ref = next(it)
    pbd_ref = next(it)
    psc_ref = next(it)
    cw_ref = next(it)
    wout_ref = next(it)
    if has_state:
        c0_ref = next(it)
        n0_ref = next(it)
        m0_ref = next(it)
    if n_prev:
        cprev_ref = next(it)
        nprev_ref = next(it)
    x1_ref = next(it)
    if emit_state:
        cout_ref = next(it)
        nout_ref = next(it)
        mout_ref = next(it)
    z_all = next(it)
    gcol_all = next(it)
    grow_all = next(it)
    cols_all = next(it)
    bd_all = next(it)
    ycat_all = next(it)

    nblk = seq_len // CHUNK
    cond = cond_base if cond_base == 0 else cond_base + pl.program_id(0)
    sh1, sc1, gt1 = _mod_rows(mod_ref, cond)[:3]
    ng = ng_ref[0]
    seqs = range(nseq)
    units = [(q, r) for q in seqs for r in range(nblk)]

    zeros_halo = jnp.zeros((HALO, Z_COLS - C_PZ), F32)
    for q in seqs:
        z_all[q, 0:HALO, C_PZ:Z_COLS] = zeros_halo
        z_all[q, HALO + seq_len:2 * HALO + seq_len, C_PZ:Z_COLS] = zeros_halo

    g_in = ng[0:1] * (1.0 + sc1)
    hbs = []
    for q, r in units:
        rows = slice(r * CHUNK, (r + 1) * CHUNK)
        xb = x_ref[q, rows, :]
        if add_pos:
            pieces = []
            for g in range(CHUNK // GRID_W):
                gi = r * (CHUNK // GRID_W) + g
                er = jnp.broadcast_to(er_ref[gi:gi + 1, :], (GRID_W, D_MODEL // 2))
                pieces.append(jnp.concatenate([er, ec_ref[...]], axis=1))
            xb = xb + jnp.concatenate(pieces, axis=0)
            x1_ref[q, rows, :] = xb
        hbs.append((_rms(xb, g_in) + sh1).astype(BF16))
    hb = jnp.concatenate(hbs, axis=0)
    gates_all = jnp.dot(hb, win_ref[0, :, C_GATE:W_COLS], preferred_element_type=F32) + gb_ref[0]
    for cg in range(Z_COLS // Z_GROUP):
        zc = jnp.dot(hb, win_ref[0, :, cg * Z_GROUP:(cg + 1) * Z_GROUP], preferred_element_type=F32)
        for u, (q, r) in enumerate(units):
            z_all[q, HALO + r * CHUNK:HALO + (r + 1) * CHUNK, cg * Z_GROUP:(cg + 1) * Z_GROUP] = (
                zc[u * CHUNK:(u + 1) * CHUNK])
    for u, (q, r) in enumerate(units):
        gates = gates_all[u * CHUNK:(u + 1) * CHUNK]
        gcol_all[q, r] = gates
        grow, cols = _gate_stats(gates)
        grow_all[q, r] = grow
        cols_all[q, r] = cols

    for q in seqs:
        if has_state:
            state0 = [[_pair_state(c0_ref, n0_ref, q, d, p) for p in range(PAIRS)] for d in range(2)]
            m0 = m0_ref[q, 0]
        else:
            state0 = [[None] * PAIRS, [None] * PAIRS]
            m0 = jnp.zeros((1, LANES), F32)
        fin, m_row = _mlstm(z_all.at[q], gcol_all.at[q], grow_all.at[q], cols_all.at[q], bd_all.at[q],
                            ycat_all.at[q], nblk, state0, m0, need_final=emit_state)
        if emit_state:
            for d in range(2):
                for p in range(PAIRS):
                    for j in range(2):
                        cout_ref[q, n_prev, d, 2 * p + j] = fin[d][p][DH * j:DH * (j + 1), DH * j:DH * (j + 1)]
                    n_t = fin[d][p][:, LANES:].T
                    nout_ref[q, n_prev, d, 2 * p:2 * p + 1, :] = n_t[0:1, 0:DH]
                    nout_ref[q, n_prev, d, 2 * p + 1:2 * p + 2, :] = n_t[DH:DH + 1, DH:LANES]
            mout_ref[q] = m_row
    if emit_state and n_prev:
        cout_ref[:, 0:n_prev] = cprev_ref[...]
        nout_ref[:, 0:n_prev] = nprev_ref[...]

    lane_g = lax.broadcasted_iota(jnp.int32, (CHUNK, D_GROUP), 1) // (D_GROUP // 4)
    lg = lax.broadcasted_iota(jnp.int32, (GMLP_CHUNK, D_GROUP), 1) // (D_GROUP // 4)
    row_c = lax.broadcasted_iota(jnp.int32, (CHUNK, D_GROUP), 0)
    half = jnp.where(lane_g == 0, 1, jnp.where(lane_g == 1, 2, jnp.where(lane_g == 2, 4, 8)))
    cw = cw_ref[0]
    ws_cat = jnp.concatenate([ws_ref[0, g] for g in range(4)], axis=1).astype(BF16)
    for q, r in units:
        z_ref = z_all.at[q]
        ycat_ref = ycat_all.at[q]
        rows = slice(r * CHUNK, (r + 1) * CHUNK)
        zrows = slice(HALO + r * CHUNK, HALO + (r + 1) * CHUNK)
        hrows = slice(r * CHUNK, (r + 1) * CHUNK + 2 * HALO)
        inner = slice(HALO, HALO + CHUNK)

        for s in range(CHUNK // GMLP_CHUNK):
            zs = slice(HALO + r * CHUNK + s * GMLP_CHUNK, HALO + r * CHUNK + (s + 1) * GMLP_CHUNK)
            vch = z_ref[zs, C_GV:C_GV + D_GROUP]
            v_stack = jnp.concatenate([jnp.where(lg == g, vch, 0.0) for g in range(4)], axis=0)
            mixed = gbias_ref[0] + jnp.dot(ws_cat, v_stack.astype(BF16), preferred_element_type=F32)
            yb = z_ref[zs, C_GU:C_GU + D_GROUP] * mixed
            ys = slice(r * CHUNK + s * GMLP_CHUNK, r * CHUNK + (s + 1) * GMLP_CHUNK)
            ycat_ref[ys, D_GROUP:2 * D_GROUP] = yb.astype(BF16)

        pz = z_ref[hrows, C_PZ:C_PZ + D_GROUP]
        a2 = pz + _shift_down(pz, 1)
        a4 = a2 + _shift_down(a2, 2)
        a8 = a4 + _shift_down(a4, 4)
        a16 = a8 + _shift_down(a8, 8)
        win = [a2[inner], _shift_up(a4, 1)[inner], _shift_up(a8, 3)[inner], _shift_up(a16, 7)[inner]]
        wsum = jnp.where(lane_g == 0, win[0], jnp.where(lane_g == 1, win[1],
                         jnp.where(lane_g == 2, win[2], win[3])))
        t_abs = row_c + r * CHUNK
        cnt = (jnp.minimum(t_abs + half, seq_len) - jnp.maximum(t_abs - half, 0)).astype(F32)
        pooled = wsum / cnt - pz[inner]
        yc = (jnp.dot(pooled.astype(BF16), pbd_ref[0], preferred_element_type=F32)
              * psc_ref[layer:layer + 1, :])
        ycat_ref[rows, 2 * D_GROUP:3 * D_GROUP] = yc.astype(BF16)

        u = z_ref[hrows, C_CC:C_CC + D_GROUP] * z_ref[hrows, C_CX:C_CX + D_GROUP]
        conv = (cw[0:1] * _shift_down(u, 1)[inner] + cw[1:2] * u[inner]
                + cw[2:3] * _shift_up(u, 1)[inner])
        yd = z_ref[zrows, C_CB:C_CB + D_GROUP] * conv
        ycat_ref[rows, 3 * D_GROUP:4 * D_GROUP] = yd.astype(BF16)

    y_all = jnp.dot(jnp.concatenate([ycat_all[q, r * CHUNK:(r + 1) * CHUNK, :] for q, r in units], axis=0),
                    wout_ref[0], preferred_element_type=F32)
    g_y = gt1 * ng[1:2]
    for u, (q, r) in enumerate(units):
        rows = slice(r * CHUNK, (r + 1) * CHUNK)
        xb = x1_ref[q, rows, :] if add_pos else x_ref[q, rows, :]
        x1_ref[q, rows, :] = xb + _rms(y_all[u * CHUNK:(u + 1) * CHUNK], g_y)


def _layer_spec(shape, layer):
    nd = len(shape)
    return pl.BlockSpec((1,) + tuple(shape[1:]), lambda b: (layer,) + (0,) * (nd - 1),
                        pipeline_mode=pl.Buffered(1))


def _mixer_call(x, layer, mod_all, wts, cond_base, pos=None, state=None, emit_state=False,
                prev_states=None, nseq=1):
    bsz, seq_len, _ = x.shape
    nblk = seq_len // CHUNK
    add_pos = pos is not None
    has_state = state is not None
    n_prev = 0 if prev_states is None else prev_states[0].shape[1]
    args = [x]
    in_specs = [pl.BlockSpec((nseq, seq_len, D_MODEL), lambda b: (b, 0, 0))]
    if add_pos:
        args += list(pos)
        in_specs += [pl.BlockSpec(p.shape, lambda b: (0, 0), pipeline_mode=pl.Buffered(1)) for p in pos]
    args.append(mod_all)
    in_specs.append(_layer_spec(mod_all.shape, layer))
    for name in ("norm_g", "w_in", "gate_b", "gmlp_ws", "gmlp_bias", "pool_bd", "pool_scale",
                 "conv_w", "w_out"):
        args.append(wts[name])
        if wts[name].ndim == 2:
            in_specs.append(pl.BlockSpec(wts[name].shape, lambda b: (0, 0), pipeline_mode=pl.Buffered(1)))
        else:
            in_specs.append(_layer_spec(wts[name].shape, layer))
    if has_state:
        args += list(state)
        in_specs += [pl.BlockSpec((nseq, 1, 2, HEADS, DH, DH), lambda b: (b, layer, 0, 0, 0, 0)),
                     pl.BlockSpec((nseq, 1, 2, HEADS, DH), lambda b: (b, layer, 0, 0, 0)),
                     pl.BlockSpec((nseq, 1, 1, LANES), lambda b: (b, layer, 0, 0))]
    out_shape = [jax.ShapeDtypeStruct((bsz, seq_len, D_MODEL), F32)]
    out_specs = [pl.BlockSpec((nseq, seq_len, D_MODEL), lambda b: (b, 0, 0))]
    if emit_state:
        if n_prev:
            args += list(prev_states)
            in_specs += [pl.BlockSpec((nseq, n_prev, 2, HEADS, DH, DH), lambda b: (b, 0, 0, 0, 0, 0)),
                         pl.BlockSpec((nseq, n_prev, 2, HEADS, DH), lambda b: (b, 0, 0, 0, 0))]
        out_shape += [jax.ShapeDtypeStruct((bsz, n_prev + 1, 2, HEADS, DH, DH), F32),
                      jax.ShapeDtypeStruct((bsz, n_prev + 1, 2, HEADS, DH), F32),
                      jax.ShapeDtypeStruct((bsz, 1, LANES), F32)]
        out_specs += [pl.BlockSpec((nseq, n_prev + 1, 2, HEADS, DH, DH), lambda b: (b, 0, 0, 0, 0, 0)),
                      pl.BlockSpec((nseq, n_prev + 1, 2, HEADS, DH), lambda b: (b, 0, 0, 0, 0)),
                      pl.BlockSpec((nseq, 1, LANES), lambda b: (b, 0, 0))]
    kern = functools.partial(_mixer_kernel, layer=layer, seq_len=seq_len, nseq=nseq, add_pos=add_pos,
                             has_state=has_state, emit_state=emit_state, cond_base=cond_base,
                             n_prev=n_prev)
    return pl.pallas_call(
        kern,
        out_shape=out_shape,
        grid=(bsz // nseq,),
        in_specs=in_specs,
        out_specs=out_specs,
        scratch_shapes=[pltpu.VMEM((nseq, seq_len + 2 * HALO, Z_COLS), F32),
                        pltpu.VMEM((nseq, nblk, CHUNK, LANES), F32),
                        pltpu.VMEM((nseq, nblk, 16, CHUNK), F32),
                        pltpu.VMEM((nseq, nblk, CHUNK, LANES), F32),
                        pltpu.VMEM((nseq, 2, nblk, PAIRS, LANES, 2 * LANES), BF16),
                        pltpu.VMEM((nseq, seq_len, D_MODEL), BF16)],
        compiler_params=pltpu.CompilerParams(
            dimension_semantics=("arbitrary",), vmem_limit_bytes=VMEM_LIMIT),
        name="mixer_t%d" % seq_len,
    )(*args)


def _ffn_kernel(mod_ref, ng_ref, cw_ref, x1p_hbm, x1s_hbm, up_hbm, down_hbm, outp_hbm, outs_hbm,
                up_bf, down_bf, act_ref, h2_ref, x1_buf, o_buf, up_stage, down_stage,
                in_sems, out_sems, w_sems, *, layer, tiles_p, tiles_s, seq_p, seq_s):
    i = pl.program_id(0)
    n_tiles = tiles_p + tiles_s
    rows = FFN_ROWS
    half = FFN_SUB // 2
    nsub = D_FF // half
    slot = i % 2
    is_prompt = i < tiles_p

    def piece_rows(tile, first_tile, j):
        base = (tile - first_tile) * rows + j * FFN_GROUP
        return pl.ds(base if isinstance(base, int) else pl.multiple_of(base, FFN_GROUP), FFN_GROUP)

    def x1_copies(x1_hbm, tile, first_tile, buf):
        return [pltpu.make_async_copy(x1_hbm.at[piece_rows(tile, first_tile, j)],
                                      x1_buf.at[buf, :, j, :], in_sems.at[buf, j]) for j in range(8)]

    def out_copies(out_hbm, tile, first_tile, buf):
        return [pltpu.make_async_copy(o_buf.at[buf, :, j, :],
                                      out_hbm.at[piece_rows(tile, first_tile, j)], out_sems.at[buf, j])
                for j in range(8)]

    def fetch(tile, buf):
        @pl.when(tile < tiles_p)
        def _():
            for cp in x1_copies(x1p_hbm, tile, 0, buf):
                cp.start()

        @pl.when(tile >= tiles_p)
        def _():
            for cp in x1_copies(x1s_hbm, tile, tiles_p, buf):
                cp.start()

    def await_fetch(buf):
        for cp in x1_copies(x1p_hbm, 0, 0, buf):
            cp.wait()

    def await_store(buf):
        for cp in out_copies(outp_hbm, 0, 0, buf):
            cp.wait()

    up_rows = up_stage.shape[1]
    down_rows = down_stage.shape[1]

    def slab(c, n):
        return pl.ds(c * n, n) if isinstance(c, int) else pl.ds(pl.multiple_of(c * n, n), n)

    def weight_copies(c, buf):
        return (pltpu.make_async_copy(up_hbm.at[layer, slab(c, up_rows), :], up_stage.at[buf],
                                      w_sems.at[buf, 0]),
                pltpu.make_async_copy(down_hbm.at[layer, slab(c, down_rows), :], down_stage.at[buf],
                                      w_sems.at[buf, 1]))

    @pl.when(i == 0)
    def _():
        for cp in x1_copies(x1p_hbm, 0, 0, 0):
            cp.start()
        depth = up_stage.shape[0]
        for c0 in range(depth - 1):
            for cp in weight_copies(c0, c0):
                cp.start()

        def convert(c, carry):
            buf = c % depth
            ahead = c + depth - 1

            @pl.when(ahead < FFN_WEIGHT_CHUNKS)
            def _():
                for cp in weight_copies(ahead, ahead % depth):
                    cp.start()

            for cp in weight_copies(c, buf):
                cp.wait()
            r = slab(c, up_rows)
            for s in range(nsub):
                up_bf[s, r, 0:half] = up_stage[buf, :, s * half:(s + 1) * half].astype(BF16)
                up_bf[s, r, half:FFN_SUB] = (
                    up_stage[buf, :, D_FF + s * half:D_FF + (s + 1) * half].astype(BF16))
            down_bf[slab(c, down_rows), :] = down_stage[buf].astype(BF16)
            return carry

        lax.fori_loop(0, FFN_WEIGHT_CHUNKS, convert, 0)

    await_fetch(slot)

    @pl.when(i >= 2)
    def _():
        await_store(slot)

    @pl.when(i + 1 < n_tiles)
    def _():
        fetch(i + 1, 1 - slot)

    cond = jnp.where(is_prompt, 0, i - tiles_p + 1)
    _, _, _, sh2, sc2, gt2 = _mod_rows(mod_ref, cond)
    ng = ng_ref[0]
    x1 = x1_buf[slot].reshape(rows, D_MODEL)
    h2_ref[...] = (_rms(x1, ng[2:3] * (1.0 + sc2)) + sh2).astype(BF16)
    h2 = h2_ref[...]

    sub = lax.broadcasted_iota(jnp.int32, (8, FFN_SUB), 0)
    gp, gs = seq_p // FFN_GROUP, seq_s // FFN_GROUP
    starts = ((sub % gs) == 0) | (((sub % gp) == 0) & is_prompt)
    ends = ((sub % gs) == gs - 1) | (((sub % gp) == gp - 1) & is_prompt)
    for s in range(nsub):
        a = jnp.dot(h2, up_bf[s], preferred_element_type=F32)
        cw = jnp.concatenate([cw_ref[0, :, s * half:(s + 1) * half],
                              0.5 * cw_ref[0, :, D_FF + s * half:D_FF + (s + 1) * half]], axis=1)
        a_prev = jnp.concatenate(
            [jnp.where(starts, 0.0, pltpu.roll(a[rows - 8:rows], 1, axis=0)), a[0:rows - 8]], axis=0)
        a_next = jnp.concatenate(
            [a[8:rows], jnp.where(ends, 0.0, pltpu.roll(a[0:8], 7, axis=0))], axis=0)
        ac = cw[0:1] * a_prev + cw[1:2] * a + cw[2:3] * a_next
        hg = ac[:, half:]
        act = (hg * (1.0 + jnp.tanh(hg))) * ac[:, :half]
        act_ref[:, s * half:(s + 1) * half] = act.astype(BF16)
    y = jnp.dot(act_ref[...], down_bf[...], preferred_element_type=F32)
    out = x1_buf[slot].reshape(rows, D_MODEL) + gt2 * _rms(y, ng[3:4])
    o_buf[slot] = out.reshape(FFN_GROUP, 8, D_MODEL)

    @pl.when(is_prompt)
    def _():
        for cp in out_copies(outp_hbm, i, 0, slot):
            cp.start()

    @pl.when(jnp.logical_not(is_prompt))
    def _():
        for cp in out_copies(outs_hbm, i, tiles_p, slot):
            cp.start()

    @pl.when(i == n_tiles - 1)
    def _():
        await_store(1 - slot)
        await_store(slot)


def _ffn_call(x1p, x1s, layer, mod_all, wts, seq_p, seq_s):
    tiles_p = x1p.shape[0] // FFN_ROWS
    tiles_s = x1s.shape[0] // FFN_ROWS
    assert tiles_p >= 1 and tiles_s >= 1 and tiles_p + tiles_s >= 2
    assert FFN_ROWS == 8 * FFN_GROUP and seq_p % FFN_GROUP == 0 and seq_s % seq_p == 0
    half = FFN_SUB // 2

    def resident(shape):
        nd = len(shape)
        return pl.BlockSpec((1,) + tuple(shape[1:]), lambda i: (layer,) + (0,) * (nd - 1),
                            pipeline_mode=pl.Buffered(1))

    hbm = pl.BlockSpec(memory_space=pl.ANY)
    return pl.pallas_call(
        functools.partial(_ffn_kernel, layer=layer, tiles_p=tiles_p, tiles_s=tiles_s,
                          seq_p=seq_p, seq_s=seq_s),
        out_shape=[jax.ShapeDtypeStruct(x1p.shape, F32), jax.ShapeDtypeStruct(x1s.shape, F32)],
        grid=(tiles_p + tiles_s,),
        in_specs=[resident(mod_all.shape), resident(wts["norm_g"].shape), resident(wts["ffn_conv"].shape),
                  hbm, hbm, hbm, hbm],
        out_specs=[hbm, hbm],
        scratch_shapes=[pltpu.VMEM((D_FF // half, D_MODEL, FFN_SUB), BF16),
                        pltpu.VMEM((D_FF, D_MODEL), BF16),
                        pltpu.VMEM((FFN_ROWS, D_FF), BF16),
                        pltpu.VMEM((FFN_ROWS, D_MODEL), BF16),
                        pltpu.VMEM((2, FFN_GROUP, 8, D_MODEL), F32),
                        pltpu.VMEM((2, FFN_GROUP, 8, D_MODEL), F32),
                        pltpu.VMEM((FFN_WEIGHT_BUFS, D_MODEL // FFN_WEIGHT_CHUNKS, 2 * D_FF), F32),
                        pltpu.VMEM((FFN_WEIGHT_BUFS, D_FF // FFN_WEIGHT_CHUNKS, D_MODEL), F32),
                        pltpu.SemaphoreType.DMA((2, 8)),
                        pltpu.SemaphoreType.DMA((2, 8)),
                        pltpu.SemaphoreType.DMA((FFN_WEIGHT_BUFS, 2))],
        compiler_params=pltpu.CompilerParams(
            dimension_semantics=("arbitrary",), vmem_limit_bytes=VMEM_LIMIT),
        name="ffn",
    )(mod_all, wts["norm_g"], wts["ffn_conv"], x1p, x1s, wts["ffn_up"], wts["ffn_down"])


def _pos_tables(t_len):
    quarter = D_MODEL // 4
    omega = (1.0 / (np.float32(POS_BASE) ** (np.arange(quarter, dtype=np.float32) / np.float32(quarter))))
    omega = omega.astype(np.float32)

    def emb(p):
        a = p.astype(np.float32)[:, None] * omega[None, :]
        return np.concatenate([np.sin(a), np.cos(a)], axis=-1).astype(np.float32)

    return jnp.asarray(emb(np.arange(t_len // GRID_W))), jnp.asarray(emb(np.arange(GRID_W)))


def _stabiliser_rows(state_m):
    bs = state_m.shape[0]
    m = jnp.pad(state_m.astype(F32), ((0, 0), (0, 0), (0, 0), (0, 16 - HEADS)))
    return jnp.pad(m.reshape(bs, DEPTH, 1, 32), ((0, 0), (0, 0), (0, 0), (L_MF, LANES - 32 - L_MF)))


def kernel(x_prompt, x_sample, state_C, state_n, state_m, c, c_ctx, norm_g, ada_w, ada_b, w_in,
           gate_b, gmlp_ws, gmlp_b, pool_w, pool_scale, conv_w, w_out, ffn_up, ffn_conv, ffn_down):
    bp, tp, _ = x_prompt.shape
    bs, ts, _ = x_sample.shape
    assert tp == CHUNK and ts % CHUNK == 0 and FFN_ROWS % tp == 0 and ts == FFN_ROWS

    conds = jnp.concatenate([c_ctx[None, :], c, jnp.zeros((7 - bs, D_MODEL), F32)], axis=0)
    mod_all = _modulation(conds, ada_w, ada_b)
    win_p, wout_p = _prep_weights(w_in, w_out)

    gb = jnp.pad(gate_b, ((0, 0), (0, 0), (0, 8 - HEADS))).reshape(DEPTH, 1, 32)
    dg = D_GROUP // 4
    wts = {
        "norm_g": norm_g,
        "w_in": win_p,
        "gate_b": jnp.pad(gb, ((0, 0), (0, 0), (0, LANES - 32))),
        "gmlp_ws": gmlp_ws,
        "gmlp_bias": jnp.repeat(jnp.swapaxes(gmlp_b, 1, 2), dg, axis=2),
        "pool_bd": jnp.einsum("lgcd,gh->lgchd", pool_w, jnp.eye(4, dtype=F32)).reshape(
            DEPTH, D_GROUP, D_GROUP).astype(BF16),
        "pool_scale": pool_scale,
        "conv_w": conv_w,
        "w_out": wout_p,
        "ffn_up": ffn_up,
        "ffn_conv": ffn_conv,
        "ffn_down": ffn_down,
    }
    pos = _pos_tables(ts)
    state = (state_C.astype(F32), state_n.astype(F32), _stabiliser_rows(state_m))

    xp, xs = x_prompt, x_sample
    prev, ms = None, []
    for l in range(DEPTH):
        x1p, c_all, n_all, m_fin = _mixer_call(xp, l, mod_all, wts, 0, emit_state=True,
                                               prev_states=prev, nseq=MIXER_ROWS // tp)
        prev = (c_all, n_all)
        ms.append(jnp.stack([m_fin[:, 0, L_MF:L_MF + HEADS], m_fin[:, 0, L_MB:L_MB + HEADS]], axis=1))

        (x1s,) = _mixer_call(xs, l, mod_all, wts, 1, pos=pos if l == 0 else None, state=state)

        xp, xs = _ffn_call(x1p.reshape(bp * tp, D_MODEL), x1s.reshape(bs * ts, D_MODEL),
                           l, mod_all, wts, tp, ts)
        xp = xp.reshape(bp, tp, D_MODEL)
        xs = xs.reshape(bs, ts, D_MODEL)

    return (xp, xs, c_all, n_all, jnp.stack(ms, axis=1))
```

```python
import functools

import numpy as np
import jax
import jax.numpy as jnp
from jax import lax
from jax.experimental import pallas as pl
from jax.experimental.pallas import tpu as pltpu

F32 = jnp.float32
BF16 = jnp.bfloat16

D_MODEL = 1024
DEPTH = 2
HEADS = 4
PAIRS = HEADS // 2
DH = 64
D_GROUP = 256
D_FF = 2816
N_MOD = 6
EPS = 1e-6
GRID_W = 64
POS_BASE = 10000.0
D_IN = 2576
N_GATE = 4 * HEADS

CHUNK = 256
GMLP_CHUNK = 128
HALO = 8
LANES = 128

C_Q, C_K, C_V, C_O = 0, 256, 512, 768
C_GU, C_GV, C_PZ, C_CB, C_CC, C_CX = 1024, 1280, 1536, 1792, 2048, 2304
Z_COLS = 2560
C_GATE = Z_COLS
W_COLS = Z_COLS + LANES
Z_GROUP = 512
L_BF, L_MF, L_BB, L_MB = 0, 8, 16, 24
L_STEP = L_MF - L_BF

MIXER_ROWS = 1024
FFN_ROWS = 1024
FFN_GROUP = FFN_ROWS // 8
FFN_SUB = 512
FFN_WEIGHT_CHUNKS = 16
FFN_WEIGHT_BUFS = 6
VMEM_LIMIT = 60 * 1024 * 1024


def _rms(x, g):
    ms = jnp.mean(x * x, axis=-1, keepdims=True)
    return x * lax.rsqrt(ms + EPS) * g


def _sigmoid(x):
    return 1.0 / (1.0 + jnp.exp(-x))


def _log_sigmoid(x):
    return jnp.minimum(x, 0.0) - jnp.log1p(jnp.exp(-jnp.abs(x)))


def _scan_lanes(x, op, reverse, fill):
    n = x.shape[1]
    lane = lax.broadcasted_iota(jnp.int32, x.shape, 1)
    k = 1
    while k < n:
        if reverse:
            sh = jnp.where(lane < n - k, pltpu.roll(x, n - k, axis=1), fill)
        else:
            sh = jnp.where(lane >= k, pltpu.roll(x, k, axis=1), fill)
        x = op(x, sh)
        k *= 2
    return x


def _shift_down(x, k):
    return pltpu.roll(x, k, axis=0)


def _shift_up(x, k):
    return pltpu.roll(x, x.shape[0] - k, axis=0)


def _prep_kernel(wt_ref, sel_ref, wout_ref, win_o_ref, wout_o_ref):
    for n in range(Z_COLS // LANES):
        r0 = LANES * n if n < 4 * D_GROUP // LANES else LANES * n + N_GATE
        win_o_ref[0, :, LANES * n:LANES * (n + 1)] = wt_ref[0, r0:r0 + LANES, :].T.astype(BF16)
    g = wt_ref[0, 4 * D_GROUP:4 * D_GROUP + LANES, :].T.astype(BF16)
    win_o_ref[0, :, C_GATE:W_COLS] = jnp.dot(g, sel_ref[...], preferred_element_type=F32).astype(BF16)
    wout_o_ref[...] = wout_ref[...].astype(BF16)


def _prep_weights(w_in, w_out):
    sel = np.zeros((LANES, LANES), np.float32)
    for j in range(N_GATE):
        sel[j, 8 * (j // HEADS) + j % HEADS] = 1.0
    rows = D_MODEL // 2
    return pl.pallas_call(
        _prep_kernel,
        out_shape=[jax.ShapeDtypeStruct((DEPTH, D_MODEL, W_COLS), BF16),
                   jax.ShapeDtypeStruct(w_out.shape, BF16)],
        grid=(DEPTH, D_MODEL // rows),
        in_specs=[pl.BlockSpec((1, D_IN, rows), lambda l, i: (l, 0, i)),
                  pl.BlockSpec((LANES, LANES), lambda l, i: (0, 0)),
                  pl.BlockSpec((1, rows, D_MODEL), lambda l, i: (l, i, 0))],
        out_specs=[pl.BlockSpec((1, rows, W_COLS), lambda l, i: (l, i, 0)),
                   pl.BlockSpec((1, rows, D_MODEL), lambda l, i: (l, i, 0))],
        compiler_params=pltpu.CompilerParams(dimension_semantics=("arbitrary", "arbitrary"),
                                             vmem_limit_bytes=VMEM_LIMIT),
        name="prep_mixer_w")(jnp.swapaxes(w_in, 1, 2), jnp.asarray(sel, BF16), w_out)


def _mod_kernel(c_ref, w_ref, b_ref, o_ref):
    c = c_ref[...]
    s = (c * _sigmoid(c)).astype(BF16)
    bias = b_ref[pl.ds(pl.program_id(0), 1), :]
    o_ref[0] = jnp.dot(s, w_ref[0].astype(BF16), preferred_element_type=F32) + bias


def _modulation(conds, ada_w, ada_b):
    n_out = N_MOD * D_MODEL
    bn = 1536
    return pl.pallas_call(
        _mod_kernel,
        out_shape=jax.ShapeDtypeStruct((DEPTH, 8, n_out), F32),
        grid=(DEPTH, n_out // bn),
        in_specs=[
            pl.BlockSpec((8, D_MODEL), lambda l, j: (0, 0)),
            pl.BlockSpec((1, D_MODEL, bn), lambda l, j: (l, 0, j)),
            pl.BlockSpec((DEPTH, bn), lambda l, j: (0, j)),
        ],
        out_specs=pl.BlockSpec((1, 8, bn), lambda l, j: (l, 0, j)),
        compiler_params=pltpu.CompilerParams(
            dimension_semantics=("arbitrary", "arbitrary"), vmem_limit_bytes=VMEM_LIMIT),
        name="adaln_mod",
    )(conds, ada_w, ada_b)


def _mod_rows(mod_ref, cond):
    row = mod_ref[0, pl.ds(cond, 1), :]
    return [row[:, i * D_MODEL:(i + 1) * D_MODEL] for i in range(N_MOD)]


def _gate_stats(gates):
    gt = gates.T
    b_f = _scan_lanes(_log_sigmoid(gt[8:16]), jnp.add, False, 0.0)
    g_f = gt[0:8] - b_f
    cm_f = _scan_lanes(g_f, jnp.maximum, False, -jnp.inf)
    b_b = _scan_lanes(_log_sigmoid(gt[24:32]), jnp.add, True, 0.0)
    g_b = gt[16:24] - b_b
    cm_b = _scan_lanes(g_b, jnp.maximum, True, -jnp.inf)
    stack = jnp.concatenate([b_f, cm_f, b_b, cm_b, jnp.zeros((LANES - 32, CHUNK), F32)], axis=0)
    return jnp.concatenate([g_f, g_b], axis=0), stack.T


def _pair_cols(mat, lane0, lane_lt_dh):
    return jnp.where(lane_lt_dh, mat[:, lane0:lane0 + 1], mat[:, lane0 + 1:lane0 + 2])


def _mlstm(z_ref, gcol_ref, grow_ref, cols_ref, bd_ref, ycat_ref, nc, state0, m0, need_final):
    lane_row = lax.broadcasted_iota(jnp.int32, (1, LANES), 1)
    lane = lax.broadcasted_iota(jnp.int32, (CHUNK, LANES), 1)
    lt_dh = lane < DH
    row_ll = lax.broadcasted_iota(jnp.int32, (CHUNK, CHUNK), 0)
    col_ll = lax.broadcasted_iota(jnp.int32, (CHUNK, CHUNK), 1)
    tri = (col_ll <= row_ll, col_ll >= row_ll)
    bd_row = lax.broadcasted_iota(jnp.int32, (LANES, 2 * LANES), 0)
    bd_col = lax.broadcasted_iota(jnp.int32, (LANES, 2 * LANES), 1)
    bd_mask = (bd_row < DH) == ((bd_col & (LANES - 1)) < DH)
    dirs = ((0, L_BF, L_MF, CHUNK - 1), (1, L_BB, L_MB, 0))

    m_start = [[None] * nc, [None] * nc]
    m_fin = [None, None]
    for d, _, _, last in dirs:
        m = m0
        order = range(nc) if d == 0 else range(nc - 1, -1, -1)
        for step, c in enumerate(order):
            m_start[d][c] = m
            if need_final or step < nc - 1:
                cl = cols_ref[c, last:last + 1, :]
                m = pltpu.roll(cl, L_STEP, axis=1) + jnp.maximum(m, cl)
        m_fin[d] = m

    have_state = [[False] * nc, [False] * nc]
    fin = [None, None]
    for d, l_b, l_m, last in dirs:
        bd = list(state0[d])
        order = range(nc) if d == 0 else range(nc - 1, -1, -1)
        for step, c in enumerate(order):
            if bd[0] is not None:
                have_state[d][c] = True
                for p in range(PAIRS):
                    bd_ref[d, c, p] = bd[p].astype(BF16)
            if not (need_final or step < nc - 1):
                continue
            rows = slice(HALO + c * CHUNK, HALO + (c + 1) * CHUNK)
            cols = cols_ref[c]
            m_last = jnp.maximum(cols[last:last + 1, :], m_start[d][c])
            a_col = jnp.exp(gcol_ref[c] - cols - pltpu.roll(m_last, LANES - L_STEP, axis=1))
            a_prev = jnp.exp(m_start[d][c] - m_last)
            for p in range(PAIRS):
                a_pair = _pair_cols(a_col, l_b + 2 * p, lt_dh)
                v_pair = z_ref[rows, C_V + LANES * p:C_V + LANES * (p + 1)]
                rhs = jnp.concatenate([v_pair * a_pair, a_pair], axis=1).astype(BF16)
                k_pair = (z_ref[rows, C_K + LANES * p:C_K + LANES * (p + 1)] * (DH ** -0.5)).astype(BF16)
                upd = lax.dot_general(k_pair, rhs, (((0,), (0,)), ((), ())),
                                      preferred_element_type=F32)
                upd = jnp.where(bd_mask, upd, 0.0)
                if bd[p] is None:
                    bd[p] = upd
                else:
                    scale = jnp.where(bd_row < DH, a_prev[:, l_m + 2 * p:l_m + 2 * p + 1],
                                      a_prev[:, l_m + 2 * p + 1:l_m + 2 * p + 2])
                    bd[p] = scale * bd[p] + upd
        fin[d] = bd

    for c in range(nc):
        rows = slice(HALO + c * CHUNK, HALO + (c + 1) * CHUNK)
        cols = cols_ref[c]
        grow = grow_ref[c]
        m_prev = jnp.where(lane_row < L_BB, m_start[0][c], m_start[1][c])
        m_all = jnp.maximum(cols, m_prev)
        w_inter = jnp.exp(m_prev - m_all)
        e_neg = jnp.exp(-(pltpu.roll(cols, L_STEP, axis=1) + m_all))
        for p in range(PAIRS):
            q_pair = z_ref[rows, C_Q + LANES * p:C_Q + LANES * (p + 1)].astype(BF16)
            k_f32 = z_ref[rows, C_K + LANES * p:C_K + LANES * (p + 1)] * (DH ** -0.5)
            v_pair = z_ref[rows, C_V + LANES * p:C_V + LANES * (p + 1)]
            p_cat = ([], [])
            va = []
            for j in range(2):
                h = 2 * p + j
                own = lt_dh if j == 0 else jnp.logical_not(lt_dh)
                k_own = jnp.where(own, k_f32, 0.0).astype(BF16)
                s = lax.dot_general(q_pair, k_own, (((1,), (1,)), ((), ())),
                                    preferred_element_type=F32)
                va.append(jnp.concatenate([jnp.where(own, v_pair, 0.0), jnp.where(own, 1.0, 0.0)],
                                          axis=1).astype(BF16))
                for d, _, l_m, _ in dirs:
                    w = jnp.exp(jnp.where(tri[d], grow[8 * d + h:8 * d + h + 1] - m_all[:, l_m + h:l_m + h + 1],
                                          -jnp.inf))
                    p_cat[d].append((s * w).astype(BF16))
            va = jnp.concatenate(va, axis=0)
            h_sum = None
            for d, _, l_m, _ in dirs:
                tot = jnp.dot(jnp.concatenate(p_cat[d], axis=1), va, preferred_element_type=F32)
                if have_state[d][c]:
                    e_pair = _pair_cols(w_inter, l_m + 2 * p, lt_dh)
                    qc = jnp.dot(q_pair, bd_ref[d, c, p], preferred_element_type=F32)
                    tot = tot + jnp.concatenate([e_pair, e_pair], axis=1) * qc
                floor = _pair_cols(e_neg, l_m + 2 * p, lt_dh)
                h_dir = tot[:, :LANES] / jnp.maximum(jnp.abs(tot[:, LANES:]), floor)
                h_sum = h_dir if h_sum is None else h_sum + h_dir
            o_pair = z_ref[rows, C_O + LANES * p:C_O + LANES * (p + 1)]
            ycat_ref[c * CHUNK:(c + 1) * CHUNK, LANES * p:LANES * (p + 1)] = (
                _sigmoid(o_pair) * h_sum).astype(BF16)

    m_row = jnp.where(lane_row < L_BB, m_fin[0], m_fin[1])
    return fin, m_row


def _pair_state(c_ref, n_ref, q, d, p):
    eye = (lax.broadcasted_iota(jnp.int32, (DH, DH), 0) == lax.broadcasted_iota(jnp.int32, (DH, DH), 1))
    zero = jnp.zeros((DH, DH), F32)
    blocks = []
    for j in range(2):
        h = 2 * p + j
        n_row = n_ref[q, 0, d, h:h + 1, :]
        n_col = jnp.sum(jnp.where(eye, n_row, 0.0), axis=1, keepdims=True)
        row = [zero] * 4
        row[j] = c_ref[q, 0, d, h]
        row[2 + j] = jnp.broadcast_to(n_col, (DH, DH))
        blocks.append(jnp.concatenate(row, axis=1))
    return jnp.concatenate(blocks, axis=0)


def _mixer_kernel(*refs, layer, seq_len, nseq, add_pos, has_state, emit_state, cond_base, n_prev):
    it = iter(refs)
    x_ref = next(it)
    if add_pos:
        er_ref = next(it)
        ec_ref = next(it)
    mod_ref = next(it)
    ng_ref = next(it)
    win_ref = next(it)
    gb_ref = next(it)
    ws_ref = next(it)
    gbias_ref = next(it)
    pbd_ref = next(it)
    psc_ref = next(it)
    cw_ref = next(it)
    wout_ref = next(it)
    if has_state:
        c0_ref = next(it)
        n0_ref = next(it)
        m0_ref = next(it)
    if n_prev:
        cprev_ref = next(it)
        nprev_ref = next(it)
    x1_ref = next(it)
    if emit_state:
        cout_ref = next(it)
        nout_ref = next(it)
        mout_ref = next(it)
    z_all = next(it)
    gcol_all = next(it)
    grow_all = next(it)
    cols_all = next(it)
    bd_all = next(it)
    ycat_all = next(it)

    nblk = seq_len // CHUNK
    cond = cond_base if cond_base == 0 else cond_base + pl.program_id(0)
    sh1, sc1, gt1 = _mod_rows(mod_ref, cond)[:3]
    ng = ng_ref[0]
    seqs = range(nseq)
    units = [(q, r) for q in seqs for r in range(nblk)]

    zeros_halo = jnp.zeros((HALO, Z_COLS - C_PZ), F32)
    for q in seqs:
        z_all[q, 0:HALO, C_PZ:Z_COLS] = zeros_halo
        z_all[q, HALO + seq_len:2 * HALO + seq_len, C_PZ:Z_COLS] = zeros_halo

    g_in = ng[0:1] * (1.0 + sc1)
    hbs = []
    for q, r in units:
        rows = slice(r * CHUNK, (r + 1) * CHUNK)
        xb = x_ref[q, rows, :]
        if add_pos:
            pieces = []
            for g in range(CHUNK // GRID_W):
                gi = r * (CHUNK // GRID_W) + g
                er = jnp.broadcast_to(er_ref[gi:gi + 1, :], (GRID_W, D_MODEL // 2))
                pieces.append(jnp.concatenate([er, ec_ref[...]], axis=1))
            xb = xb + jnp.concatenate(pieces, axis=0)
            x1_ref[q, rows, :] = xb
        hbs.append((_rms(xb, g_in) + sh1).astype(BF16))
    hb = jnp.concatenate(hbs, axis=0)
    gates_all = jnp.dot(hb, win_ref[0, :, C_GATE:W_COLS], preferred_element_type=F32) + gb_ref[0]
    for cg in range(Z_COLS // Z_GROUP):
        zc = jnp.dot(hb, win_ref[0, :, cg * Z_GROUP:(cg + 1) * Z_GROUP], preferred_element_type=F32)
        for u, (q, r) in enumerate(units):
            z_all[q, HALO + r * CHUNK:HALO + (r + 1) * CHUNK, cg * Z_GROUP:(cg + 1) * Z_GROUP] = (
                zc[u * CHUNK:(u + 1) * CHUNK])
    for u, (q, r) in enumerate(units):
        gates = gates_all[u * CHUNK:(u + 1) * CHUNK]
        gcol_all[q, r] = gates
        grow, cols = _gate_stats(gates)
        grow_all[q, r] = grow
        cols_all[q, r] = cols

    for q in seqs:
        if has_state:
            state0 = [[_pair_state(c0_ref, n0_ref, q, d, p) for p in range(PAIRS)] for d in range(2)]
            m0 = m0_ref[q, 0]
        else:
            state0 = [[None] * PAIRS, [None] * PAIRS]
            m0 = jnp.zeros((1, LANES), F32)
        fin, m_row = _mlstm(z_all.at[q], gcol_all.at[q], grow_all.at[q], cols_all.at[q], bd_all.at[q],
                            ycat_all.at[q], nblk, state0, m0, need_final=emit_state)
        if emit_state:
            for d in range(2):
                for p in range(PAIRS):
                    for j in range(2):
                        cout_ref[q, n_prev, d, 2 * p + j] = fin[d][p][DH * j:DH * (j + 1), DH * j:DH * (j + 1)]
                    n_t = fin[d][p][:, LANES:].T
                    nout_ref[q, n_prev, d, 2 * p:2 * p + 1, :] = n_t[0:1, 0:DH]
                    nout_ref[q, n_prev, d, 2 * p + 1:2 * p + 2, :] = n_t[DH:DH + 1, DH:LANES]
            mout_ref[q] = m_row
    if emit_state and n_prev:
        cout_ref[:, 0:n_prev] = cprev_ref[...]
        nout_ref[:, 0:n_prev] = nprev_ref[...]

    lane_g = lax.broadcasted_iota(jnp.int32, (CHUNK, D_GROUP), 1) // (D_GROUP // 4)
    lg = lax.broadcasted_iota(jnp.int32, (GMLP_CHUNK, D_GROUP), 1) // (D_GROUP // 4)
    row_c = lax.broadcasted_iota(jnp.int32, (CHUNK, D_GROUP), 0)
    half = jnp.where(lane_g == 0, 1, jnp.where(lane_g == 1, 2, jnp.where(lane_g == 2, 4, 8)))
    cw = cw_ref[0]
    ws_cat = jnp.concatenate([ws_ref[0, g] for g in range(4)], axis=1).astype(BF16)
    for q, r in units:
        z_ref = z_all.at[q]
        ycat_ref = ycat_all.at[q]
        rows = slice(r * CHUNK, (r + 1) * CHUNK)
        zrows = slice(HALO + r * CHUNK, HALO + (r + 1) * CHUNK)
        hrows = slice(r * CHUNK, (r + 1) * CHUNK + 2 * HALO)
        inner = slice(HALO, HALO + CHUNK)

        for s in range(CHUNK // GMLP_CHUNK):
            zs = slice(HALO + r * CHUNK + s * GMLP_CHUNK, HALO + r * CHUNK + (s + 1) * GMLP_CHUNK)
            vch = z_ref[zs, C_GV:C_GV + D_GROUP]
            v_stack = jnp.concatenate([jnp.where(lg == g, vch, 0.0) for g in range(4)], axis=0)
            mixed = gbias_ref[0] + jnp.dot(ws_cat, v_stack.astype(BF16), preferred_element_type=F32)
            yb = z_ref[zs, C_GU:C_GU + D_GROUP] * mixed
            ys = slice(r * CHUNK + s * GMLP_CHUNK, r * CHUNK + (s + 1) * GMLP_CHUNK)
            ycat_ref[ys, D_GROUP:2 * D_GROUP] = yb.astype(BF16)

        pz = z_ref[hrows, C_PZ:C_PZ + D_GROUP]
        a2 = pz + _shift_down(pz, 1)
        a4 = a2 + _shift_down(a2, 2)
        a8 = a4 + _shift_down(a4, 4)
        a16 = a8 + _shift_down(a8, 8)
        win = [a2[inner], _shift_up(a4, 1)[inner], _shift_up(a8, 3)[inner], _shift_up(a16, 7)[inner]]
        wsum = jnp.where(lane_g == 0, win[0], jnp.where(lane_g == 1, win[1],
                         jnp.where(lane_g == 2, win[2], win[3])))
        t_abs = row_c + r * CHUNK
        cnt = (jnp.minimum(t_abs + half, seq_len) - jnp.maximum(t_abs - half, 0)).astype(F32)
        pooled = wsum / cnt - pz[inner]
        yc = (jnp.dot(pooled.astype(BF16), pbd_ref[0], preferred_element_type=F32)
              * psc_ref[layer:layer + 1, :])
        ycat_ref[rows, 2 * D_GROUP:3 * D_GROUP] = yc.astype(BF16)

        u = z_ref[hrows, C_CC:C_CC + D_GROUP] * z_ref[hrows, C_CX:C_CX + D_GROUP]
        conv = (cw[0:1] * _shift_down(u, 1)[inner] + cw[1:2] * u[inner]
                + cw[2:3] * _shift_up(u, 1)[inner])
        yd = z_ref[zrows, C_CB:C_CB + D_GROUP] * conv
        ycat_ref[rows, 3 * D_GROUP:4 * D_GROUP] = yd.astype(BF16)

    y_all = jnp.dot(jnp.concatenate([ycat_all[q, r * CHUNK:(r + 1) * CHUNK, :] for q, r in units], axis=0),
                    wout_ref[0], preferred_element_type=F32)
    g_y = gt1 * ng[1:2]
    for u, (q, r) in enumerate(units):
        rows = slice(r * CHUNK, (r + 1) * CHUNK)
        xb = x1_ref[q, rows, :] if add_pos else x_ref[q, rows, :]
        x1_ref[q, rows, :] = xb + _rms(y_all[u * CHUNK:(u + 1) * CHUNK], g_y)


def _layer_spec(shape, layer):
    nd = len(shape)
    return pl.BlockSpec((1,) + tuple(shape[1:]), lambda b: (layer,) + (0,) * (nd - 1),
                        pipeline_mode=pl.Buffered(1))


def _mixer_call(x, layer, mod_all, wts, cond_base, pos=None, state=None, emit_state=False,
                prev_states=None, nseq=1):
    bsz, seq_len, _ = x.shape
    nblk = seq_len // CHUNK
    add_pos = pos is not None
    has_state = state is not None
    n_prev = 0 if prev_states is None else prev_states[0].shape[1]
    args = [x]
    in_specs = [pl.BlockSpec((nseq, seq_len, D_MODEL), lambda b: (b, 0, 0))]
    if add_pos:
        args += list(pos)
        in_specs += [pl.BlockSpec(p.shape, lambda b: (0, 0), pipeline_mode=pl.Buffered(1)) for p in pos]
    args.append(mod_all)
    in_specs.append(_layer_spec(mod_all.shape, layer))
    for name in ("norm_g", "w_in", "gate_b", "gmlp_ws", "gmlp_bias", "pool_bd", "pool_scale",
                 "conv_w", "w_out"):
        args.append(wts[name])
        if wts[name].ndim == 2:
            in_specs.append(pl.BlockSpec(wts[name].shape, lambda b: (0, 0), pipeline_mode=pl.Buffered(1)))
        else:
            in_specs.append(_layer_spec(wts[name].shape, layer))
    if has_state:
        args += list(state)
        in_specs += [pl.BlockSpec((nseq, 1, 2, HEADS, DH, DH), lambda b: (b, layer, 0, 0, 0, 0)),
                     pl.BlockSpec((nseq, 1, 2, HEADS, DH), lambda b: (b, layer, 0, 0, 0)),
                     pl.BlockSpec((nseq, 1, 1, LANES), lambda b: (b, layer, 0, 0))]
    out_shape = [jax.ShapeDtypeStruct((bsz, seq_len, D_MODEL), F32)]
    out_specs = [pl.BlockSpec((nseq, seq_len, D_MODEL), lambda b: (b, 0, 0))]
    if emit_state:
        if n_prev:
            args += list(prev_states)
            in_specs += [pl.BlockSpec((nseq, n_prev, 2, HEADS, DH, DH), lambda b: (b, 0, 0, 0, 0, 0)),
                         pl.BlockSpec((nseq, n_prev, 2, HEADS, DH), lambda b: (b, 0, 0, 0, 0))]
        out_shape += [jax.ShapeDtypeStruct((bsz, n_prev + 1, 2, HEADS, DH, DH), F32),
                      jax.ShapeDtypeStruct((bsz, n_prev + 1, 2, HEADS, DH), F32),
                      jax.ShapeDtypeStruct((bsz, 1, LANES), F32)]
        out_specs += [pl.BlockSpec((nseq, n_prev + 1, 2, HEADS, DH, DH), lambda b: (b, 0, 0, 0, 0, 0)),
                      pl.BlockSpec((nseq, n_prev + 1, 2, HEADS, DH), lambda b: (b, 0, 0, 0, 0)),
                      pl.BlockSpec((nseq, 1, LANES), lambda b: (b, 0, 0))]
    kern = functools.partial(_mixer_kernel, layer=layer, seq_len=seq_len, nseq=nseq, add_pos=add_pos,
                             has_state=has_state, emit_state=emit_state, cond_base=cond_base,
                             n_prev=n_prev)
    return pl.pallas_call(
        kern,
        out_shape=out_shape,
        grid=(bsz // nseq,),
        in_specs=in_specs,
        out_specs=out_specs,
        scratch_shapes=[pltpu.VMEM((nseq, seq_len + 2 * HALO, Z_COLS), F32),
                        pltpu.VMEM((nseq, nblk, CHUNK, LANES), F32),
                        pltpu.VMEM((nseq, nblk, 16, CHUNK), F32),
                        pltpu.VMEM((nseq, nblk, CHUNK, LANES), F32),
                        pltpu.VMEM((nseq, 2, nblk, PAIRS, LANES, 2 * LANES), BF16),
                        pltpu.VMEM((nseq, seq_len, D_MODEL), BF16)],
        compiler_params=pltpu.CompilerParams(
            dimension_semantics=("arbitrary",), vmem_limit_bytes=VMEM_LIMIT),
        name="mixer_t%d" % seq_len,
    )(*args)


def _ffn_kernel(mod_ref, ng_ref, cw_ref, x1p_hbm, x1s_hbm, up_hbm, down_hbm, outp_hbm, outs_hbm,
                up_bf, down_bf, act_ref, h2_ref, x1_buf, o_buf, up_stage, down_stage,
                in_sems, out_sems, w_sems, *, layer, tiles_p, tiles_s, seq_p, seq_s):
    i = pl.program_id(0)
    n_tiles = tiles_p + tiles_s
    rows = FFN_ROWS
    half = FFN_SUB // 2
    nsub = D_FF // half
    slot = i % 2
    is_prompt = i < tiles_p

    def piece_rows(tile, first_tile, j):
        base = (tile - first_tile) * rows + j * FFN_GROUP
        return pl.ds(base if isinstance(base, int) else pl.multiple_of(base, FFN_GROUP), FFN_GROUP)

    def x1_copies(x1_hbm, tile, first_tile, buf):
        return [pltpu.make_async_copy(x1_hbm.at[piece_rows(tile, first_tile, j)],
                                      x1_buf.at[buf, :, j, :], in_sems.at[buf, j]) for j in range(8)]

    def out_copies(out_hbm, tile, first_tile, buf):
        return [pltpu.make_async_copy(o_buf.at[buf, :, j, :],
                                      out_hbm.at[piece_rows(tile, first_tile, j)], out_sems.at[buf, j])
                for j in range(8)]

    def fetch(tile, buf):
        @pl.when(tile < tiles_p)
        def _():
            for cp in x1_copies(x1p_hbm, tile, 0, buf):
                cp.start()

        @pl.when(tile >= tiles_p)
        def _():
            for cp in x1_copies(x1s_hbm, tile, tiles_p, buf):
                cp.start()

    def await_fetch(buf):
        for cp in x1_copies(x1p_hbm, 0, 0, buf):
            cp.wait()

    def await_store(buf):
        for cp in out_copies(outp_hbm, 0, 0, buf):
            cp.wait()

    up_rows = up_stage.shape[1]
    down_rows = down_stage.shape[1]

    def slab(c, n):
        return pl.ds(c * n, n) if isinstance(c, int) else pl.ds(pl.multiple_of(c * n, n), n)

    def weight_copies(c, buf):
        return (pltpu.make_async_copy(up_hbm.at[layer, slab(c, up_rows), :], up_stage.at[buf],
                                      w_sems.at[buf, 0]),
                pltpu.make_async_copy(down_hbm.at[layer, slab(c, down_rows), :], down_stage.at[buf],
                                      w_sems.at[buf, 1]))

    @pl.when(i == 0)
    def _():
        for cp in x1_copies(x1p_hbm, 0, 0, 0):
            cp.start()
        depth = up_stage.shape[0]
        for c0 in range(depth - 1):
            for cp in weight_copies(c0, c0):
                cp.start()

        def convert(c, carry):
            buf = c % depth
            ahead = c + depth - 1

            @pl.when(ahead < FFN_WEIGHT_CHUNKS)
            def _():
                for cp in weight_copies(ahead, ahead % depth):
                    cp.start()

            for cp in weight_copies(c, buf):
                cp.wait()
            r = slab(c, up_rows)
            for s in range(nsub):
                up_bf[s, r, 0:half] = up_stage[buf, :, s * half:(s + 1) * half].astype(BF16)
                up_bf[s, r, half:FFN_SUB] = (
                    up_stage[buf, :, D_FF + s * half:D_FF + (s + 1) * half].astype(BF16))
            down_bf[slab(c, down_rows), :] = down_stage[buf].astype(BF16)
            return carry

        lax.fori_loop(0, FFN_WEIGHT_CHUNKS, convert, 0)

    await_fetch(slot)

    @pl.when(i >= 2)
    def _():
        await_store(slot)

    @pl.when(i + 1 < n_tiles)
    def _():
        fetch(i + 1, 1 - slot)

    cond = jnp.where(is_prompt, 0, i - tiles_p + 1)
    _, _, _, sh2, sc2, gt2 = _mod_rows(mod_ref, cond)
    ng = ng_ref[0]
    x1 = x1_buf[slot].reshape(rows, D_MODEL)
    h2_ref[...] = (_rms(x1, ng[2:3] * (1.0 + sc2)) + sh2).astype(BF16)
    h2 = h2_ref[...]

    sub = lax.broadcasted_iota(jnp.int32, (8, FFN_SUB), 0)
    gp, gs = seq_p // FFN_GROUP, seq_s // FFN_GROUP
    starts = ((sub % gs) == 0) | (((sub % gp) == 0) & is_prompt)
    ends = ((sub % gs) == gs - 1) | (((sub % gp) == gp - 1) & is_prompt)
    for s in range(nsub):
        a = jnp.dot(h2, up_bf[s], preferred_element_type=F32)
        cw = jnp.concatenate([cw_ref[0, :, s * half:(s + 1) * half],
                              0.5 * cw_ref[0, :, D_FF + s * half:D_FF + (s + 1) * half]], axis=1)
        a_prev = jnp.concatenate(
            [jnp.where(starts, 0.0, pltpu.roll(a[rows - 8:rows], 1, axis=0)), a[0:rows - 8]], axis=0)
        a_next = jnp.concatenate(
            [a[8:rows], jnp.where(ends, 0.0, pltpu.roll(a[0:8], 7, axis=0))], axis=0)
        ac = cw[0:1] * a_prev + cw[1:2] * a + cw[2:3] * a_next
        hg = ac[:, half:]
        act = (hg * (1.0 + jnp.tanh(hg))) * ac[:, :half]
        act_ref[:, s * half:(s + 1) * half] = act.astype(BF16)
    y = jnp.dot(act_ref[...], down_bf[...], preferred_element_type=F32)
    out = x1_buf[slot].reshape(rows, D_MODEL) + gt2 * _rms(y, ng[3:4])
    o_buf[slot] = out.reshape(FFN_GROUP, 8, D_MODEL)

    @pl.when(is_prompt)
    def _():
        for cp in out_copies(outp_hbm, i, 0, slot):
            cp.start()

    @pl.when(jnp.logical_not(is_prompt))
    def _():
        for cp in out_copies(outs_hbm, i, tiles_p, slot):
            cp.start()

    @pl.when(i == n_tiles - 1)
    def _():
        await_store(1 - slot)
        await_store(slot)


def _ffn_call(x1p, x1s, layer, mod_all, wts, seq_p, seq_s):
    tiles_p = x1p.shape[0] // FFN_ROWS
    tiles_s = x1s.shape[0] // FFN_ROWS
    assert tiles_p >= 1 and tiles_s >= 1 and tiles_p + tiles_s >= 2
    assert FFN_ROWS == 8 * FFN_GROUP and seq_p % FFN_GROUP == 0 and seq_s % seq_p == 0
    half = FFN_SUB // 2

    def resident(shape):
        nd = len(shape)
        return pl.BlockSpec((1,) + tuple(shape[1:]), lambda i: (layer,) + (0,) * (nd - 1),
                            pipeline_mode=pl.Buffered(1))

    hbm = pl.BlockSpec(memory_space=pl.ANY)
    return pl.pallas_call(
        functools.partial(_ffn_kernel, layer=layer, tiles_p=tiles_p, tiles_s=tiles_s,
                          seq_p=seq_p, seq_s=seq_s),
        out_shape=[jax.ShapeDtypeStruct(x1p.shape, F32), jax.ShapeDtypeStruct(x1s.shape, F32)],
        grid=(tiles_p + tiles_s,),
        in_specs=[resident(mod_all.shape), resident(wts["norm_g"].shape), resident(wts["ffn_conv"].shape),
                  hbm, hbm, hbm, hbm],
        out_specs=[hbm, hbm],
        scratch_shapes=[pltpu.VMEM((D_FF // half, D_MODEL, FFN_SUB), BF16),
                        pltpu.VMEM((D_FF, D_MODEL), BF16),
                        pltpu.VMEM((FFN_ROWS, D_FF), BF16),
                        pltpu.VMEM((FFN_ROWS, D_MODEL), BF16),
                        pltpu.VMEM((2, FFN_GROUP, 8, D_MODEL), F32),
                        pltpu.VMEM((2, FFN_GROUP, 8, D_MODEL), F32),
                        pltpu.VMEM((FFN_WEIGHT_BUFS, D_MODEL // FFN_WEIGHT_CHUNKS, 2 * D_FF), F32),
                        pltpu.VMEM((FFN_WEIGHT_BUFS, D_FF // FFN_WEIGHT_CHUNKS, D_MODEL), F32),
                        pltpu.SemaphoreType.DMA((2, 8)),
                        pltpu.SemaphoreType.DMA((2, 8)),
                        pltpu.SemaphoreType.DMA((FFN_WEIGHT_BUFS, 2))],
        compiler_params=pltpu.CompilerParams(
            dimension_semantics=("arbitrary",), vmem_limit_bytes=VMEM_LIMIT),
        name="ffn",
    )(mod_all, wts["norm_g"], wts["ffn_conv"], x1p, x1s, wts["ffn_up"], wts["ffn_down"])


def _pos_tables(t_len):
    quarter = D_MODEL // 4
    omega = (1.0 / (np.float32(POS_BASE) ** (np.arange(quarter, dtype=np.float32) / np.float32(quarter))))
    omega = omega.astype(np.float32)

    def emb(p):
        a = p.astype(np.float32)[:, None] * omega[None, :]
        return np.concatenate([np.sin(a), np.cos(a)], axis=-1).astype(np.float32)

    return jnp.asarray(emb(np.arange(t_len // GRID_W))), jnp.asarray(emb(np.arange(GRID_W)))


def _stabiliser_rows(state_m):
    bs = state_m.shape[0]
    m = jnp.pad(state_m.astype(F32), ((0, 0), (0, 0), (0, 0), (0, 16 - HEADS)))
    return jnp.pad(m.reshape(bs, DEPTH, 1, 32), ((0, 0), (0, 0), (0, 0), (L_MF, LANES - 32 - L_MF)))


def kernel(x_prompt, x_sample, state_C, state_n, state_m, c, c_ctx, norm_g, ada_w, ada_b, w_in,
           gate_b, gmlp_ws, gmlp_b, pool_w, pool_scale, conv_w, w_out, ffn_up, ffn_conv, ffn_down):
    bp, tp, _ = x_prompt.shape
    bs, ts, _ = x_sample.shape
    assert tp == CHUNK and ts % CHUNK == 0 and FFN_ROWS % tp == 0 and ts == FFN_ROWS

    conds = jnp.concatenate([c_ctx[None, :], c, jnp.zeros((7 - bs, D_MODEL), F32)], axis=0)
    mod_all = _modulation(conds, ada_w, ada_b)
    win_p, wout_p = _prep_weights(w_in, w_out)

    gb = jnp.pad(gate_b, ((0, 0), (0, 0), (0, 8 - HEADS))).reshape(DEPTH, 1, 32)
    dg = D_GROUP // 4
    wts = {
        "norm_g": norm_g,
        "w_in": win_p,
        "gate_b": jnp.pad(gb, ((0, 0), (0, 0), (0, LANES - 32))),
        "gmlp_ws": gmlp_ws,
        "gmlp_bias": jnp.repeat(jnp.swapaxes(gmlp_b, 1, 2), dg, axis=2),
        "pool_bd": jnp.einsum("lgcd,gh->lgchd", pool_w, jnp.eye(4, dtype=F32)).reshape(
            DEPTH, D_GROUP, D_GROUP).astype(BF16),
        "pool_scale": pool_scale,
        "conv_w": conv_w,
        "w_out": wout_p,
        "ffn_up": ffn_up,
        "ffn_conv": ffn_conv,
        "ffn_down": ffn_down,
    }
    pos = _pos_tables(ts)
    state = (state_C.astype(F32), state_n.astype(F32), _stabiliser_rows(state_m))

    xp, xs = x_prompt, x_sample
    prev, ms = None, []
    for l in range(DEPTH):
        x1p, c_all, n_all, m_fin = _mixer_call(xp, l, mod_all, wts, 0, emit_state=True,
                                               prev_states=prev, nseq=MIXER_ROWS // tp)
        prev = (c_all, n_all)
        ms.append(jnp.stack([m_fin[:, 0, L_MF:L_MF + HEADS], m_fin[:, 0, L_MB:L_MB + HEADS]], axis=1))

        (x1s,) = _mixer_call(xs, l, mod_all, wts, 1, pos=pos if l == 0 else None, state=state)

        xp, xs = _ffn_call(x1p.reshape(bp * tp, D_MODEL), x1s.reshape(bs * ts, D_MODEL),
                           l, mod_all, wts, tp, ts)
        xp = xp.reshape(bp, tp, D_MODEL)
        xs = xs.reshape(bs, ts, D_MODEL)

    return (xp, xs, c_all, n_all, jnp.stack(ms, axis=1))
```
